```python
import math
import jax, jax.numpy as jnp
from jax import lax
import numpy as np

D_MODEL = 1024
BATCH = 4
SEQ = 8192
DEPTH = 1

CHUNK = 64
Q_BLOCK = 128
ATT_HEADS = 8
ATT_HEAD_DIM = 64
ATT_WIDTH = ATT_HEADS * ATT_HEAD_DIM
IDX_HEADS = 8
IDX_DIM = 64
TOPK_MAX = 256
REL_BUCKETS = 32
REL_MAX_DIST = 1024
CONV_CH = 512
CONV_WIDTH = 31
N_BRANCH = 2
PEER_HEADS = 8
N_KEYS = 128
N_EXPERTS = N_KEYS * N_KEYS
PEER_QDIM = 256
PEER_HALF = PEER_QDIM // 2
PEER_TOPK = 16
PEER_TOKEN_BLOCK = 128
PLE_DIM = 256
EPS = 1e-6

IN_WIDTHS = (ATT_WIDTH, ATT_WIDTH, ATT_WIDTH, IDX_HEADS * IDX_DIM, IDX_DIM, IDX_HEADS, 2 * CONV_CH, N_BRANCH * D_MODEL)
IN_TOTAL = 3 * ATT_WIDTH + IDX_HEADS * IDX_DIM + IDX_DIM + IDX_HEADS + 2 * CONV_CH + N_BRANCH * D_MODEL

kernel_name = 'hybrid_dsa_conformer_peer_block'


def _split_points():
    pts, acc = [], 0
    for w in IN_WIDTHS[:-1]:
        acc += w
        pts.append(acc)
    return pts


def rms_norm(x, g):
    x32 = x.astype(jnp.float32)
    y = x32 * lax.rsqrt(jnp.mean(x32 * x32, axis=-1, keepdims=True) + EPS)
    return (y * g.astype(jnp.float32)).astype(x.dtype)


def layer_norm(x, g, b):
    x32 = x.astype(jnp.float32)
    mu = jnp.mean(x32, axis=-1, keepdims=True)
    xc = x32 - mu
    y = xc * lax.rsqrt(jnp.mean(xc * xc, axis=-1, keepdims=True) + EPS)
    return (y * g.astype(jnp.float32) + b.astype(jnp.float32)).astype(x.dtype)


def t5_bucket(rel):
    half = REL_BUCKETS // 2
    max_exact = half // 2
    ret = jnp.where(rel > 0, half, 0)
    n = jnp.abs(rel)
    nf = jnp.maximum(n, 1).astype(jnp.float32)
    large = max_exact + (jnp.log(nf / max_exact) / math.log(REL_MAX_DIST / max_exact) * (half - max_exact)).astype(jnp.int32)
    large = jnp.minimum(large, half - 1)
    return ret + jnp.where(n < max_exact, n, large)


def dsa_attention(q, k, v, qi, ki, wi, rel_bias):
    B, S = q.shape[0], q.shape[1]
    topk = min(TOPK_MAX, S // 4)
    nb = S // Q_BLOCK
    key_chunk = jnp.arange(S, dtype=jnp.int32) // CHUNK
    scale = ATT_HEAD_DIM ** -0.5
    idx_scale = (IDX_DIM ** -0.5) * (IDX_HEADS ** -0.5)

    def blockify(a):
        return jnp.moveaxis(a.reshape((B, nb, Q_BLOCK) + a.shape[2:]), 1, 0)

    def one_block(args):
        qb, qib, wib, qpos = args
        qchunk = qpos // CHUNK
        dots = jax.nn.relu(jnp.einsum('bqhd,bsd->bqhs', qib, ki))
        score = jnp.einsum('bqhs,bqh->bqs', dots, wib) * idx_scale
        visible = key_chunk[None, :] <= qchunk[:, None]
        score = jnp.where(visible[None], score, -jnp.inf)
        _, sel = lax.top_k(score, topk)
        kg = jax.vmap(lambda kk, ii: kk[ii])(k, sel)
        vg = jax.vmap(lambda vv, ii: vv[ii])(v, sel)
        logits = jnp.einsum('bqhd,bqkhd->bqhk', qb, kg).astype(jnp.float32) * scale
        bias = rel_bias[t5_bucket(sel - qpos[None, :, None])]
        logits = logits + jnp.transpose(bias, (0, 1, 3, 2)).astype(jnp.float32)
        ok = (sel // CHUNK) <= qchunk[None, :, None]
        logits = jnp.where(ok[:, :, None, :], logits, -jnp.inf)
        probs = jax.nn.softmax(logits, axis=-1).astype(vg.dtype)
        return jnp.einsum('bqhk,bqkhd->bqhd', probs, vg)

    qpos_blocks = jnp.arange(S, dtype=jnp.int32).reshape(nb, Q_BLOCK)
    out = lax.map(one_block, (blockify(q), blockify(qi), blockify(wi), qpos_blocks))
    return jnp.moveaxis(out, 0, 1).reshape(B, S, ATT_WIDTH)


def conformer_conv(glu_in, conv_w, conv_b, ln_g, ln_b, w_proj):
    a, g = jnp.split(glu_in, 2, axis=-1)
    u = a * jax.nn.sigmoid(g)
    y = lax.conv_general_dilated(u, conv_w, window_strides=(1,), padding=[(CONV_WIDTH - 1, 0)],
                                 dimension_numbers=('NWC', 'WIO', 'NWC'), feature_group_count=CONV_CH)
    y = layer_norm(y + conv_b, ln_g, ln_b)
    return jax.nn.silu(y) @ w_proj


def peer(h, w_query, sub_keys, expert_u, expert_v):
    B, S, D = h.shape
    q = (h @ w_query).reshape(B, S, PEER_HEADS, 2, PEER_HALF)
    sc = jnp.einsum('bshcd,hckd->bshck', q, sub_keys)
    sv, si = lax.top_k(sc, PEER_TOPK)
    n_cand = PEER_TOPK * PEER_TOPK
    cand_s = (sv[..., 0, :, None] + sv[..., 1, None, :]).reshape(B, S, PEER_HEADS, n_cand)
    cand_i = (si[..., 0, :, None] * N_KEYS + si[..., 1, None, :]).reshape(B, S, PEER_HEADS, n_cand)
    top_s, top_j = lax.top_k(cand_s, PEER_TOPK)
    eidx = jnp.take_along_axis(cand_i, top_j, axis=-1)
    gate = jax.nn.softmax(top_s.astype(jnp.float32), axis=-1).astype(h.dtype)
    n_sel = PEER_HEADS * PEER_TOPK
    nb = (B * S) // PEER_TOKEN_BLOCK
    hb = h.reshape(nb, PEER_TOKEN_BLOCK, D)
    eb = eidx.reshape(nb, PEER_TOKEN_BLOCK, n_sel)
    gb = gate.reshape(nb, PEER_TOKEN_BLOCK, n_sel)

    def one_block(args):
        xt, et, gt = args
        u = expert_u[et]
        v = expert_v[et]
        act = jax.nn.gelu(jnp.einsum('tnd,td->tn', u, xt)) * gt
        return jnp.einsum('tn,tnd->td', act, v)

    return lax.map(one_block, (hb, eb, gb)).reshape(B, S, D)


def setup_inputs(seed: int = 0) -> dict:
    key = jax.random.key(seed)
    ks = jax.random.split(key, 24)
    f = jnp.float32

    def nrm(k, shape, scale):
        return jax.random.normal(k, shape, f) * scale

    def gain(k, shape):
        return 1.0 + 0.05 * jax.random.normal(k, shape, f)

    return {
        'x': nrm(ks[0], (BATCH, SEQ, D_MODEL), 1.0),
        'p': nrm(ks[1], (DEPTH, BATCH, SEQ, PLE_DIM), 1.0),
        'rel_bias': nrm(ks[2], (REL_BUCKETS, ATT_HEADS), 0.1),
        'attn_norm_g': gain(ks[3], (DEPTH, D_MODEL)),
        'w_in': nrm(ks[4], (DEPTH, D_MODEL, IN_TOTAL), D_MODEL ** -0.5),
        'b_gate': nrm(ks[5], (DEPTH, N_BRANCH * D_MODEL), 0.02),
        'q_norm_g': gain(ks[6], (DEPTH, ATT_HEAD_DIM)),
        'k_norm_g': gain(ks[7], (DEPTH, ATT_HEAD_DIM)),
        'w_att_out': nrm(ks[8], (DEPTH, ATT_WIDTH, D_MODEL), ATT_WIDTH ** -0.5),
        'conv_w': nrm(ks[9], (DEPTH, CONV_WIDTH, 1, CONV_CH), CONV_WIDTH ** -0.5),
        'conv_b': nrm(ks[10], (DEPTH, CONV_CH), 0.02),
        'conv_ln_g': gain(ks[11], (DEPTH, CONV_CH)),
        'conv_ln_b': nrm(ks[12], (DEPTH, CONV_CH), 0.02),
        'w_conv_out': nrm(ks[13], (DEPTH, CONV_CH, D_MODEL), CONV_CH ** -0.5),
        'w_out': nrm(ks[14], (DEPTH, D_MODEL, D_MODEL), D_MODEL ** -0.5),
        'ffn_norm_g': gain(ks[15], (DEPTH, D_MODEL)),
        'w_peer_q': nrm(ks[16], (DEPTH, D_MODEL, PEER_HEADS * PEER_QDIM), D_MODEL ** -0.5),
        'peer_sub_keys': nrm(ks[17], (DEPTH, PEER_HEADS, 2, N_KEYS, PEER_HALF), PEER_HALF ** -0.5),
        'peer_u': nrm(ks[18], (DEPTH, N_EXPERTS, D_MODEL), D_MODEL ** -0.5),
        'peer_v': nrm(ks[19], (DEPTH, N_EXPERTS, D_MODEL), PEER_HEADS ** -0.5),
        'ple_norm_g': gain(ks[20], (DEPTH, D_MODEL)),
        'w_ple_gate': nrm(ks[21], (DEPTH, D_MODEL, D_MODEL), D_MODEL ** -0.5),
        'w_ple_proj': nrm(ks[22], (DEPTH, PLE_DIM, D_MODEL), PLE_DIM ** -0.5),
    }


def reference(x, p, rel_bias, attn_norm_g, w_in, b_gate, q_norm_g, k_norm_g, w_att_out,
              conv_w, conv_b, conv_ln_g, conv_ln_b, w_conv_out, w_out, ffn_norm_g,
              w_peer_q, peer_sub_keys, peer_u, peer_v, ple_norm_g, w_ple_gate, w_ple_proj):
    B, S, D = x.shape
    for i in range(DEPTH):
        h = rms_norm(x, attn_norm_g[i])
        proj = h @ w_in[i]
        q, k, v, qi, ki, wi, glu_in, gates = jnp.split(proj, _split_points(), axis=-1)
        q = rms_norm(q.reshape(B, S, ATT_HEADS, ATT_HEAD_DIM), q_norm_g[i])
        k = rms_norm(k.reshape(B, S, ATT_HEADS, ATT_HEAD_DIM), k_norm_g[i])
        v = v.reshape(B, S, ATT_HEADS, ATT_HEAD_DIM)
        qi = qi.reshape(B, S, IDX_HEADS, IDX_DIM)
        y_att = dsa_attention(q, k, v, qi, ki, wi, rel_bias) @ w_att_out[i]
        y_conv = conformer_conv(glu_in, conv_w[i], conv_b[i], conv_ln_g[i], conv_ln_b[i], w_conv_out[i])
        g = jax.nn.sigmoid(gates + b_gate[i]).reshape(B, S, N_BRANCH, D)
        mixed = g[:, :, 0, :] * y_att + g[:, :, 1, :] * y_conv
        x = x + mixed @ w_out[i]
        h2 = rms_norm(x, ffn_norm_g[i])
        x = x + peer(h2, w_peer_q[i], peer_sub_keys[i], peer_u[i], peer_v[i])
        h3 = rms_norm(x, ple_norm_g[i])
        x = x + jax.nn.sigmoid(h3 @ w_ple_gate[i]) * (p[i] @ w_ple_proj[i])
    return x
```

```python
import functools
import math

import numpy as np
import jax
import jax.numpy as jnp
from jax import lax
from jax.experimental import pallas as pl
from jax.experimental.pallas import tpu as pltpu

CHUNK = 64
ATT_HEADS = 8
ATT_HEAD_DIM = 64
IDX_HEADS = 8
IDX_DIM = 64
TOPK_MAX = 256
REL_BUCKETS = 32
REL_MAX_DIST = 1024
CONV_CH = 512
CONV_WIDTH = 31
N_BRANCH = 2
PEER_HEADS = 8
N_KEYS = 128
PEER_TOPK = 16
EPS = 1e-6

LANES = 128
SUBLANES = 8
VMEM_LIMIT = 56 * 1024 * 1024

QB = 128
INT_MIN = -(2 ** 31)
NEG_BIG = -1e30

F32 = jnp.float32
BF16 = jnp.bfloat16
I32 = jnp.int32


def _nt_dot(a, b, precision=None):
    return lax.dot_general(a, b, (((1,), (1,)), ((), ())), precision=precision,
                           preferred_element_type=F32)


def _dot(a, b):
    return jnp.dot(a, b, preferred_element_type=F32)


def _single(shape, index_map):
    return pl.BlockSpec(shape, index_map, pipeline_mode=pl.Buffered(1))


def _params(n_grid_dims):
    return pltpu.CompilerParams(dimension_semantics=("arbitrary",) * n_grid_dims,
                                vmem_limit_bytes=VMEM_LIMIT)


def _t5_bucket_np(rel):
    half = REL_BUCKETS // 2
    max_exact = half // 2
    ret = np.where(rel > 0, half, 0)
    n = np.abs(rel)
    nf = np.maximum(n, 1).astype(np.float32)
    large = max_exact + (np.log(nf / np.float32(max_exact)) / np.float32(math.log(REL_MAX_DIST / max_exact))
                         * np.float32(half - max_exact)).astype(np.int32)
    large = np.minimum(large, half - 1)
    return ret + np.where(n < max_exact, n, large)


def _num_near_tiles(seq):
    n = np.arange(1, max(seq, 2 * REL_MAX_DIST) + 1)
    b = _t5_bucket_np(-n)
    sat = REL_BUCKETS // 2 - 1
    unsat = np.nonzero(b != sat)[0]
    n_sat = int(n[unsat[-1]]) + 1 if unsat.size else 1
    return -(-(n_sat + QB - 1) // QB)


def _bias_kernel(rb_ref, o_ref, *, n_near):
    d = pl.program_id(0)
    i = lax.broadcasted_iota(I32, (QB, QB), 0)
    j = lax.broadcasted_iota(I32, (QB, QB), 1)
    rel = i - j - d * QB
    rel = jnp.where(d >= n_near, -8 * REL_MAX_DIST, rel)
    half = REL_BUCKETS // 2
    max_exact = half // 2
    ret = jnp.where(rel > 0, half, 0)
    n = jnp.abs(rel)
    nf = jnp.maximum(n, 1).astype(F32)
    large = max_exact + (jnp.log(nf / max_exact) / math.log(REL_MAX_DIST / max_exact)
                         * (half - max_exact)).astype(I32)
    large = jnp.minimum(large, half - 1)
    bucket = ret + jnp.where(n < max_exact, n, large)
    for h in range(ATT_HEADS):
        acc = jnp.zeros((QB, QB), F32)
        for b in range(REL_BUCKETS):
            acc = jnp.where(bucket == b, rb_ref[b, h], acc)
        o_ref[0, h] = acc


def _bias_tiles(rel_bias, n_near):
    return pl.pallas_call(
        functools.partial(_bias_kernel, n_near=n_near),
        grid=(n_near + 1,),
        in_specs=[pl.BlockSpec(memory_space=pltpu.SMEM)],
        out_specs=pl.BlockSpec((1, ATT_HEADS, QB, QB), lambda d: (d, 0, 0, 0)),
        out_shape=jax.ShapeDtypeStruct((n_near + 1, ATT_HEADS, QB, QB), F32),
        compiler_params=_params(1),
        name="bias_tiles",
    )(rel_bias)


def _inproj_kernel(x_ref, g_ref, wa_ref, wvt_ref, wwit_ref, wglu_ref, wgate_ref, bgate_ref, gq_ref, gk_ref,
                   q_ref, k_ref, qi_ref, ki_ref, vt_ref, wit_ref, u_ref, gate_ref, *, tm):
    x = x_ref[...]
    h = x * lax.rsqrt(jnp.mean(x * x, axis=-1, keepdims=True) + EPS) * g_ref[...]
    hb = h.astype(BF16)
    hp = ATT_HEADS * LANES
    ya = _dot(hb, wa_ref[...])
    for h_i in range(ATT_HEADS):
        sl = slice(h_i * LANES, (h_i + 1) * LANES)
        qh = ya[:, h_i * LANES:(h_i + 1) * LANES]
        ms = jnp.sum(qh * qh, axis=-1, keepdims=True) * (1.0 / ATT_HEAD_DIM)
        q_ref[:, sl] = (qh * lax.rsqrt(ms + EPS) * gq_ref[...]).astype(BF16)
        kh = ya[:, hp + h_i * LANES:hp + (h_i + 1) * LANES]
        ms = jnp.sum(kh * kh, axis=-1, keepdims=True) * (1.0 / ATT_HEAD_DIM)
        k_ref[:, sl] = (kh * lax.rsqrt(ms + EPS) * gk_ref[...]).astype(BF16)
    qi_ref[...] = ya[:, 2 * hp:3 * hp].astype(BF16)
    ki_ref[...] = ya[:, 3 * hp:3 * hp + LANES].astype(BF16)
    vt = _nt_dot(wvt_ref[...], hb).astype(BF16)
    for c in range(tm // QB):
        vt_ref[c] = vt[:, c * QB:(c + 1) * QB]
    wit_ref[...] = _nt_dot(wwit_ref[...], hb)
    glu = _dot(hb, wglu_ref[...])
    u_ref[...] = glu[:, :CONV_CH] * jax.nn.sigmoid(glu[:, CONV_CH:])
    gate_ref[...] = jax.nn.sigmoid(_dot(hb, wgate_ref[...]) + bgate_ref[...]).astype(BF16)


def _pad_heads(w, nh, hd):
    d = w.shape[0]
    w3 = w.reshape(d, nh, hd)
    w3 = jnp.pad(w3, ((0, 0), (0, 0), (0, LANES - hd)))
    return w3.reshape(d, nh * LANES)


def _in_proj(x2, attn_g, w_in, b_gate, q_g, k_g, tm=256):
    T, D = x2.shape
    aw = ATT_HEADS * ATT_HEAD_DIM
    iw = IDX_HEADS * IDX_DIM
    o = 0
    wq = w_in[:, o:o + aw]; o += aw
    wk = w_in[:, o:o + aw]; o += aw
    wv = w_in[:, o:o + aw]; o += aw
    wqi = w_in[:, o:o + iw]; o += iw
    wki = w_in[:, o:o + IDX_DIM]; o += IDX_DIM
    wwi = w_in[:, o:o + IDX_HEADS]; o += IDX_HEADS
    wglu = w_in[:, o:o + 2 * CONV_CH]; o += 2 * CONV_CH
    wgate = w_in[:, o:o + N_BRANCH * D]
    wa = jnp.concatenate([_pad_heads(wq, ATT_HEADS, ATT_HEAD_DIM), _pad_heads(wk, ATT_HEADS, ATT_HEAD_DIM),
                          _pad_heads(wqi, IDX_HEADS, IDX_DIM),
                          jnp.pad(wki, ((0, 0), (0, LANES - IDX_DIM)))], axis=1).astype(BF16)
    na = wa.shape[1]
    hp = ATT_HEADS * LANES
    pad_g = lambda g, s: jnp.pad(g * s, (0, LANES - ATT_HEAD_DIM)).reshape(1, LANES)
    gq = pad_g(q_g, ATT_HEAD_DIM ** -0.5)
    gk = pad_g(k_g, 1.0)
    const = lambda *shape: _single(shape, lambda i: (0,) * len(shape))
    outs = pl.pallas_call(
        functools.partial(_inproj_kernel, tm=tm),
        grid=(T // tm,),
        in_specs=[pl.BlockSpec((tm, D), lambda i: (i, 0)), const(1, D), const(D, na), const(aw, D),
                  const(IDX_HEADS, D), const(D, 2 * CONV_CH), const(D, N_BRANCH * D), const(1, N_BRANCH * D),
                  const(1, LANES), const(1, LANES)],
        out_specs=[pl.BlockSpec((tm, hp), lambda i: (i, 0)), pl.BlockSpec((tm, hp), lambda i: (i, 0)),
                   pl.BlockSpec((tm, hp), lambda i: (i, 0)), pl.BlockSpec((tm, LANES), lambda i: (i, 0)),
                   pl.BlockSpec((tm // QB, aw, QB), lambda i: (i, 0, 0)),
                   pl.BlockSpec((IDX_HEADS, tm), lambda i: (0, i)),
                   pl.BlockSpec((tm, CONV_CH), lambda i: (i, 0)),
                   pl.BlockSpec((tm, N_BRANCH * D), lambda i: (i, 0))],
        out_shape=[jax.ShapeDtypeStruct((T, hp), BF16), jax.ShapeDtypeStruct((T, hp), BF16),
                   jax.ShapeDtypeStruct((T, hp), BF16), jax.ShapeDtypeStruct((T, LANES), BF16),
                   jax.ShapeDtypeStruct((T // QB, aw, QB), BF16),
                   jax.ShapeDtypeStruct((IDX_HEADS, T), F32),
                   jax.ShapeDtypeStruct((T, CONV_CH), F32),
                   jax.ShapeDtypeStruct((T, N_BRANCH * D), BF16)],
        compiler_params=_params(1),
        name="in_proj",
    )(x2, attn_g.reshape(1, D), wa, wv.T.astype(BF16), wwi.T.astype(BF16), wglu.astype(BF16),
      wgate.astype(BF16), b_gate.reshape(1, N_BRANCH * D), gq, gk)
    return outs


def _attn_kernel(q_ref, qi_ref, wit_ref, k_ref, ki_ref, vt_ref, bias_ref, o_ref, key_scr, att_scr,
                 *, seq, n_near, topk):
    qb = pl.program_id(1)
    nkb = qb + 1
    lane_t = lax.broadcasted_iota(I32, (1, QB), 1) + qb * QB
    qchunk = lane_t // CHUNK
    sub = lax.broadcasted_iota(I32, (QB, QB), 0)
    idx_scale = (IDX_DIM ** -0.5) * (IDX_HEADS ** -0.5)
    wrow = wit_ref[...] * idx_scale

    def score_block(j, carry):
        kib = ki_ref[0, j]
        acc = jnp.zeros((QB, QB), F32)
        for h in range(IDX_HEADS):
            d = _nt_dot(kib, qi_ref[:, h * LANES:(h + 1) * LANES])
            acc = acc + jnp.maximum(d, 0.0) * wrow[h:h + 1, :]
        bits = pltpu.bitcast(acc, I32)
        skey = bits ^ ((bits >> 31) & 0x7FFFFFFF)
        visible = (sub + j * QB) // CHUNK <= qchunk
        key_scr[j] = jnp.where(visible, skey, INT_MIN)
        return carry

    lax.fori_loop(0, nkb, score_block, 0)

    nvis = (qchunk + 1) * CHUNK
    kk = jnp.minimum(topk, nvis)

    def count(pred):
        def body(j, acc8):
            m = jnp.where(pred(key_scr[j], j), 1, 0)
            return acc8 + m.reshape(QB // SUBLANES, SUBLANES, QB).sum(axis=0)
        acc8 = lax.fori_loop(0, nkb, body, jnp.zeros((SUBLANES, QB), I32))
        return acc8.sum(axis=0, keepdims=True)

    def bit_step(i, carry):
        ans, cnt = carry
        cand = ans + lax.shift_left(jnp.int32(1), 31 - i)
        c = count(lambda blk, j: blk >= cand)
        take = c >= kk
        return jnp.where(take, cand, ans), jnp.where(take, c, cnt)

    thr, cnt = lax.fori_loop(0, 32, bit_step,
                             (jnp.full((1, QB), INT_MIN, I32), jnp.full((1, QB), 0, I32) + nkb * QB))

    @pl.when(jnp.max(cnt - kk) > 0)
    def _():
        n_gt = count(lambda blk, j: blk > thr)
        need = kk - n_gt

        def idx_step(i, jbound):
            cand = jbound + lax.shift_left(jnp.int32(1), (seq.bit_length() - 1) - i)
            c = count(lambda blk, j: (blk == thr) & (sub + j * QB < cand))
            return jnp.where(c <= need, cand, jbound)

        jbound = lax.fori_loop(0, seq.bit_length(), idx_step, jnp.zeros((1, QB), I32))

        def drop(j, carry):
            blk = key_scr[j]
            key_scr[j] = jnp.where((blk == thr) & (sub + j * QB >= jbound), INT_MIN, blk)
            return carry

        lax.fori_loop(0, nkb, drop, 0)

    for h in range(ATT_HEADS):
        qh = q_ref[:, h * LANES:(h + 1) * LANES]

        def att_block(j, carry, h=h, qh=qh):
            m, l, acc = carry
            kb = k_ref[0, j, :, h * LANES:(h + 1) * LANES]
            s = _nt_dot(kb, qh) + bias_ref[jnp.minimum(qb - j, n_near), h]
            s = jnp.where(key_scr[j] >= thr, s, -jnp.inf)
            m_new = jnp.maximum(m, jnp.max(s, axis=0, keepdims=True))
            p = jnp.exp(s - m_new)
            alpha = jnp.exp(m - m_new)
            l = alpha * l + jnp.sum(p, axis=0, keepdims=True)
            vtb = vt_ref[0, j, h * ATT_HEAD_DIM:(h + 1) * ATT_HEAD_DIM, :]
            acc = alpha * acc + _dot(vtb, p.astype(BF16))
            return m_new, l, acc

        m0 = jnp.full((1, QB), NEG_BIG, F32)
        l0 = jnp.zeros((1, QB), F32)
        a0 = jnp.zeros((ATT_HEAD_DIM, QB), F32)
        _, l, acc = lax.fori_loop(0, nkb, att_block, (m0, l0, a0))
        att_scr[h * ATT_HEAD_DIM:(h + 1) * ATT_HEAD_DIM, :] = acc / l
    o_ref[...] = att_scr[...].T.astype(o_ref.dtype)


def _attention(q, k, qi, ki, vt, wit, bias, B, S, n_near):
    T = B * S
    nqb = S // QB
    hp = ATT_HEADS * LANES
    aw = ATT_HEADS * ATT_HEAD_DIM
    topk = min(TOPK_MAX, S // 4)
    k4 = k.reshape(B, nqb, QB, hp)
    ki4 = ki.reshape(B, nqb, QB, LANES)
    vt4 = vt.reshape(B, nqb, aw, QB)
    return pl.pallas_call(
        functools.partial(_attn_kernel, seq=S, n_near=n_near, topk=topk),
        grid=(B, nqb),
        in_specs=[pl.BlockSpec((QB, hp), lambda b, i: (b * nqb + i, 0)),
                  pl.BlockSpec((QB, hp), lambda b, i: (b * nqb + i, 0)),
                  pl.BlockSpec((IDX_HEADS, QB), lambda b, i: (0, b * nqb + i)),
                  _single((1, nqb, QB, hp), lambda b, i: (b, 0, 0, 0)),
                  _single((1, nqb, QB, LANES), lambda b, i: (b, 0, 0, 0)),
                  _single((1, nqb, aw, QB), lambda b, i: (b, 0, 0, 0)),
                  _single((n_near + 1, ATT_HEADS, QB, QB), lambda b, i: (0, 0, 0, 0))],
        out_specs=pl.BlockSpec((QB, aw), lambda b, i: (b * nqb + i, 0)),
        out_shape=jax.ShapeDtypeStruct((T, aw), BF16),
        scratch_shapes=[pltpu.VMEM((nqb, QB, QB), I32), pltpu.VMEM((aw, QB), F32)],
        compiler_params=_params(2),
        name="dsa_attention",
    )(q, qi, wit, k4, ki4, vt4, bias)


HALO = 32


def _mix_kernel(att_ref, u_ref, halo_ref, gate_ref, x_ref, cw_ref, cb_ref, lng_ref, lnb_ref, wao_ref, wco_ref,
                wout_ref, gffn_ref, x1_ref, h2_ref, ext_scr, *, tm, tiles_per_seq):
    i = pl.program_id(0)
    first = (i % tiles_per_seq) == 0
    ext_scr[0:HALO, :] = jnp.where(first, 0.0, halo_ref[...])
    ext_scr[HALO:HALO + tm, :] = u_ref[...]
    y = jnp.zeros((tm, CONV_CH), F32)
    for j in range(CONV_WIDTH):
        y = y + cw_ref[j:j + 1, :] * ext_scr[pl.ds(HALO - (CONV_WIDTH - 1) + j, tm), :]
    y = y + cb_ref[...]
    mu = jnp.mean(y, axis=-1, keepdims=True)
    yc = y - mu
    yn = yc * lax.rsqrt(jnp.mean(yc * yc, axis=-1, keepdims=True) + EPS) * lng_ref[...] + lnb_ref[...]
    z = yn * jax.nn.sigmoid(yn)
    y_conv = _dot(z.astype(BF16), wco_ref[...])
    y_att = _dot(att_ref[...], wao_ref[...])
    d = y_att.shape[1]
    g = gate_ref[...]
    mixed = g[:, :d].astype(F32) * y_att + g[:, d:].astype(F32) * y_conv
    x1 = x_ref[...] + _dot(mixed.astype(BF16), wout_ref[...])
    x1_ref[...] = x1
    h2_ref[...] = x1 * lax.rsqrt(jnp.mean(x1 * x1, axis=-1, keepdims=True) + EPS) * gffn_ref[...]


def _mix(att, u, gate, x2, conv_w, conv_b, ln_g, ln_b, w_att_out, w_conv_out, w_out, ffn_g, S, tm=256):
    T, D = x2.shape
    aw = att.shape[1]
    const = lambda *shape: _single(shape, lambda i: (0,) * len(shape))
    hb = tm // HALO
    return pl.pallas_call(
        functools.partial(_mix_kernel, tm=tm, tiles_per_seq=S // tm),
        grid=(T // tm,),
        in_specs=[pl.BlockSpec((tm, aw), lambda i: (i, 0)),
                  pl.BlockSpec((tm, CONV_CH), lambda i: (i, 0)),
                  pl.BlockSpec((HALO, CONV_CH), lambda i: (jnp.maximum(i * hb - 1, 0), 0)),
                  pl.BlockSpec((tm, N_BRANCH * D), lambda i: (i, 0)),
                  pl.BlockSpec((tm, D), lambda i: (i, 0)),
                  const(CONV_WIDTH, CONV_CH), const(1, CONV_CH), const(1, CONV_CH), const(1, CONV_CH),
                  const(aw, D), const(CONV_CH, D), const(D, D), const(1, D)],
        out_specs=[pl.BlockSpec((tm, D), lambda i: (i, 0)), pl.BlockSpec((tm, D), lambda i: (i, 0))],
        out_shape=[jax.ShapeDtypeStruct((T, D), F32), jax.ShapeDtypeStruct((T, D), F32)],
        scratch_shapes=[pltpu.VMEM((HALO + tm, CONV_CH), F32)],
        compiler_params=_params(1),
        name="mix_out_proj",
    )(att, u, u, gate, x2, conv_w.reshape(CONV_WIDTH, CONV_CH), conv_b.reshape(1, CONV_CH),
      ln_g.reshape(1, CONV_CH), ln_b.reshape(1, CONV_CH), w_att_out.astype(BF16), w_conv_out.astype(BF16),
      w_out.astype(BF16), ffn_g.reshape(1, D))


def _top_rows(sc, k, payload=None):
    rows = sc.shape[0]
    iota = lax.broadcasted_iota(I32, sc.shape, 0)
    out_row = lax.broadcasted_iota(I32, (k, sc.shape[1]), 0)
    vals = jnp.zeros((k, sc.shape[1]), F32)
    idxs = jnp.zeros((k, sc.shape[1]), I32)
    for r in range(k):
        m = jnp.max(sc, axis=0, keepdims=True)
        idx = jnp.min(jnp.where(sc == m, iota, rows), axis=0, keepdims=True)
        hit = iota == idx
        rec = idx if payload is None else jnp.max(jnp.where(hit, payload, -1), axis=0, keepdims=True)
        vals = jnp.where(out_row == r, m, vals)
        idxs = jnp.where(out_row == r, rec, idxs)
        sc = jnp.where(hit, -jnp.inf, sc)
    return vals, idxs


def _route_kernel(h2_ref, wq_ref, sk_ref, e_ref, g_ref, qt_scr, et_scr, gt_scr, *, tm):
    half = N_KEYS
    qt_scr[...] = _nt_dot(wq_ref[...], h2_ref[...].astype(BF16)).astype(BF16)

    def head(h, carry):
        tops = []
        for c in range(2):
            row0 = pl.multiple_of((h * 2 + c) * half, half)
            sc = _dot(sk_ref[h * 2 + c], qt_scr[pl.ds(row0, half), :])
            tops.append(_top_rows(sc, PEER_TOPK))
        (a, ia), (b, ib) = tops
        cand = jnp.concatenate([a[r:r + 1, :] + b for r in range(PEER_TOPK)], axis=0)
        cidx = jnp.concatenate([ia[r:r + 1, :] * N_KEYS + ib for r in range(PEER_TOPK)], axis=0)
        ts, te = _top_rows(cand, PEER_TOPK, payload=cidx)
        ex = jnp.exp(ts - ts[0:1, :])
        gate = ex / jnp.sum(ex, axis=0, keepdims=True)
        r0 = pl.multiple_of(h * PEER_TOPK, PEER_TOPK)
        et_scr[pl.ds(r0, PEER_TOPK), :] = te
        gt_scr[pl.ds(r0, PEER_TOPK), :] = gate
        return carry

    lax.fori_loop(0, PEER_HEADS, head, 0)
    e_ref[...] = et_scr[...].T
    g_ref[...] = gt_scr[...].T


def _peer_route(h2, w_peer_q, sub_keys, tm=256):
    T, D = h2.shape
    nsel = PEER_HEADS * PEER_TOPK
    qd = w_peer_q.shape[1]
    half = sub_keys.shape[-1]
    skb = sub_keys.reshape(PEER_HEADS * 2, N_KEYS, half).astype(BF16)
    return pl.pallas_call(
        functools.partial(_route_kernel, tm=tm),
        grid=(T // tm,),
        in_specs=[pl.BlockSpec((tm, D), lambda i: (i, 0)),
                  _single((qd, D), lambda i: (0, 0)),
                  _single((PEER_HEADS * 2, N_KEYS, half), lambda i: (0, 0, 0))],
        out_specs=[pl.BlockSpec((tm, nsel), lambda i: (i, 0)), pl.BlockSpec((tm, nsel), lambda i: (i, 0))],
        out_shape=[jax.ShapeDtypeStruct((T, nsel), I32), jax.ShapeDtypeStruct((T, nsel), F32)],
        scratch_shapes=[pltpu.VMEM((qd, tm), BF16), pltpu.VMEM((nsel, tm), I32), pltpu.VMEM((nsel, tm), F32)],
        compiler_params=_params(1),
        name="peer_route",
    )(h2, w_peer_q.T.astype(BF16), skb)


TOK = 8


def _gelu_tanh(x):
    return 0.5 * x * (1.0 + jnp.tanh(math.sqrt(2.0 / math.pi) * (x + 0.044715 * (x * x * x))))


def _sublane_sums(ps):
    sub = lax.broadcasted_iota(I32, (SUBLANES, LANES), 0)
    lvl, stride = list(ps), SUBLANES // 2
    while len(lvl) > 1:
        half = len(lvl) // 2
        low = (sub & stride) == 0
        nxt = []
        for n in range(half):
            a, b = lvl[n], lvl[n + half]
            nxt.append(jnp.where(low, a + pltpu.roll(a, SUBLANES - stride, 0), b + pltpu.roll(b, stride, 0)))
        lvl, stride = nxt, stride // 2
    return lvl[0]


def _expert_kernel(idx0_ref, idxn_ref, h2_ref, x1_ref, gate_ref, uv_ref, o_ref, buf, sem, abc_scr, *, nsel):
    i = pl.program_id(0)
    nsteps = pl.num_programs(0)
    slot = i % 2
    rows = TOK * nsel
    dsub = h2_ref.shape[1]

    def row_copy(e, s, r):
        return pltpu.make_async_copy(uv_ref.at[e], buf.at[s, r], sem.at[s])

    def issue(idx_ref, s):
        def per_token(t, carry):
            for n in range(nsel):
                row_copy(idx_ref[t, n], s, t * nsel + n).start()
            return carry
        lax.fori_loop(0, TOK, per_token, 0)

    @pl.when(i == 0)
    def _():
        issue(idx0_ref, 0)

    @pl.when(i + 1 < nsteps)
    def _():
        issue(idxn_ref, 1 - slot)

    pltpu.make_async_copy(uv_ref.at[pl.ds(0, rows)], buf.at[slot], sem.at[slot]).wait()

    ones = jnp.ones((SUBLANES, LANES), F32)
    row_id = lax.broadcasted_iota(I32, (TOK, nsel), 0)

    def dots(t, s8):
        xt = h2_ref[t]
        qs = []
        for g in range(nsel // SUBLANES):
            ps = [buf[slot, t * nsel + g * SUBLANES + k, 0:dsub, :] * xt for k in range(SUBLANES)]
            qs.append(_sublane_sums(ps))
        q = jnp.concatenate(qs, axis=0)
        srow = _nt_dot(ones, q, precision=lax.Precision.HIGHEST)
        return jnp.where(row_id == t, srow[0:TOK, :], s8)

    s8 = lax.fori_loop(0, TOK, dots, jnp.zeros((TOK, nsel), F32))
    a8 = _gelu_tanh(s8) * gate_ref[...]
    a8t = jnp.concatenate([a8, jnp.zeros((nsel - TOK, nsel), F32)], axis=0).T

    for t in range(TOK):
        abc_scr[...] = jnp.broadcast_to(a8t[:, t:t + 1], (nsel, LANES))
        acc = jnp.zeros((dsub, LANES), F32)
        for n in range(nsel):
            acc = acc + abc_scr[n:n + 1, :] * buf[slot, t * nsel + n, dsub:2 * dsub, :]
        o_ref[t] = x1_ref[t] + acc


def _peer_experts(eidx, gates, h2, x1, peer_u, peer_v):
    T, D = x1.shape
    nsel = eidx.shape[1]
    dsub = D // LANES
    ne = peer_u.shape[0]
    uv = jnp.concatenate([peer_u.reshape(ne, dsub, LANES), peer_v.reshape(ne, dsub, LANES)], axis=1)
    nsteps = T // TOK
    out = pl.pallas_call(
        functools.partial(_expert_kernel, nsel=nsel),
        grid=(nsteps,),
        in_specs=[pl.BlockSpec((TOK, nsel), lambda i: (0, 0), memory_space=pltpu.SMEM),
                  pl.BlockSpec((TOK, nsel), lambda i: (jnp.minimum(i + 1, nsteps - 1), 0), memory_space=pltpu.SMEM),
                  pl.BlockSpec((TOK, dsub, LANES), lambda i: (i, 0, 0)),
                  pl.BlockSpec((TOK, dsub, LANES), lambda i: (i, 0, 0)),
                  pl.BlockSpec((TOK, nsel), lambda i: (i, 0)),
                  pl.BlockSpec(memory_space=pl.ANY)],
        out_specs=pl.BlockSpec((TOK, dsub, LANES), lambda i: (i, 0, 0)),
        out_shape=jax.ShapeDtypeStruct((T, dsub, LANES), F32),
        scratch_shapes=[pltpu.VMEM((2, TOK * nsel, 2 * dsub, LANES), F32), pltpu.SemaphoreType.DMA((2,)),
                        pltpu.VMEM((nsel, LANES), F32)],
        compiler_params=_params(1),
        name="peer_experts",
    )(eidx, eidx, h2.reshape(T, dsub, LANES), x1.reshape(T, dsub, LANES), gates, uv)
    return out.reshape(T, D)


def _ple_kernel(x_ref, p_ref, g_ref, wg_ref, wp_ref, o_ref):
    x = x_ref[...]
    h = x * lax.rsqrt(jnp.mean(x * x, axis=-1, keepdims=True) + EPS) * g_ref[...]
    gate = jax.nn.sigmoid(_dot(h.astype(BF16), wg_ref[...]))
    o_ref[...] = x + gate * _dot(p_ref[...].astype(BF16), wp_ref[...])


def _ple(x2, p2, ple_g, w_gate, w_proj, tm=512):
    T, D = x2.shape
    pd = p2.shape[1]
    return pl.pallas_call(
        _ple_kernel,
        grid=(T // tm,),
        in_specs=[pl.BlockSpec((tm, D), lambda i: (i, 0)), pl.BlockSpec((tm, pd), lambda i: (i, 0)),
                  _single((1, D), lambda i: (0, 0)), _single((D, D), lambda i: (0, 0)),
                  _single((pd, D), lambda i: (0, 0))],
        out_specs=pl.BlockSpec((tm, D), lambda i: (i, 0)),
        out_shape=jax.ShapeDtypeStruct((T, D), F32),
        compiler_params=_params(1),
        name="ple",
    )(x2, p2, ple_g.reshape(1, D), w_gate.astype(BF16), w_proj.astype(BF16))


def kernel(x, p, rel_bias, attn_norm_g, w_in, b_gate, q_norm_g, k_norm_g, w_att_out, conv_w, conv_b, conv_ln_g,
           conv_ln_b, w_conv_out, w_out, ffn_norm_g, w_peer_q, peer_sub_keys, peer_u, peer_v, ple_norm_g,
           w_ple_gate, w_ple_proj):
    B, S, D = x.shape
    depth = w_in.shape[0]
    T = B * S
    assert S % 256 == 0 and D % LANES == 0 and T % 512 == 0
    n_near = _num_near_tiles(S)
    bias = _bias_tiles(rel_bias, n_near)
    x2 = x.reshape(T, D)
    for i in range(depth):
        q, k, qi, ki, vt, wit, u, gate = _in_proj(x2, attn_norm_g[i], w_in[i], b_gate[i], q_norm_g[i], k_norm_g[i])
        att = _attention(q, k, qi, ki, vt, wit, bias, B, S, n_near)
        x1, h2 = _mix(att, u, gate, x2, conv_w[i], conv_b[i], conv_ln_g[i], conv_ln_b[i], w_att_out[i],
                      w_conv_out[i], w_out[i], ffn_norm_g[i], S)
        eidx, gates = _peer_route(h2, w_peer_q[i], peer_sub_keys[i])
        x2 = _peer_experts(eidx, gates, h2, x1, peer_u[i], peer_v[i])
        x2 = _ple(x2, p[i].reshape(T, -1), ple_norm_g[i], w_ple_gate[i], w_ple_proj[i])
    return x2.reshape(B, S, D)
```

```python
import functools
import math

import numpy as np
import jax
import jax.numpy as jnp
from jax import lax
from jax.experimental import pallas as pl
from jax.experimental.pallas import tpu as pltpu

CHUNK = 64
ATT_HEADS = 8
ATT_HEAD_DIM = 64
IDX_HEADS = 8
IDX_DIM = 64
TOPK_MAX = 256
REL_BUCKETS = 32
REL_MAX_DIST = 1024
CONV_CH = 512
CONV_WIDTH = 31
N_BRANCH = 2
PEER_HEADS = 8
N_KEYS = 128
PEER_TOPK = 16
EPS = 1e-6

LANES = 128
SUBLANES = 8
VMEM_LIMIT = 56 * 1024 * 1024

QB = 128
INT_MIN = -(2 ** 31)
NEG_BIG = -1e30
LOG2E = math.log2(math.e)

F32 = jnp.float32
BF16 = jnp.bfloat16
I32 = jnp.int32


def _nt_dot(a, b, precision=None):
    return lax.dot_general(a, b, (((1,), (1,)), ((), ())), precision=precision,
                           preferred_element_type=F32)


def _dot(a, b):
    return jnp.dot(a, b, preferred_element_type=F32)


def _single(shape, index_map):
    return pl.BlockSpec(shape, index_map, pipeline_mode=pl.Buffered(1))


def _params(n_grid_dims):
    return pltpu.CompilerParams(dimension_semantics=("arbitrary",) * n_grid_dims,
                                vmem_limit_bytes=VMEM_LIMIT)


def _t5_bucket_np(rel):
    half = REL_BUCKETS // 2
    max_exact = half // 2
    ret = np.where(rel > 0, half, 0)
    n = np.abs(rel)
    nf = np.maximum(n, 1).astype(np.float32)
    large = max_exact + (np.log(nf / np.float32(max_exact)) / np.float32(math.log(REL_MAX_DIST / max_exact))
                         * np.float32(half - max_exact)).astype(np.int32)
    large = np.minimum(large, half - 1)
    return ret + np.where(n < max_exact, n, large)


def _num_near_tiles(seq):
    n = np.arange(1, max(seq, 2 * REL_MAX_DIST) + 1)
    b = _t5_bucket_np(-n)
    sat = REL_BUCKETS // 2 - 1
    unsat = np.nonzero(b != sat)[0]
    n_sat = int(n[unsat[-1]]) + 1 if unsat.size else 1
    return -(-(n_sat + QB - 1) // QB)


def _bias_kernel(rb_ref, o_ref, *, n_near):
    d = pl.program_id(0)
    i = lax.broadcasted_iota(I32, (QB, QB), 0)
    j = lax.broadcasted_iota(I32, (QB, QB), 1)
    rel = i - j - d * QB
    rel = jnp.where(d >= n_near, -8 * REL_MAX_DIST, rel)
    half = REL_BUCKETS // 2
    max_exact = half // 2
    ret = jnp.where(rel > 0, half, 0)
    n = jnp.abs(rel)
    nf = jnp.maximum(n, 1).astype(F32)
    large = max_exact + (jnp.log(nf / max_exact) / math.log(REL_MAX_DIST / max_exact)
                         * (half - max_exact)).astype(I32)
    large = jnp.minimum(large, half - 1)
    bucket = ret + jnp.where(n < max_exact, n, large)
    for h in range(ATT_HEADS):
        acc = jnp.zeros((QB, QB), F32)
        for b in range(REL_BUCKETS):
            acc = jnp.where(bucket == b, rb_ref[b, h], acc)
        o_ref[0, h] = acc * LOG2E


def _bias_tiles(rel_bias, n_near):
    return pl.pallas_call(
        functools.partial(_bias_kernel, n_near=n_near),
        grid=(n_near + 1,),
        in_specs=[pl.BlockSpec(memory_space=pltpu.SMEM)],
        out_specs=pl.BlockSpec((1, ATT_HEADS, QB, QB), lambda d: (d, 0, 0, 0)),
        out_shape=jax.ShapeDtypeStruct((n_near + 1, ATT_HEADS, QB, QB), F32),
        compiler_params=_params(1),
        name="bias_tiles",
    )(rel_bias)


def _inproj_kernel(x_ref, g_ref, wa_ref, wvt_ref, wwit_ref, wglu_ref, wgate_ref, bgate_ref, gq_ref, gk_ref,
                   q_ref, k_ref, qi_ref, ki_ref, vt_ref, wit_ref, u_ref, gate_ref, *, tm):
    x = x_ref[...]
    h = x * lax.rsqrt(jnp.mean(x * x, axis=-1, keepdims=True) + EPS) * g_ref[...]
    hb = h.astype(BF16)
    hp = ATT_HEADS * LANES
    ya = _dot(hb, wa_ref[...])
    for h_i in range(ATT_HEADS):
        sl = slice(h_i * LANES, (h_i + 1) * LANES)
        qh = ya[:, h_i * LANES:(h_i + 1) * LANES]
        ms = jnp.sum(qh * qh, axis=-1, keepdims=True) * (1.0 / ATT_HEAD_DIM)
        q_ref[:, sl] = (qh * lax.rsqrt(ms + EPS) * gq_ref[...]).astype(BF16)
        kh = ya[:, hp + h_i * LANES:hp + (h_i + 1) * LANES]
        ms = jnp.sum(kh * kh, axis=-1, keepdims=True) * (1.0 / ATT_HEAD_DIM)
        k_ref[:, sl] = (kh * lax.rsqrt(ms + EPS) * gk_ref[...]).astype(BF16)
    qi_ref[...] = ya[:, 2 * hp:3 * hp].astype(BF16)
    ki_ref[...] = ya[:, 3 * hp:3 * hp + LANES].astype(BF16)
    vt = _nt_dot(wvt_ref[...], hb).astype(BF16)
    for c in range(tm // QB):
        vt_ref[c] = vt[:, c * QB:(c + 1) * QB]
    wit_ref[...] = _nt_dot(wwit_ref[...], hb)
    glu = _dot(hb, wglu_ref[...])
    u_ref[...] = glu[:, :CONV_CH] * jax.nn.sigmoid(glu[:, CONV_CH:])
    gate_ref[...] = jax.nn.sigmoid(_dot(hb, wgate_ref[...]) + bgate_ref[...]).astype(BF16)


def _pad_heads(w, nh, hd):
    d = w.shape[0]
    w3 = w.reshape(d, nh, hd)
    w3 = jnp.pad(w3, ((0, 0), (0, 0), (0, LANES - hd)))
    return w3.reshape(d, nh * LANES)


def _in_proj(x2, attn_g, w_in, b_gate, q_g, k_g, tm=256):
    T, D = x2.shape
    aw = ATT_HEADS * ATT_HEAD_DIM
    iw = IDX_HEADS * IDX_DIM
    o = 0
    wq = w_in[:, o:o + aw]; o += aw
    wk = w_in[:, o:o + aw]; o += aw
    wv = w_in[:, o:o + aw]; o += aw
    wqi = w_in[:, o:o + iw]; o += iw
    wki = w_in[:, o:o + IDX_DIM]; o += IDX_DIM
    wwi = w_in[:, o:o + IDX_HEADS]; o += IDX_HEADS
    wglu = w_in[:, o:o + 2 * CONV_CH]; o += 2 * CONV_CH
    wgate = w_in[:, o:o + N_BRANCH * D]
    wa = jnp.concatenate([_pad_heads(wq, ATT_HEADS, ATT_HEAD_DIM), _pad_heads(wk, ATT_HEADS, ATT_HEAD_DIM),
                          _pad_heads(wqi, IDX_HEADS, IDX_DIM),
                          jnp.pad(wki, ((0, 0), (0, LANES - IDX_DIM)))], axis=1).astype(BF16)
    na = wa.shape[1]
    hp = ATT_HEADS * LANES
    pad_g = lambda g, s: jnp.pad(g * s, (0, LANES - ATT_HEAD_DIM)).reshape(1, LANES)
    gq = pad_g(q_g, ATT_HEAD_DIM ** -0.5 * LOG2E)
    gk = pad_g(k_g, 1.0)
    const = lambda *shape: _single(shape, lambda i: (0,) * len(shape))
    outs = pl.pallas_call(
        functools.partial(_inproj_kernel, tm=tm),
        grid=(T // tm,),
        in_specs=[pl.BlockSpec((tm, D), lambda i: (i, 0)), const(1, D), const(D, na), const(aw, D),
                  const(IDX_HEADS, D), const(D, 2 * CONV_CH), const(D, N_BRANCH * D), const(1, N_BRANCH * D),
                  const(1, LANES), const(1, LANES)],
        out_specs=[pl.BlockSpec((tm, hp), lambda i: (i, 0)), pl.BlockSpec((tm, hp), lambda i: (i, 0)),
                   pl.BlockSpec((tm, hp), lambda i: (i, 0)), pl.BlockSpec((tm, LANES), lambda i: (i, 0)),
                   pl.BlockSpec((tm // QB, aw, QB), lambda i: (i, 0, 0)),
                   pl.BlockSpec((IDX_HEADS, tm), lambda i: (0, i)),
                   pl.BlockSpec((tm, CONV_CH), lambda i: (i, 0)),
                   pl.BlockSpec((tm, N_BRANCH * D), lambda i: (i, 0))],
        out_shape=[jax.ShapeDtypeStruct((T, hp), BF16), jax.ShapeDtypeStruct((T, hp), BF16),
                   jax.ShapeDtypeStruct((T, hp), BF16), jax.ShapeDtypeStruct((T, LANES), BF16),
                   jax.ShapeDtypeStruct((T // QB, aw, QB), BF16),
                   jax.ShapeDtypeStruct((IDX_HEADS, T), F32),
                   jax.ShapeDtypeStruct((T, CONV_CH), F32),
                   jax.ShapeDtypeStruct((T, N_BRANCH * D), BF16)],
        compiler_params=_params(1),
        name="in_proj",
    )(x2, attn_g.reshape(1, D), wa, wv.T.astype(BF16), wwi.T.astype(BF16), wglu.astype(BF16),
      wgate.astype(BF16), b_gate.reshape(1, N_BRANCH * D), gq, gk)
    return outs


def _attn_kernel(q_ref, qi_ref, wit_ref, k_ref, ki_ref, vt_ref, bias_ref, o_ref, key_scr, att_scr, *scr,
                 seq, n_near, topk):
    qb = pl.program_id(1)
    npair = (qb + 2) // 2
    lane_t = lax.broadcasted_iota(I32, (1, QB), 1) + qb * QB
    qchunk = lane_t // CHUNK
    sub = lax.broadcasted_iota(I32, (QB, QB), 0)
    idx_scale = (IDX_DIM ** -0.5) * (IDX_HEADS ** -0.5)
    wrow = wit_ref[...] * idx_scale

    def pair_loop(body, init):
        def pair(j2, carry):
            return body(2 * j2 + 1, body(2 * j2, carry))
        return lax.fori_loop(0, npair, pair, init)

    qi_all = jnp.concatenate([qi_ref[:, h * LANES:(h + 1) * LANES] for h in range(IDX_HEADS)], axis=0)

    def score_block(j, carry):
        d = _nt_dot(ki_ref[0, j], qi_all)
        acc = jnp.zeros((QB, QB), F32)
        for h in range(IDX_HEADS):
            acc = acc + jnp.maximum(d[:, h * QB:(h + 1) * QB], 0.0) * wrow[h:h + 1, :]
        bits = pltpu.bitcast(acc, I32)
        skey = bits ^ ((bits >> 31) & 0x7FFFFFFF)
        visible = (sub + j * QB) // CHUNK <= qchunk
        key_scr[j] = jnp.where(visible, skey, INT_MIN)
        return carry

    pair_loop(score_block, 0)

    nvis = (qchunk + 1) * CHUNK
    kk = jnp.minimum(topk, nvis)

    def count(pred):
        def body(j, acc8):
            m = jnp.where(pred(key_scr[j], j), 1, 0)
            return acc8 + m.reshape(QB // SUBLANES, SUBLANES, QB).sum(axis=0)
        return pair_loop(body, jnp.zeros((SUBLANES, QB), I32)).sum(axis=0, keepdims=True)

    def bit_step(i, carry):
        ans, cnt = carry
        cand = ans + lax.shift_left(jnp.int32(1), 31 - i)
        c = count(lambda blk, j: blk >= cand)
        take = c >= kk
        return jnp.where(take, cand, ans), jnp.where(take, c, cnt)

    thr, cnt = lax.fori_loop(0, 32, bit_step,
                             (jnp.full((1, QB), INT_MIN, I32), jnp.full((1, QB), 0, I32) + 2 * npair * QB))

    @pl.when(jnp.max(cnt - kk) > 0)
    def _():
        n_gt = count(lambda blk, j: blk > thr)
        need = kk - n_gt

        def idx_step(i, jbound):
            cand = jbound + lax.shift_left(jnp.int32(1), (seq.bit_length() - 1) - i)
            c = count(lambda blk, j: (blk == thr) & (sub + j * QB < cand))
            return jnp.where(c <= need, cand, jbound)

        jbound = lax.fori_loop(0, seq.bit_length(), idx_step, jnp.zeros((1, QB), I32))

        def drop(j, carry):
            blk = key_scr[j]
            key_scr[j] = jnp.where((blk == thr) & (sub + j * QB >= jbound), INT_MIN, blk)
            return carry

        pair_loop(drop, 0)

    acc_refs, s_even, s_odd = scr[:ATT_HEADS], scr[ATT_HEADS], scr[ATT_HEADS + 1]
    for acc_ref in acc_refs:
        acc_ref[...] = jnp.zeros(acc_ref.shape, F32)
    head_row = lax.broadcasted_iota(I32, (ATT_HEADS, QB), 0)
    last_block = k_ref.shape[1] - 1

    def qk(j, s_ref):
        jc = jnp.minimum(j, last_block)
        for h in range(ATT_HEADS):
            s_ref[h] = _nt_dot(k_ref[0, jc, :, h * LANES:(h + 1) * LANES], q_ref[:, h * LANES:(h + 1) * LANES])

    def softmax_pv(j, s_ref, carry):
        m_all, l_all = carry
        masked = jnp.where(key_scr[j] >= thr, 0.0, -jnp.inf)
        tile = jnp.clip(qb - j, 0, n_near)
        for h in range(ATT_HEADS):
            rows = slice(h * ATT_HEAD_DIM, (h + 1) * ATT_HEAD_DIM)
            s = s_ref[h] + bias_ref[tile, h] + masked
            m = m_all[h:h + 1, :]
            m_new = jnp.maximum(m, jnp.max(s, axis=0, keepdims=True))
            p = jnp.exp2(s - m_new)
            alpha = jnp.exp2(m - m_new)
            l_new = alpha * l_all[h:h + 1, :] + jnp.sum(p, axis=0, keepdims=True)
            m_all = jnp.where(head_row == h, m_new, m_all)
            l_all = jnp.where(head_row == h, l_new, l_all)
            acc_refs[h][...] = alpha * acc_refs[h][...] + _dot(vt_ref[0, j, rows, :], p.astype(BF16))
        return m_all, l_all

    def att_pair(j2, carry):
        qk(2 * j2 + 1, s_odd)
        carry = softmax_pv(2 * j2, s_even, carry)
        qk(2 * j2 + 2, s_even)
        return softmax_pv(2 * j2 + 1, s_odd, carry)

    qk(0, s_even)
    _, l_all = lax.fori_loop(0, npair, att_pair,
                             (jnp.full((ATT_HEADS, QB), NEG_BIG, F32), jnp.zeros((ATT_HEADS, QB), F32)))
    for h in range(ATT_HEADS):
        rows = slice(h * ATT_HEAD_DIM, (h + 1) * ATT_HEAD_DIM)
        att_scr[rows, :] = acc_refs[h][...] / l_all[h:h + 1, :]
    o_ref[...] = att_scr[...].T.astype(o_ref.dtype)


def _attention(q, k, qi, ki, vt, wit, bias, B, S, n_near):
    T = B * S
    nqb = S // QB
    hp = ATT_HEADS * LANES
    aw = ATT_HEADS * ATT_HEAD_DIM
    topk = min(TOPK_MAX, S // 4)
    k4 = k.reshape(B, nqb, QB, hp)
    ki4 = ki.reshape(B, nqb, QB, LANES)
    vt4 = vt.reshape(B, nqb, aw, QB)
    return pl.pallas_call(
        functools.partial(_attn_kernel, seq=S, n_near=n_near, topk=topk),
        grid=(B, nqb),
        in_specs=[pl.BlockSpec((QB, hp), lambda b, i: (b * nqb + i, 0)),
                  pl.BlockSpec((QB, hp), lambda b, i: (b * nqb + i, 0)),
                  pl.BlockSpec((IDX_HEADS, QB), lambda b, i: (0, b * nqb + i)),
                  _single((1, nqb, QB, hp), lambda b, i: (b, 0, 0, 0)),
                  _single((1, nqb, QB, LANES), lambda b, i: (b, 0, 0, 0)),
                  _single((1, nqb, aw, QB), lambda b, i: (b, 0, 0, 0)),
                  _single((n_near + 1, ATT_HEADS, QB, QB), lambda b, i: (0, 0, 0, 0))],
        out_specs=pl.BlockSpec((QB, aw), lambda b, i: (b * nqb + i, 0)),
        out_shape=jax.ShapeDtypeStruct((T, aw), BF16),
        scratch_shapes=[pltpu.VMEM((nqb, QB, QB), I32), pltpu.VMEM((aw, QB), F32)]
        + [pltpu.VMEM((ATT_HEAD_DIM, QB), F32) for _ in range(ATT_HEADS)]
        + [pltpu.VMEM((ATT_HEADS, QB, QB), F32) for _ in range(2)],
        compiler_params=_params(2),
        name="dsa_attention",
    )(q, qi, wit, k4, ki4, vt4, bias)


HALO = 32


def _mix_kernel(att_ref, u_ref, halo_ref, gate_ref, x_ref, cw_ref, cb_ref, lng_ref, lnb_ref, wao_ref, wco_ref,
                wout_ref, gffn_ref, x1_ref, h2_ref, ext_scr, *, tm, tiles_per_seq):
    i = pl.program_id(0)
    first = (i % tiles_per_seq) == 0
    ext_scr[0:HALO, :] = jnp.where(first, 0.0, halo_ref[...])
    ext_scr[HALO:HALO + tm, :] = u_ref[...]
    y = jnp.zeros((tm, CONV_CH), F32)
    for j in range(CONV_WIDTH):
        y = y + cw_ref[j:j + 1, :] * ext_scr[pl.ds(HALO - (CONV_WIDTH - 1) + j, tm), :]
    y = y + cb_ref[...]
    mu = jnp.mean(y, axis=-1, keepdims=True)
    yc = y - mu
    yn = yc * lax.rsqrt(jnp.mean(yc * yc, axis=-1, keepdims=True) + EPS) * lng_ref[...] + lnb_ref[...]
    z = yn * jax.nn.sigmoid(yn)
    y_conv = _dot(z.astype(BF16), wco_ref[...])
    y_att = _dot(att_ref[...], wao_ref[...])
    d = y_att.shape[1]
    g = gate_ref[...]
    mixed = g[:, :d].astype(F32) * y_att + g[:, d:].astype(F32) * y_conv
    x1 = x_ref[...] + _dot(mixed.astype(BF16), wout_ref[...])
    x1_ref[...] = x1
    h2_ref[...] = x1 * lax.rsqrt(jnp.mean(x1 * x1, axis=-1, keepdims=True) + EPS) * gffn_ref[...]


def _mix(att, u, gate, x2, conv_w, conv_b, ln_g, ln_b, w_att_out, w_conv_out, w_out, ffn_g, S, tm=256):
    T, D = x2.shape
    aw = att.shape[1]
    const = lambda *shape: _single(shape, lambda i: (0,) * len(shape))
    hb = tm // HALO
    return pl.pallas_call(
        functools.partial(_mix_kernel, tm=tm, tiles_per_seq=S // tm),
        grid=(T // tm,),
        in_specs=[pl.BlockSpec((tm, aw), lambda i: (i, 0)),
                  pl.BlockSpec((tm, CONV_CH), lambda i: (i, 0)),
                  pl.BlockSpec((HALO, CONV_CH), lambda i: (jnp.maximum(i * hb - 1, 0), 0)),
                  pl.BlockSpec((tm, N_BRANCH * D), lambda i: (i, 0)),
                  pl.BlockSpec((tm, D), lambda i: (i, 0)),
                  const(CONV_WIDTH, CONV_CH), const(1, CONV_CH), const(1, CONV_CH), const(1, CONV_CH),
                  const(aw, D), const(CONV_CH, D), const(D, D), const(1, D)],
        out_specs=[pl.BlockSpec((tm, D), lambda i: (i, 0)), pl.BlockSpec((tm, D), lambda i: (i, 0))],
        out_shape=[jax.ShapeDtypeStruct((T, D), F32), jax.ShapeDtypeStruct((T, D), F32)],
        scratch_shapes=[pltpu.VMEM((HALO + tm, CONV_CH), F32)],
        compiler_params=_params(1),
        name="mix_out_proj",
    )(att, u, u, gate, x2, conv_w.reshape(CONV_WIDTH, CONV_CH), conv_b.reshape(1, CONV_CH),
      ln_g.reshape(1, CONV_CH), ln_b.reshape(1, CONV_CH), w_att_out.astype(BF16), w_conv_out.astype(BF16),
      w_out.astype(BF16), ffn_g.reshape(1, D))


def _top_rows(sc, k, payload=None):
    rows = sc.shape[0]
    iota = lax.broadcasted_iota(I32, sc.shape, 0)
    out_row = lax.broadcasted_iota(I32, (k, sc.shape[1]), 0)
    vals = jnp.zeros((k, sc.shape[1]), F32)
    idxs = jnp.zeros((k, sc.shape[1]), I32)
    for r in range(k):
        m = jnp.max(sc, axis=0, keepdims=True)
        idx = jnp.min(jnp.where(sc == m, iota, rows), axis=0, keepdims=True)
        hit = iota == idx
        rec = idx if payload is None else jnp.max(jnp.where(hit, payload, -1), axis=0, keepdims=True)
        vals = jnp.where(out_row == r, m, vals)
        idxs = jnp.where(out_row == r, rec, idxs)
        sc = jnp.where(hit, -jnp.inf, sc)
    return vals, idxs


def _route_kernel(h2_ref, wq_ref, sk_ref, e_ref, g_ref, qt_scr, et_scr, gt_scr, *, tm):
    half = N_KEYS
    qt_scr[...] = _nt_dot(wq_ref[...], h2_ref[...].astype(BF16)).astype(BF16)

    def head(h, carry):
        tops = []
        for c in range(2):
            row0 = pl.multiple_of((h * 2 + c) * half, half)
            sc = _dot(sk_ref[h * 2 + c], qt_scr[pl.ds(row0, half), :])
            tops.append(_top_rows(sc, PEER_TOPK))
        (a, ia), (b, ib) = tops
        cand = jnp.concatenate([a[r:r + 1, :] + b for r in range(PEER_TOPK)], axis=0)
        cidx = jnp.concatenate([ia[r:r + 1, :] * N_KEYS + ib for r in range(PEER_TOPK)], axis=0)
        ts, te = _top_rows(cand, PEER_TOPK, payload=cidx)
        ex = jnp.exp(ts - ts[0:1, :])
        gate = ex / jnp.sum(ex, axis=0, keepdims=True)
        r0 = pl.multiple_of(h * PEER_TOPK, PEER_TOPK)
        et_scr[pl.ds(r0, PEER_TOPK), :] = te
        gt_scr[pl.ds(r0, PEER_TOPK), :] = gate
        return carry

    lax.fori_loop(0, PEER_HEADS, head, 0)
    e_ref[...] = et_scr[...].T
    g_ref[...] = gt_scr[...].T


def _peer_route(h2, w_peer_q, sub_keys, tm=256):
    T, D = h2.shape
    nsel = PEER_HEADS * PEER_TOPK
    qd = w_peer_q.shape[1]
    half = sub_keys.shape[-1]
    skb = sub_keys.reshape(PEER_HEADS * 2, N_KEYS, half).astype(BF16)
    return pl.pallas_call(
        functools.partial(_route_kernel, tm=tm),
        grid=(T // tm,),
        in_specs=[pl.BlockSpec((tm, D), lambda i: (i, 0)),
                  _single((qd, D), lambda i: (0, 0)),
                  _single((PEER_HEADS * 2, N_KEYS, half), lambda i: (0, 0, 0))],
        out_specs=[pl.BlockSpec((tm, nsel), lambda i: (i, 0)), pl.BlockSpec((tm, nsel), lambda i: (i, 0))],
        out_shape=[jax.ShapeDtypeStruct((T, nsel), I32), jax.ShapeDtypeStruct((T, nsel), F32)],
        scratch_shapes=[pltpu.VMEM((qd, tm), BF16), pltpu.VMEM((nsel, tm), I32), pltpu.VMEM((nsel, tm), F32)],
        compiler_params=_params(1),
        name="peer_route",
    )(h2, w_peer_q.T.astype(BF16), skb)


TOK = 8


def _gelu_tanh(x):
    return 0.5 * x * (1.0 + jnp.tanh(math.sqrt(2.0 / math.pi) * (x + 0.044715 * (x * x * x))))


def _sublane_sums(ps):
    sub = lax.broadcasted_iota(I32, (SUBLANES, LANES), 0)
    lvl, stride = list(ps), SUBLANES // 2
    while len(lvl) > 1:
        half = len(lvl) // 2
        low = (sub & stride) == 0
        nxt = []
        for n in range(half):
            a, b = lvl[n], lvl[n + half]
            nxt.append(jnp.where(low, a + pltpu.roll(a, SUBLANES - stride, 0), b + pltpu.roll(b, stride, 0)))
        lvl, stride = nxt, stride // 2
    return lvl[0]


def _expert_kernel(idxc_ref, idxn_ref, h2_ref, x1_ref, gate_ref, uv_ref, o_ref, buf, sem, abc_scr, *, nsel):
    i = pl.program_id(0)
    nsteps = pl.num_programs(0)
    rows = TOK * nsel
    dsub = h2_ref.shape[1]

    def issue_token(idx_ref, tok, s, t):
        for n in range(nsel):
            pltpu.make_async_copy(uv_ref.at[idx_ref[tok, n]], buf.at[s, t * nsel + n],
                                  sem.at[s]).start(priority=n % 2)

    def wait_slot(s):
        pltpu.make_async_copy(uv_ref.at[pl.ds(0, rows)], buf.at[s], sem.at[s]).wait()

    @pl.when(i == 0)
    def _():
        for t in range(TOK):
            issue_token(idxc_ref, t, 0, t)

    ones = jnp.ones((SUBLANES, LANES), F32)
    row_id = lax.broadcasted_iota(I32, (TOK, nsel), 0)

    for grp in range(2):
        base = grp * TOK
        wait_slot(grp)
        s8 = jnp.zeros((TOK, nsel), F32)
        for t in range(TOK):
            if grp == 0:
                issue_token(idxc_ref, TOK + t, 1, t)
            else:
                issue_token(idxn_ref, t, 0, t)
            xt = h2_ref[base + t]
            qs = []
            for g in range(nsel // SUBLANES):
                ps = [buf[grp, t * nsel + g * SUBLANES + k, 0:dsub, :] * xt for k in range(SUBLANES)]
                qs.append(_sublane_sums(ps))
            q = jnp.concatenate(qs, axis=0)
            srow = _nt_dot(ones, q, precision=lax.Precision.HIGHEST)
            s8 = jnp.where(row_id == t, srow[0:TOK, :], s8)
        a8 = _gelu_tanh(s8) * gate_ref[base:base + TOK, :]
        a8t = jnp.concatenate([a8, jnp.zeros((nsel - TOK, nsel), F32)], axis=0).T

        for t in range(TOK):
            abc_scr[t] = jnp.broadcast_to(a8t[:, t:t + 1], (nsel, LANES))
            accs = [jnp.zeros((dsub, LANES), F32) for _ in range(4)]
            for n in range(nsel):
                accs[n % 4] = accs[n % 4] + abc_scr[t, n:n + 1, :] * buf[grp, t * nsel + n, dsub:2 * dsub, :]
            o_ref[base + t] = x1_ref[base + t] + ((accs[0] + accs[1]) + (accs[2] + accs[3]))

    @pl.when(i == nsteps - 1)
    def _():
        wait_slot(0)


def _peer_experts(eidx, gates, h2, x1, peer_u, peer_v):
    T, D = x1.shape
    nsel = eidx.shape[1]
    dsub = D // LANES
    ne = peer_u.shape[0]
    uv = jnp.concatenate([peer_u.reshape(ne, dsub, LANES), peer_v.reshape(ne, dsub, LANES)], axis=1)
    tb = 2 * TOK
    nsteps = T // tb
    out = pl.pallas_call(
        functools.partial(_expert_kernel, nsel=nsel),
        grid=(nsteps,),
        in_specs=[pl.BlockSpec((tb, nsel), lambda i: (i, 0), memory_space=pltpu.SMEM),
                  pl.BlockSpec((tb, nsel), lambda i: (jnp.minimum(i + 1, nsteps - 1), 0), memory_space=pltpu.SMEM),
                  pl.BlockSpec((tb, dsub, LANES), lambda i: (i, 0, 0)),
                  pl.BlockSpec((tb, dsub, LANES), lambda i: (i, 0, 0)),
                  pl.BlockSpec((tb, nsel), lambda i: (i, 0)),
                  pl.BlockSpec(memory_space=pl.ANY)],
        out_specs=pl.BlockSpec((tb, dsub, LANES), lambda i: (i, 0, 0)),
        out_shape=jax.ShapeDtypeStruct((T, dsub, LANES), F32),
        scratch_shapes=[pltpu.VMEM((2, TOK * nsel, 2 * dsub, LANES), F32), pltpu.SemaphoreType.DMA((2,)),
                        pltpu.VMEM((TOK, nsel, LANES), F32)],
        compiler_params=_params(1),
        name="peer_experts",
    )(eidx, eidx, h2.reshape(T, dsub, LANES), x1.reshape(T, dsub, LANES), gates, uv)
    return out.reshape(T, D)


def _ple_kernel(x_ref, p_ref, g_ref, wg_ref, wp_ref, o_ref):
    x = x_ref[...]
    h = x * lax.rsqrt(jnp.mean(x * x, axis=-1, keepdims=True) + EPS) * g_ref[...]
    gate = jax.nn.sigmoid(_dot(h.astype(BF16), wg_ref[...]))
    o_ref[...] = x + gate * _dot(p_ref[...].astype(BF16), wp_ref[...])


def _ple(x2, p2, ple_g, w_gate, w_proj, tm=512):
    T, D = x2.shape
    pd = p2.shape[1]
    return pl.pallas_call(
        _ple_kernel,
        grid=(T // tm,),
        in_specs=[pl.BlockSpec((tm, D), lambda i: (i, 0)), pl.BlockSpec((tm, pd), lambda i: (i, 0)),
                  _single((1, D), lambda i: (0, 0)), _single((D, D), lambda i: (0, 0)),
                  _single((pd, D), lambda i: (0, 0))],
        out_specs=pl.BlockSpec((tm, D), lambda i: (i, 0)),
        out_shape=jax.ShapeDtypeStruct((T, D), F32),
        compiler_params=_params(1),
        name="ple",
    )(x2, p2, ple_g.reshape(1, D), w_gate.astype(BF16), w_proj.astype(BF16))


def kernel(x, p, rel_bias, attn_norm_g, w_in, b_gate, q_norm_g, k_norm_g, w_att_out, conv_w, conv_b, conv_ln_g,
           conv_ln_b, w_conv_out, w_out, ffn_norm_g, w_peer_q, peer_sub_keys, peer_u, peer_v, ple_norm_g,
           w_ple_gate, w_ple_proj):
    B, S, D = x.shape
    depth = w_in.shape[0]
    T = B * S
    assert S % 256 == 0 and D % LANES == 0 and T % 512 == 0
    n_near = _num_near_tiles(S)
    bias = _bias_tiles(rel_bias, n_near)
    x2 = x.reshape(T, D)
    for i in range(depth):
        q, k, qi, ki, vt, wit, u, gate = _in_proj(x2, attn_norm_g[i], w_in[i], b_gate[i], q_norm_g[i], k_norm_g[i])
        att = _attention(q, k, qi, ki, vt, wit, bias, B, S, n_near)
        x1, h2 = _mix(att, u, gate, x2, conv_w[i], conv_b[i], conv_ln_g[i], conv_ln_b[i], w_att_out[i],
                      w_conv_out[i], w_out[i], ffn_norm_g[i], S)
        eidx, gates = _peer_route(h2, w_peer_q[i], peer_sub_keys[i])
        x2 = _peer_experts(eidx, gates, h2, x1, peer_u[i], peer_v[i])
        x2 = _ple(x2, p[i].reshape(T, -1), ple_norm_g[i], w_ple_gate[i], w_ple_proj[i])
    return x2.reshape(B, S, D)
```

```python
import functools
import math

import numpy as np
import jax
import jax.numpy as jnp
from jax import lax
from jax.experimental import pallas as pl
from jax.experimental.pallas import tpu as pltpu

CHUNK = 64
ATT_HEADS = 8
ATT_HEAD_DIM = 64
IDX_HEADS = 8
IDX_DIM = 64
TOPK_MAX = 256
REL_BUCKETS = 32
REL_MAX_DIST = 1024
CONV_CH = 512
CONV_WIDTH = 31
N_BRANCH = 2
PEER_HEADS = 8
N_KEYS = 128
PEER_TOPK = 16
EPS = 1e-6

LANES = 128
SUBLANES = 8
VMEM_LIMIT = 56 * 1024 * 1024

QB = 128
INT_MIN = -(2 ** 31)
NEG_BIG = -1e30
LOG2E = math.log2(math.e)

F32 = jnp.float32
BF16 = jnp.bfloat16
I32 = jnp.int32


def _nt_dot(a, b, precision=None):
    return lax.dot_general(a, b, (((1,), (1,)), ((), ())), precision=precision,
                           preferred_element_type=F32)


def _dot(a, b):
    return jnp.dot(a, b, preferred_element_type=F32)


def _single(shape, index_map):
    return pl.BlockSpec(shape, index_map, pipeline_mode=pl.Buffered(1))


def _params(n_grid_dims):
    return pltpu.CompilerParams(dimension_semantics=("arbitrary",) * n_grid_dims,
                                vmem_limit_bytes=VMEM_LIMIT)


def _t5_bucket_np(rel):
    half = REL_BUCKETS // 2
    max_exact = half // 2
    ret = np.where(rel > 0, half, 0)
    n = np.abs(rel)
    nf = np.maximum(n, 1).astype(np.float32)
    large = max_exact + (np.log(nf / np.float32(max_exact)) / np.float32(math.log(REL_MAX_DIST / max_exact))
                         * np.float32(half - max_exact)).astype(np.int32)
    large = np.minimum(large, half - 1)
    return ret + np.where(n < max_exact, n, large)


def _num_near_tiles(seq):
    n = np.arange(1, max(seq, 2 * REL_MAX_DIST) + 1)
    b = _t5_bucket_np(-n)
    sat = REL_BUCKETS // 2 - 1
    unsat = np.nonzero(b != sat)[0]
    n_sat = int(n[unsat[-1]]) + 1 if unsat.size else 1
    return -(-(n_sat + QB - 1) // QB)


def _bias_kernel(rb_ref, o_ref, *, n_near):
    d = pl.program_id(0)
    i = lax.broadcasted_iota(I32, (QB, QB), 0)
    j = lax.broadcasted_iota(I32, (QB, QB), 1)
    rel = i - j - d * QB
    rel = jnp.where(d >= n_near, -8 * REL_MAX_DIST, rel)
    half = REL_BUCKETS // 2
    max_exact = half // 2
    ret = jnp.where(rel > 0, half, 0)
    n = jnp.abs(rel)
    nf = jnp.maximum(n, 1).astype(F32)
    large = max_exact + (jnp.log(nf / max_exact) / math.log(REL_MAX_DIST / max_exact)
                         * (half - max_exact)).astype(I32)
    large = jnp.minimum(large, half - 1)
    bucket = ret + jnp.where(n < max_exact, n, large)
    for h in range(ATT_HEADS):
        acc = jnp.zeros((QB, QB), F32)
        for b in range(REL_BUCKETS):
            acc = jnp.where(bucket == b, rb_ref[b, h], acc)
        o_ref[0, h] = acc * LOG2E


def _bias_tiles(rel_bias, n_near):
    return pl.pallas_call(
        functools.partial(_bias_kernel, n_near=n_near),
        grid=(n_near + 1,),
        in_specs=[pl.BlockSpec(memory_space=pltpu.SMEM)],
        out_specs=pl.BlockSpec((1, ATT_HEADS, QB, QB), lambda d: (d, 0, 0, 0)),
        out_shape=jax.ShapeDtypeStruct((n_near + 1, ATT_HEADS, QB, QB), F32),
        compiler_params=_params(1),
        name="bias_tiles",
    )(rel_bias)


def _inproj_kernel(x_ref, g_ref, wa_ref, wvt_ref, wwit_ref, wglu_ref, wgate_ref, bgate_ref, gq_ref, gk_ref,
                   q_ref, k_ref, qi_ref, ki_ref, vt_ref, wit_ref, u_ref, gate_ref, *, tm):
    x = x_ref[...]
    h = x * lax.rsqrt(jnp.mean(x * x, axis=-1, keepdims=True) + EPS) * g_ref[...]
    hb = h.astype(BF16)
    hp = ATT_HEADS * LANES
    ya = _dot(hb, wa_ref[...])
    for h_i in range(ATT_HEADS):
        sl = slice(h_i * LANES, (h_i + 1) * LANES)
        qh = ya[:, h_i * LANES:(h_i + 1) * LANES]
        ms = jnp.sum(qh * qh, axis=-1, keepdims=True) * (1.0 / ATT_HEAD_DIM)
        q_ref[:, sl] = (qh * lax.rsqrt(ms + EPS) * gq_ref[...]).astype(BF16)
        kh = ya[:, hp + h_i * LANES:hp + (h_i + 1) * LANES]
        ms = jnp.sum(kh * kh, axis=-1, keepdims=True) * (1.0 / ATT_HEAD_DIM)
        k_ref[:, sl] = (kh * lax.rsqrt(ms + EPS) * gk_ref[...]).astype(BF16)
    qi_ref[...] = ya[:, 2 * hp:3 * hp].astype(BF16)
    ki_ref[...] = ya[:, 3 * hp:3 * hp + LANES].astype(BF16)
    vt = _nt_dot(wvt_ref[...], hb).astype(BF16)
    for c in range(tm // QB):
        vt_ref[c] = vt[:, c * QB:(c + 1) * QB]
    wit_ref[...] = _nt_dot(wwit_ref[...], hb)
    glu = _dot(hb, wglu_ref[...])
    u_ref[...] = glu[:, :CONV_CH] * jax.nn.sigmoid(glu[:, CONV_CH:])
    gate_ref[...] = jax.nn.sigmoid(_dot(hb, wgate_ref[...]) + bgate_ref[...]).astype(BF16)


def _pad_heads(w, nh, hd):
    d = w.shape[0]
    w3 = w.reshape(d, nh, hd)
    w3 = jnp.pad(w3, ((0, 0), (0, 0), (0, LANES - hd)))
    return w3.reshape(d, nh * LANES)


def _in_proj(x2, attn_g, w_in, b_gate, q_g, k_g, tm=256):
    T, D = x2.shape
    aw = ATT_HEADS * ATT_HEAD_DIM
    iw = IDX_HEADS * IDX_DIM
    o = 0
    wq = w_in[:, o:o + aw]; o += aw
    wk = w_in[:, o:o + aw]; o += aw
    wv = w_in[:, o:o + aw]; o += aw
    wqi = w_in[:, o:o + iw]; o += iw
    wki = w_in[:, o:o + IDX_DIM]; o += IDX_DIM
    wwi = w_in[:, o:o + IDX_HEADS]; o += IDX_HEADS
    wglu = w_in[:, o:o + 2 * CONV_CH]; o += 2 * CONV_CH
    wgate = w_in[:, o:o + N_BRANCH * D]
    wa = jnp.concatenate([_pad_heads(wq, ATT_HEADS, ATT_HEAD_DIM), _pad_heads(wk, ATT_HEADS, ATT_HEAD_DIM),
                          _pad_heads(wqi, IDX_HEADS, IDX_DIM),
                          jnp.pad(wki, ((0, 0), (0, LANES - IDX_DIM)))], axis=1).astype(BF16)
    na = wa.shape[1]
    hp = ATT_HEADS * LANES
    pad_g = lambda g, s: jnp.pad(g * s, (0, LANES - ATT_HEAD_DIM)).reshape(1, LANES)
    gq = pad_g(q_g, ATT_HEAD_DIM ** -0.5 * LOG2E)
    gk = pad_g(k_g, 1.0)
    const = lambda *shape: _single(shape, lambda i: (0,) * len(shape))
    outs = pl.pallas_call(
        functools.partial(_inproj_kernel, tm=tm),
        grid=(T // tm,),
        in_specs=[pl.BlockSpec((tm, D), lambda i: (i, 0)), const(1, D), const(D, na), const(aw, D),
                  const(IDX_HEADS, D), const(D, 2 * CONV_CH), const(D, N_BRANCH * D), const(1, N_BRANCH * D),
                  const(1, LANES), const(1, LANES)],
        out_specs=[pl.BlockSpec((tm, hp), lambda i: (i, 0)), pl.BlockSpec((tm, hp), lambda i: (i, 0)),
                   pl.BlockSpec((tm, hp), lambda i: (i, 0)), pl.BlockSpec((tm, LANES), lambda i: (i, 0)),
                   pl.BlockSpec((tm // QB, aw, QB), lambda i: (i, 0, 0)),
                   pl.BlockSpec((IDX_HEADS, tm), lambda i: (0, i)),
                   pl.BlockSpec((tm, CONV_CH), lambda i: (i, 0)),
                   pl.BlockSpec((tm, N_BRANCH * D), lambda i: (i, 0))],
        out_shape=[jax.ShapeDtypeStruct((T, hp), BF16), jax.ShapeDtypeStruct((T, hp), BF16),
                   jax.ShapeDtypeStruct((T, hp), BF16), jax.ShapeDtypeStruct((T, LANES), BF16),
                   jax.ShapeDtypeStruct((T // QB, aw, QB), BF16),
                   jax.ShapeDtypeStruct((IDX_HEADS, T), F32),
                   jax.ShapeDtypeStruct((T, CONV_CH), F32),
                   jax.ShapeDtypeStruct((T, N_BRANCH * D), BF16)],
        compiler_params=_params(1),
        name="in_proj",
    )(x2, attn_g.reshape(1, D), wa, wv.T.astype(BF16), wwi.T.astype(BF16), wglu.astype(BF16),
      wgate.astype(BF16), b_gate.reshape(1, N_BRANCH * D), gq, gk)
    return outs


def _attn_kernel(q_ref, qi_ref, wit_ref, k_ref, ki_ref, vt_ref, bias_ref, o_ref, key_scr, att_scr, *scr,
                 seq, n_near, topk):
    qb = pl.program_id(1)
    npair = (qb + 2) // 2
    lane_t = lax.broadcasted_iota(I32, (1, QB), 1) + qb * QB
    qchunk = lane_t // CHUNK
    sub = lax.broadcasted_iota(I32, (QB, QB), 0)
    idx_scale = (IDX_DIM ** -0.5) * (IDX_HEADS ** -0.5)
    wrow = wit_ref[...] * idx_scale

    def pair_loop(body, init):
        def pair(j2, carry):
            return body(2 * j2 + 1, body(2 * j2, carry))
        return lax.fori_loop(0, npair, pair, init)

    qi_all = jnp.concatenate([qi_ref[:, h * LANES:(h + 1) * LANES] for h in range(IDX_HEADS)], axis=0)

    def score_block(j, carry):
        d = _nt_dot(ki_ref[0, j], qi_all)
        acc = jnp.zeros((QB, QB), F32)
        for h in range(IDX_HEADS):
            acc = acc + jnp.maximum(d[:, h * QB:(h + 1) * QB], 0.0) * wrow[h:h + 1, :]
        bits = pltpu.bitcast(acc, I32)
        skey = bits ^ ((bits >> 31) & 0x7FFFFFFF)
        visible = (sub + j * QB) // CHUNK <= qchunk
        key_scr[j] = jnp.where(visible, skey, INT_MIN)
        return carry

    pair_loop(score_block, 0)

    nvis = (qchunk + 1) * CHUNK
    kk = jnp.minimum(topk, nvis)

    def count(pred):
        def body(j, acc8):
            m = jnp.where(pred(key_scr[j], j), 1, 0)
            return acc8 + m.reshape(QB // SUBLANES, SUBLANES, QB).sum(axis=0)
        return pair_loop(body, jnp.zeros((SUBLANES, QB), I32)).sum(axis=0, keepdims=True)

    def bit_step(i, carry):
        ans, cnt = carry
        cand = ans + lax.shift_left(jnp.int32(1), 31 - i)
        c = count(lambda blk, j: blk >= cand)
        take = c >= kk
        return jnp.where(take, cand, ans), jnp.where(take, c, cnt)

    thr, cnt = lax.fori_loop(0, 32, bit_step,
                             (jnp.full((1, QB), INT_MIN, I32), jnp.full((1, QB), 0, I32) + 2 * npair * QB))

    @pl.when(jnp.max(cnt - kk) > 0)
    def _():
        n_gt = count(lambda blk, j: blk > thr)
        need = kk - n_gt

        def idx_step(i, jbound):
            cand = jbound + lax.shift_left(jnp.int32(1), (seq.bit_length() - 1) - i)
            c = count(lambda blk, j: (blk == thr) & (sub + j * QB < cand))
            return jnp.where(c <= need, cand, jbound)

        jbound = lax.fori_loop(0, seq.bit_length(), idx_step, jnp.zeros((1, QB), I32))

        def drop(j, carry):
            blk = key_scr[j]
            key_scr[j] = jnp.where((blk == thr) & (sub + j * QB >= jbound), INT_MIN, blk)
            return carry

        pair_loop(drop, 0)

    acc_refs, s_even, s_odd = scr[:ATT_HEADS], scr[ATT_HEADS], scr[ATT_HEADS + 1]
    for acc_ref in acc_refs:
        acc_ref[...] = jnp.zeros(acc_ref.shape, F32)
    head_row = lax.broadcasted_iota(I32, (ATT_HEADS, QB), 0)
    last_block = k_ref.shape[1] - 1

    def qk(j, s_ref):
        jc = jnp.minimum(j, last_block)
        for h in range(ATT_HEADS):
            s_ref[h] = _nt_dot(k_ref[0, jc, :, h * LANES:(h + 1) * LANES], q_ref[:, h * LANES:(h + 1) * LANES])

    def softmax_pv(j, s_ref, carry):
        m_all, l_all = carry
        masked = jnp.where(key_scr[j] >= thr, 0.0, -jnp.inf)
        tile = jnp.clip(qb - j, 0, n_near)
        for h in range(ATT_HEADS):
            rows = slice(h * ATT_HEAD_DIM, (h + 1) * ATT_HEAD_DIM)
            s = s_ref[h] + bias_ref[tile, h] + masked
            m = m_all[h:h + 1, :]
            m_new = jnp.maximum(m, jnp.max(s, axis=0, keepdims=True))
            p = jnp.exp2(s - m_new)
            alpha = jnp.exp2(m - m_new)
            l_new = alpha * l_all[h:h + 1, :] + jnp.sum(p, axis=0, keepdims=True)
            m_all = jnp.where(head_row == h, m_new, m_all)
            l_all = jnp.where(head_row == h, l_new, l_all)
            acc_refs[h][...] = alpha * acc_refs[h][...] + _dot(vt_ref[0, j, rows, :], p.astype(BF16))
        return m_all, l_all

    def att_pair(j2, carry):
        qk(2 * j2 + 1, s_odd)
        carry = softmax_pv(2 * j2, s_even, carry)
        qk(2 * j2 + 2, s_even)
        return softmax_pv(2 * j2 + 1, s_odd, carry)

    qk(0, s_even)
    _, l_all = lax.fori_loop(0, npair, att_pair,
                             (jnp.full((ATT_HEADS, QB), NEG_BIG, F32), jnp.zeros((ATT_HEADS, QB), F32)))
    for h in range(ATT_HEADS):
        rows = slice(h * ATT_HEAD_DIM, (h + 1) * ATT_HEAD_DIM)
        att_scr[rows, :] = acc_refs[h][...] / l_all[h:h + 1, :]
    o_ref[...] = att_scr[...].T.astype(o_ref.dtype)


def _attention(q, k, qi, ki, vt, wit, bias, B, S, n_near):
    T = B * S
    nqb = S // QB
    hp = ATT_HEADS * LANES
    aw = ATT_HEADS * ATT_HEAD_DIM
    topk = min(TOPK_MAX, S // 4)
    k4 = k.reshape(B, nqb, QB, hp)
    ki4 = ki.reshape(B, nqb, QB, LANES)
    vt4 = vt.reshape(B, nqb, aw, QB)
    return pl.pallas_call(
        functools.partial(_attn_kernel, seq=S, n_near=n_near, topk=topk),
        grid=(B, nqb),
        in_specs=[pl.BlockSpec((QB, hp), lambda b, i: (b * nqb + i, 0)),
                  pl.BlockSpec((QB, hp), lambda b, i: (b * nqb + i, 0)),
                  pl.BlockSpec((IDX_HEADS, QB), lambda b, i: (0, b * nqb + i)),
                  _single((1, nqb, QB, hp), lambda b, i: (b, 0, 0, 0)),
                  _single((1, nqb, QB, LANES), lambda b, i: (b, 0, 0, 0)),
                  _single((1, nqb, aw, QB), lambda b, i: (b, 0, 0, 0)),
                  _single((n_near + 1, ATT_HEADS, QB, QB), lambda b, i: (0, 0, 0, 0))],
        out_specs=pl.BlockSpec((QB, aw), lambda b, i: (b * nqb + i, 0)),
        out_shape=jax.ShapeDtypeStruct((T, aw), BF16),
        scratch_shapes=[pltpu.VMEM((nqb, QB, QB), I32), pltpu.VMEM((aw, QB), F32)]
        + [pltpu.VMEM((ATT_HEAD_DIM, QB), F32) for _ in range(ATT_HEADS)]
        + [pltpu.VMEM((ATT_HEADS, QB, QB), F32) for _ in range(2)],
        compiler_params=_params(2),
        name="dsa_attention",
    )(q, qi, wit, k4, ki4, vt4, bias)


HALO = 32


def _mix_kernel(att_ref, u_ref, halo_ref, gate_ref, x_ref, cw_ref, cb_ref, lng_ref, lnb_ref, wao_ref, wco_ref,
                wout_ref, gffn_ref, x1_ref, h2_ref, ext_scr, *, tm, tiles_per_seq):
    i = pl.program_id(0)
    first = (i % tiles_per_seq) == 0
    ext_scr[0:HALO, :] = jnp.where(first, 0.0, halo_ref[...])
    ext_scr[HALO:HALO + tm, :] = u_ref[...]
    y = jnp.zeros((tm, CONV_CH), F32)
    for j in range(CONV_WIDTH):
        y = y + cw_ref[j:j + 1, :] * ext_scr[pl.ds(HALO - (CONV_WIDTH - 1) + j, tm), :]
    y = y + cb_ref[...]
    mu = jnp.mean(y, axis=-1, keepdims=True)
    yc = y - mu
    yn = yc * lax.rsqrt(jnp.mean(yc * yc, axis=-1, keepdims=True) + EPS) * lng_ref[...] + lnb_ref[...]
    z = yn * jax.nn.sigmoid(yn)
    y_conv = _dot(z.astype(BF16), wco_ref[...])
    y_att = _dot(att_ref[...], wao_ref[...])
    d = y_att.shape[1]
    g = gate_ref[...]
    mixed = g[:, :d].astype(F32) * y_att + g[:, d:].astype(F32) * y_conv
    x1 = x_ref[...] + _dot(mixed.astype(BF16), wout_ref[...])
    x1_ref[...] = x1
    h2_ref[...] = x1 * lax.rsqrt(jnp.mean(x1 * x1, axis=-1, keepdims=True) + EPS) * gffn_ref[...]


def _mix(att, u, gate, x2, conv_w, conv_b, ln_g, ln_b, w_att_out, w_conv_out, w_out, ffn_g, S, tm=256):
    T, D = x2.shape
    aw = att.shape[1]
    const = lambda *shape: _single(shape, lambda i: (0,) * len(shape))
    hb = tm // HALO
    return pl.pallas_call(
        functools.partial(_mix_kernel, tm=tm, tiles_per_seq=S // tm),
        grid=(T // tm,),
        in_specs=[pl.BlockSpec((tm, aw), lambda i: (i, 0)),
                  pl.BlockSpec((tm, CONV_CH), lambda i: (i, 0)),
                  pl.BlockSpec((HALO, CONV_CH), lambda i: (jnp.maximum(i * hb - 1, 0), 0)),
                  pl.BlockSpec((tm, N_BRANCH * D), lambda i: (i, 0)),
                  pl.BlockSpec((tm, D), lambda i: (i, 0)),
                  const(CONV_WIDTH, CONV_CH), const(1, CONV_CH), const(1, CONV_CH), const(1, CONV_CH),
                  const(aw, D), const(CONV_CH, D), const(D, D), const(1, D)],
        out_specs=[pl.BlockSpec((tm, D), lambda i: (i, 0)), pl.BlockSpec((tm, D), lambda i: (i, 0))],
        out_shape=[jax.ShapeDtypeStruct((T, D), F32), jax.ShapeDtypeStruct((T, D), F32)],
        scratch_shapes=[pltpu.VMEM((HALO + tm, CONV_CH), F32)],
        compiler_params=_params(1),
        name="mix_out_proj",
    )(att, u, u, gate, x2, conv_w.reshape(CONV_WIDTH, CONV_CH), conv_b.reshape(1, CONV_CH),
      ln_g.reshape(1, CONV_CH), ln_b.reshape(1, CONV_CH), w_att_out.astype(BF16), w_conv_out.astype(BF16),
      w_out.astype(BF16), ffn_g.reshape(1, D))


def _top_rows(sc, k, payload=None):
    rows = sc.shape[0]
    iota = lax.broadcasted_iota(I32, sc.shape, 0)
    out_row = lax.broadcasted_iota(I32, (k, sc.shape[1]), 0)
    vals = jnp.zeros((k, sc.shape[1]), F32)
    idxs = jnp.zeros((k, sc.shape[1]), I32)
    for r in range(k):
        m = jnp.max(sc, axis=0, keepdims=True)
        idx = jnp.min(jnp.where(sc == m, iota, rows), axis=0, keepdims=True)
        hit = iota == idx
        rec = idx if payload is None else jnp.max(jnp.where(hit, payload, -1), axis=0, keepdims=True)
        vals = jnp.where(out_row == r, m, vals)
        idxs = jnp.where(out_row == r, rec, idxs)
        sc = jnp.where(hit, -jnp.inf, sc)
    return vals, idxs


def _route_kernel(h2_ref, wq_ref, sk_ref, e_ref, g_ref, qt_scr, et_scr, gt_scr, *, tm):
    half = N_KEYS
    qt_scr[...] = _nt_dot(wq_ref[...], h2_ref[...].astype(BF16)).astype(BF16)

    def head(h, carry):
        tops = []
        for c in range(2):
            row0 = pl.multiple_of((h * 2 + c) * half, half)
            sc = _dot(sk_ref[h * 2 + c], qt_scr[pl.ds(row0, half), :])
            tops.append(_top_rows(sc, PEER_TOPK))
        (a, ia), (b, ib) = tops
        k = PEER_TOPK
        g = SUBLANES
        assert (k // 2) % g == 0 and k % g == 0
        row = lax.broadcasted_iota(I32, (g, a.shape[1]), 0)
        cand_parts, cidx_parts = [], []

        def add(av, iav, bv, ibv, valid_rows):
            s = av + bv
            if valid_rows < g:
                s = jnp.where(row < valid_rows, s, -jnp.inf)
            cand_parts.append(s)
            cidx_parts.append(iav * N_KEYS + ibv)

        for i in range(k):
            nj = k // (i + 1)
            if nj >= g:
                for j0 in range(0, nj, g):
                    add(a[i:i + 1, :], ia[i:i + 1, :], b[j0:j0 + g, :], ib[j0:j0 + g, :], g)
            elif nj > 1:
                add(a[i:i + 1, :], ia[i:i + 1, :], b[0:g, :], ib[0:g, :], nj)
            elif i % g == 0:
                add(a[i:i + g, :], ia[i:i + g, :], b[0:1, :], ib[0:1, :], g)
        cand = jnp.concatenate(cand_parts, axis=0)
        cidx = jnp.concatenate(cidx_parts, axis=0)
        ts, te = _top_rows(cand, PEER_TOPK, payload=cidx)
        ex = jnp.exp(ts - ts[0:1, :])
        gate = ex / jnp.sum(ex, axis=0, keepdims=True)
        r0 = pl.multiple_of(h * PEER_TOPK, PEER_TOPK)
        et_scr[pl.ds(r0, PEER_TOPK), :] = te
        gt_scr[pl.ds(r0, PEER_TOPK), :] = gate
        return carry

    lax.fori_loop(0, PEER_HEADS, head, 0)
    e_ref[...] = et_scr[...].T
    g_ref[...] = gt_scr[...].T


def _peer_route(h2, w_peer_q, sub_keys, tm=256):
    T, D = h2.shape
    nsel = PEER_HEADS * PEER_TOPK
    qd = w_peer_q.shape[1]
    half = sub_keys.shape[-1]
    skb = sub_keys.reshape(PEER_HEADS * 2, N_KEYS, half).astype(BF16)
    return pl.pallas_call(
        functools.partial(_route_kernel, tm=tm),
        grid=(T // tm,),
        in_specs=[pl.BlockSpec((tm, D), lambda i: (i, 0)),
                  _single((qd, D), lambda i: (0, 0)),
                  _single((PEER_HEADS * 2, N_KEYS, half), lambda i: (0, 0, 0))],
        out_specs=[pl.BlockSpec((tm, nsel), lambda i: (i, 0)), pl.BlockSpec((tm, nsel), lambda i: (i, 0))],
        out_shape=[jax.ShapeDtypeStruct((T, nsel), I32), jax.ShapeDtypeStruct((T, nsel), F32)],
        scratch_shapes=[pltpu.VMEM((qd, tm), BF16), pltpu.VMEM((nsel, tm), I32), pltpu.VMEM((nsel, tm), F32)],
        compiler_params=_params(1),
        name="peer_route",
    )(h2, w_peer_q.T.astype(BF16), skb)


TOK = 8


def _gelu_tanh(x):
    return 0.5 * x * (1.0 + jnp.tanh(math.sqrt(2.0 / math.pi) * (x + 0.044715 * (x * x * x))))


def _sublane_sums(ps):
    sub = lax.broadcasted_iota(I32, (SUBLANES, LANES), 0)
    lvl, stride = list(ps), SUBLANES // 2
    while len(lvl) > 1:
        half = len(lvl) // 2
        low = (sub & stride) == 0
        nxt = []
        for n in range(half):
            a, b = lvl[n], lvl[n + half]
            nxt.append(jnp.where(low, a + pltpu.roll(a, SUBLANES - stride, 0), b + pltpu.roll(b, stride, 0)))
        lvl, stride = nxt, stride // 2
    return lvl[0]


def _pack_uv(u, v):
    ne, d = u.shape
    hi = lax.bitcast_convert_type(u.astype(BF16), jnp.uint16).astype(jnp.uint32)
    lo = lax.bitcast_convert_type(v.astype(BF16), jnp.uint16).astype(jnp.uint32)
    return ((hi << 16) | lo).reshape(ne, d // LANES, LANES)


def _u_of(word):
    return pltpu.bitcast(word & jnp.uint32(0xFFFF0000), F32)


def _v_of(word):
    return pltpu.bitcast(word << 16, F32)


def _expert_kernel(idxc_ref, idxn_ref, h2_ref, x1_ref, gate_ref, uv_ref, o_ref, buf, sem, abc_scr, *, nsel):
    i = pl.program_id(0)
    nsteps = pl.num_programs(0)
    rows = TOK * nsel
    dsub = h2_ref.shape[1]

    def issue_token(idx_ref, tok, s, t):
        for n in range(nsel):
            pltpu.make_async_copy(uv_ref.at[idx_ref[tok, n]], buf.at[s, t * nsel + n],
                                  sem.at[s]).start(priority=n % 2)

    def wait_slot(s):
        pltpu.make_async_copy(uv_ref.at[pl.ds(0, rows)], buf.at[s], sem.at[s]).wait()

    @pl.when(i == 0)
    def _():
        for t in range(TOK):
            issue_token(idxc_ref, t, 0, t)

    ones = jnp.ones((SUBLANES, LANES), F32)
    row_id = lax.broadcasted_iota(I32, (TOK, nsel), 0)

    for grp in range(2):
        base = grp * TOK
        wait_slot(grp)
        s8 = jnp.zeros((TOK, nsel), F32)
        for t in range(TOK):
            if grp == 0:
                issue_token(idxc_ref, TOK + t, 1, t)
            else:
                issue_token(idxn_ref, t, 0, t)
            xt = h2_ref[base + t]
            qs = []
            for g in range(nsel // SUBLANES):
                ps = [_u_of(buf[grp, t * nsel + g * SUBLANES + k]) * xt for k in range(SUBLANES)]
                qs.append(_sublane_sums(ps))
            q = jnp.concatenate(qs, axis=0)
            srow = _nt_dot(ones, q, precision=lax.Precision.HIGHEST)
            s8 = jnp.where(row_id == t, srow[0:TOK, :], s8)
        a8 = _gelu_tanh(s8) * gate_ref[base:base + TOK, :]
        a8t = jnp.concatenate([a8, jnp.zeros((nsel - TOK, nsel), F32)], axis=0).T

        for t in range(TOK):
            abc_scr[t] = jnp.broadcast_to(a8t[:, t:t + 1], (nsel, LANES))
            accs = [jnp.zeros((dsub, LANES), F32) for _ in range(4)]
            for n in range(nsel):
                accs[n % 4] = accs[n % 4] + abc_scr[t, n:n + 1, :] * _v_of(buf[grp, t * nsel + n])
            o_ref[base + t] = x1_ref[base + t] + ((accs[0] + accs[1]) + (accs[2] + accs[3]))

    @pl.when(i == nsteps - 1)
    def _():
        wait_slot(0)


def _peer_experts(eidx, gates, h2, x1, peer_u, peer_v):
    T, D = x1.shape
    nsel = eidx.shape[1]
    dsub = D // LANES
    ne = peer_u.shape[0]
    uv = _pack_uv(peer_u, peer_v)
    tb = 2 * TOK
    nsteps = T // tb
    out = pl.pallas_call(
        functools.partial(_expert_kernel, nsel=nsel),
        grid=(nsteps,),
        in_specs=[pl.BlockSpec((tb, nsel), lambda i: (i, 0), memory_space=pltpu.SMEM),
                  pl.BlockSpec((tb, nsel), lambda i: (jnp.minimum(i + 1, nsteps - 1), 0), memory_space=pltpu.SMEM),
                  pl.BlockSpec((tb, dsub, LANES), lambda i: (i, 0, 0)),
                  pl.BlockSpec((tb, dsub, LANES), lambda i: (i, 0, 0)),
                  pl.BlockSpec((tb, nsel), lambda i: (i, 0)),
                  pl.BlockSpec(memory_space=pl.ANY)],
        out_specs=pl.BlockSpec((tb, dsub, LANES), lambda i: (i, 0, 0)),
        out_shape=jax.ShapeDtypeStruct((T, dsub, LANES), F32),
        scratch_shapes=[pltpu.VMEM((2, TOK * nsel, dsub, LANES), jnp.uint32), pltpu.SemaphoreType.DMA((2,)),
                        pltpu.VMEM((TOK, nsel, LANES), F32)],
        compiler_params=_params(1),
        name="peer_experts",
    )(eidx, eidx, h2.reshape(T, dsub, LANES), x1.reshape(T, dsub, LANES), gates, uv)
    return out.reshape(T, D)


def _ple_kernel(x_ref, p_ref, g_ref, wg_ref, wp_ref, o_ref):
    x = x_ref[...]
    h = x * lax.rsqrt(jnp.mean(x * x, axis=-1, keepdims=True) + EPS) * g_ref[...]
    gate = jax.nn.sigmoid(_dot(h.astype(BF16), wg_ref[...]))
    o_ref[...] = x + gate * _dot(p_ref[...].astype(BF16), wp_ref[...])


def _ple(x2, p2, ple_g, w_gate, w_proj, tm=512):
    T, D = x2.shape
    pd = p2.shape[1]
    return pl.pallas_call(
        _ple_kernel,
        grid=(T // tm,),
        in_specs=[pl.BlockSpec((tm, D), lambda i: (i, 0)), pl.BlockSpec((tm, pd), lambda i: (i, 0)),
                  _single((1, D), lambda i: (0, 0)), _single((D, D), lambda i: (0, 0)),
                  _single((pd, D), lambda i: (0, 0))],
        out_specs=pl.BlockSpec((tm, D), lambda i: (i, 0)),
        out_shape=jax.ShapeDtypeStruct((T, D), F32),
        compiler_params=_params(1),
        name="ple",
    )(x2, p2, ple_g.reshape(1, D), w_gate.astype(BF16), w_proj.astype(BF16))


def kernel(x, p, rel_bias, attn_norm_g, w_in, b_gate, q_norm_g, k_norm_g, w_att_out, conv_w, conv_b, conv_ln_g,
           conv_ln_b, w_conv_out, w_out, ffn_norm_g, w_peer_q, peer_sub_keys, peer_u, peer_v, ple_norm_g,
           w_ple_gate, w_ple_proj):
    B, S, D = x.shape
    depth = w_in.shape[0]
    T = B * S
    assert S % 256 == 0 and D % LANES == 0 and T % 512 == 0
    n_near = _num_near_tiles(S)
    bias = _bias_tiles(rel_bias, n_near)
    x2 = x.reshape(T, D)
    for i in range(depth):
        q, k, qi, ki, vt, wit, u, gate = _in_proj(x2, attn_norm_g[i], w_in[i], b_gate[i], q_norm_g[i], k_norm_g[i])
        att = _attention(q, k, qi, ki, vt, wit, bias, B, S, n_near)
        x1, h2 = _mix(att, u, gate, x2, conv_w[i], conv_b[i], conv_ln_g[i], conv_ln_b[i], w_att_out[i],
                      w_conv_out[i], w_out[i], ffn_norm_g[i], S)
        eidx, gates = _peer_route(h2, w_peer_q[i], peer_sub_keys[i])
        x2 = _peer_experts(eidx, gates, h2, x1, peer_u[i], peer_v[i])
        x2 = _ple(x2, p[i].reshape(T, -1), ple_norm_g[i], w_ple_gate[i], w_ple_proj[i])
    return x2.reshape(B, S, D)
```

```python
import functools
import math

import numpy as np
import jax
import jax.numpy as jnp
from jax import lax
from jax.experimental import pallas as pl
from jax.experimental.pallas import tpu as pltpu

CHUNK = 64
ATT_HEADS = 8
ATT_HEAD_DIM = 64
IDX_HEADS = 8
IDX_DIM = 64
TOPK_MAX = 256
REL_BUCKETS = 32
REL_MAX_DIST = 1024
CONV_CH = 512
CONV_WIDTH = 31
N_BRANCH = 2
PEER_HEADS = 8
N_KEYS = 128
PEER_TOPK = 16
EPS = 1e-6

LANES = 128
SUBLANES = 8
VMEM_LIMIT = 56 * 1024 * 1024

QB = 128
KPAIR = 2 * QB
KQUAD = 4 * QB
INT_MIN = -(2 ** 31)
NEG_BIG = -1e30
LOG2E = math.log2(math.e)

F32 = jnp.float32
BF16 = jnp.bfloat16
I32 = jnp.int32


def _nt_dot(a, b, precision=None):
    return lax.dot_general(a, b, (((1,), (1,)), ((), ())), precision=precision,
                           preferred_element_type=F32)


def _dot(a, b):
    return jnp.dot(a, b, preferred_element_type=F32)


def _single(shape, index_map):
    return pl.BlockSpec(shape, index_map, pipeline_mode=pl.Buffered(1))


def _params(n_grid_dims):
    return pltpu.CompilerParams(dimension_semantics=("arbitrary",) * n_grid_dims,
                                vmem_limit_bytes=VMEM_LIMIT)


def _t5_bucket_np(rel):
    half = REL_BUCKETS // 2
    max_exact = half // 2
    ret = np.where(rel > 0, half, 0)
    n = np.abs(rel)
    nf = np.maximum(n, 1).astype(np.float32)
    large = max_exact + (np.log(nf / np.float32(max_exact)) / np.float32(math.log(REL_MAX_DIST / max_exact))
                         * np.float32(half - max_exact)).astype(np.int32)
    large = np.minimum(large, half - 1)
    return ret + np.where(n < max_exact, n, large)


def _num_near_tiles(seq):
    n = np.arange(1, max(seq, 2 * REL_MAX_DIST) + 1)
    b = _t5_bucket_np(-n)
    sat = REL_BUCKETS // 2 - 1
    unsat = np.nonzero(b != sat)[0]
    n_sat = int(n[unsat[-1]]) + 1 if unsat.size else 1
    return -(-(n_sat + QB - 1) // QB)


def _bias_kernel(rb_ref, o_ref, *, n_near):
    d = pl.program_id(0)
    i = lax.broadcasted_iota(I32, (QB, QB), 0)
    j = lax.broadcasted_iota(I32, (QB, QB), 1)
    rel = i - j - d * QB
    rel = jnp.where(d >= n_near, -8 * REL_MAX_DIST, rel)
    half = REL_BUCKETS // 2
    max_exact = half // 2
    ret = jnp.where(rel > 0, half, 0)
    n = jnp.abs(rel)
    nf = jnp.maximum(n, 1).astype(F32)
    large = max_exact + (jnp.log(nf / max_exact) / math.log(REL_MAX_DIST / max_exact)
                         * (half - max_exact)).astype(I32)
    large = jnp.minimum(large, half - 1)
    bucket = ret + jnp.where(n < max_exact, n, large)
    for h in range(ATT_HEADS):
        acc = jnp.zeros((QB, QB), F32)
        for b in range(REL_BUCKETS):
            acc = jnp.where(bucket == b, rb_ref[b, h], acc)
        o_ref[0, h] = acc * LOG2E


def _bias_tiles(rel_bias, n_near):
    return pl.pallas_call(
        functools.partial(_bias_kernel, n_near=n_near),
        grid=(n_near + 1,),
        in_specs=[pl.BlockSpec(memory_space=pltpu.SMEM)],
        out_specs=pl.BlockSpec((1, ATT_HEADS, QB, QB), lambda d: (d, 0, 0, 0)),
        out_shape=jax.ShapeDtypeStruct((n_near + 1, ATT_HEADS, QB, QB), F32),
        compiler_params=_params(1),
        name="bias_tiles",
    )(rel_bias)


def _inproj_kernel(x_ref, g_ref, wa_ref, wvt_ref, wwit_ref, wglu_ref, wgate_ref, bgate_ref, gq_ref, gk_ref,
                   q_ref, k_ref, qi_ref, ki_ref, vt_ref, wit_ref, u_ref, gate_ref, *, tm):
    x = x_ref[...]
    h = x * lax.rsqrt(jnp.mean(x * x, axis=-1, keepdims=True) + EPS) * g_ref[...]
    hb = h.astype(BF16)
    hp = ATT_HEADS * LANES
    ya = _dot(hb, wa_ref[...])
    for h_i in range(ATT_HEADS):
        sl = slice(h_i * LANES, (h_i + 1) * LANES)
        qh = ya[:, h_i * LANES:(h_i + 1) * LANES]
        ms = jnp.sum(qh * qh, axis=-1, keepdims=True) * (1.0 / ATT_HEAD_DIM)
        q_ref[:, sl] = (qh * lax.rsqrt(ms + EPS) * gq_ref[...]).astype(BF16)
        kh = ya[:, hp + h_i * LANES:hp + (h_i + 1) * LANES]
        ms = jnp.sum(kh * kh, axis=-1, keepdims=True) * (1.0 / ATT_HEAD_DIM)
        k_ref[:, sl] = (kh * lax.rsqrt(ms + EPS) * gk_ref[...]).astype(BF16)
    qi_ref[...] = ya[:, 2 * hp:3 * hp].astype(BF16)
    ki_ref[...] = ya[:, 3 * hp:3 * hp + LANES].astype(BF16)
    vt = _nt_dot(wvt_ref[...], hb).astype(BF16)
    for c in range(tm // KPAIR):
        vt_ref[c] = vt[:, c * KPAIR:(c + 1) * KPAIR]
    wit_ref[...] = _nt_dot(wwit_ref[...], hb)
    glu = _dot(hb, wglu_ref[...])
    u_ref[...] = glu[:, :CONV_CH] * jax.nn.sigmoid(glu[:, CONV_CH:])
    gate_ref[...] = jax.nn.sigmoid(_dot(hb, wgate_ref[...]) + bgate_ref[...]).astype(BF16)


def _pad_heads(w, nh, hd):
    d = w.shape[0]
    w3 = w.reshape(d, nh, hd)
    w3 = jnp.pad(w3, ((0, 0), (0, 0), (0, LANES - hd)))
    return w3.reshape(d, nh * LANES)


def _in_proj(x2, attn_g, w_in, b_gate, q_g, k_g, tm=256):
    T, D = x2.shape
    aw = ATT_HEADS * ATT_HEAD_DIM
    iw = IDX_HEADS * IDX_DIM
    o = 0
    wq = w_in[:, o:o + aw]; o += aw
    wk = w_in[:, o:o + aw]; o += aw
    wv = w_in[:, o:o + aw]; o += aw
    wqi = w_in[:, o:o + iw]; o += iw
    wki = w_in[:, o:o + IDX_DIM]; o += IDX_DIM
    wwi = w_in[:, o:o + IDX_HEADS]; o += IDX_HEADS
    wglu = w_in[:, o:o + 2 * CONV_CH]; o += 2 * CONV_CH
    wgate = w_in[:, o:o + N_BRANCH * D]
    wa = jnp.concatenate([_pad_heads(wq, ATT_HEADS, ATT_HEAD_DIM), _pad_heads(wk, ATT_HEADS, ATT_HEAD_DIM),
                          _pad_heads(wqi, IDX_HEADS, IDX_DIM),
                          jnp.pad(wki, ((0, 0), (0, LANES - IDX_DIM)))], axis=1).astype(BF16)
    na = wa.shape[1]
    hp = ATT_HEADS * LANES
    pad_g = lambda g, s: jnp.pad(g * s, (0, LANES - ATT_HEAD_DIM)).reshape(1, LANES)
    gq = pad_g(q_g, ATT_HEAD_DIM ** -0.5 * LOG2E)
    gk = pad_g(k_g, 1.0)
    const = lambda *shape: _single(shape, lambda i: (0,) * len(shape))
    outs = pl.pallas_call(
        functools.partial(_inproj_kernel, tm=tm),
        grid=(T // tm,),
        in_specs=[pl.BlockSpec((tm, D), lambda i: (i, 0)), const(1, D), const(D, na), const(aw, D),
                  const(IDX_HEADS, D), const(D, 2 * CONV_CH), const(D, N_BRANCH * D), const(1, N_BRANCH * D),
                  const(1, LANES), const(1, LANES)],
        out_specs=[pl.BlockSpec((tm, hp), lambda i: (i, 0)), pl.BlockSpec((tm, hp), lambda i: (i, 0)),
                   pl.BlockSpec((tm, hp), lambda i: (i, 0)), pl.BlockSpec((tm, LANES), lambda i: (i, 0)),
                   pl.BlockSpec((tm // KPAIR, aw, KPAIR), lambda i: (i, 0, 0)),
                   pl.BlockSpec((IDX_HEADS, tm), lambda i: (0, i)),
                   pl.BlockSpec((tm, CONV_CH), lambda i: (i, 0)),
                   pl.BlockSpec((tm, N_BRANCH * D), lambda i: (i, 0))],
        out_shape=[jax.ShapeDtypeStruct((T, hp), BF16), jax.ShapeDtypeStruct((T, hp), BF16),
                   jax.ShapeDtypeStruct((T, hp), BF16), jax.ShapeDtypeStruct((T, LANES), BF16),
                   jax.ShapeDtypeStruct((T // KPAIR, aw, KPAIR), BF16),
                   jax.ShapeDtypeStruct((IDX_HEADS, T), F32),
                   jax.ShapeDtypeStruct((T, CONV_CH), F32),
                   jax.ShapeDtypeStruct((T, N_BRANCH * D), BF16)],
        compiler_params=_params(1),
        name="in_proj",
    )(x2, attn_g.reshape(1, D), wa, wv.T.astype(BF16), wwi.T.astype(BF16), wglu.astype(BF16),
      wgate.astype(BF16), b_gate.reshape(1, N_BRANCH * D), gq, gk)
    return outs


def _attn_kernel(q_ref, qi_ref, wit_ref, k_ref, ki_ref, vt_ref, bias_ref, o_ref, key_scr, att_scr, *scr,
                 seq, n_near, topk):
    qb = pl.program_id(1)
    nquad = (qb + 4) // 4
    lane_t = lax.broadcasted_iota(I32, (1, QB), 1) + qb * QB
    qchunk = lane_t // CHUNK
    sub = lax.broadcasted_iota(I32, (KQUAD, QB), 0)
    idx_scale = (IDX_DIM ** -0.5) * (IDX_HEADS ** -0.5)
    wrow = wit_ref[...] * idx_scale

    qi_all = jnp.concatenate([qi_ref[:, h * LANES:(h + 1) * LANES] for h in range(IDX_HEADS)], axis=0)

    def score_quad(j, carry):
        d = _nt_dot(ki_ref[0, j], qi_all)
        acc = jnp.zeros((KQUAD, QB), F32)
        for h in range(IDX_HEADS):
            acc = acc + jnp.maximum(d[:, h * QB:(h + 1) * QB], 0.0) * wrow[h:h + 1, :]
        bits = pltpu.bitcast(acc, I32)
        skey = bits ^ ((bits >> 31) & 0x7FFFFFFF)
        visible = (sub + j * KQUAD) // CHUNK <= qchunk
        key_scr[j] = jnp.where(visible, skey, INT_MIN)
        return carry

    lax.fori_loop(0, nquad, score_quad, 0)

    nvis = (qchunk + 1) * CHUNK
    kk = jnp.minimum(topk, nvis)

    def count(pred):
        def body(j, acc8):
            m = jnp.where(pred(key_scr[j], j), 1, 0)
            return acc8 + m.reshape(KQUAD // SUBLANES, SUBLANES, QB).sum(axis=0)
        return lax.fori_loop(0, nquad, body, jnp.zeros((SUBLANES, QB), I32)).sum(axis=0, keepdims=True)

    def bit_step(i, carry):
        ans, cnt = carry
        cand = ans + lax.shift_left(jnp.int32(1), 31 - i)
        c = count(lambda blk, j: blk >= cand)
        take = c >= kk
        return jnp.where(take, cand, ans), jnp.where(take, c, cnt)

    thr, cnt = lax.fori_loop(0, 32, bit_step,
                             (jnp.full((1, QB), INT_MIN, I32), jnp.full((1, QB), 0, I32) + nquad * KQUAD))

    @pl.when(jnp.max(cnt - kk) > 0)
    def _():
        n_gt = count(lambda blk, j: blk > thr)
        need = kk - n_gt

        def idx_step(i, jbound):
            cand = jbound + lax.shift_left(jnp.int32(1), (seq.bit_length() - 1) - i)
            c = count(lambda blk, j: (blk == thr) & (sub + j * KQUAD < cand))
            return jnp.where(c <= need, cand, jbound)

        jbound = lax.fori_loop(0, seq.bit_length(), idx_step, jnp.zeros((1, QB), I32))

        def drop(j, carry):
            blk = key_scr[j]
            key_scr[j] = jnp.where((blk == thr) & (sub + j * KQUAD >= jbound), INT_MIN, blk)
            return carry

        lax.fori_loop(0, nquad, drop, 0)

    acc_refs, s_even, s_odd = scr[:ATT_HEADS], scr[ATT_HEADS], scr[ATT_HEADS + 1]
    for acc_ref in acc_refs:
        acc_ref[...] = jnp.zeros(acc_ref.shape, F32)
    head_row = lax.broadcasted_iota(I32, (ATT_HEADS, QB), 0)
    last_pair = k_ref.shape[1] - 1

    def qk(jp, s_ref):
        jc = jnp.minimum(jp, last_pair)
        for h in range(ATT_HEADS):
            s_ref[h] = _nt_dot(k_ref[0, jc, :, h * LANES:(h + 1) * LANES], q_ref[:, h * LANES:(h + 1) * LANES])

    def softmax_pv(jq, half, s_ref, carry):
        m_all, l_all = carry
        jp = 2 * jq + half
        keys = key_scr[jq, half * KPAIR:(half + 1) * KPAIR, :]
        masked = jnp.where(keys >= thr, 0.0, -jnp.inf)
        tile0 = jnp.clip(qb - 2 * jp, 0, n_near)
        tile1 = jnp.clip(qb - 2 * jp - 1, 0, n_near)
        for h in range(ATT_HEADS):
            rows = slice(h * ATT_HEAD_DIM, (h + 1) * ATT_HEAD_DIM)
            bias = jnp.concatenate([bias_ref[tile0, h], bias_ref[tile1, h]], axis=0)
            s = s_ref[h] + bias + masked
            m = m_all[h:h + 1, :]
            m_new = jnp.maximum(m, jnp.max(s, axis=0, keepdims=True))
            p = jnp.exp2(s - m_new)
            alpha = jnp.exp2(m - m_new)
            l_new = alpha * l_all[h:h + 1, :] + jnp.sum(p, axis=0, keepdims=True)
            m_all = jnp.where(head_row == h, m_new, m_all)
            l_all = jnp.where(head_row == h, l_new, l_all)
            acc_refs[h][...] = alpha * acc_refs[h][...] + _dot(vt_ref[0, jp, rows, :], p.astype(BF16))
        return m_all, l_all

    def att_quad(jq, carry):
        qk(2 * jq + 1, s_odd)
        carry = softmax_pv(jq, 0, s_even, carry)
        qk(2 * jq + 2, s_even)
        return softmax_pv(jq, 1, s_odd, carry)

    qk(0, s_even)
    _, l_all = lax.fori_loop(0, nquad, att_quad,
                             (jnp.full((ATT_HEADS, QB), NEG_BIG, F32), jnp.zeros((ATT_HEADS, QB), F32)))
    for h in range(ATT_HEADS):
        rows = slice(h * ATT_HEAD_DIM, (h + 1) * ATT_HEAD_DIM)
        att_scr[rows, :] = acc_refs[h][...] / l_all[h:h + 1, :]
    o_ref[...] = att_scr[...].T.astype(o_ref.dtype)


def _attention(q, k, qi, ki, vt, wit, bias, B, S, n_near):
    T = B * S
    nqb = S // QB
    hp = ATT_HEADS * LANES
    aw = ATT_HEADS * ATT_HEAD_DIM
    topk = min(TOPK_MAX, S // 4)
    npr, nqd = S // KPAIR, S // KQUAD
    k4 = k.reshape(B, npr, KPAIR, hp)
    ki4 = ki.reshape(B, nqd, KQUAD, LANES)
    vt4 = vt.reshape(B, npr, aw, KPAIR)
    return pl.pallas_call(
        functools.partial(_attn_kernel, seq=S, n_near=n_near, topk=topk),
        grid=(B, nqb),
        in_specs=[pl.BlockSpec((QB, hp), lambda b, i: (b * nqb + i, 0)),
                  pl.BlockSpec((QB, hp), lambda b, i: (b * nqb + i, 0)),
                  pl.BlockSpec((IDX_HEADS, QB), lambda b, i: (0, b * nqb + i)),
                  _single((1, npr, KPAIR, hp), lambda b, i: (b, 0, 0, 0)),
                  _single((1, nqd, KQUAD, LANES), lambda b, i: (b, 0, 0, 0)),
                  _single((1, npr, aw, KPAIR), lambda b, i: (b, 0, 0, 0)),
                  _single((n_near + 1, ATT_HEADS, QB, QB), lambda b, i: (0, 0, 0, 0))],
        out_specs=pl.BlockSpec((QB, aw), lambda b, i: (b * nqb + i, 0)),
        out_shape=jax.ShapeDtypeStruct((T, aw), BF16),
        scratch_shapes=[pltpu.VMEM((nqd, KQUAD, QB), I32), pltpu.VMEM((aw, QB), F32)]
        + [pltpu.VMEM((ATT_HEAD_DIM, QB), F32) for _ in range(ATT_HEADS)]
        + [pltpu.VMEM((ATT_HEADS, KPAIR, QB), F32) for _ in range(2)],
        compiler_params=_params(2),
        name="dsa_attention",
    )(q, qi, wit, k4, ki4, vt4, bias)


HALO = 32


def _mix_kernel(att_ref, u_ref, halo_ref, gate_ref, x_ref, cw_ref, cb_ref, lng_ref, lnb_ref, wao_ref, wco_ref,
                wout_ref, gffn_ref, x1_ref, h2_ref, ext_scr, *, tm, tiles_per_seq):
    i = pl.program_id(0)
    first = (i % tiles_per_seq) == 0
    ext_scr[0:HALO, :] = jnp.where(first, 0.0, halo_ref[...])
    ext_scr[HALO:HALO + tm, :] = u_ref[...]
    y = jnp.zeros((tm, CONV_CH), F32)
    for j in range(CONV_WIDTH):
        y = y + cw_ref[j:j + 1, :] * ext_scr[pl.ds(HALO - (CONV_WIDTH - 1) + j, tm), :]
    y = y + cb_ref[...]
    mu = jnp.mean(y, axis=-1, keepdims=True)
    yc = y - mu
    yn = yc * lax.rsqrt(jnp.mean(yc * yc, axis=-1, keepdims=True) + EPS) * lng_ref[...] + lnb_ref[...]
    z = yn * jax.nn.sigmoid(yn)
    y_conv = _dot(z.astype(BF16), wco_ref[...])
    y_att = _dot(att_ref[...], wao_ref[...])
    d = y_att.shape[1]
    g = gate_ref[...]
    mixed = g[:, :d].astype(F32) * y_att + g[:, d:].astype(F32) * y_conv
    x1 = x_ref[...] + _dot(mixed.astype(BF16), wout_ref[...])
    x1_ref[...] = x1
    h2_ref[...] = x1 * lax.rsqrt(jnp.mean(x1 * x1, axis=-1, keepdims=True) + EPS) * gffn_ref[...]


def _mix(att, u, gate, x2, conv_w, conv_b, ln_g, ln_b, w_att_out, w_conv_out, w_out, ffn_g, S, tm=256):
    T, D = x2.shape
    aw = att.shape[1]
    const = lambda *shape: _single(shape, lambda i: (0,) * len(shape))
    hb = tm // HALO
    return pl.pallas_call(
        functools.partial(_mix_kernel, tm=tm, tiles_per_seq=S // tm),
        grid=(T // tm,),
        in_specs=[pl.BlockSpec((tm, aw), lambda i: (i, 0)),
                  pl.BlockSpec((tm, CONV_CH), lambda i: (i, 0)),
                  pl.BlockSpec((HALO, CONV_CH), lambda i: (jnp.maximum(i * hb - 1, 0), 0)),
                  pl.BlockSpec((tm, N_BRANCH * D), lambda i: (i, 0)),
                  pl.BlockSpec((tm, D), lambda i: (i, 0)),
                  const(CONV_WIDTH, CONV_CH), const(1, CONV_CH), const(1, CONV_CH), const(1, CONV_CH),
                  const(aw, D), const(CONV_CH, D), const(D, D), const(1, D)],
        out_specs=[pl.BlockSpec((tm, D), lambda i: (i, 0)), pl.BlockSpec((tm, D), lambda i: (i, 0))],
        out_shape=[jax.ShapeDtypeStruct((T, D), F32), jax.ShapeDtypeStruct((T, D), F32)],
        scratch_shapes=[pltpu.VMEM((HALO + tm, CONV_CH), F32)],
        compiler_params=_params(1),
        name="mix_out_proj",
    )(att, u, u, gate, x2, conv_w.reshape(CONV_WIDTH, CONV_CH), conv_b.reshape(1, CONV_CH),
      ln_g.reshape(1, CONV_CH), ln_b.reshape(1, CONV_CH), w_att_out.astype(BF16), w_conv_out.astype(BF16),
      w_out.astype(BF16), ffn_g.reshape(1, D))


def _top_rows(sc, k, payload=None):
    rows = sc.shape[0]
    iota = lax.broadcasted_iota(I32, sc.shape, 0)
    out_row = lax.broadcasted_iota(I32, (k, sc.shape[1]), 0)
    vals = jnp.zeros((k, sc.shape[1]), F32)
    idxs = jnp.zeros((k, sc.shape[1]), I32)
    for r in range(k):
        m = jnp.max(sc, axis=0, keepdims=True)
        idx = jnp.min(jnp.where(sc == m, iota, rows), axis=0, keepdims=True)
        hit = iota == idx
        rec = idx if payload is None else jnp.max(jnp.where(hit, payload, -1), axis=0, keepdims=True)
        vals = jnp.where(out_row == r, m, vals)
        idxs = jnp.where(out_row == r, rec, idxs)
        sc = jnp.where(hit, -jnp.inf, sc)
    return vals, idxs


def _route_kernel(h2_ref, wq_ref, sk_ref, e_ref, g_ref, qt_scr, et_scr, gt_scr, *, tm):
    half = N_KEYS
    qt_scr[...] = _nt_dot(wq_ref[...], h2_ref[...].astype(BF16)).astype(BF16)

    def head(h, carry):
        tops = []
        for c in range(2):
            row0 = pl.multiple_of((h * 2 + c) * half, half)
            sc = _dot(sk_ref[h * 2 + c], qt_scr[pl.ds(row0, half), :])
            tops.append(_top_rows(sc, PEER_TOPK))
        (a, ia), (b, ib) = tops
        k = PEER_TOPK
        g = SUBLANES
        assert (k // 2) % g == 0 and k % g == 0
        row = lax.broadcasted_iota(I32, (g, a.shape[1]), 0)
        cand_parts, cidx_parts = [], []

        def add(av, iav, bv, ibv, valid_rows):
            s = av + bv
            if valid_rows < g:
                s = jnp.where(row < valid_rows, s, -jnp.inf)
            cand_parts.append(s)
            cidx_parts.append(iav * N_KEYS + ibv)

        for i in range(k):
            nj = k // (i + 1)
            if nj >= g:
                for j0 in range(0, nj, g):
                    add(a[i:i + 1, :], ia[i:i + 1, :], b[j0:j0 + g, :], ib[j0:j0 + g, :], g)
            elif nj > 1:
                add(a[i:i + 1, :], ia[i:i + 1, :], b[0:g, :], ib[0:g, :], nj)
            elif i % g == 0:
                add(a[i:i + g, :], ia[i:i + g, :], b[0:1, :], ib[0:1, :], g)
        cand = jnp.concatenate(cand_parts, axis=0)
        cidx = jnp.concatenate(cidx_parts, axis=0)
        ts, te = _top_rows(cand, PEER_TOPK, payload=cidx)
        ex = jnp.exp(ts - ts[0:1, :])
        gate = ex / jnp.sum(ex, axis=0, keepdims=True)
        r0 = pl.multiple_of(h * PEER_TOPK, PEER_TOPK)
        et_scr[pl.ds(r0, PEER_TOPK), :] = te
        gt_scr[pl.ds(r0, PEER_TOPK), :] = gate
        return carry

    lax.fori_loop(0, PEER_HEADS, head, 0)
    e_ref[...] = et_scr[...].T
    g_ref[...] = gt_scr[...].T


def _peer_route(h2, w_peer_q, sub_keys, tm=256):
    T, D = h2.shape
    nsel = PEER_HEADS * PEER_TOPK
    qd = w_peer_q.shape[1]
    half = sub_keys.shape[-1]
    skb = sub_keys.reshape(PEER_HEADS * 2, N_KEYS, half).astype(BF16)
    return pl.pallas_call(
        functools.partial(_route_kernel, tm=tm),
        grid=(T // tm,),
        in_specs=[pl.BlockSpec((tm, D), lambda i: (i, 0)),
                  _single((qd, D), lambda i: (0, 0)),
                  _single((PEER_HEADS * 2, N_KEYS, half), lambda i: (0, 0, 0))],
        out_specs=[pl.BlockSpec((tm, nsel), lambda i: (i, 0)), pl.BlockSpec((tm, nsel), lambda i: (i, 0))],
        out_shape=[jax.ShapeDtypeStruct((T, nsel), I32), jax.ShapeDtypeStruct((T, nsel), F32)],
        scratch_shapes=[pltpu.VMEM((qd, tm), BF16), pltpu.VMEM((nsel, tm), I32), pltpu.VMEM((nsel, tm), F32)],
        compiler_params=_params(1),
        name="peer_route",
    )(h2, w_peer_q.T.astype(BF16), skb)


TOK = 8


def _gelu_tanh(x):
    return 0.5 * x * (1.0 + jnp.tanh(math.sqrt(2.0 / math.pi) * (x + 0.044715 * (x * x * x))))


def _sublane_sums(ps):
    sub = lax.broadcasted_iota(I32, (SUBLANES, LANES), 0)
    lvl, stride = list(ps), SUBLANES // 2
    while len(lvl) > 1:
        half = len(lvl) // 2
        low = (sub & stride) == 0
        nxt = []
        for n in range(half):
            a, b = lvl[n], lvl[n + half]
            nxt.append(jnp.where(low, a + pltpu.roll(a, SUBLANES - stride, 0), b + pltpu.roll(b, stride, 0)))
        lvl, stride = nxt, stride // 2
    return lvl[0]


def _pack_uv(u, v):
    ne, d = u.shape
    hi = lax.bitcast_convert_type(u.astype(BF16), jnp.uint16).astype(jnp.uint32)
    lo = lax.bitcast_convert_type(v.astype(BF16), jnp.uint16).astype(jnp.uint32)
    return ((hi << 16) | lo).reshape(ne, d // LANES, LANES)


def _u_of(word):
    return pltpu.bitcast(word & jnp.uint32(0xFFFF0000), F32)


def _v_of(word):
    return pltpu.bitcast(word << 16, F32)


def _expert_kernel(idxc_ref, idxn_ref, h2_ref, x1_ref, gate_ref, uv_ref, o_ref, buf, sem, abc_scr, *, nsel):
    i = pl.program_id(0)
    nsteps = pl.num_programs(0)
    rows = TOK * nsel
    dsub = h2_ref.shape[1]

    def issue_token(idx_ref, tok, s, t):
        for n in range(nsel):
            pltpu.make_async_copy(uv_ref.at[idx_ref[tok, n]], buf.at[s, t * nsel + n],
                                  sem.at[s]).start(priority=n % 2)

    def wait_slot(s):
        pltpu.make_async_copy(uv_ref.at[pl.ds(0, rows)], buf.at[s], sem.at[s]).wait()

    @pl.when(i == 0)
    def _():
        for t in range(TOK):
            issue_token(idxc_ref, t, 0, t)

    ones = jnp.ones((SUBLANES, LANES), F32)
    row_id = lax.broadcasted_iota(I32, (TOK, nsel), 0)

    for grp in range(2):
        base = grp * TOK
        wait_slot(grp)
        s8 = jnp.zeros((TOK, nsel), F32)
        for t in range(TOK):
            if grp == 0:
                issue_token(idxc_ref, TOK + t, 1, t)
            else:
                issue_token(idxn_ref, t, 0, t)
            xt = h2_ref[base + t]
            qs = []
            for g in range(nsel // SUBLANES):
                ps = [_u_of(buf[grp, t * nsel + g * SUBLANES + k]) * xt for k in range(SUBLANES)]
                qs.append(_sublane_sums(ps))
            q = jnp.concatenate(qs, axis=0)
            srow = _nt_dot(ones, q, precision=lax.Precision.HIGHEST)
            s8 = jnp.where(row_id == t, srow[0:TOK, :], s8)
        a8 = _gelu_tanh(s8) * gate_ref[base:base + TOK, :]
        a8t = jnp.concatenate([a8, jnp.zeros((nsel - TOK, nsel), F32)], axis=0).T

        for t in range(TOK):
            abc_scr[t] = jnp.broadcast_to(a8t[:, t:t + 1], (nsel, LANES))
            accs = [jnp.zeros((dsub, LANES), F32) for _ in range(4)]
            for n in range(nsel):
                accs[n % 4] = accs[n % 4] + abc_scr[t, n:n + 1, :] * _v_of(buf[grp, t * nsel + n])
            o_ref[base + t] = x1_ref[base + t] + ((accs[0] + accs[1]) + (accs[2] + accs[3]))

    @pl.when(i == nsteps - 1)
    def _():
        wait_slot(0)


def _peer_experts(eidx, gates, h2, x1, peer_u, peer_v):
    T, D = x1.shape
    nsel = eidx.shape[1]
    dsub = D // LANES
    ne = peer_u.shape[0]
    uv = _pack_uv(peer_u, peer_v)
    tb = 2 * TOK
    nsteps = T // tb
    out = pl.pallas_call(
        functools.partial(_expert_kernel, nsel=nsel),
        grid=(nsteps,),
        in_specs=[pl.BlockSpec((tb, nsel), lambda i: (i, 0), memory_space=pltpu.SMEM),
                  pl.BlockSpec((tb, nsel), lambda i: (jnp.minimum(i + 1, nsteps - 1), 0), memory_space=pltpu.SMEM),
                  pl.BlockSpec((tb, dsub, LANES), lambda i: (i, 0, 0)),
                  pl.BlockSpec((tb, dsub, LANES), lambda i: (i, 0, 0)),
                  pl.BlockSpec((tb, nsel), lambda i: (i, 0)),
                  pl.BlockSpec(memory_space=pl.ANY)],
        out_specs=pl.BlockSpec((tb, dsub, LANES), lambda i: (i, 0, 0)),
        out_shape=jax.ShapeDtypeStruct((T, dsub, LANES), F32),
        scratch_shapes=[pltpu.VMEM((2, TOK * nsel, dsub, LANES), jnp.uint32), pltpu.SemaphoreType.DMA((2,)),
                        pltpu.VMEM((TOK, nsel, LANES), F32)],
        compiler_params=_params(1),
        name="peer_experts",
    )(eidx, eidx, h2.reshape(T, dsub, LANES), x1.reshape(T, dsub, LANES), gates, uv)
    return out.reshape(T, D)


def _ple_kernel(x_ref, p_ref, g_ref, wg_ref, wp_ref, o_ref):
    x = x_ref[...]
    h = x * lax.rsqrt(jnp.mean(x * x, axis=-1, keepdims=True) + EPS) * g_ref[...]
    gate = jax.nn.sigmoid(_dot(h.astype(BF16), wg_ref[...]))
    o_ref[...] = x + gate * _dot(p_ref[...].astype(BF16), wp_ref[...])


def _ple(x2, p2, ple_g, w_gate, w_proj, tm=512):
    T, D = x2.shape
    pd = p2.shape[1]
    return pl.pallas_call(
        _ple_kernel,
        grid=(T // tm,),
        in_specs=[pl.BlockSpec((tm, D), lambda i: (i, 0)), pl.BlockSpec((tm, pd), lambda i: (i, 0)),
                  _single((1, D), lambda i: (0, 0)), _single((D, D), lambda i: (0, 0)),
                  _single((pd, D), lambda i: (0, 0))],
        out_specs=pl.BlockSpec((tm, D), lambda i: (i, 0)),
        out_shape=jax.ShapeDtypeStruct((T, D), F32),
        compiler_params=_params(1),
        name="ple",
    )(x2, p2, ple_g.reshape(1, D), w_gate.astype(BF16), w_proj.astype(BF16))


def kernel(x, p, rel_bias, attn_norm_g, w_in, b_gate, q_norm_g, k_norm_g, w_att_out, conv_w, conv_b, conv_ln_g,
           conv_ln_b, w_conv_out, w_out, ffn_norm_g, w_peer_q, peer_sub_keys, peer_u, peer_v, ple_norm_g,
           w_ple_gate, w_ple_proj):
    B, S, D = x.shape
    depth = w_in.shape[0]
    T = B * S
    assert S % KQUAD == 0 and D % LANES == 0 and T % 512 == 0
    n_near = _num_near_tiles(S)
    bias = _bias_tiles(rel_bias, n_near)
    x2 = x.reshape(T, D)
    for i in range(depth):
        q, k, qi, ki, vt, wit, u, gate = _in_proj(x2, attn_norm_g[i], w_in[i], b_gate[i], q_norm_g[i], k_norm_g[i])
        att = _attention(q, k, qi, ki, vt, wit, bias, B, S, n_near)
        x1, h2 = _mix(att, u, gate, x2, conv_w[i], conv_b[i], conv_ln_g[i], conv_ln_b[i], w_att_out[i],
                      w_conv_out[i], w_out[i], ffn_norm_g[i], S)
        eidx, gates = _peer_route(h2, w_peer_q[i], peer_sub_keys[i])
        x2 = _peer_experts(eidx, gates, h2, x1, peer_u[i], peer_v[i])
        x2 = _ple(x2, p[i].reshape(T, -1), ple_norm_g[i], w_ple_gate[i], w_ple_proj[i])
    return x2.reshape(B, S, D)
```

```python
import functools
import math

import numpy as np
import jax
import jax.numpy as jnp
from jax import lax
from jax.experimental import pallas as pl
from jax.experimental.pallas import tpu as pltpu

CHUNK = 64
ATT_HEADS = 8
ATT_HEAD_DIM = 64
IDX_HEADS = 8
IDX_DIM = 64
TOPK_MAX = 256
REL_BUCKETS = 32
REL_MAX_DIST = 1024
CONV_CH = 512
CONV_WIDTH = 31
N_BRANCH = 2
PEER_HEADS = 8
N_KEYS = 128
PEER_TOPK = 16
EPS = 1e-6

LANES = 128
SUBLANES = 8
VMEM_LIMIT = 56 * 1024 * 1024

QB = 128
KPAIR = 2 * QB
KQUAD = 4 * QB
INT_MIN = -(2 ** 31)
NEG_BIG = -1e30
LOG2E = math.log2(math.e)

F32 = jnp.float32
BF16 = jnp.bfloat16
I32 = jnp.int32


def _nt_dot(a, b, precision=None):
    return lax.dot_general(a, b, (((1,), (1,)), ((), ())), precision=precision,
                           preferred_element_type=F32)


def _dot(a, b):
    return jnp.dot(a, b, preferred_element_type=F32)


def _single(shape, index_map):
    return pl.BlockSpec(shape, index_map)


def _params(n_grid_dims):
    return pltpu.CompilerParams(dimension_semantics=("arbitrary",) * n_grid_dims,
                                vmem_limit_bytes=VMEM_LIMIT)


def _t5_bucket_np(rel):
    half = REL_BUCKETS // 2
    max_exact = half // 2
    ret = np.where(rel > 0, half, 0)
    n = np.abs(rel)
    nf = np.maximum(n, 1).astype(np.float32)
    large = max_exact + (np.log(nf / np.float32(max_exact)) / np.float32(math.log(REL_MAX_DIST / max_exact))
                         * np.float32(half - max_exact)).astype(np.int32)
    large = np.minimum(large, half - 1)
    return ret + np.where(n < max_exact, n, large)


def _num_near_tiles(seq):
    n = np.arange(1, max(seq, 2 * REL_MAX_DIST) + 1)
    b = _t5_bucket_np(-n)
    sat = REL_BUCKETS // 2 - 1
    unsat = np.nonzero(b != sat)[0]
    n_sat = int(n[unsat[-1]]) + 1 if unsat.size else 1
    return -(-(n_sat + QB - 1) // QB)


def _bias_kernel(rb_ref, o_ref, *, n_near):
    d = pl.program_id(0)
    i = lax.broadcasted_iota(I32, (QB, QB), 0)
    j = lax.broadcasted_iota(I32, (QB, QB), 1)
    rel = i - j - d * QB
    rel = jnp.where(d >= n_near, -8 * REL_MAX_DIST, rel)
    half = REL_BUCKETS // 2
    max_exact = half // 2
    ret = jnp.where(rel > 0, half, 0)
    n = jnp.abs(rel)
    nf = jnp.maximum(n, 1).astype(F32)
    large = max_exact + (jnp.log(nf / max_exact) / math.log(REL_MAX_DIST / max_exact)
                         * (half - max_exact)).astype(I32)
    large = jnp.minimum(large, half - 1)
    bucket = ret + jnp.where(n < max_exact, n, large)
    for h in range(ATT_HEADS):
        acc = jnp.zeros((QB, QB), F32)
        for b in range(REL_BUCKETS):
            acc = jnp.where(bucket == b, rb_ref[b, h], acc)
        o_ref[0, h] = acc * LOG2E


def _bias_tiles(rel_bias, n_near):
    return pl.pallas_call(
        functools.partial(_bias_kernel, n_near=n_near),
        grid=(n_near + 1,),
        in_specs=[pl.BlockSpec(memory_space=pltpu.SMEM)],
        out_specs=pl.BlockSpec((1, ATT_HEADS, QB, QB), lambda d: (d, 0, 0, 0)),
        out_shape=jax.ShapeDtypeStruct((n_near + 1, ATT_HEADS, QB, QB), F32),
        compiler_params=_params(1),
        name="bias_tiles",
    )(rel_bias)


def _inproj_kernel(x_ref, g_ref, wa_ref, wvt_ref, wwit_ref, wglu_ref, wgate_ref, bgate_ref, gq_ref, gk_ref,
                   q_ref, k_ref, qi_ref, ki_ref, vt_ref, wit_ref, u_ref, gate_ref, *, tm):
    x = x_ref[...]
    h = x * lax.rsqrt(jnp.mean(x * x, axis=-1, keepdims=True) + EPS) * g_ref[...]
    hb = h.astype(BF16)
    hp = ATT_HEADS * LANES
    ya = _dot(hb, wa_ref[...])
    for h_i in range(ATT_HEADS):
        sl = slice(h_i * LANES, (h_i + 1) * LANES)
        qh = ya[:, h_i * LANES:(h_i + 1) * LANES]
        ms = jnp.sum(qh * qh, axis=-1, keepdims=True) * (1.0 / ATT_HEAD_DIM)
        q_ref[:, sl] = (qh * lax.rsqrt(ms + EPS) * gq_ref[...]).astype(BF16)
        kh = ya[:, hp + h_i * LANES:hp + (h_i + 1) * LANES]
        ms = jnp.sum(kh * kh, axis=-1, keepdims=True) * (1.0 / ATT_HEAD_DIM)
        k_ref[:, sl] = (kh * lax.rsqrt(ms + EPS) * gk_ref[...]).astype(BF16)
    qi_ref[...] = ya[:, 2 * hp:3 * hp].astype(BF16)
    ki_ref[...] = ya[:, 3 * hp:3 * hp + LANES].astype(BF16)
    vt = _nt_dot(wvt_ref[...], hb).astype(BF16)
    for c in range(tm // KPAIR):
        vt_ref[c] = vt[:, c * KPAIR:(c + 1) * KPAIR]
    wit_ref[...] = _nt_dot(wwit_ref[...], hb)
    glu = _dot(hb, wglu_ref[...])
    u_ref[...] = glu[:, :CONV_CH] * jax.nn.sigmoid(glu[:, CONV_CH:])
    gate_ref[...] = jax.nn.sigmoid(_dot(hb, wgate_ref[...]) + bgate_ref[...]).astype(BF16)


def _pad_heads(w, nh, hd):
    d = w.shape[0]
    w3 = w.reshape(d, nh, hd)
    w3 = jnp.pad(w3, ((0, 0), (0, 0), (0, LANES - hd)))
    return w3.reshape(d, nh * LANES)


def _in_proj(x2, attn_g, w_in, b_gate, q_g, k_g, tm=256):
    T, D = x2.shape
    aw = ATT_HEADS * ATT_HEAD_DIM
    iw = IDX_HEADS * IDX_DIM
    o = 0
    wq = w_in[:, o:o + aw]; o += aw
    wk = w_in[:, o:o + aw]; o += aw
    wv = w_in[:, o:o + aw]; o += aw
    wqi = w_in[:, o:o + iw]; o += iw
    wki = w_in[:, o:o + IDX_DIM]; o += IDX_DIM
    wwi = w_in[:, o:o + IDX_HEADS]; o += IDX_HEADS
    wglu = w_in[:, o:o + 2 * CONV_CH]; o += 2 * CONV_CH
    wgate = w_in[:, o:o + N_BRANCH * D]
    wa = jnp.concatenate([_pad_heads(wq, ATT_HEADS, ATT_HEAD_DIM), _pad_heads(wk, ATT_HEADS, ATT_HEAD_DIM),
                          _pad_heads(wqi, IDX_HEADS, IDX_DIM),
                          jnp.pad(wki, ((0, 0), (0, LANES - IDX_DIM)))], axis=1).astype(BF16)
    na = wa.shape[1]
    hp = ATT_HEADS * LANES
    pad_g = lambda g, s: jnp.pad(g * s, (0, LANES - ATT_HEAD_DIM)).reshape(1, LANES)
    gq = pad_g(q_g, ATT_HEAD_DIM ** -0.5 * LOG2E)
    gk = pad_g(k_g, 1.0)
    const = lambda *shape: _single(shape, lambda i: (0,) * len(shape))
    outs = pl.pallas_call(
        functools.partial(_inproj_kernel, tm=tm),
        grid=(T // tm,),
        in_specs=[pl.BlockSpec((tm, D), lambda i: (i, 0)), const(1, D), const(D, na), const(aw, D),
                  const(IDX_HEADS, D), const(D, 2 * CONV_CH), const(D, N_BRANCH * D), const(1, N_BRANCH * D),
                  const(1, LANES), const(1, LANES)],
        out_specs=[pl.BlockSpec((tm, hp), lambda i: (i, 0)), pl.BlockSpec((tm, hp), lambda i: (i, 0)),
                   pl.BlockSpec((tm, hp), lambda i: (i, 0)), pl.BlockSpec((tm, LANES), lambda i: (i, 0)),
                   pl.BlockSpec((tm // KPAIR, aw, KPAIR), lambda i: (i, 0, 0)),
                   pl.BlockSpec((IDX_HEADS, tm), lambda i: (0, i)),
                   pl.BlockSpec((tm, CONV_CH), lambda i: (i, 0)),
                   pl.BlockSpec((tm, N_BRANCH * D), lambda i: (i, 0))],
        out_shape=[jax.ShapeDtypeStruct((T, hp), BF16), jax.ShapeDtypeStruct((T, hp), BF16),
                   jax.ShapeDtypeStruct((T, hp), BF16), jax.ShapeDtypeStruct((T, LANES), BF16),
                   jax.ShapeDtypeStruct((T // KPAIR, aw, KPAIR), BF16),
                   jax.ShapeDtypeStruct((IDX_HEADS, T), F32),
                   jax.ShapeDtypeStruct((T, CONV_CH), F32),
                   jax.ShapeDtypeStruct((T, N_BRANCH * D), BF16)],
        compiler_params=_params(1),
        name="in_proj",
    )(x2, attn_g.reshape(1, D), wa, wv.T.astype(BF16), wwi.T.astype(BF16), wglu.astype(BF16),
      wgate.astype(BF16), b_gate.reshape(1, N_BRANCH * D), gq, gk)
    return outs


def _attn_kernel(q_ref, qi_ref, wit_ref, k_hbm, ki_hbm, vt_hbm, bias_hbm, o_ref, k_ref, ki_ref, vt_ref, bias_ref,
                 load_sem, key_scr, att_scr, *scr, seq, n_near, topk):
    qb = pl.program_id(1)

    @pl.when(qb == 0)
    def _():
        b = pl.program_id(0)
        loads = [pltpu.make_async_copy(k_hbm.at[b], k_ref, load_sem.at[0]),
                 pltpu.make_async_copy(ki_hbm.at[b], ki_ref, load_sem.at[1]),
                 pltpu.make_async_copy(vt_hbm.at[b], vt_ref, load_sem.at[2]),
                 pltpu.make_async_copy(bias_hbm, bias_ref, load_sem.at[3])]
        for c in loads:
            c.start()
        for c in loads:
            c.wait()

    nquad = (qb + 4) // 4
    lane_t = lax.broadcasted_iota(I32, (1, QB), 1) + qb * QB
    qchunk = lane_t // CHUNK
    sub = lax.broadcasted_iota(I32, (KQUAD, QB), 0)
    idx_scale = (IDX_DIM ** -0.5) * (IDX_HEADS ** -0.5)
    wrow = wit_ref[...] * idx_scale

    qi_all = jnp.concatenate([qi_ref[:, h * LANES:(h + 1) * LANES] for h in range(IDX_HEADS)], axis=0)

    def score_quad(j, carry):
        d = _nt_dot(ki_ref[j], qi_all)
        acc = jnp.zeros((KQUAD, QB), F32)
        for h in range(IDX_HEADS):
            acc = acc + jnp.maximum(d[:, h * QB:(h + 1) * QB], 0.0) * wrow[h:h + 1, :]
        bits = pltpu.bitcast(acc, I32)
        skey = bits ^ ((bits >> 31) & 0x7FFFFFFF)
        visible = (sub + j * KQUAD) // CHUNK <= qchunk
        key_scr[j] = jnp.where(visible, skey, INT_MIN)
        return carry

    lax.fori_loop(0, nquad, score_quad, 0)

    nvis = (qchunk + 1) * CHUNK
    kk = jnp.minimum(topk, nvis)

    def count(pred):
        def body(j, acc8):
            m = jnp.where(pred(key_scr[j], j), 1, 0)
            return acc8 + m.reshape(KQUAD // SUBLANES, SUBLANES, QB).sum(axis=0)
        return lax.fori_loop(0, nquad, body, jnp.zeros((SUBLANES, QB), I32)).sum(axis=0, keepdims=True)

    def bit_step(i, carry):
        ans, cnt = carry
        cand = ans + lax.shift_left(jnp.int32(1), 31 - i)
        c = count(lambda blk, j: blk >= cand)
        take = c >= kk
        return jnp.where(take, cand, ans), jnp.where(take, c, cnt)

    thr, cnt = lax.fori_loop(0, 32, bit_step,
                             (jnp.full((1, QB), INT_MIN, I32), jnp.full((1, QB), 0, I32) + nquad * KQUAD))

    @pl.when(jnp.max(cnt - kk) > 0)
    def _():
        n_gt = count(lambda blk, j: blk > thr)
        need = kk - n_gt

        def idx_step(i, jbound):
            cand = jbound + lax.shift_left(jnp.int32(1), (seq.bit_length() - 1) - i)
            c = count(lambda blk, j: (blk == thr) & (sub + j * KQUAD < cand))
            return jnp.where(c <= need, cand, jbound)

        jbound = lax.fori_loop(0, seq.bit_length(), idx_step, jnp.zeros((1, QB), I32))

        def drop(j, carry):
            blk = key_scr[j]
            key_scr[j] = jnp.where((blk == thr) & (sub + j * KQUAD >= jbound), INT_MIN, blk)
            return carry

        lax.fori_loop(0, nquad, drop, 0)

    acc_refs, s_even, s_odd = scr[:ATT_HEADS], scr[ATT_HEADS], scr[ATT_HEADS + 1]
    for acc_ref in acc_refs:
        acc_ref[...] = jnp.zeros(acc_ref.shape, F32)
    head_row = lax.broadcasted_iota(I32, (ATT_HEADS, QB), 0)
    last_pair = k_ref.shape[0] - 1

    def qk(jp, s_ref):
        jc = jnp.minimum(jp, last_pair)
        for h in range(ATT_HEADS):
            s_ref[h] = _nt_dot(k_ref[jc, :, h * LANES:(h + 1) * LANES], q_ref[:, h * LANES:(h + 1) * LANES])

    def softmax_pv(jq, half, s_ref, carry):
        m_all, l_all = carry
        jp = 2 * jq + half
        keys = key_scr[jq, half * KPAIR:(half + 1) * KPAIR, :]
        masked = jnp.where(keys >= thr, 0.0, -jnp.inf)
        tile0 = jnp.clip(qb - 2 * jp, 0, n_near)
        tile1 = jnp.clip(qb - 2 * jp - 1, 0, n_near)
        for h in range(ATT_HEADS):
            rows = slice(h * ATT_HEAD_DIM, (h + 1) * ATT_HEAD_DIM)
            bias = jnp.concatenate([bias_ref[tile0, h], bias_ref[tile1, h]], axis=0)
            s = s_ref[h] + bias + masked
            m = m_all[h:h + 1, :]
            m_new = jnp.maximum(m, jnp.max(s, axis=0, keepdims=True))
            p = jnp.exp2(s - m_new)
            alpha = jnp.exp2(m - m_new)
            l_new = alpha * l_all[h:h + 1, :] + jnp.sum(p, axis=0, keepdims=True)
            m_all = jnp.where(head_row == h, m_new, m_all)
            l_all = jnp.where(head_row == h, l_new, l_all)
            acc_refs[h][...] = alpha * acc_refs[h][...] + _dot(vt_ref[jp, rows, :], p.astype(BF16))
        return m_all, l_all

    def att_quad(jq, carry):
        qk(2 * jq + 1, s_odd)
        carry = softmax_pv(jq, 0, s_even, carry)
        qk(2 * jq + 2, s_even)
        return softmax_pv(jq, 1, s_odd, carry)

    qk(0, s_even)
    _, l_all = lax.fori_loop(0, nquad, att_quad,
                             (jnp.full((ATT_HEADS, QB), NEG_BIG, F32), jnp.zeros((ATT_HEADS, QB), F32)))
    for h in range(ATT_HEADS):
        rows = slice(h * ATT_HEAD_DIM, (h + 1) * ATT_HEAD_DIM)
        att_scr[rows, :] = acc_refs[h][...] / l_all[h:h + 1, :]
    o_ref[...] = att_scr[...].T.astype(o_ref.dtype)


def _attention(q, k, qi, ki, vt, wit, bias, B, S, n_near):
    T = B * S
    nqb = S // QB
    hp = ATT_HEADS * LANES
    aw = ATT_HEADS * ATT_HEAD_DIM
    topk = min(TOPK_MAX, S // 4)
    npr, nqd = S // KPAIR, S // KQUAD
    k4 = k.reshape(B, npr, KPAIR, hp)
    ki4 = ki.reshape(B, nqd, KQUAD, LANES)
    vt4 = vt.reshape(B, npr, aw, KPAIR)
    return pl.pallas_call(
        functools.partial(_attn_kernel, seq=S, n_near=n_near, topk=topk),
        grid=(B, nqb),
        in_specs=[pl.BlockSpec((QB, hp), lambda b, i: (b * nqb + i, 0)),
                  pl.BlockSpec((QB, hp), lambda b, i: (b * nqb + i, 0)),
                  pl.BlockSpec((IDX_HEADS, QB), lambda b, i: (0, b * nqb + i)),
                  pl.BlockSpec(memory_space=pl.ANY), pl.BlockSpec(memory_space=pl.ANY),
                  pl.BlockSpec(memory_space=pl.ANY), pl.BlockSpec(memory_space=pl.ANY)],
        out_specs=pl.BlockSpec((QB, aw), lambda b, i: (b * nqb + i, 0)),
        out_shape=jax.ShapeDtypeStruct((T, aw), BF16),
        scratch_shapes=[pltpu.VMEM((npr, KPAIR, hp), BF16), pltpu.VMEM((nqd, KQUAD, LANES), BF16),
                        pltpu.VMEM((npr, aw, KPAIR), BF16), pltpu.VMEM((n_near + 1, ATT_HEADS, QB, QB), F32),
                        pltpu.SemaphoreType.DMA((4,)),
                        pltpu.VMEM((nqd, KQUAD, QB), I32), pltpu.VMEM((aw, QB), F32)]
        + [pltpu.VMEM((ATT_HEAD_DIM, QB), F32) for _ in range(ATT_HEADS)]
        + [pltpu.VMEM((ATT_HEADS, KPAIR, QB), F32) for _ in range(2)],
        compiler_params=_params(2),
        name="dsa_attention",
    )(q, qi, wit, k4, ki4, vt4, bias)


HALO = 32


def _mix_kernel(att_ref, u_ref, halo_ref, gate_ref, x_ref, cw_ref, cb_ref, lng_ref, lnb_ref, wao_ref, wco_ref,
                wout_ref, gffn_ref, x1_ref, h2_ref, ext_scr, *, tm, tiles_per_seq):
    i = pl.program_id(0)
    first = (i % tiles_per_seq) == 0
    ext_scr[0:HALO, :] = jnp.where(first, 0.0, halo_ref[...])
    ext_scr[HALO:HALO + tm, :] = u_ref[...]
    y = jnp.zeros((tm, CONV_CH), F32)
    for j in range(CONV_WIDTH):
        y = y + cw_ref[j:j + 1, :] * ext_scr[pl.ds(HALO - (CONV_WIDTH - 1) + j, tm), :]
    y = y + cb_ref[...]
    mu = jnp.mean(y, axis=-1, keepdims=True)
    yc = y - mu
    yn = yc * lax.rsqrt(jnp.mean(yc * yc, axis=-1, keepdims=True) + EPS) * lng_ref[...] + lnb_ref[...]
    z = yn * jax.nn.sigmoid(yn)
    y_conv = _dot(z.astype(BF16), wco_ref[...])
    y_att = _dot(att_ref[...], wao_ref[...])
    d = y_att.shape[1]
    g = gate_ref[...]
    mixed = g[:, :d].astype(F32) * y_att + g[:, d:].astype(F32) * y_conv
    x1 = x_ref[...] + _dot(mixed.astype(BF16), wout_ref[...])
    x1_ref[...] = x1
    h2_ref[...] = x1 * lax.rsqrt(jnp.mean(x1 * x1, axis=-1, keepdims=True) + EPS) * gffn_ref[...]


def _mix(att, u, gate, x2, conv_w, conv_b, ln_g, ln_b, w_att_out, w_conv_out, w_out, ffn_g, S, tm=256):
    T, D = x2.shape
    aw = att.shape[1]
    const = lambda *shape: _single(shape, lambda i: (0,) * len(shape))
    hb = tm // HALO
    return pl.pallas_call(
        functools.partial(_mix_kernel, tm=tm, tiles_per_seq=S // tm),
        grid=(T // tm,),
        in_specs=[pl.BlockSpec((tm, aw), lambda i: (i, 0)),
                  pl.BlockSpec((tm, CONV_CH), lambda i: (i, 0)),
                  pl.BlockSpec((HALO, CONV_CH), lambda i: (jnp.maximum(i * hb - 1, 0), 0)),
                  pl.BlockSpec((tm, N_BRANCH * D), lambda i: (i, 0)),
                  pl.BlockSpec((tm, D), lambda i: (i, 0)),
                  const(CONV_WIDTH, CONV_CH), const(1, CONV_CH), const(1, CONV_CH), const(1, CONV_CH),
                  const(aw, D), const(CONV_CH, D), const(D, D), const(1, D)],
        out_specs=[pl.BlockSpec((tm, D), lambda i: (i, 0)), pl.BlockSpec((tm, D), lambda i: (i, 0))],
        out_shape=[jax.ShapeDtypeStruct((T, D), F32), jax.ShapeDtypeStruct((T, D), F32)],
        scratch_shapes=[pltpu.VMEM((HALO + tm, CONV_CH), F32)],
        compiler_params=_params(1),
        name="mix_out_proj",
    )(att, u, u, gate, x2, conv_w.reshape(CONV_WIDTH, CONV_CH), conv_b.reshape(1, CONV_CH),
      ln_g.reshape(1, CONV_CH), ln_b.reshape(1, CONV_CH), w_att_out.astype(BF16), w_conv_out.astype(BF16),
      w_out.astype(BF16), ffn_g.reshape(1, D))


def _top_rows(sc, k, payload=None):
    rows = sc.shape[0]
    iota = lax.broadcasted_iota(I32, sc.shape, 0)
    out_row = lax.broadcasted_iota(I32, (k, sc.shape[1]), 0)
    vals = jnp.zeros((k, sc.shape[1]), F32)
    idxs = jnp.zeros((k, sc.shape[1]), I32)
    for r in range(k):
        m = jnp.max(sc, axis=0, keepdims=True)
        idx = jnp.min(jnp.where(sc == m, iota, rows), axis=0, keepdims=True)
        hit = iota == idx
        rec = idx if payload is None else jnp.max(jnp.where(hit, payload, -1), axis=0, keepdims=True)
        vals = jnp.where(out_row == r, m, vals)
        idxs = jnp.where(out_row == r, rec, idxs)
        sc = jnp.where(hit, -jnp.inf, sc)
    return vals, idxs


def _route_kernel(h2_ref, wq_ref, sk_ref, e_ref, g_ref, qt_scr, et_scr, gt_scr, *, tm):
    half = N_KEYS
    qt_scr[...] = _nt_dot(wq_ref[...], h2_ref[...].astype(BF16)).astype(BF16)

    def head(h, carry):
        tops = []
        for c in range(2):
            row0 = pl.multiple_of((h * 2 + c) * half, half)
            sc = _dot(sk_ref[h * 2 + c], qt_scr[pl.ds(row0, half), :])
            tops.append(_top_rows(sc, PEER_TOPK))
        (a, ia), (b, ib) = tops
        k = PEER_TOPK
        g = SUBLANES
        assert (k // 2) % g == 0 and k % g == 0
        row = lax.broadcasted_iota(I32, (g, a.shape[1]), 0)
        cand_parts, cidx_parts = [], []

        def add(av, iav, bv, ibv, valid_rows):
            s = av + bv
            if valid_rows < g:
                s = jnp.where(row < valid_rows, s, -jnp.inf)
            cand_parts.append(s)
            cidx_parts.append(iav * N_KEYS + ibv)

        for i in range(k):
            nj = k // (i + 1)
            if nj >= g:
                for j0 in range(0, nj, g):
                    add(a[i:i + 1, :], ia[i:i + 1, :], b[j0:j0 + g, :], ib[j0:j0 + g, :], g)
            elif nj > 1:
                add(a[i:i + 1, :], ia[i:i + 1, :], b[0:g, :], ib[0:g, :], nj)
            elif i % g == 0:
                add(a[i:i + g, :], ia[i:i + g, :], b[0:1, :], ib[0:1, :], g)
        cand = jnp.concatenate(cand_parts, axis=0)
        cidx = jnp.concatenate(cidx_parts, axis=0)
        ts, te = _top_rows(cand, PEER_TOPK, payload=cidx)
        ex = jnp.exp(ts - ts[0:1, :])
        gate = ex / jnp.sum(ex, axis=0, keepdims=True)
        r0 = pl.multiple_of(h * PEER_TOPK, PEER_TOPK)
        et_scr[pl.ds(r0, PEER_TOPK), :] = te
        gt_scr[pl.ds(r0, PEER_TOPK), :] = gate
        return carry

    lax.fori_loop(0, PEER_HEADS, head, 0)
    e_ref[...] = et_scr[...].T
    g_ref[...] = gt_scr[...].T


def _peer_route(h2, w_peer_q, sub_keys, tm=256):
    T, D = h2.shape
    nsel = PEER_HEADS * PEER_TOPK
    qd = w_peer_q.shape[1]
    half = sub_keys.shape[-1]
    skb = sub_keys.reshape(PEER_HEADS * 2, N_KEYS, half).astype(BF16)
    return pl.pallas_call(
        functools.partial(_route_kernel, tm=tm),
        grid=(T // tm,),
        in_specs=[pl.BlockSpec((tm, D), lambda i: (i, 0)),
                  _single((qd, D), lambda i: (0, 0)),
                  _single((PEER_HEADS * 2, N_KEYS, half), lambda i: (0, 0, 0))],
        out_specs=[pl.BlockSpec((tm, nsel), lambda i: (i, 0)), pl.BlockSpec((tm, nsel), lambda i: (i, 0))],
        out_shape=[jax.ShapeDtypeStruct((T, nsel), I32), jax.ShapeDtypeStruct((T, nsel), F32)],
        scratch_shapes=[pltpu.VMEM((qd, tm), BF16), pltpu.VMEM((nsel, tm), I32), pltpu.VMEM((nsel, tm), F32)],
        compiler_params=_params(1),
        name="peer_route",
    )(h2, w_peer_q.T.astype(BF16), skb)


TOK = 8


def _gelu_tanh(x):
    return 0.5 * x * (1.0 + jnp.tanh(math.sqrt(2.0 / math.pi) * (x + 0.044715 * (x * x * x))))


def _sublane_sums(ps):
    sub = lax.broadcasted_iota(I32, (SUBLANES, LANES), 0)
    lvl, stride = list(ps), SUBLANES // 2
    while len(lvl) > 1:
        half = len(lvl) // 2
        low = (sub & stride) == 0
        nxt = []
        for n in range(half):
            a, b = lvl[n], lvl[n + half]
            nxt.append(jnp.where(low, a + pltpu.roll(a, SUBLANES - stride, 0), b + pltpu.roll(b, stride, 0)))
        lvl, stride = nxt, stride // 2
    return lvl[0]


def _pack_uv(u, v):
    ne, d = u.shape
    hi = lax.bitcast_convert_type(u.astype(BF16), jnp.uint16).astype(jnp.uint32)
    lo = lax.bitcast_convert_type(v.astype(BF16), jnp.uint16).astype(jnp.uint32)
    return ((hi << 16) | lo).reshape(ne, d // LANES, LANES)


def _u_of(word):
    return pltpu.bitcast(word & jnp.uint32(0xFFFF0000), F32)


def _v_of(word):
    return pltpu.bitcast(word << 16, F32)


def _expert_kernel(idxc_ref, idxn_ref, h2_ref, x1_ref, gate_ref, uv_ref, o_ref, buf, sem, abc_scr, *, nsel):
    i = pl.program_id(0)
    nsteps = pl.num_programs(0)
    rows = TOK * nsel
    dsub = h2_ref.shape[1]

    def issue_token(idx_ref, tok, s, t):
        for n in range(nsel):
            pltpu.make_async_copy(uv_ref.at[idx_ref[tok, n]], buf.at[s, t * nsel + n],
                                  sem.at[s]).start(priority=n % 2)

    def wait_slot(s):
        pltpu.make_async_copy(uv_ref.at[pl.ds(0, rows)], buf.at[s], sem.at[s]).wait()

    @pl.when(i == 0)
    def _():
        for t in range(TOK):
            issue_token(idxc_ref, t, 0, t)

    ones = jnp.ones((SUBLANES, LANES), F32)
    row_id = lax.broadcasted_iota(I32, (TOK, nsel), 0)

    for grp in range(2):
        base = grp * TOK
        wait_slot(grp)
        s8 = jnp.zeros((TOK, nsel), F32)
        for t in range(TOK):
            if grp == 0:
                issue_token(idxc_ref, TOK + t, 1, t)
            else:
                issue_token(idxn_ref, t, 0, t)
            xt = h2_ref[base + t]
            qs = []
            for g in range(nsel // SUBLANES):
                ps = [_u_of(buf[grp, t * nsel + g * SUBLANES + k]) * xt for k in range(SUBLANES)]
                qs.append(_sublane_sums(ps))
            q = jnp.concatenate(qs, axis=0)
            srow = _nt_dot(ones, q, precision=lax.Precision.HIGHEST)
            s8 = jnp.where(row_id == t, srow[0:TOK, :], s8)
        a8 = _gelu_tanh(s8) * gate_ref[base:base + TOK, :]
        a8t = jnp.concatenate([a8, jnp.zeros((nsel - TOK, nsel), F32)], axis=0).T

        for t in range(TOK):
            abc_scr[t] = jnp.broadcast_to(a8t[:, t:t + 1], (nsel, LANES))
            accs = [jnp.zeros((dsub, LANES), F32) for _ in range(4)]
            for n in range(nsel):
                accs[n % 4] = accs[n % 4] + abc_scr[t, n:n + 1, :] * _v_of(buf[grp, t * nsel + n])
            o_ref[base + t] = x1_ref[base + t] + ((accs[0] + accs[1]) + (accs[2] + accs[3]))

    @pl.when(i == nsteps - 1)
    def _():
        wait_slot(0)


def _peer_experts(eidx, gates, h2, x1, peer_u, peer_v):
    T, D = x1.shape
    nsel = eidx.shape[1]
    dsub = D // LANES
    ne = peer_u.shape[0]
    uv = _pack_uv(peer_u, peer_v)
    tb = 2 * TOK
    nsteps = T // tb
    out = pl.pallas_call(
        functools.partial(_expert_kernel, nsel=nsel),
        grid=(nsteps,),
        in_specs=[pl.BlockSpec((tb, nsel), lambda i: (i, 0), memory_space=pltpu.SMEM),
                  pl.BlockSpec((tb, nsel), lambda i: (jnp.minimum(i + 1, nsteps - 1), 0), memory_space=pltpu.SMEM),
                  pl.BlockSpec((tb, dsub, LANES), lambda i: (i, 0, 0)),
                  pl.BlockSpec((tb, dsub, LANES), lambda i: (i, 0, 0)),
                  pl.BlockSpec((tb, nsel), lambda i: (i, 0)),
                  pl.BlockSpec(memory_space=pl.ANY)],
        out_specs=pl.BlockSpec((tb, dsub, LANES), lambda i: (i, 0, 0)),
        out_shape=jax.ShapeDtypeStruct((T, dsub, LANES), F32),
        scratch_shapes=[pltpu.VMEM((2, TOK * nsel, dsub, LANES), jnp.uint32), pltpu.SemaphoreType.DMA((2,)),
                        pltpu.VMEM((TOK, nsel, LANES), F32)],
        compiler_params=_params(1),
        name="peer_experts",
    )(eidx, eidx, h2.reshape(T, dsub, LANES), x1.reshape(T, dsub, LANES), gates, uv)
    return out.reshape(T, D)


def _ple_kernel(x_ref, p_ref, g_ref, wg_ref, wp_ref, o_ref):
    x = x_ref[...]
    h = x * lax.rsqrt(jnp.mean(x * x, axis=-1, keepdims=True) + EPS) * g_ref[...]
    gate = jax.nn.sigmoid(_dot(h.astype(BF16), wg_ref[...]))
    o_ref[...] = x + gate * _dot(p_ref[...].astype(BF16), wp_ref[...])


def _ple(x2, p2, ple_g, w_gate, w_proj, tm=512):
    T, D = x2.shape
    pd = p2.shape[1]
    return pl.pallas_call(
        _ple_kernel,
        grid=(T // tm,),
        in_specs=[pl.BlockSpec((tm, D), lambda i: (i, 0)), pl.BlockSpec((tm, pd), lambda i: (i, 0)),
                  _single((1, D), lambda i: (0, 0)), _single((D, D), lambda i: (0, 0)),
                  _single((pd, D), lambda i: (0, 0))],
        out_specs=pl.BlockSpec((tm, D), lambda i: (i, 0)),
        out_shape=jax.ShapeDtypeStruct((T, D), F32),
        compiler_params=_params(1),
        name="ple",
    )(x2, p2, ple_g.reshape(1, D), w_gate.astype(BF16), w_proj.astype(BF16))


def kernel(x, p, rel_bias, attn_norm_g, w_in, b_gate, q_norm_g, k_norm_g, w_att_out, conv_w, conv_b, conv_ln_g,
           conv_ln_b, w_conv_out, w_out, ffn_norm_g, w_peer_q, peer_sub_keys, peer_u, peer_v, ple_norm_g,
           w_ple_gate, w_ple_proj):
    B, S, D = x.shape
    depth = w_in.shape[0]
    T = B * S
    assert S % KQUAD == 0 and D % LANES == 0 and T % 512 == 0
    n_near = _num_near_tiles(S)
    bias = _bias_tiles(rel_bias, n_near)
    x2 = x.reshape(T, D)
    for i in range(depth):
        q, k, qi, ki, vt, wit, u, gate = _in_proj(x2, attn_norm_g[i], w_in[i], b_gate[i], q_norm_g[i], k_norm_g[i])
        att = _attention(q, k, qi, ki, vt, wit, bias, B, S, n_near)
        x1, h2 = _mix(att, u, gate, x2, conv_w[i], conv_b[i], conv_ln_g[i], conv_ln_b[i], w_att_out[i],
                      w_conv_out[i], w_out[i], ffn_norm_g[i], S)
        eidx, gates = _peer_route(h2, w_peer_q[i], peer_sub_keys[i])
        x2 = _peer_experts(eidx, gates, h2, x1, peer_u[i], peer_v[i])
        x2 = _ple(x2, p[i].reshape(T, -1), ple_norm_g[i], w_ple_gate[i], w_ple_proj[i])
    return x2.reshape(B, S, D)
```

```python
import functools
import math

import numpy as np
import jax
import jax.numpy as jnp
from jax import lax
from jax.experimental import pallas as pl
from jax.experimental.pallas import tpu as pltpu

CHUNK = 64
ATT_HEADS = 8
ATT_HEAD_DIM = 64
IDX_HEADS = 8
IDX_DIM = 64
TOPK_MAX = 256
REL_BUCKETS = 32
REL_MAX_DIST = 1024
CONV_CH = 512
CONV_WIDTH = 31
N_BRANCH = 2
PEER_HEADS = 8
N_KEYS = 128
PEER_TOPK = 16
EPS = 1e-6

LANES = 128
SUBLANES = 8
VMEM_LIMIT = 56 * 1024 * 1024

QB = 128
KPAIR = 2 * QB
KQUAD = 4 * QB
INT_MIN = -(2 ** 31)
NEG_BIG = -1e30
LOG2E = math.log2(math.e)

F32 = jnp.float32
BF16 = jnp.bfloat16
I32 = jnp.int32


def _nt_dot(a, b, precision=None):
    return lax.dot_general(a, b, (((1,), (1,)), ((), ())), precision=precision,
                           preferred_element_type=F32)


def _dot(a, b):
    return jnp.dot(a, b, preferred_element_type=F32)


def _single(shape, index_map):
    return pl.BlockSpec(shape, index_map)


def _params(n_grid_dims):
    return pltpu.CompilerParams(dimension_semantics=("arbitrary",) * n_grid_dims,
                                vmem_limit_bytes=VMEM_LIMIT)


def _t5_bucket_np(rel):
    half = REL_BUCKETS // 2
    max_exact = half // 2
    ret = np.where(rel > 0, half, 0)
    n = np.abs(rel)
    nf = np.maximum(n, 1).astype(np.float32)
    large = max_exact + (np.log(nf / np.float32(max_exact)) / np.float32(math.log(REL_MAX_DIST / max_exact))
                         * np.float32(half - max_exact)).astype(np.int32)
    large = np.minimum(large, half - 1)
    return ret + np.where(n < max_exact, n, large)


def _num_near_tiles(seq):
    n = np.arange(1, max(seq, 2 * REL_MAX_DIST) + 1)
    b = _t5_bucket_np(-n)
    sat = REL_BUCKETS // 2 - 1
    unsat = np.nonzero(b != sat)[0]
    n_sat = int(n[unsat[-1]]) + 1 if unsat.size else 1
    return -(-(n_sat + QB - 1) // QB)


def _bias_kernel(rb_ref, o_ref, *, n_near):
    d = pl.program_id(0)
    i = lax.broadcasted_iota(I32, (QB, QB), 0)
    j = lax.broadcasted_iota(I32, (QB, QB), 1)
    rel = i - j - d * QB
    rel = jnp.where(d >= n_near, -8 * REL_MAX_DIST, rel)
    half = REL_BUCKETS // 2
    max_exact = half // 2
    ret = jnp.where(rel > 0, half, 0)
    n = jnp.abs(rel)
    nf = jnp.maximum(n, 1).astype(F32)
    large = max_exact + (jnp.log(nf / max_exact) / math.log(REL_MAX_DIST / max_exact)
                         * (half - max_exact)).astype(I32)
    large = jnp.minimum(large, half - 1)
    bucket = ret + jnp.where(n < max_exact, n, large)
    for h in range(ATT_HEADS):
        acc = jnp.zeros((QB, QB), F32)
        for b in range(REL_BUCKETS):
            acc = jnp.where(bucket == b, rb_ref[b, h], acc)
        o_ref[0, h] = acc * LOG2E


def _bias_tiles(rel_bias, n_near):
    return pl.pallas_call(
        functools.partial(_bias_kernel, n_near=n_near),
        grid=(n_near + 1,),
        in_specs=[pl.BlockSpec(memory_space=pltpu.SMEM)],
        out_specs=pl.BlockSpec((1, ATT_HEADS, QB, QB), lambda d: (d, 0, 0, 0)),
        out_shape=jax.ShapeDtypeStruct((n_near + 1, ATT_HEADS, QB, QB), F32),
        compiler_params=_params(1),
        name="bias_tiles",
    )(rel_bias)


def _inproj_kernel(x_ref, g_ref, wa_ref, wvt_ref, wwit_ref, wglu_ref, wgate_ref, bgate_ref, gq_ref, gk_ref,
                   q_ref, k_ref, qi_ref, ki_ref, vt_ref, wit_ref, u_ref, gate_ref, *, tm):
    x = x_ref[...]
    h = x * lax.rsqrt(jnp.mean(x * x, axis=-1, keepdims=True) + EPS) * g_ref[...]
    hb = h.astype(BF16)
    hp = ATT_HEADS * LANES
    ya = _dot(hb, wa_ref[...])
    for h_i in range(ATT_HEADS):
        sl = slice(h_i * LANES, (h_i + 1) * LANES)
        qh = ya[:, h_i * LANES:(h_i + 1) * LANES]
        ms = jnp.sum(qh * qh, axis=-1, keepdims=True) * (1.0 / ATT_HEAD_DIM)
        q_ref[:, sl] = (qh * lax.rsqrt(ms + EPS) * gq_ref[...]).astype(BF16)
        kh = ya[:, hp + h_i * LANES:hp + (h_i + 1) * LANES]
        ms = jnp.sum(kh * kh, axis=-1, keepdims=True) * (1.0 / ATT_HEAD_DIM)
        k_ref[:, sl] = (kh * lax.rsqrt(ms + EPS) * gk_ref[...]).astype(BF16)
    qi_ref[...] = ya[:, 2 * hp:3 * hp].astype(BF16)
    ki_ref[...] = ya[:, 3 * hp:3 * hp + LANES].astype(BF16)
    vt = _nt_dot(wvt_ref[...], hb).astype(BF16)
    for c in range(tm // KPAIR):
        vt_ref[c] = vt[:, c * KPAIR:(c + 1) * KPAIR]
    wit_ref[...] = _nt_dot(wwit_ref[...], hb)
    glu = _dot(hb, wglu_ref[...])
    u_ref[...] = glu[:, :CONV_CH] * jax.nn.sigmoid(glu[:, CONV_CH:])
    gate_ref[...] = jax.nn.sigmoid(_dot(hb, wgate_ref[...]) + bgate_ref[...]).astype(BF16)


def _pad_heads(w, nh, hd):
    d = w.shape[0]
    w3 = w.reshape(d, nh, hd)
    w3 = jnp.pad(w3, ((0, 0), (0, 0), (0, LANES - hd)))
    return w3.reshape(d, nh * LANES)


def _in_proj(x2, attn_g, w_in, b_gate, q_g, k_g, tm=256):
    T, D = x2.shape
    aw = ATT_HEADS * ATT_HEAD_DIM
    iw = IDX_HEADS * IDX_DIM
    o = 0
    wq = w_in[:, o:o + aw]; o += aw
    wk = w_in[:, o:o + aw]; o += aw
    wv = w_in[:, o:o + aw]; o += aw
    wqi = w_in[:, o:o + iw]; o += iw
    wki = w_in[:, o:o + IDX_DIM]; o += IDX_DIM
    wwi = w_in[:, o:o + IDX_HEADS]; o += IDX_HEADS
    wglu = w_in[:, o:o + 2 * CONV_CH]; o += 2 * CONV_CH
    wgate = w_in[:, o:o + N_BRANCH * D]
    wa = jnp.concatenate([_pad_heads(wq, ATT_HEADS, ATT_HEAD_DIM), _pad_heads(wk, ATT_HEADS, ATT_HEAD_DIM),
                          _pad_heads(wqi, IDX_HEADS, IDX_DIM),
                          jnp.pad(wki, ((0, 0), (0, LANES - IDX_DIM)))], axis=1).astype(BF16)
    na = wa.shape[1]
    hp = ATT_HEADS * LANES
    pad_g = lambda g, s: jnp.pad(g * s, (0, LANES - ATT_HEAD_DIM)).reshape(1, LANES)
    gq = pad_g(q_g, ATT_HEAD_DIM ** -0.5 * LOG2E)
    gk = pad_g(k_g, 1.0)
    const = lambda *shape: _single(shape, lambda i: (0,) * len(shape))
    outs = pl.pallas_call(
        functools.partial(_inproj_kernel, tm=tm),
        grid=(T // tm,),
        in_specs=[pl.BlockSpec((tm, D), lambda i: (i, 0)), const(1, D), const(D, na), const(aw, D),
                  const(IDX_HEADS, D), const(D, 2 * CONV_CH), const(D, N_BRANCH * D), const(1, N_BRANCH * D),
                  const(1, LANES), const(1, LANES)],
        out_specs=[pl.BlockSpec((tm, hp), lambda i: (i, 0)), pl.BlockSpec((tm, hp), lambda i: (i, 0)),
                   pl.BlockSpec((tm, hp), lambda i: (i, 0)), pl.BlockSpec((tm, LANES), lambda i: (i, 0)),
                   pl.BlockSpec((tm // KPAIR, aw, KPAIR), lambda i: (i, 0, 0)),
                   pl.BlockSpec((IDX_HEADS, tm), lambda i: (0, i)),
                   pl.BlockSpec((tm, CONV_CH), lambda i: (i, 0)),
                   pl.BlockSpec((tm, N_BRANCH * D), lambda i: (i, 0))],
        out_shape=[jax.ShapeDtypeStruct((T, hp), BF16), jax.ShapeDtypeStruct((T, hp), BF16),
                   jax.ShapeDtypeStruct((T, hp), BF16), jax.ShapeDtypeStruct((T, LANES), BF16),
                   jax.ShapeDtypeStruct((T // KPAIR, aw, KPAIR), BF16),
                   jax.ShapeDtypeStruct((IDX_HEADS, T), F32),
                   jax.ShapeDtypeStruct((T, CONV_CH), F32),
                   jax.ShapeDtypeStruct((T, N_BRANCH * D), BF16)],
        compiler_params=_params(1),
        name="in_proj",
    )(x2, attn_g.reshape(1, D), wa, wv.T.astype(BF16), wwi.T.astype(BF16), wglu.astype(BF16),
      wgate.astype(BF16), b_gate.reshape(1, N_BRANCH * D), gq, gk)
    return outs


def _attn_kernel(q_ref, qi_ref, wit_ref, k_hbm, ki_hbm, vt_hbm, bias_hbm, o_ref, k_ref, ki_ref, vt_ref, bias_ref,
                 load_sem, key_scr, att_scr, *scr, seq, n_near, topk):
    qb = pl.program_id(1)

    @pl.when(qb == 0)
    def _():
        b = pl.program_id(0)
        loads = [pltpu.make_async_copy(k_hbm.at[b], k_ref, load_sem.at[0]),
                 pltpu.make_async_copy(ki_hbm.at[b], ki_ref, load_sem.at[1]),
                 pltpu.make_async_copy(vt_hbm.at[b], vt_ref, load_sem.at[2]),
                 pltpu.make_async_copy(bias_hbm, bias_ref, load_sem.at[3])]
        for c in loads:
            c.start()
        for c in loads:
            c.wait()

    nquad = (qb + 4) // 4
    lane_t = lax.broadcasted_iota(I32, (1, QB), 1) + qb * QB
    qchunk = lane_t // CHUNK
    sub = lax.broadcasted_iota(I32, (KQUAD, QB), 0)
    idx_scale = (IDX_DIM ** -0.5) * (IDX_HEADS ** -0.5)
    wrow = wit_ref[...] * idx_scale

    qi_all = jnp.concatenate([qi_ref[:, h * LANES:(h + 1) * LANES] for h in range(IDX_HEADS)], axis=0)

    def score_quad(j, carry):
        d = _nt_dot(ki_ref[j], qi_all)
        acc = jnp.zeros((KQUAD, QB), F32)
        for h in range(IDX_HEADS):
            acc = acc + jnp.maximum(d[:, h * QB:(h + 1) * QB], 0.0) * wrow[h:h + 1, :]
        bits = pltpu.bitcast(acc, I32)
        skey = bits ^ ((bits >> 31) & 0x7FFFFFFF)
        visible = (sub + j * KQUAD) // CHUNK <= qchunk
        key_scr[j] = jnp.where(visible, skey, INT_MIN)
        return carry

    lax.fori_loop(0, nquad, score_quad, 0)

    nvis = (qchunk + 1) * CHUNK
    kk = jnp.minimum(topk, nvis)

    def count(pred):
        def body(j, acc8):
            m = jnp.where(pred(key_scr[j], j), 1, 0)
            return acc8 + m.reshape(KQUAD // SUBLANES, SUBLANES, QB).sum(axis=0)
        return lax.fori_loop(0, nquad, body, jnp.zeros((SUBLANES, QB), I32)).sum(axis=0, keepdims=True)

    def bit_step(i, carry):
        ans, cnt = carry
        cand = ans + lax.shift_left(jnp.int32(1), 31 - i)
        c = count(lambda blk, j: blk >= cand)
        take = c >= kk
        return jnp.where(take, cand, ans), jnp.where(take, c, cnt)

    thr, cnt = lax.fori_loop(0, 32, bit_step,
                             (jnp.full((1, QB), INT_MIN, I32), jnp.full((1, QB), 0, I32) + nquad * KQUAD))

    @pl.when(jnp.max(cnt - kk) > 0)
    def _():
        n_gt = count(lambda blk, j: blk > thr)
        need = kk - n_gt

        def idx_step(i, jbound):
            cand = jbound + lax.shift_left(jnp.int32(1), (seq.bit_length() - 1) - i)
            c = count(lambda blk, j: (blk == thr) & (sub + j * KQUAD < cand))
            return jnp.where(c <= need, cand, jbound)

        jbound = lax.fori_loop(0, seq.bit_length(), idx_step, jnp.zeros((1, QB), I32))

        def drop(j, carry):
            blk = key_scr[j]
            key_scr[j] = jnp.where((blk == thr) & (sub + j * KQUAD >= jbound), INT_MIN, blk)
            return carry

        lax.fori_loop(0, nquad, drop, 0)

    acc_refs, s_even, s_odd = scr[:ATT_HEADS], scr[ATT_HEADS], scr[ATT_HEADS + 1]
    for acc_ref in acc_refs:
        acc_ref[...] = jnp.zeros(acc_ref.shape, F32)
    head_row = lax.broadcasted_iota(I32, (ATT_HEADS, QB), 0)
    last_pair = k_ref.shape[0] - 1

    def qk(jp, s_ref):
        jc = jnp.minimum(jp, last_pair)
        for h in range(ATT_HEADS):
            s_ref[h] = _nt_dot(k_ref[jc, :, h * LANES:(h + 1) * LANES], q_ref[:, h * LANES:(h + 1) * LANES])

    def softmax_pv(jq, half, s_ref, carry):
        m_all, l_all = carry
        jp = 2 * jq + half
        keys = key_scr[jq, half * KPAIR:(half + 1) * KPAIR, :]
        masked = jnp.where(keys >= thr, 0.0, -jnp.inf)
        tile0 = jnp.clip(qb - 2 * jp, 0, n_near)
        tile1 = jnp.clip(qb - 2 * jp - 1, 0, n_near)
        for h in range(ATT_HEADS):
            rows = slice(h * ATT_HEAD_DIM, (h + 1) * ATT_HEAD_DIM)
            bias = jnp.concatenate([bias_ref[tile0, h], bias_ref[tile1, h]], axis=0)
            s = s_ref[h] + bias + masked
            m = m_all[h:h + 1, :]
            m_new = jnp.maximum(m, jnp.max(s, axis=0, keepdims=True))
            p = jnp.exp2(s - m_new)
            alpha = jnp.exp2(m - m_new)
            l_new = alpha * l_all[h:h + 1, :] + jnp.sum(p, axis=0, keepdims=True)
            m_all = jnp.where(head_row == h, m_new, m_all)
            l_all = jnp.where(head_row == h, l_new, l_all)
            acc_refs[h][...] = alpha * acc_refs[h][...] + _dot(vt_ref[jp, rows, :], p.astype(BF16))
        return m_all, l_all

    def att_quad(jq, carry):
        qk(2 * jq + 1, s_odd)
        carry = softmax_pv(jq, 0, s_even, carry)
        qk(2 * jq + 2, s_even)
        return softmax_pv(jq, 1, s_odd, carry)

    qk(0, s_even)
    _, l_all = lax.fori_loop(0, nquad, att_quad,
                             (jnp.full((ATT_HEADS, QB), NEG_BIG, F32), jnp.zeros((ATT_HEADS, QB), F32)))
    for h in range(ATT_HEADS):
        rows = slice(h * ATT_HEAD_DIM, (h + 1) * ATT_HEAD_DIM)
        att_scr[rows, :] = acc_refs[h][...] / l_all[h:h + 1, :]
    o_ref[...] = att_scr[...].T.astype(o_ref.dtype)


def _attention(q, k, qi, ki, vt, wit, bias, B, S, n_near):
    T = B * S
    nqb = S // QB
    hp = ATT_HEADS * LANES
    aw = ATT_HEADS * ATT_HEAD_DIM
    topk = min(TOPK_MAX, S // 4)
    npr, nqd = S // KPAIR, S // KQUAD
    k4 = k.reshape(B, npr, KPAIR, hp)
    ki4 = ki.reshape(B, nqd, KQUAD, LANES)
    vt4 = vt.reshape(B, npr, aw, KPAIR)
    return pl.pallas_call(
        functools.partial(_attn_kernel, seq=S, n_near=n_near, topk=topk),
        grid=(B, nqb),
        in_specs=[pl.BlockSpec((QB, hp), lambda b, i: (b * nqb + i, 0)),
                  pl.BlockSpec((QB, hp), lambda b, i: (b * nqb + i, 0)),
                  pl.BlockSpec((IDX_HEADS, QB), lambda b, i: (0, b * nqb + i)),
                  pl.BlockSpec(memory_space=pl.ANY), pl.BlockSpec(memory_space=pl.ANY),
                  pl.BlockSpec(memory_space=pl.ANY), pl.BlockSpec(memory_space=pl.ANY)],
        out_specs=pl.BlockSpec((QB, aw), lambda b, i: (b * nqb + i, 0)),
        out_shape=jax.ShapeDtypeStruct((T, aw), BF16),
        scratch_shapes=[pltpu.VMEM((npr, KPAIR, hp), BF16), pltpu.VMEM((nqd, KQUAD, LANES), BF16),
                        pltpu.VMEM((npr, aw, KPAIR), BF16), pltpu.VMEM((n_near + 1, ATT_HEADS, QB, QB), F32),
                        pltpu.SemaphoreType.DMA((4,)),
                        pltpu.VMEM((nqd, KQUAD, QB), I32), pltpu.VMEM((aw, QB), F32)]
        + [pltpu.VMEM((ATT_HEAD_DIM, QB), F32) for _ in range(ATT_HEADS)]
        + [pltpu.VMEM((ATT_HEADS, KPAIR, QB), F32) for _ in range(2)],
        compiler_params=_params(2),
        name="dsa_attention",
    )(q, qi, wit, k4, ki4, vt4, bias)


HALO = 32


def _mix_kernel(att_ref, u_ref, halo_ref, gate_ref, x_ref, cw_ref, cb_ref, lng_ref, lnb_ref, wao_ref, wco_ref,
                wout_ref, gffn_ref, x1_ref, h2_ref, ext_scr, *, tm, tiles_per_seq):
    i = pl.program_id(0)
    first = (i % tiles_per_seq) == 0
    ext_scr[0:HALO, :] = jnp.where(first, 0.0, halo_ref[...])
    ext_scr[HALO:HALO + tm, :] = u_ref[...]
    y = jnp.zeros((tm, CONV_CH), F32)
    for j in range(CONV_WIDTH):
        y = y + cw_ref[j:j + 1, :] * ext_scr[pl.ds(HALO - (CONV_WIDTH - 1) + j, tm), :]
    y = y + cb_ref[...]
    mu = jnp.mean(y, axis=-1, keepdims=True)
    yc = y - mu
    yn = yc * lax.rsqrt(jnp.mean(yc * yc, axis=-1, keepdims=True) + EPS) * lng_ref[...] + lnb_ref[...]
    z = yn * jax.nn.sigmoid(yn)
    y_conv = _dot(z.astype(BF16), wco_ref[...])
    y_att = _dot(att_ref[...], wao_ref[...])
    d = y_att.shape[1]
    g = gate_ref[...]
    mixed = g[:, :d].astype(F32) * y_att + g[:, d:].astype(F32) * y_conv
    x1 = x_ref[...] + _dot(mixed.astype(BF16), wout_ref[...])
    x1_ref[...] = x1
    h2_ref[...] = x1 * lax.rsqrt(jnp.mean(x1 * x1, axis=-1, keepdims=True) + EPS) * gffn_ref[...]


def _mix(att, u, gate, x2, conv_w, conv_b, ln_g, ln_b, w_att_out, w_conv_out, w_out, ffn_g, S, tm=256):
    T, D = x2.shape
    aw = att.shape[1]
    const = lambda *shape: _single(shape, lambda i: (0,) * len(shape))
    hb = tm // HALO
    return pl.pallas_call(
        functools.partial(_mix_kernel, tm=tm, tiles_per_seq=S // tm),
        grid=(T // tm,),
        in_specs=[pl.BlockSpec((tm, aw), lambda i: (i, 0)),
                  pl.BlockSpec((tm, CONV_CH), lambda i: (i, 0)),
                  pl.BlockSpec((HALO, CONV_CH), lambda i: (jnp.maximum(i * hb - 1, 0), 0)),
                  pl.BlockSpec((tm, N_BRANCH * D), lambda i: (i, 0)),
                  pl.BlockSpec((tm, D), lambda i: (i, 0)),
                  const(CONV_WIDTH, CONV_CH), const(1, CONV_CH), const(1, CONV_CH), const(1, CONV_CH),
                  const(aw, D), const(CONV_CH, D), const(D, D), const(1, D)],
        out_specs=[pl.BlockSpec((tm, D), lambda i: (i, 0)), pl.BlockSpec((tm, D), lambda i: (i, 0))],
        out_shape=[jax.ShapeDtypeStruct((T, D), F32), jax.ShapeDtypeStruct((T, D), F32)],
        scratch_shapes=[pltpu.VMEM((HALO + tm, CONV_CH), F32)],
        compiler_params=_params(1),
        name="mix_out_proj",
    )(att, u, u, gate, x2, conv_w.reshape(CONV_WIDTH, CONV_CH), conv_b.reshape(1, CONV_CH),
      ln_g.reshape(1, CONV_CH), ln_b.reshape(1, CONV_CH), w_att_out.astype(BF16), w_conv_out.astype(BF16),
      w_out.astype(BF16), ffn_g.reshape(1, D))


def _top_rows(sc, k, payload=None):
    rows = sc.shape[0]
    iota = lax.broadcasted_iota(I32, sc.shape, 0)
    out_row = lax.broadcasted_iota(I32, (k, sc.shape[1]), 0)
    vals = jnp.zeros((k, sc.shape[1]), F32)
    idxs = jnp.zeros((k, sc.shape[1]), I32)
    for r in range(k):
        m = jnp.max(sc, axis=0, keepdims=True)
        idx = jnp.min(jnp.where(sc == m, iota, rows), axis=0, keepdims=True)
        hit = iota == idx
        rec = idx if payload is None else jnp.max(jnp.where(hit, payload, -1), axis=0, keepdims=True)
        vals = jnp.where(out_row == r, m, vals)
        idxs = jnp.where(out_row == r, rec, idxs)
        sc = jnp.where(hit, -jnp.inf, sc)
    return vals, idxs


def _route_kernel(h2_ref, wq_ref, sk_ref, e_ref, g_ref, qt_scr, et_scr, gt_scr, *, tm):
    half = N_KEYS
    qt_scr[...] = _nt_dot(wq_ref[...], h2_ref[...].astype(BF16)).astype(BF16)

    def head(h, carry):
        tops = []
        for c in range(2):
            row0 = pl.multiple_of((h * 2 + c) * half, half)
            sc = _dot(sk_ref[h * 2 + c], qt_scr[pl.ds(row0, half), :])
            tops.append(_top_rows(sc, PEER_TOPK))
        (a, ia), (b, ib) = tops
        k = PEER_TOPK
        g = SUBLANES
        assert (k // 2) % g == 0 and k % g == 0
        row = lax.broadcasted_iota(I32, (g, a.shape[1]), 0)
        cand_parts, cidx_parts = [], []

        def add(av, iav, bv, ibv, valid_rows):
            s = av + bv
            if valid_rows < g:
                s = jnp.where(row < valid_rows, s, -jnp.inf)
            cand_parts.append(s)
            cidx_parts.append(iav * N_KEYS + ibv)

        for i in range(k):
            nj = k // (i + 1)
            if nj >= g:
                for j0 in range(0, nj, g):
                    add(a[i:i + 1, :], ia[i:i + 1, :], b[j0:j0 + g, :], ib[j0:j0 + g, :], g)
            elif nj > 1:
                add(a[i:i + 1, :], ia[i:i + 1, :], b[0:g, :], ib[0:g, :], nj)
            elif i % g == 0:
                add(a[i:i + g, :], ia[i:i + g, :], b[0:1, :], ib[0:1, :], g)
        cand = jnp.concatenate(cand_parts, axis=0)
        cidx = jnp.concatenate(cidx_parts, axis=0)
        ts, te = _top_rows(cand, PEER_TOPK, payload=cidx)
        ex = jnp.exp(ts - ts[0:1, :])
        gate = ex / jnp.sum(ex, axis=0, keepdims=True)
        r0 = pl.multiple_of(h * PEER_TOPK, PEER_TOPK)
        et_scr[pl.ds(r0, PEER_TOPK), :] = te
        gt_scr[pl.ds(r0, PEER_TOPK), :] = gate
        return carry

    lax.fori_loop(0, PEER_HEADS, head, 0)
    e_ref[...] = et_scr[...].T
    g_ref[...] = gt_scr[...].T


def _peer_route(h2, w_peer_q, sub_keys, tm=256):
    T, D = h2.shape
    nsel = PEER_HEADS * PEER_TOPK
    qd = w_peer_q.shape[1]
    half = sub_keys.shape[-1]
    skb = sub_keys.reshape(PEER_HEADS * 2, N_KEYS, half).astype(BF16)
    return pl.pallas_call(
        functools.partial(_route_kernel, tm=tm),
        grid=(T // tm,),
        in_specs=[pl.BlockSpec((tm, D), lambda i: (i, 0)),
                  _single((qd, D), lambda i: (0, 0)),
                  _single((PEER_HEADS * 2, N_KEYS, half), lambda i: (0, 0, 0))],
        out_specs=[pl.BlockSpec((tm, nsel), lambda i: (i, 0)), pl.BlockSpec((tm, nsel), lambda i: (i, 0))],
        out_shape=[jax.ShapeDtypeStruct((T, nsel), I32), jax.ShapeDtypeStruct((T, nsel), F32)],
        scratch_shapes=[pltpu.VMEM((qd, tm), BF16), pltpu.VMEM((nsel, tm), I32), pltpu.VMEM((nsel, tm), F32)],
        compiler_params=_params(1),
        name="peer_route",
    )(h2, w_peer_q.T.astype(BF16), skb)


TOK = 8


def _gelu_tanh(x):
    return 0.5 * x * (1.0 + jnp.tanh(math.sqrt(2.0 / math.pi) * (x + 0.044715 * (x * x * x))))


def _sublane_sums(ps):
    sub = lax.broadcasted_iota(I32, (SUBLANES, LANES), 0)
    lvl, stride = list(ps), SUBLANES // 2
    while len(lvl) > 1:
        half = len(lvl) // 2
        low = (sub & stride) == 0
        nxt = []
        for n in range(half):
            a, b = lvl[n], lvl[n + half]
            nxt.append(jnp.where(low, a + pltpu.roll(a, SUBLANES - stride, 0), b + pltpu.roll(b, stride, 0)))
        lvl, stride = nxt, stride // 2
    return lvl[0]


def _pack_uv(u, v):
    ne, d = u.shape
    hi = lax.bitcast_convert_type(u.astype(BF16), jnp.uint16).astype(jnp.uint32)
    lo = lax.bitcast_convert_type(v.astype(BF16), jnp.uint16).astype(jnp.uint32)
    return ((hi << 16) | lo).reshape(ne, d // LANES, LANES)


def _u_of(word):
    return pltpu.bitcast(word & jnp.uint32(0xFFFF0000), F32)


def _v_of(word):
    return pltpu.bitcast(word << 16, F32)


def _expert_kernel(idxc_ref, idxn_ref, h2_ref, x1_ref, gate_ref, uv_ref, o_ref, buf, sem, abc_scr, *, nsel):
    i = pl.program_id(0)
    nsteps = pl.num_programs(0)
    rows = TOK * nsel
    dsub = h2_ref.shape[1]

    def issue_token(idx_ref, tok, s, t, n0=0, n1=nsel):
        for n in range(n0, n1):
            pltpu.make_async_copy(uv_ref.at[idx_ref[tok, n]], buf.at[s, t * nsel + n],
                                  sem.at[s]).start(priority=n % 2)

    def wait_slot(s):
        pltpu.make_async_copy(uv_ref.at[pl.ds(0, rows)], buf.at[s], sem.at[s]).wait()

    @pl.when(i == 0)
    def _():
        for t in range(TOK):
            issue_token(idxc_ref, t, 0, t)

    lane_id = lax.broadcasted_iota(I32, (nsel, LANES), 1)
    n_early = (3 * nsel) // 4

    for grp in range(2):
        base = grp * TOK
        wait_slot(grp)
        nxt_idx, nxt_tok, nxt_slot = (idxc_ref, TOK, 1) if grp == 0 else (idxn_ref, 0, 0)
        st = jnp.zeros((nsel, LANES), F32)
        for t in range(TOK):
            issue_token(nxt_idx, nxt_tok + t, nxt_slot, t, 0, n_early)
            xt = h2_ref[base + t]
            qs = []
            for g in range(nsel // SUBLANES):
                ps = [_u_of(buf[grp, t * nsel + g * SUBLANES + k]) * xt for k in range(SUBLANES)]
                qs.append(_sublane_sums(ps))
            q = jnp.concatenate(qs, axis=0)
            st = jnp.where(lane_id == t, jnp.sum(q, axis=1, keepdims=True), st)
        g8 = gate_ref[base:base + TOK, :]
        gt = jnp.concatenate([g8, jnp.zeros((nsel - TOK, nsel), F32)], axis=0).T
        at = _gelu_tanh(st) * gt

        for t in range(TOK):
            issue_token(nxt_idx, nxt_tok + t, nxt_slot, t, n_early, nsel)
            abc_scr[t] = jnp.broadcast_to(at[:, t:t + 1], (nsel, LANES))
            accs = [jnp.zeros((dsub, LANES), F32) for _ in range(4)]
            for n in range(nsel):
                accs[n % 4] = accs[n % 4] + abc_scr[t, n:n + 1, :] * _v_of(buf[grp, t * nsel + n])
            o_ref[base + t] = x1_ref[base + t] + ((accs[0] + accs[1]) + (accs[2] + accs[3]))

    @pl.when(i == nsteps - 1)
    def _():
        wait_slot(0)


def _peer_experts(eidx, gates, h2, x1, peer_u, peer_v):
    T, D = x1.shape
    nsel = eidx.shape[1]
    dsub = D // LANES
    ne = peer_u.shape[0]
    uv = _pack_uv(peer_u, peer_v)
    tb = 2 * TOK
    nsteps = T // tb
    out = pl.pallas_call(
        functools.partial(_expert_kernel, nsel=nsel),
        grid=(nsteps,),
        in_specs=[pl.BlockSpec((tb, nsel), lambda i: (i, 0), memory_space=pltpu.SMEM),
                  pl.BlockSpec((tb, nsel), lambda i: (jnp.minimum(i + 1, nsteps - 1), 0), memory_space=pltpu.SMEM),
                  pl.BlockSpec((tb, dsub, LANES), lambda i: (i, 0, 0)),
                  pl.BlockSpec((tb, dsub, LANES), lambda i: (i, 0, 0)),
                  pl.BlockSpec((tb, nsel), lambda i: (i, 0)),
                  pl.BlockSpec(memory_space=pl.ANY)],
        out_specs=pl.BlockSpec((tb, dsub, LANES), lambda i: (i, 0, 0)),
        out_shape=jax.ShapeDtypeStruct((T, dsub, LANES), F32),
        scratch_shapes=[pltpu.VMEM((2, TOK * nsel, dsub, LANES), jnp.uint32), pltpu.SemaphoreType.DMA((2,)),
                        pltpu.VMEM((TOK, nsel, LANES), F32)],
        compiler_params=_params(1),
        name="peer_experts",
    )(eidx, eidx, h2.reshape(T, dsub, LANES), x1.reshape(T, dsub, LANES), gates, uv)
    return out.reshape(T, D)


def _ple_kernel(x_ref, p_ref, g_ref, wg_ref, wp_ref, o_ref):
    x = x_ref[...]
    h = x * lax.rsqrt(jnp.mean(x * x, axis=-1, keepdims=True) + EPS) * g_ref[...]
    gate = jax.nn.sigmoid(_dot(h.astype(BF16), wg_ref[...]))
    o_ref[...] = x + gate * _dot(p_ref[...].astype(BF16), wp_ref[...])


def _ple(x2, p2, ple_g, w_gate, w_proj, tm=512):
    T, D = x2.shape
    pd = p2.shape[1]
    return pl.pallas_call(
        _ple_kernel,
        grid=(T // tm,),
        in_specs=[pl.BlockSpec((tm, D), lambda i: (i, 0)), pl.BlockSpec((tm, pd), lambda i: (i, 0)),
                  _single((1, D), lambda i: (0, 0)), _single((D, D), lambda i: (0, 0)),
                  _single((pd, D), lambda i: (0, 0))],
        out_specs=pl.BlockSpec((tm, D), lambda i: (i, 0)),
        out_shape=jax.ShapeDtypeStruct((T, D), F32),
        compiler_params=_params(1),
        name="ple",
    )(x2, p2, ple_g.reshape(1, D), w_gate.astype(BF16), w_proj.astype(BF16))


def kernel(x, p, rel_bias, attn_norm_g, w_in, b_gate, q_norm_g, k_norm_g, w_att_out, conv_w, conv_b, conv_ln_g,
           conv_ln_b, w_conv_out, w_out, ffn_norm_g, w_peer_q, peer_sub_keys, peer_u, peer_v, ple_norm_g,
           w_ple_gate, w_ple_proj):
    B, S, D = x.shape
    depth = w_in.shape[0]
    T = B * S
    assert S % KQUAD == 0 and D % LANES == 0 and T % 512 == 0
    n_near = _num_near_tiles(S)
    bias = _bias_tiles(rel_bias, n_near)
    x2 = x.reshape(T, D)
    for i in range(depth):
        q, k, qi, ki, vt, wit, u, gate = _in_proj(x2, attn_norm_g[i], w_in[i], b_gate[i], q_norm_g[i], k_norm_g[i])
        att = _attention(q, k, qi, ki, vt, wit, bias, B, S, n_near)
        x1, h2 = _mix(att, u, gate, x2, conv_w[i], conv_b[i], conv_ln_g[i], conv_ln_b[i], w_att_out[i],
                      w_conv_out[i], w_out[i], ffn_norm_g[i], S)
        eidx, gates = _peer_route(h2, w_peer_q[i], peer_sub_keys[i])
        x2 = _peer_experts(eidx, gates, h2, x1, peer_u[i], peer_v[i])
        x2 = _ple(x2, p[i].reshape(T, -1), ple_norm_g[i], w_ple_gate[i], w_ple_proj[i])
    return x2.reshape(B, S, D)
```

```python
import functools
import math

import numpy as np
import jax
import jax.numpy as jnp
from jax import lax
from jax.experimental import pallas as pl
from jax.experimental.pallas import tpu as pltpu
from jax.experimental.pallas import tpu_sc as plsc

CHUNK = 64
ATT_HEADS = 8
ATT_HEAD_DIM = 64
IDX_HEADS = 8
IDX_DIM = 64
TOPK_MAX = 256
REL_BUCKETS = 32
REL_MAX_DIST = 1024
CONV_CH = 512
CONV_WIDTH = 31
N_BRANCH = 2
PEER_HEADS = 8
N_KEYS = 128
PEER_TOPK = 16
EPS = 1e-6

LANES = 128
SUBLANES = 8
VMEM_LIMIT = 56 * 1024 * 1024

QB = 128
KPAIR = 2 * QB
KQUAD = 4 * QB
INT_MIN = -(2 ** 31)
NEG_BIG = -1e30
LOG2E = math.log2(math.e)

F32 = jnp.float32
BF16 = jnp.bfloat16
I32 = jnp.int32


def _nt_dot(a, b, precision=None):
    return lax.dot_general(a, b, (((1,), (1,)), ((), ())), precision=precision,
                           preferred_element_type=F32)


def _dot(a, b):
    return jnp.dot(a, b, preferred_element_type=F32)


def _single(shape, index_map):
    return pl.BlockSpec(shape, index_map)


def _params(n_grid_dims):
    return pltpu.CompilerParams(dimension_semantics=("arbitrary",) * n_grid_dims,
                                vmem_limit_bytes=VMEM_LIMIT)


def _t5_bucket_np(rel):
    half = REL_BUCKETS // 2
    max_exact = half // 2
    ret = np.where(rel > 0, half, 0)
    n = np.abs(rel)
    nf = np.maximum(n, 1).astype(np.float32)
    large = max_exact + (np.log(nf / np.float32(max_exact)) / np.float32(math.log(REL_MAX_DIST / max_exact))
                         * np.float32(half - max_exact)).astype(np.int32)
    large = np.minimum(large, half - 1)
    return ret + np.where(n < max_exact, n, large)


def _num_near_tiles(seq):
    n = np.arange(1, max(seq, 2 * REL_MAX_DIST) + 1)
    b = _t5_bucket_np(-n)
    sat = REL_BUCKETS // 2 - 1
    unsat = np.nonzero(b != sat)[0]
    n_sat = int(n[unsat[-1]]) + 1 if unsat.size else 1
    return -(-(n_sat + QB - 1) // QB)


def _bias_kernel(rb_ref, o_ref, *, n_near):
    d = pl.program_id(0)
    i = lax.broadcasted_iota(I32, (QB, QB), 0)
    j = lax.broadcasted_iota(I32, (QB, QB), 1)
    rel = i - j - d * QB
    rel = jnp.where(d >= n_near, -8 * REL_MAX_DIST, rel)
    half = REL_BUCKETS // 2
    max_exact = half // 2
    ret = jnp.where(rel > 0, half, 0)
    n = jnp.abs(rel)
    nf = jnp.maximum(n, 1).astype(F32)
    large = max_exact + (jnp.log(nf / max_exact) / math.log(REL_MAX_DIST / max_exact)
                         * (half - max_exact)).astype(I32)
    large = jnp.minimum(large, half - 1)
    bucket = ret + jnp.where(n < max_exact, n, large)
    for h in range(ATT_HEADS):
        acc = jnp.zeros((QB, QB), F32)
        for b in range(REL_BUCKETS):
            acc = jnp.where(bucket == b, rb_ref[b, h], acc)
        o_ref[0, h] = acc * LOG2E


def _bias_tiles(rel_bias, n_near):
    return pl.pallas_call(
        functools.partial(_bias_kernel, n_near=n_near),
        grid=(n_near + 1,),
        in_specs=[pl.BlockSpec(memory_space=pltpu.SMEM)],
        out_specs=pl.BlockSpec((1, ATT_HEADS, QB, QB), lambda d: (d, 0, 0, 0)),
        out_shape=jax.ShapeDtypeStruct((n_near + 1, ATT_HEADS, QB, QB), F32),
        compiler_params=_params(1),
        name="bias_tiles",
    )(rel_bias)


def _inproj_kernel(x_ref, g_ref, wa_ref, wvt_ref, wwit_ref, wglu_ref, wgate_ref, bgate_ref, gq_ref, gk_ref,
                   q_ref, k_ref, qi_ref, ki_ref, vt_ref, wit_ref, u_ref, gate_ref, *, tm):
    x = x_ref[...]
    h = x * lax.rsqrt(jnp.mean(x * x, axis=-1, keepdims=True) + EPS) * g_ref[...]
    hb = h.astype(BF16)
    hp = ATT_HEADS * LANES
    ya = _dot(hb, wa_ref[...])
    for h_i in range(ATT_HEADS):
        sl = slice(h_i * LANES, (h_i + 1) * LANES)
        qh = ya[:, h_i * LANES:(h_i + 1) * LANES]
        ms = jnp.sum(qh * qh, axis=-1, keepdims=True) * (1.0 / ATT_HEAD_DIM)
        q_ref[:, sl] = (qh * lax.rsqrt(ms + EPS) * gq_ref[...]).astype(BF16)
        kh = ya[:, hp + h_i * LANES:hp + (h_i + 1) * LANES]
        ms = jnp.sum(kh * kh, axis=-1, keepdims=True) * (1.0 / ATT_HEAD_DIM)
        k_ref[:, sl] = (kh * lax.rsqrt(ms + EPS) * gk_ref[...]).astype(BF16)
    qi_ref[...] = ya[:, 2 * hp:3 * hp].astype(BF16)
    ki_ref[...] = ya[:, 3 * hp:3 * hp + LANES].astype(BF16)
    vt = _nt_dot(wvt_ref[...], hb).astype(BF16)
    for c in range(tm // KPAIR):
        vt_ref[c] = vt[:, c * KPAIR:(c + 1) * KPAIR]
    wit_ref[...] = _nt_dot(wwit_ref[...], hb)
    glu = _dot(hb, wglu_ref[...])
    u_ref[...] = glu[:, :CONV_CH] * jax.nn.sigmoid(glu[:, CONV_CH:])
    gate_ref[...] = jax.nn.sigmoid(_dot(hb, wgate_ref[...]) + bgate_ref[...]).astype(BF16)


def _pad_heads(w, nh, hd):
    d = w.shape[0]
    w3 = w.reshape(d, nh, hd)
    w3 = jnp.pad(w3, ((0, 0), (0, 0), (0, LANES - hd)))
    return w3.reshape(d, nh * LANES)


def _in_proj(x2, attn_g, w_in, b_gate, q_g, k_g, tm=256):
    T, D = x2.shape
    aw = ATT_HEADS * ATT_HEAD_DIM
    iw = IDX_HEADS * IDX_DIM
    o = 0
    wq = w_in[:, o:o + aw]; o += aw
    wk = w_in[:, o:o + aw]; o += aw
    wv = w_in[:, o:o + aw]; o += aw
    wqi = w_in[:, o:o + iw]; o += iw
    wki = w_in[:, o:o + IDX_DIM]; o += IDX_DIM
    wwi = w_in[:, o:o + IDX_HEADS]; o += IDX_HEADS
    wglu = w_in[:, o:o + 2 * CONV_CH]; o += 2 * CONV_CH
    wgate = w_in[:, o:o + N_BRANCH * D]
    wa = jnp.concatenate([_pad_heads(wq, ATT_HEADS, ATT_HEAD_DIM), _pad_heads(wk, ATT_HEADS, ATT_HEAD_DIM),
                          _pad_heads(wqi, IDX_HEADS, IDX_DIM),
                          jnp.pad(wki, ((0, 0), (0, LANES - IDX_DIM)))], axis=1).astype(BF16)
    na = wa.shape[1]
    hp = ATT_HEADS * LANES
    pad_g = lambda g, s: jnp.pad(g * s, (0, LANES - ATT_HEAD_DIM)).reshape(1, LANES)
    gq = pad_g(q_g, ATT_HEAD_DIM ** -0.5 * LOG2E)
    gk = pad_g(k_g, 1.0)
    const = lambda *shape: _single(shape, lambda i: (0,) * len(shape))
    outs = pl.pallas_call(
        functools.partial(_inproj_kernel, tm=tm),
        grid=(T // tm,),
        in_specs=[pl.BlockSpec((tm, D), lambda i: (i, 0)), const(1, D), const(D, na), const(aw, D),
                  const(IDX_HEADS, D), const(D, 2 * CONV_CH), const(D, N_BRANCH * D), const(1, N_BRANCH * D),
                  const(1, LANES), const(1, LANES)],
        out_specs=[pl.BlockSpec((tm, hp), lambda i: (i, 0)), pl.BlockSpec((tm, hp), lambda i: (i, 0)),
                   pl.BlockSpec((tm, hp), lambda i: (i, 0)), pl.BlockSpec((tm, LANES), lambda i: (i, 0)),
                   pl.BlockSpec((tm // KPAIR, aw, KPAIR), lambda i: (i, 0, 0)),
                   pl.BlockSpec((IDX_HEADS, tm), lambda i: (0, i)),
                   pl.BlockSpec((tm, CONV_CH), lambda i: (i, 0)),
                   pl.BlockSpec((tm, N_BRANCH * D), lambda i: (i, 0))],
        out_shape=[jax.ShapeDtypeStruct((T, hp), BF16), jax.ShapeDtypeStruct((T, hp), BF16),
                   jax.ShapeDtypeStruct((T, hp), BF16), jax.ShapeDtypeStruct((T, LANES), BF16),
                   jax.ShapeDtypeStruct((T // KPAIR, aw, KPAIR), BF16),
                   jax.ShapeDtypeStruct((IDX_HEADS, T), F32),
                   jax.ShapeDtypeStruct((T, CONV_CH), F32),
                   jax.ShapeDtypeStruct((T, N_BRANCH * D), BF16)],
        compiler_params=_params(1),
        name="in_proj",
    )(x2, attn_g.reshape(1, D), wa, wv.T.astype(BF16), wwi.T.astype(BF16), wglu.astype(BF16),
      wgate.astype(BF16), b_gate.reshape(1, N_BRANCH * D), gq, gk)
    return outs


def _attn_kernel(q_ref, qi_ref, wit_ref, k_hbm, ki_hbm, vt_hbm, bias_hbm, o_ref, k_ref, ki_ref, vt_ref, bias_ref,
                 load_sem, key_scr, att_scr, *scr, seq, n_near, topk):
    qb = pl.program_id(1)

    @pl.when(qb == 0)
    def _():
        b = pl.program_id(0)
        loads = [pltpu.make_async_copy(k_hbm.at[b], k_ref, load_sem.at[0]),
                 pltpu.make_async_copy(ki_hbm.at[b], ki_ref, load_sem.at[1]),
                 pltpu.make_async_copy(vt_hbm.at[b], vt_ref, load_sem.at[2]),
                 pltpu.make_async_copy(bias_hbm, bias_ref, load_sem.at[3])]
        for c in loads:
            c.start()
        for c in loads:
            c.wait()

    nquad = (qb + 4) // 4
    lane_t = lax.broadcasted_iota(I32, (1, QB), 1) + qb * QB
    qchunk = lane_t // CHUNK
    sub = lax.broadcasted_iota(I32, (KQUAD, QB), 0)
    idx_scale = (IDX_DIM ** -0.5) * (IDX_HEADS ** -0.5)
    wrow = wit_ref[...] * idx_scale

    qi_all = jnp.concatenate([qi_ref[:, h * LANES:(h + 1) * LANES] for h in range(IDX_HEADS)], axis=0)

    def score_quad(j, carry):
        d = _nt_dot(ki_ref[j], qi_all)
        acc = jnp.zeros((KQUAD, QB), F32)
        for h in range(IDX_HEADS):
            acc = acc + jnp.maximum(d[:, h * QB:(h + 1) * QB], 0.0) * wrow[h:h + 1, :]
        bits = pltpu.bitcast(acc, I32)
        skey = bits ^ ((bits >> 31) & 0x7FFFFFFF)
        visible = (sub + j * KQUAD) // CHUNK <= qchunk
        key_scr[j] = jnp.where(visible, skey, INT_MIN)
        return carry

    lax.fori_loop(0, nquad, score_quad, 0)

    nvis = (qchunk + 1) * CHUNK
    kk = jnp.minimum(topk, nvis)

    def count(pred):
        def body(j, acc8):
            m = jnp.where(pred(key_scr[j], j), 1, 0)
            return acc8 + m.reshape(KQUAD // SUBLANES, SUBLANES, QB).sum(axis=0)
        return lax.fori_loop(0, nquad, body, jnp.zeros((SUBLANES, QB), I32)).sum(axis=0, keepdims=True)

    def bit_step(i, carry):
        ans, cnt = carry
        cand = ans + lax.shift_left(jnp.int32(1), 31 - i)
        c = count(lambda blk, j: blk >= cand)
        take = c >= kk
        return jnp.where(take, cand, ans), jnp.where(take, c, cnt)

    thr, cnt = lax.fori_loop(0, 32, bit_step,
                             (jnp.full((1, QB), INT_MIN, I32), jnp.full((1, QB), 0, I32) + nquad * KQUAD))

    @pl.when(jnp.max(cnt - kk) > 0)
    def _():
        n_gt = count(lambda blk, j: blk > thr)
        need = kk - n_gt

        def idx_step(i, jbound):
            cand = jbound + lax.shift_left(jnp.int32(1), (seq.bit_length() - 1) - i)
            c = count(lambda blk, j: (blk == thr) & (sub + j * KQUAD < cand))
            return jnp.where(c <= need, cand, jbound)

        jbound = lax.fori_loop(0, seq.bit_length(), idx_step, jnp.zeros((1, QB), I32))

        def drop(j, carry):
            blk = key_scr[j]
            key_scr[j] = jnp.where((blk == thr) & (sub + j * KQUAD >= jbound), INT_MIN, blk)
            return carry

        lax.fori_loop(0, nquad, drop, 0)

    acc_refs, s_even, s_odd = scr[:ATT_HEADS], scr[ATT_HEADS], scr[ATT_HEADS + 1]
    for acc_ref in acc_refs:
        acc_ref[...] = jnp.zeros(acc_ref.shape, F32)
    head_row = lax.broadcasted_iota(I32, (ATT_HEADS, QB), 0)
    last_pair = k_ref.shape[0] - 1

    def qk(jp, s_ref):
        jc = jnp.minimum(jp, last_pair)
        for h in range(ATT_HEADS):
            s_ref[h] = _nt_dot(k_ref[jc, :, h * LANES:(h + 1) * LANES], q_ref[:, h * LANES:(h + 1) * LANES])

    def softmax_pv(jq, half, s_ref, carry):
        m_all, l_all = carry
        jp = 2 * jq + half
        keys = key_scr[jq, half * KPAIR:(half + 1) * KPAIR, :]
        masked = jnp.where(keys >= thr, 0.0, -jnp.inf)
        tile0 = jnp.clip(qb - 2 * jp, 0, n_near)
        tile1 = jnp.clip(qb - 2 * jp - 1, 0, n_near)
        for h in range(ATT_HEADS):
            rows = slice(h * ATT_HEAD_DIM, (h + 1) * ATT_HEAD_DIM)
            bias = jnp.concatenate([bias_ref[tile0, h], bias_ref[tile1, h]], axis=0)
            s = s_ref[h] + bias + masked
            m = m_all[h:h + 1, :]
            m_new = jnp.maximum(m, jnp.max(s, axis=0, keepdims=True))
            p = jnp.exp2(s - m_new)
            alpha = jnp.exp2(m - m_new)
            l_new = alpha * l_all[h:h + 1, :] + jnp.sum(p, axis=0, keepdims=True)
            m_all = jnp.where(head_row == h, m_new, m_all)
            l_all = jnp.where(head_row == h, l_new, l_all)
            acc_refs[h][...] = alpha * acc_refs[h][...] + _dot(vt_ref[jp, rows, :], p.astype(BF16))
        return m_all, l_all

    def att_quad(jq, carry):
        qk(2 * jq + 1, s_odd)
        carry = softmax_pv(jq, 0, s_even, carry)
        qk(2 * jq + 2, s_even)
        return softmax_pv(jq, 1, s_odd, carry)

    qk(0, s_even)
    _, l_all = lax.fori_loop(0, nquad, att_quad,
                             (jnp.full((ATT_HEADS, QB), NEG_BIG, F32), jnp.zeros((ATT_HEADS, QB), F32)))
    for h in range(ATT_HEADS):
        rows = slice(h * ATT_HEAD_DIM, (h + 1) * ATT_HEAD_DIM)
        att_scr[rows, :] = acc_refs[h][...] / l_all[h:h + 1, :]
    o_ref[...] = att_scr[...].T.astype(o_ref.dtype)


def _attention(q, k, qi, ki, vt, wit, bias, B, S, n_near):
    T = B * S
    nqb = S // QB
    hp = ATT_HEADS * LANES
    aw = ATT_HEADS * ATT_HEAD_DIM
    topk = min(TOPK_MAX, S // 4)
    npr, nqd = S // KPAIR, S // KQUAD
    k4 = k.reshape(B, npr, KPAIR, hp)
    ki4 = ki.reshape(B, nqd, KQUAD, LANES)
    vt4 = vt.reshape(B, npr, aw, KPAIR)
    return pl.pallas_call(
        functools.partial(_attn_kernel, seq=S, n_near=n_near, topk=topk),
        grid=(B, nqb),
        in_specs=[pl.BlockSpec((QB, hp), lambda b, i: (b * nqb + i, 0)),
                  pl.BlockSpec((QB, hp), lambda b, i: (b * nqb + i, 0)),
                  pl.BlockSpec((IDX_HEADS, QB), lambda b, i: (0, b * nqb + i)),
                  pl.BlockSpec(memory_space=pl.ANY), pl.BlockSpec(memory_space=pl.ANY),
                  pl.BlockSpec(memory_space=pl.ANY), pl.BlockSpec(memory_space=pl.ANY)],
        out_specs=pl.BlockSpec((QB, aw), lambda b, i: (b * nqb + i, 0)),
        out_shape=jax.ShapeDtypeStruct((T, aw), BF16),
        scratch_shapes=[pltpu.VMEM((npr, KPAIR, hp), BF16), pltpu.VMEM((nqd, KQUAD, LANES), BF16),
                        pltpu.VMEM((npr, aw, KPAIR), BF16), pltpu.VMEM((n_near + 1, ATT_HEADS, QB, QB), F32),
                        pltpu.SemaphoreType.DMA((4,)),
                        pltpu.VMEM((nqd, KQUAD, QB), I32), pltpu.VMEM((aw, QB), F32)]
        + [pltpu.VMEM((ATT_HEAD_DIM, QB), F32) for _ in range(ATT_HEADS)]
        + [pltpu.VMEM((ATT_HEADS, KPAIR, QB), F32) for _ in range(2)],
        compiler_params=_params(2),
        name="dsa_attention",
    )(q, qi, wit, k4, ki4, vt4, bias)


HALO = 32


def _mix_kernel(att_ref, u_ref, halo_ref, gate_ref, x_ref, cw_ref, cb_ref, lng_ref, lnb_ref, wao_ref, wco_ref,
                wout_ref, gffn_ref, x1_ref, h2_ref, ext_scr, *, tm, tiles_per_seq):
    i = pl.program_id(0)
    first = (i % tiles_per_seq) == 0
    ext_scr[0:HALO, :] = jnp.where(first, 0.0, halo_ref[...])
    ext_scr[HALO:HALO + tm, :] = u_ref[...]
    y = jnp.zeros((tm, CONV_CH), F32)
    for j in range(CONV_WIDTH):
        y = y + cw_ref[j:j + 1, :] * ext_scr[pl.ds(HALO - (CONV_WIDTH - 1) + j, tm), :]
    y = y + cb_ref[...]
    mu = jnp.mean(y, axis=-1, keepdims=True)
    yc = y - mu
    yn = yc * lax.rsqrt(jnp.mean(yc * yc, axis=-1, keepdims=True) + EPS) * lng_ref[...] + lnb_ref[...]
    z = yn * jax.nn.sigmoid(yn)
    y_conv = _dot(z.astype(BF16), wco_ref[...])
    y_att = _dot(att_ref[...], wao_ref[...])
    d = y_att.shape[1]
    g = gate_ref[...]
    mixed = g[:, :d].astype(F32) * y_att + g[:, d:].astype(F32) * y_conv
    x1 = x_ref[...] + _dot(mixed.astype(BF16), wout_ref[...])
    x1_ref[...] = x1
    h2_ref[...] = x1 * lax.rsqrt(jnp.mean(x1 * x1, axis=-1, keepdims=True) + EPS) * gffn_ref[...]


def _mix(att, u, gate, x2, conv_w, conv_b, ln_g, ln_b, w_att_out, w_conv_out, w_out, ffn_g, S, tm=256):
    T, D = x2.shape
    aw = att.shape[1]
    const = lambda *shape: _single(shape, lambda i: (0,) * len(shape))
    hb = tm // HALO
    return pl.pallas_call(
        functools.partial(_mix_kernel, tm=tm, tiles_per_seq=S // tm),
        grid=(T // tm,),
        in_specs=[pl.BlockSpec((tm, aw), lambda i: (i, 0)),
                  pl.BlockSpec((tm, CONV_CH), lambda i: (i, 0)),
                  pl.BlockSpec((HALO, CONV_CH), lambda i: (jnp.maximum(i * hb - 1, 0), 0)),
                  pl.BlockSpec((tm, N_BRANCH * D), lambda i: (i, 0)),
                  pl.BlockSpec((tm, D), lambda i: (i, 0)),
                  const(CONV_WIDTH, CONV_CH), const(1, CONV_CH), const(1, CONV_CH), const(1, CONV_CH),
                  const(aw, D), const(CONV_CH, D), const(D, D), const(1, D)],
        out_specs=[pl.BlockSpec((tm, D), lambda i: (i, 0)), pl.BlockSpec((tm, D), lambda i: (i, 0))],
        out_shape=[jax.ShapeDtypeStruct((T, D), F32), jax.ShapeDtypeStruct((T, D), F32)],
        scratch_shapes=[pltpu.VMEM((HALO + tm, CONV_CH), F32)],
        compiler_params=_params(1),
        name="mix_out_proj",
    )(att, u, u, gate, x2, conv_w.reshape(CONV_WIDTH, CONV_CH), conv_b.reshape(1, CONV_CH),
      ln_g.reshape(1, CONV_CH), ln_b.reshape(1, CONV_CH), w_att_out.astype(BF16), w_conv_out.astype(BF16),
      w_out.astype(BF16), ffn_g.reshape(1, D))


def _top_rows(sc, k, payload=None):
    rows = sc.shape[0]
    iota = lax.broadcasted_iota(I32, sc.shape, 0)
    out_row = lax.broadcasted_iota(I32, (k, sc.shape[1]), 0)
    vals = jnp.zeros((k, sc.shape[1]), F32)
    idxs = jnp.zeros((k, sc.shape[1]), I32)
    for r in range(k):
        m = jnp.max(sc, axis=0, keepdims=True)
        idx = jnp.min(jnp.where(sc == m, iota, rows), axis=0, keepdims=True)
        hit = iota == idx
        rec = idx if payload is None else jnp.max(jnp.where(hit, payload, -1), axis=0, keepdims=True)
        vals = jnp.where(out_row == r, m, vals)
        idxs = jnp.where(out_row == r, rec, idxs)
        sc = jnp.where(hit, -jnp.inf, sc)
    return vals, idxs


def _route_kernel(h2_ref, wq_ref, sk_ref, e_ref, g_ref, qt_scr, et_scr, gt_scr, *, tm):
    half = N_KEYS
    qt_scr[...] = _nt_dot(wq_ref[...], h2_ref[...].astype(BF16)).astype(BF16)

    def head(h, carry):
        tops = []
        for c in range(2):
            row0 = pl.multiple_of((h * 2 + c) * half, half)
            sc = _dot(sk_ref[h * 2 + c], qt_scr[pl.ds(row0, half), :])
            tops.append(_top_rows(sc, PEER_TOPK))
        (a, ia), (b, ib) = tops
        k = PEER_TOPK
        g = SUBLANES
        assert (k // 2) % g == 0 and k % g == 0
        row = lax.broadcasted_iota(I32, (g, a.shape[1]), 0)
        cand_parts, cidx_parts = [], []

        def add(av, iav, bv, ibv, valid_rows):
            s = av + bv
            if valid_rows < g:
                s = jnp.where(row < valid_rows, s, -jnp.inf)
            cand_parts.append(s)
            cidx_parts.append(iav * N_KEYS + ibv)

        for i in range(k):
            nj = k // (i + 1)
            if nj >= g:
                for j0 in range(0, nj, g):
                    add(a[i:i + 1, :], ia[i:i + 1, :], b[j0:j0 + g, :], ib[j0:j0 + g, :], g)
            elif nj > 1:
                add(a[i:i + 1, :], ia[i:i + 1, :], b[0:g, :], ib[0:g, :], nj)
            elif i % g == 0:
                add(a[i:i + g, :], ia[i:i + g, :], b[0:1, :], ib[0:1, :], g)
        cand = jnp.concatenate(cand_parts, axis=0)
        cidx = jnp.concatenate(cidx_parts, axis=0)
        ts, te = _top_rows(cand, PEER_TOPK, payload=cidx)
        ex = jnp.exp(ts - ts[0:1, :])
        gate = ex / jnp.sum(ex, axis=0, keepdims=True)
        r0 = pl.multiple_of(h * PEER_TOPK, PEER_TOPK)
        et_scr[pl.ds(r0, PEER_TOPK), :] = te
        gt_scr[pl.ds(r0, PEER_TOPK), :] = gate
        return carry

    lax.fori_loop(0, PEER_HEADS, head, 0)
    e_ref[...] = et_scr[...].T
    g_ref[...] = gt_scr[...].T


def _peer_route(h2, w_peer_q, sub_keys, tm=256):
    T, D = h2.shape
    nsel = PEER_HEADS * PEER_TOPK
    qd = w_peer_q.shape[1]
    half = sub_keys.shape[-1]
    skb = sub_keys.reshape(PEER_HEADS * 2, N_KEYS, half).astype(BF16)
    return pl.pallas_call(
        functools.partial(_route_kernel, tm=tm),
        grid=(T // tm,),
        in_specs=[pl.BlockSpec((tm, D), lambda i: (i, 0)),
                  _single((qd, D), lambda i: (0, 0)),
                  _single((PEER_HEADS * 2, N_KEYS, half), lambda i: (0, 0, 0))],
        out_specs=[pl.BlockSpec((tm, nsel), lambda i: (i, 0)), pl.BlockSpec((tm, nsel), lambda i: (i, 0))],
        out_shape=[jax.ShapeDtypeStruct((T, nsel), I32), jax.ShapeDtypeStruct((T, nsel), F32)],
        scratch_shapes=[pltpu.VMEM((qd, tm), BF16), pltpu.VMEM((nsel, tm), I32), pltpu.VMEM((nsel, tm), F32)],
        compiler_params=_params(1),
        name="peer_route",
    )(h2, w_peer_q.T.astype(BF16), skb)


TOK = 8


def _gelu_tanh(x):
    return 0.5 * x * (1.0 + jnp.tanh(math.sqrt(2.0 / math.pi) * (x + 0.044715 * (x * x * x))))


def _sublane_sums(ps):
    sub = lax.broadcasted_iota(I32, (SUBLANES, LANES), 0)
    lvl, stride = list(ps), SUBLANES // 2
    while len(lvl) > 1:
        half = len(lvl) // 2
        low = (sub & stride) == 0
        nxt = []
        for n in range(half):
            a, b = lvl[n], lvl[n + half]
            nxt.append(jnp.where(low, a + pltpu.roll(a, SUBLANES - stride, 0), b + pltpu.roll(b, stride, 0)))
        lvl, stride = nxt, stride // 2
    return lvl[0]


def _pack_uv(u, v):
    ne, d = u.shape
    hi = lax.bitcast_convert_type(u.astype(BF16), jnp.uint16).astype(jnp.uint32)
    lo = lax.bitcast_convert_type(v.astype(BF16), jnp.uint16).astype(jnp.uint32)
    return ((hi << 16) | lo).reshape(ne, d // LANES, LANES)


def _u_of(word):
    return pltpu.bitcast(word & jnp.uint32(0xFFFF0000), F32)


def _v_of(word):
    return pltpu.bitcast(word << 16, F32)


def _expert_group(row, h2_ref, x1_ref, gate_ref, o_ref, abc_scr, base, nsel, before_dots=None, before_sum=None):
    dsub = h2_ref.shape[1]
    lane_id = lax.broadcasted_iota(I32, (nsel, LANES), 1)
    st = jnp.zeros((nsel, LANES), F32)
    for t in range(TOK):
        if before_dots is not None:
            before_dots(t)
        xt = h2_ref[base + t]
        qs = []
        for g in range(nsel // SUBLANES):
            ps = [_u_of(row(t, g * SUBLANES + k)) * xt for k in range(SUBLANES)]
            qs.append(_sublane_sums(ps))
        q = jnp.concatenate(qs, axis=0)
        st = jnp.where(lane_id == t, jnp.sum(q, axis=1, keepdims=True), st)
    g8 = gate_ref[base:base + TOK, :]
    gt = jnp.concatenate([g8, jnp.zeros((nsel - TOK, nsel), F32)], axis=0).T
    at = _gelu_tanh(st) * gt

    for t in range(TOK):
        if before_sum is not None:
            before_sum(t)
        abc_scr[t] = jnp.broadcast_to(at[:, t:t + 1], (nsel, LANES))
        accs = [jnp.zeros((dsub, LANES), F32) for _ in range(4)]
        for n in range(nsel):
            accs[n % 4] = accs[n % 4] + abc_scr[t, n:n + 1, :] * _v_of(row(t, n))
        o_ref[base + t] = x1_ref[base + t] + ((accs[0] + accs[1]) + (accs[2] + accs[3]))


def _expert_kernel(idxc_ref, idxn_ref, h2_ref, x1_ref, gate_ref, uv_ref, o_ref, buf, sem, abc_scr, *, nsel):
    i = pl.program_id(0)
    nsteps = pl.num_programs(0)
    rows = TOK * nsel

    def issue_token(idx_ref, tok, s, t, n0=0, n1=nsel):
        for n in range(n0, n1):
            pltpu.make_async_copy(uv_ref.at[idx_ref[tok, n]], buf.at[s, t * nsel + n],
                                  sem.at[s]).start(priority=n % 2)

    def wait_slot(s):
        pltpu.make_async_copy(uv_ref.at[pl.ds(0, rows)], buf.at[s], sem.at[s]).wait()

    @pl.when(i == 0)
    def _():
        for t in range(TOK):
            issue_token(idxc_ref, t, 0, t)

    n_early = (3 * nsel) // 4

    for grp in range(2):
        wait_slot(grp)
        nxt_idx, nxt_tok, nxt_slot = (idxc_ref, TOK, 1) if grp == 0 else (idxn_ref, 0, 0)
        _expert_group(lambda t, n, grp=grp: buf[grp, t * nsel + n], h2_ref, x1_ref, gate_ref, o_ref, abc_scr,
                      grp * TOK, nsel,
                      before_dots=lambda t: issue_token(nxt_idx, nxt_tok + t, nxt_slot, t, 0, n_early),
                      before_sum=lambda t: issue_token(nxt_idx, nxt_tok + t, nxt_slot, t, n_early, nsel))

    @pl.when(i == nsteps - 1)
    def _():
        wait_slot(0)


def _expert_staged_kernel(rows_ref, h2_ref, x1_ref, gate_ref, prev_ref, o_ref, abc_scr, *, nsel):
    del prev_ref
    for grp in range(h2_ref.shape[0] // TOK):
        _expert_group(lambda t, n, grp=grp: rows_ref[(grp * TOK + t) * nsel + n], h2_ref, x1_ref, gate_ref, o_ref,
                      abc_scr, grp * TOK, nsel)


SC_CORES = 2
SC_SUBCORES = 16
SC_CHUNK = 32
SC_SLAB = 2048


def _sc_gather(table, idx):
    n = idx.shape[0]
    nw = SC_CORES * SC_SUBCORES
    per_w = n // nw
    assert n % nw == 0 and per_w % SC_SLAB == 0 and SC_SLAB % (2 * SC_CHUNK) == 0
    nslab, nchunk = per_w // SC_SLAB, SC_SLAB // SC_CHUNK
    row_shape = table.shape[1:]
    mesh = plsc.VectorSubcoreMesh(core_axis_name="c", subcore_axis_name="s")

    @functools.partial(
        pl.kernel, mesh=mesh, out_type=jax.ShapeDtypeStruct((n,) + row_shape, table.dtype),
        scratch_types=[pltpu.VMEM((SC_SLAB,), jnp.int32),
                       pltpu.VMEM((SC_CHUNK,) + row_shape, table.dtype),
                       pltpu.VMEM((SC_CHUNK,) + row_shape, table.dtype),
                       pltpu.SemaphoreType.DMA, pltpu.SemaphoreType.DMA,
                       pltpu.SemaphoreType.DMA, pltpu.SemaphoreType.DMA])
    def gather(table_hbm, idx_hbm, out_hbm, idx_v, rows0, rows1, g0, g1, w0, w1):
        wid = lax.axis_index("s") * SC_CORES + lax.axis_index("c")
        bufs, gsem, wsem = (rows0, rows1), (g0, g1), (w0, w1)

        def gather_copy(c, b):
            return pltpu.make_async_copy(table_hbm.at[idx_v.at[pl.ds(c * SC_CHUNK, SC_CHUNK)]], bufs[b], gsem[b])

        @pl.loop(0, nslab)
        def _(sl):
            base = wid * per_w + sl * SC_SLAB
            pltpu.sync_copy(idx_hbm.at[pl.ds(base, SC_SLAB)], idx_v)

            def write_copy(c, b):
                return pltpu.make_async_copy(bufs[b], out_hbm.at[pl.ds(base + c * SC_CHUNK, SC_CHUNK)], wsem[b])

            gather_copy(0, 0).start()

            @pl.loop(0, nchunk, step=2)
            def _(c0):
                for b in range(2):
                    c = c0 + b
                    gather_copy(c, b).wait()

                    @pl.when(c >= 1)
                    def _():
                        write_copy(c - 1, 1 - b).wait()

                    @pl.when(c + 1 < nchunk)
                    def _():
                        gather_copy(c + 1, 1 - b).start()

                    write_copy(c, b).start()

            write_copy(nchunk - 1, 1).wait()

    return gather(table, idx)


def _experts_dma(eidx, gates, h2, x1, uv, ta):
    T, dsub, _ = x1.shape
    nsel = eidx.shape[1]
    tb = 2 * TOK
    nsteps = ta // tb
    return pl.pallas_call(
        functools.partial(_expert_kernel, nsel=nsel),
        grid=(nsteps,),
        in_specs=[pl.BlockSpec((tb, nsel), lambda i: (i, 0), memory_space=pltpu.SMEM),
                  pl.BlockSpec((tb, nsel), lambda i: (jnp.minimum(i + 1, nsteps - 1), 0), memory_space=pltpu.SMEM),
                  pl.BlockSpec((tb, dsub, LANES), lambda i: (i, 0, 0)),
                  pl.BlockSpec((tb, dsub, LANES), lambda i: (i, 0, 0)),
                  pl.BlockSpec((tb, nsel), lambda i: (i, 0)),
                  pl.BlockSpec(memory_space=pl.ANY)],
        out_specs=pl.BlockSpec((tb, dsub, LANES), lambda i: (i, 0, 0)),
        out_shape=jax.ShapeDtypeStruct((T, dsub, LANES), F32),
        scratch_shapes=[pltpu.VMEM((2, TOK * nsel, dsub, LANES), jnp.uint32), pltpu.SemaphoreType.DMA((2,)),
                        pltpu.VMEM((TOK, nsel, LANES), F32)],
        compiler_params=_params(1),
        name="peer_experts",
    )(eidx, eidx, h2, x1, gates, uv)


def _experts_staged(rows, gates, h2, x1, out, ta):
    T, dsub, _ = x1.shape
    nsel = gates.shape[1]
    tb = 2 * TOK
    first = ta // tb
    tok = lambda i: (first + i, 0, 0)
    return pl.pallas_call(
        functools.partial(_expert_staged_kernel, nsel=nsel),
        grid=((T - ta) // tb,),
        in_specs=[pl.BlockSpec((tb * nsel, dsub, LANES), lambda i: (i, 0, 0)),
                  pl.BlockSpec((tb, dsub, LANES), tok),
                  pl.BlockSpec((tb, dsub, LANES), tok),
                  pl.BlockSpec((tb, nsel), lambda i: (first + i, 0)),
                  pl.BlockSpec(memory_space=pl.ANY)],
        out_specs=pl.BlockSpec((tb, dsub, LANES), tok),
        out_shape=jax.ShapeDtypeStruct((T, dsub, LANES), F32),
        scratch_shapes=[pltpu.VMEM((TOK, nsel, LANES), F32)],
        input_output_aliases={4: 0},
        compiler_params=_params(1),
        name="peer_experts_staged",
    )(rows, h2, x1, gates, out)


def _staged_tokens(T):
    unit = SC_CORES * SC_SUBCORES * SC_SLAB // (PEER_HEADS * PEER_TOPK)
    return (7 * T // 16) // unit * unit


def _peer_experts(eidx, gates, h2, x1, peer_u, peer_v):
    T, D = x1.shape
    nsel = eidx.shape[1]
    dsub = D // LANES
    uv = _pack_uv(peer_u, peer_v)
    h3, x3 = h2.reshape(T, dsub, LANES), x1.reshape(T, dsub, LANES)
    ts = _staged_tokens(T)
    ta = T - ts
    if ts:
        rows = _sc_gather(uv, eidx[ta:].reshape(ts * nsel))
    out = _experts_dma(eidx, gates, h3, x3, uv, ta)
    if ts:
        out = _experts_staged(rows, gates, h3, x3, out, ta)
    return out.reshape(T, D)


def _ple_kernel(x_ref, p_ref, g_ref, wg_ref, wp_ref, o_ref):
    x = x_ref[...]
    h = x * lax.rsqrt(jnp.mean(x * x, axis=-1, keepdims=True) + EPS) * g_ref[...]
    gate = jax.nn.sigmoid(_dot(h.astype(BF16), wg_ref[...]))
    o_ref[...] = x + gate * _dot(p_ref[...].astype(BF16), wp_ref[...])


def _ple(x2, p2, ple_g, w_gate, w_proj, tm=512):
    T, D = x2.shape
    pd = p2.shape[1]
    return pl.pallas_call(
        _ple_kernel,
        grid=(T // tm,),
        in_specs=[pl.BlockSpec((tm, D), lambda i: (i, 0)), pl.BlockSpec((tm, pd), lambda i: (i, 0)),
                  _single((1, D), lambda i: (0, 0)), _single((D, D), lambda i: (0, 0)),
                  _single((pd, D), lambda i: (0, 0))],
        out_specs=pl.BlockSpec((tm, D), lambda i: (i, 0)),
        out_shape=jax.ShapeDtypeStruct((T, D), F32),
        compiler_params=_params(1),
        name="ple",
    )(x2, p2, ple_g.reshape(1, D), w_gate.astype(BF16), w_proj.astype(BF16))


def kernel(x, p, rel_bias, attn_norm_g, w_in, b_gate, q_norm_g, k_norm_g, w_att_out, conv_w, conv_b, conv_ln_g,
           conv_ln_b, w_conv_out, w_out, ffn_norm_g, w_peer_q, peer_sub_keys, peer_u, peer_v, ple_norm_g,
           w_ple_gate, w_ple_proj):
    B, S, D = x.shape
    depth = w_in.shape[0]
    T = B * S
    assert S % KQUAD == 0 and D % LANES == 0 and T % 512 == 0
    n_near = _num_near_tiles(S)
    bias = _bias_tiles(rel_bias, n_near)
    x2 = x.reshape(T, D)
    for i in range(depth):
        q, k, qi, ki, vt, wit, u, gate = _in_proj(x2, attn_norm_g[i], w_in[i], b_gate[i], q_norm_g[i], k_norm_g[i])
        att = _attention(q, k, qi, ki, vt, wit, bias, B, S, n_near)
        x1, h2 = _mix(att, u, gate, x2, conv_w[i], conv_b[i], conv_ln_g[i], conv_ln_b[i], w_att_out[i],
                      w_conv_out[i], w_out[i], ffn_norm_g[i], S)
        eidx, gates = _peer_route(h2, w_peer_q[i], peer_sub_keys[i])
        x2 = _peer_experts(eidx, gates, h2, x1, peer_u[i], peer_v[i])
        x2 = _ple(x2, p[i].reshape(T, -1), ple_norm_g[i], w_ple_gate[i], w_ple_proj[i])
    return x2.reshape(B, S, D)
```

```python
import functools
import math

import numpy as np
import jax
import jax.numpy as jnp
from jax import lax
from jax.experimental import pallas as pl
from jax.experimental.pallas import tpu as pltpu
from jax.experimental.pallas import tpu_sc as plsc

CHUNK = 64
ATT_HEADS = 8
ATT_HEAD_DIM = 64
IDX_HEADS = 8
IDX_DIM = 64
TOPK_MAX = 256
REL_BUCKETS = 32
REL_MAX_DIST = 1024
CONV_CH = 512
CONV_WIDTH = 31
N_BRANCH = 2
PEER_HEADS = 8
N_KEYS = 128
PEER_TOPK = 16
EPS = 1e-6

LANES = 128
SUBLANES = 8
VMEM_LIMIT = 56 * 1024 * 1024

QB = 128
KPAIR = 2 * QB
KQUAD = 4 * QB
INT_MIN = -(2 ** 31)
NEG_BIG = -1e30
LOG2E = math.log2(math.e)

F32 = jnp.float32
BF16 = jnp.bfloat16
I32 = jnp.int32


def _nt_dot(a, b, precision=None):
    return lax.dot_general(a, b, (((1,), (1,)), ((), ())), precision=precision,
                           preferred_element_type=F32)


def _dot(a, b):
    return jnp.dot(a, b, preferred_element_type=F32)


def _single(shape, index_map):
    return pl.BlockSpec(shape, index_map)


def _params(n_grid_dims):
    return pltpu.CompilerParams(dimension_semantics=("arbitrary",) * n_grid_dims,
                                vmem_limit_bytes=VMEM_LIMIT)


def _t5_bucket_np(rel):
    half = REL_BUCKETS // 2
    max_exact = half // 2
    ret = np.where(rel > 0, half, 0)
    n = np.abs(rel)
    nf = np.maximum(n, 1).astype(np.float32)
    large = max_exact + (np.log(nf / np.float32(max_exact)) / np.float32(math.log(REL_MAX_DIST / max_exact))
                         * np.float32(half - max_exact)).astype(np.int32)
    large = np.minimum(large, half - 1)
    return ret + np.where(n < max_exact, n, large)


def _num_near_tiles(seq):
    n = np.arange(1, max(seq, 2 * REL_MAX_DIST) + 1)
    b = _t5_bucket_np(-n)
    sat = REL_BUCKETS // 2 - 1
    unsat = np.nonzero(b != sat)[0]
    n_sat = int(n[unsat[-1]]) + 1 if unsat.size else 1
    return -(-(n_sat + QB - 1) // QB)


def _bias_kernel(rb_ref, o_ref, *, n_near):
    d = pl.program_id(0)
    i = lax.broadcasted_iota(I32, (QB, QB), 0)
    j = lax.broadcasted_iota(I32, (QB, QB), 1)
    rel = i - j - d * QB
    rel = jnp.where(d >= n_near, -8 * REL_MAX_DIST, rel)
    half = REL_BUCKETS // 2
    max_exact = half // 2
    ret = jnp.where(rel > 0, half, 0)
    n = jnp.abs(rel)
    nf = jnp.maximum(n, 1).astype(F32)
    large = max_exact + (jnp.log(nf / max_exact) / math.log(REL_MAX_DIST / max_exact)
                         * (half - max_exact)).astype(I32)
    large = jnp.minimum(large, half - 1)
    bucket = ret + jnp.where(n < max_exact, n, large)
    for h in range(ATT_HEADS):
        acc = jnp.zeros((QB, QB), F32)
        for b in range(REL_BUCKETS):
            acc = jnp.where(bucket == b, rb_ref[b, h], acc)
        o_ref[0, h] = acc * LOG2E


def _bias_tiles(rel_bias, n_near):
    return pl.pallas_call(
        functools.partial(_bias_kernel, n_near=n_near),
        grid=(n_near + 1,),
        in_specs=[pl.BlockSpec(memory_space=pltpu.SMEM)],
        out_specs=pl.BlockSpec((1, ATT_HEADS, QB, QB), lambda d: (d, 0, 0, 0)),
        out_shape=jax.ShapeDtypeStruct((n_near + 1, ATT_HEADS, QB, QB), F32),
        compiler_params=_params(1),
        name="bias_tiles",
    )(rel_bias)


def _inproj_kernel(x_ref, g_ref, wa_ref, wvt_ref, wwit_ref, wglu_ref, wgate_ref, bgate_ref, gq_ref, gk_ref,
                   q_ref, k_ref, qi_ref, ki_ref, vt_ref, wit_ref, u_ref, gate_ref, *, tm):
    x = x_ref[...]
    h = x * lax.rsqrt(jnp.mean(x * x, axis=-1, keepdims=True) + EPS) * g_ref[...]
    hb = h.astype(BF16)
    hp = ATT_HEADS * LANES
    ya = _dot(hb, wa_ref[...])
    for h_i in range(ATT_HEADS):
        sl = slice(h_i * LANES, (h_i + 1) * LANES)
        qh = ya[:, h_i * LANES:(h_i + 1) * LANES]
        ms = jnp.sum(qh * qh, axis=-1, keepdims=True) * (1.0 / ATT_HEAD_DIM)
        q_ref[:, sl] = (qh * lax.rsqrt(ms + EPS) * gq_ref[...]).astype(BF16)
        kh = ya[:, hp + h_i * LANES:hp + (h_i + 1) * LANES]
        ms = jnp.sum(kh * kh, axis=-1, keepdims=True) * (1.0 / ATT_HEAD_DIM)
        k_ref[:, sl] = (kh * lax.rsqrt(ms + EPS) * gk_ref[...]).astype(BF16)
    qi_ref[...] = ya[:, 2 * hp:3 * hp].astype(BF16)
    ki_ref[...] = ya[:, 3 * hp:3 * hp + LANES].astype(BF16)
    vt = _nt_dot(wvt_ref[...], hb).astype(BF16)
    for c in range(tm // KPAIR):
        vt_ref[c] = vt[:, c * KPAIR:(c + 1) * KPAIR]
    wit_ref[...] = _nt_dot(wwit_ref[...], hb)
    glu = _dot(hb, wglu_ref[...])
    u_ref[...] = glu[:, :CONV_CH] * jax.nn.sigmoid(glu[:, CONV_CH:])
    gate_ref[...] = jax.nn.sigmoid(_dot(hb, wgate_ref[...]) + bgate_ref[...]).astype(BF16)


def _pad_heads(w, nh, hd):
    d = w.shape[0]
    w3 = w.reshape(d, nh, hd)
    w3 = jnp.pad(w3, ((0, 0), (0, 0), (0, LANES - hd)))
    return w3.reshape(d, nh * LANES)


def _in_proj(x2, attn_g, w_in, b_gate, q_g, k_g, tm=256):
    T, D = x2.shape
    aw = ATT_HEADS * ATT_HEAD_DIM
    iw = IDX_HEADS * IDX_DIM
    o = 0
    wq = w_in[:, o:o + aw]; o += aw
    wk = w_in[:, o:o + aw]; o += aw
    wv = w_in[:, o:o + aw]; o += aw
    wqi = w_in[:, o:o + iw]; o += iw
    wki = w_in[:, o:o + IDX_DIM]; o += IDX_DIM
    wwi = w_in[:, o:o + IDX_HEADS]; o += IDX_HEADS
    wglu = w_in[:, o:o + 2 * CONV_CH]; o += 2 * CONV_CH
    wgate = w_in[:, o:o + N_BRANCH * D]
    wa = jnp.concatenate([_pad_heads(wq, ATT_HEADS, ATT_HEAD_DIM), _pad_heads(wk, ATT_HEADS, ATT_HEAD_DIM),
                          _pad_heads(wqi, IDX_HEADS, IDX_DIM),
                          jnp.pad(wki, ((0, 0), (0, LANES - IDX_DIM)))], axis=1).astype(BF16)
    na = wa.shape[1]
    hp = ATT_HEADS * LANES
    pad_g = lambda g, s: jnp.pad(g * s, (0, LANES - ATT_HEAD_DIM)).reshape(1, LANES)
    gq = pad_g(q_g, ATT_HEAD_DIM ** -0.5 * LOG2E)
    gk = pad_g(k_g, 1.0)
    const = lambda *shape: _single(shape, lambda i: (0,) * len(shape))
    outs = pl.pallas_call(
        functools.partial(_inproj_kernel, tm=tm),
        grid=(T // tm,),
        in_specs=[pl.BlockSpec((tm, D), lambda i: (i, 0)), const(1, D), const(D, na), const(aw, D),
                  const(IDX_HEADS, D), const(D, 2 * CONV_CH), const(D, N_BRANCH * D), const(1, N_BRANCH * D),
                  const(1, LANES), const(1, LANES)],
        out_specs=[pl.BlockSpec((tm, hp), lambda i: (i, 0)), pl.BlockSpec((tm, hp), lambda i: (i, 0)),
                   pl.BlockSpec((tm, hp), lambda i: (i, 0)), pl.BlockSpec((tm, LANES), lambda i: (i, 0)),
                   pl.BlockSpec((tm // KPAIR, aw, KPAIR), lambda i: (i, 0, 0)),
                   pl.BlockSpec((IDX_HEADS, tm), lambda i: (0, i)),
                   pl.BlockSpec((tm, CONV_CH), lambda i: (i, 0)),
                   pl.BlockSpec((tm, N_BRANCH * D), lambda i: (i, 0))],
        out_shape=[jax.ShapeDtypeStruct((T, hp), BF16), jax.ShapeDtypeStruct((T, hp), BF16),
                   jax.ShapeDtypeStruct((T, hp), BF16), jax.ShapeDtypeStruct((T, LANES), BF16),
                   jax.ShapeDtypeStruct((T // KPAIR, aw, KPAIR), BF16),
                   jax.ShapeDtypeStruct((IDX_HEADS, T), F32),
                   jax.ShapeDtypeStruct((T, CONV_CH), F32),
                   jax.ShapeDtypeStruct((T, N_BRANCH * D), BF16)],
        compiler_params=_params(1),
        name="in_proj",
    )(x2, attn_g.reshape(1, D), wa, wv.T.astype(BF16), wwi.T.astype(BF16), wglu.astype(BF16),
      wgate.astype(BF16), b_gate.reshape(1, N_BRANCH * D), gq, gk)
    return outs


def _attn_kernel(q_ref, qi_ref, wit_ref, k_hbm, ki_hbm, vt_hbm, bias_hbm, o_ref, k_ref, ki_ref, vt_ref, bias_ref,
                 load_sem, key_scr, att_scr, *scr, seq, n_near, topk):
    qb = pl.program_id(1)

    @pl.when(qb == 0)
    def _():
        b = pl.program_id(0)
        loads = [pltpu.make_async_copy(k_hbm.at[b], k_ref, load_sem.at[0]),
                 pltpu.make_async_copy(ki_hbm.at[b], ki_ref, load_sem.at[1]),
                 pltpu.make_async_copy(vt_hbm.at[b], vt_ref, load_sem.at[2]),
                 pltpu.make_async_copy(bias_hbm, bias_ref, load_sem.at[3])]
        for c in loads:
            c.start()
        for c in loads:
            c.wait()

    nquad = (qb + 4) // 4
    lane_t = lax.broadcasted_iota(I32, (1, QB), 1) + qb * QB
    qchunk = lane_t // CHUNK
    sub = lax.broadcasted_iota(I32, (KQUAD, QB), 0)
    idx_scale = (IDX_DIM ** -0.5) * (IDX_HEADS ** -0.5)
    wrow = wit_ref[...] * idx_scale

    qi_all = jnp.concatenate([qi_ref[:, h * LANES:(h + 1) * LANES] for h in range(IDX_HEADS)], axis=0)

    def score_quad(j, carry):
        d = _nt_dot(ki_ref[j], qi_all)
        acc = jnp.zeros((KQUAD, QB), F32)
        for h in range(IDX_HEADS):
            acc = acc + jnp.maximum(d[:, h * QB:(h + 1) * QB], 0.0) * wrow[h:h + 1, :]
        bits = pltpu.bitcast(acc, I32)
        skey = bits ^ ((bits >> 31) & 0x7FFFFFFF)
        visible = (sub + j * KQUAD) // CHUNK <= qchunk
        key_scr[j] = jnp.where(visible, skey, INT_MIN)
        return carry

    lax.fori_loop(0, nquad, score_quad, 0)

    nvis = (qchunk + 1) * CHUNK
    kk = jnp.minimum(topk, nvis)

    def count(pred):
        def body(j, acc8):
            m = jnp.where(pred(key_scr[j], j), 1, 0)
            return acc8 + m.reshape(KQUAD // SUBLANES, SUBLANES, QB).sum(axis=0)
        return lax.fori_loop(0, nquad, body, jnp.zeros((SUBLANES, QB), I32)).sum(axis=0, keepdims=True)

    def bit_step(i, carry):
        ans, cnt = carry
        cand = ans + lax.shift_left(jnp.int32(1), 31 - i)
        c = count(lambda blk, j: blk >= cand)
        take = c >= kk
        return jnp.where(take, cand, ans), jnp.where(take, c, cnt)

    thr, cnt = lax.fori_loop(0, 32, bit_step,
                             (jnp.full((1, QB), INT_MIN, I32), jnp.full((1, QB), 0, I32) + nquad * KQUAD))

    @pl.when(jnp.max(cnt - kk) > 0)
    def _():
        n_gt = count(lambda blk, j: blk > thr)
        need = kk - n_gt

        def idx_step(i, jbound):
            cand = jbound + lax.shift_left(jnp.int32(1), (seq.bit_length() - 1) - i)
            c = count(lambda blk, j: (blk == thr) & (sub + j * KQUAD < cand))
            return jnp.where(c <= need, cand, jbound)

        jbound = lax.fori_loop(0, seq.bit_length(), idx_step, jnp.zeros((1, QB), I32))

        def drop(j, carry):
            blk = key_scr[j]
            key_scr[j] = jnp.where((blk == thr) & (sub + j * KQUAD >= jbound), INT_MIN, blk)
            return carry

        lax.fori_loop(0, nquad, drop, 0)

    acc_refs, s_even, s_odd = scr[:ATT_HEADS], scr[ATT_HEADS], scr[ATT_HEADS + 1]
    for acc_ref in acc_refs:
        acc_ref[...] = jnp.zeros(acc_ref.shape, F32)
    head_row = lax.broadcasted_iota(I32, (ATT_HEADS, QB), 0)
    last_pair = k_ref.shape[0] - 1

    def qk(jp, s_ref):
        jc = jnp.minimum(jp, last_pair)
        for h in range(ATT_HEADS):
            s_ref[h] = _nt_dot(k_ref[jc, :, h * LANES:(h + 1) * LANES], q_ref[:, h * LANES:(h + 1) * LANES])

    def softmax_pv(jq, half, s_ref, carry):
        m_all, l_all = carry
        jp = 2 * jq + half
        keys = key_scr[jq, half * KPAIR:(half + 1) * KPAIR, :]
        masked = jnp.where(keys >= thr, 0.0, -jnp.inf)
        tile0 = jnp.clip(qb - 2 * jp, 0, n_near)
        tile1 = jnp.clip(qb - 2 * jp - 1, 0, n_near)
        for h in range(ATT_HEADS):
            rows = slice(h * ATT_HEAD_DIM, (h + 1) * ATT_HEAD_DIM)
            bias = jnp.concatenate([bias_ref[tile0, h], bias_ref[tile1, h]], axis=0)
            s = s_ref[h] + bias + masked
            m = m_all[h:h + 1, :]
            m_new = jnp.maximum(m, jnp.max(s, axis=0, keepdims=True))
            p = jnp.exp2(s - m_new)
            alpha = jnp.exp2(m - m_new)
            l_new = alpha * l_all[h:h + 1, :] + jnp.sum(p, axis=0, keepdims=True)
            m_all = jnp.where(head_row == h, m_new, m_all)
            l_all = jnp.where(head_row == h, l_new, l_all)
            acc_refs[h][...] = alpha * acc_refs[h][...] + _dot(vt_ref[jp, rows, :], p.astype(BF16))
        return m_all, l_all

    def att_quad(jq, carry):
        qk(2 * jq + 1, s_odd)
        carry = softmax_pv(jq, 0, s_even, carry)
        qk(2 * jq + 2, s_even)
        return softmax_pv(jq, 1, s_odd, carry)

    qk(0, s_even)
    _, l_all = lax.fori_loop(0, nquad, att_quad,
                             (jnp.full((ATT_HEADS, QB), NEG_BIG, F32), jnp.zeros((ATT_HEADS, QB), F32)))
    for h in range(ATT_HEADS):
        rows = slice(h * ATT_HEAD_DIM, (h + 1) * ATT_HEAD_DIM)
        att_scr[rows, :] = acc_refs[h][...] / l_all[h:h + 1, :]
    o_ref[...] = att_scr[...].T.astype(o_ref.dtype)


def _attention(q, k, qi, ki, vt, wit, bias, B, S, n_near):
    T = B * S
    nqb = S // QB
    hp = ATT_HEADS * LANES
    aw = ATT_HEADS * ATT_HEAD_DIM
    topk = min(TOPK_MAX, S // 4)
    npr, nqd = S // KPAIR, S // KQUAD
    k4 = k.reshape(B, npr, KPAIR, hp)
    ki4 = ki.reshape(B, nqd, KQUAD, LANES)
    vt4 = vt.reshape(B, npr, aw, KPAIR)
    return pl.pallas_call(
        functools.partial(_attn_kernel, seq=S, n_near=n_near, topk=topk),
        grid=(B, nqb),
        in_specs=[pl.BlockSpec((QB, hp), lambda b, i: (b * nqb + i, 0)),
                  pl.BlockSpec((QB, hp), lambda b, i: (b * nqb + i, 0)),
                  pl.BlockSpec((IDX_HEADS, QB), lambda b, i: (0, b * nqb + i)),
                  pl.BlockSpec(memory_space=pl.ANY), pl.BlockSpec(memory_space=pl.ANY),
                  pl.BlockSpec(memory_space=pl.ANY), pl.BlockSpec(memory_space=pl.ANY)],
        out_specs=pl.BlockSpec((QB, aw), lambda b, i: (b * nqb + i, 0)),
        out_shape=jax.ShapeDtypeStruct((T, aw), BF16),
        scratch_shapes=[pltpu.VMEM((npr, KPAIR, hp), BF16), pltpu.VMEM((nqd, KQUAD, LANES), BF16),
                        pltpu.VMEM((npr, aw, KPAIR), BF16), pltpu.VMEM((n_near + 1, ATT_HEADS, QB, QB), F32),
                        pltpu.SemaphoreType.DMA((4,)),
                        pltpu.VMEM((nqd, KQUAD, QB), I32), pltpu.VMEM((aw, QB), F32)]
        + [pltpu.VMEM((ATT_HEAD_DIM, QB), F32) for _ in range(ATT_HEADS)]
        + [pltpu.VMEM((ATT_HEADS, KPAIR, QB), F32) for _ in range(2)],
        compiler_params=_params(2),
        name="dsa_attention",
    )(q, qi, wit, k4, ki4, vt4, bias)


HALO = 32


def _mix_kernel(att_ref, u_ref, halo_ref, gate_ref, x_ref, cw_ref, cb_ref, lng_ref, lnb_ref, wao_ref, wco_ref,
                wout_ref, gffn_ref, x1_ref, h2_ref, ext_scr, *, tm, tiles_per_seq):
    i = pl.program_id(0)
    first = (i % tiles_per_seq) == 0
    ext_scr[0:HALO, :] = jnp.where(first, 0.0, halo_ref[...])
    ext_scr[HALO:HALO + tm, :] = u_ref[...]
    y = jnp.zeros((tm, CONV_CH), F32)
    for j in range(CONV_WIDTH):
        y = y + cw_ref[j:j + 1, :] * ext_scr[pl.ds(HALO - (CONV_WIDTH - 1) + j, tm), :]
    y = y + cb_ref[...]
    mu = jnp.mean(y, axis=-1, keepdims=True)
    yc = y - mu
    yn = yc * lax.rsqrt(jnp.mean(yc * yc, axis=-1, keepdims=True) + EPS) * lng_ref[...] + lnb_ref[...]
    z = yn * jax.nn.sigmoid(yn)
    y_conv = _dot(z.astype(BF16), wco_ref[...])
    y_att = _dot(att_ref[...], wao_ref[...])
    d = y_att.shape[1]
    g = gate_ref[...]
    mixed = g[:, :d].astype(F32) * y_att + g[:, d:].astype(F32) * y_conv
    x1 = x_ref[...] + _dot(mixed.astype(BF16), wout_ref[...])
    x1_ref[...] = x1
    h2_ref[...] = x1 * lax.rsqrt(jnp.mean(x1 * x1, axis=-1, keepdims=True) + EPS) * gffn_ref[...]


def _mix(att, u, gate, x2, conv_w, conv_b, ln_g, ln_b, w_att_out, w_conv_out, w_out, ffn_g, S, tm=256):
    T, D = x2.shape
    aw = att.shape[1]
    const = lambda *shape: _single(shape, lambda i: (0,) * len(shape))
    hb = tm // HALO
    return pl.pallas_call(
        functools.partial(_mix_kernel, tm=tm, tiles_per_seq=S // tm),
        grid=(T // tm,),
        in_specs=[pl.BlockSpec((tm, aw), lambda i: (i, 0)),
                  pl.BlockSpec((tm, CONV_CH), lambda i: (i, 0)),
                  pl.BlockSpec((HALO, CONV_CH), lambda i: (jnp.maximum(i * hb - 1, 0), 0)),
                  pl.BlockSpec((tm, N_BRANCH * D), lambda i: (i, 0)),
                  pl.BlockSpec((tm, D), lambda i: (i, 0)),
                  const(CONV_WIDTH, CONV_CH), const(1, CONV_CH), const(1, CONV_CH), const(1, CONV_CH),
                  const(aw, D), const(CONV_CH, D), const(D, D), const(1, D)],
        out_specs=[pl.BlockSpec((tm, D), lambda i: (i, 0)), pl.BlockSpec((tm, D), lambda i: (i, 0))],
        out_shape=[jax.ShapeDtypeStruct((T, D), F32), jax.ShapeDtypeStruct((T, D), F32)],
        scratch_shapes=[pltpu.VMEM((HALO + tm, CONV_CH), F32)],
        compiler_params=_params(1),
        name="mix_out_proj",
    )(att, u, u, gate, x2, conv_w.reshape(CONV_WIDTH, CONV_CH), conv_b.reshape(1, CONV_CH),
      ln_g.reshape(1, CONV_CH), ln_b.reshape(1, CONV_CH), w_att_out.astype(BF16), w_conv_out.astype(BF16),
      w_out.astype(BF16), ffn_g.reshape(1, D))


def _top_rows(sc, k, payload=None):
    rows = sc.shape[0]
    iota = lax.broadcasted_iota(I32, sc.shape, 0)
    out_row = lax.broadcasted_iota(I32, (k, sc.shape[1]), 0)
    vals = jnp.zeros((k, sc.shape[1]), F32)
    idxs = jnp.zeros((k, sc.shape[1]), I32)
    for r in range(k):
        m = jnp.max(sc, axis=0, keepdims=True)
        idx = jnp.min(jnp.where(sc == m, iota, rows), axis=0, keepdims=True)
        hit = iota == idx
        rec = idx if payload is None else jnp.max(jnp.where(hit, payload, -1), axis=0, keepdims=True)
        vals = jnp.where(out_row == r, m, vals)
        idxs = jnp.where(out_row == r, rec, idxs)
        sc = jnp.where(hit, -jnp.inf, sc)
    return vals, idxs


def _route_kernel(h2_ref, wq_ref, sk_ref, e_ref, g_ref, qt_scr, et_scr, gt_scr, *, tm):
    half = N_KEYS
    qt_scr[...] = _nt_dot(wq_ref[...], h2_ref[...].astype(BF16)).astype(BF16)

    def head(h, carry):
        tops = []
        for c in range(2):
            row0 = pl.multiple_of((h * 2 + c) * half, half)
            sc = _dot(sk_ref[h * 2 + c], qt_scr[pl.ds(row0, half), :])
            tops.append(_top_rows(sc, PEER_TOPK))
        (a, ia), (b, ib) = tops
        k = PEER_TOPK
        g = SUBLANES
        assert (k // 2) % g == 0 and k % g == 0
        row = lax.broadcasted_iota(I32, (g, a.shape[1]), 0)
        cand_parts, cidx_parts = [], []

        def add(av, iav, bv, ibv, valid_rows):
            s = av + bv
            if valid_rows < g:
                s = jnp.where(row < valid_rows, s, -jnp.inf)
            cand_parts.append(s)
            cidx_parts.append(iav * N_KEYS + ibv)

        for i in range(k):
            nj = k // (i + 1)
            if nj >= g:
                for j0 in range(0, nj, g):
                    add(a[i:i + 1, :], ia[i:i + 1, :], b[j0:j0 + g, :], ib[j0:j0 + g, :], g)
            elif nj > 1:
                add(a[i:i + 1, :], ia[i:i + 1, :], b[0:g, :], ib[0:g, :], nj)
            elif i % g == 0:
                add(a[i:i + g, :], ia[i:i + g, :], b[0:1, :], ib[0:1, :], g)
        cand = jnp.concatenate(cand_parts, axis=0)
        cidx = jnp.concatenate(cidx_parts, axis=0)
        ts, te = _top_rows(cand, PEER_TOPK, payload=cidx)
        ex = jnp.exp(ts - ts[0:1, :])
        gate = ex / jnp.sum(ex, axis=0, keepdims=True)
        r0 = pl.multiple_of(h * PEER_TOPK, PEER_TOPK)
        et_scr[pl.ds(r0, PEER_TOPK), :] = te
        gt_scr[pl.ds(r0, PEER_TOPK), :] = gate
        return carry

    lax.fori_loop(0, PEER_HEADS, head, 0)
    e_ref[...] = et_scr[...].T
    g_ref[...] = gt_scr[...].T


def _peer_route(h2, w_peer_q, sub_keys, tm=256):
    T, D = h2.shape
    nsel = PEER_HEADS * PEER_TOPK
    qd = w_peer_q.shape[1]
    half = sub_keys.shape[-1]
    skb = sub_keys.reshape(PEER_HEADS * 2, N_KEYS, half).astype(BF16)
    return pl.pallas_call(
        functools.partial(_route_kernel, tm=tm),
        grid=(T // tm,),
        in_specs=[pl.BlockSpec((tm, D), lambda i: (i, 0)),
                  _single((qd, D), lambda i: (0, 0)),
                  _single((PEER_HEADS * 2, N_KEYS, half), lambda i: (0, 0, 0))],
        out_specs=[pl.BlockSpec((tm, nsel), lambda i: (i, 0)), pl.BlockSpec((tm, nsel), lambda i: (i, 0))],
        out_shape=[jax.ShapeDtypeStruct((T, nsel), I32), jax.ShapeDtypeStruct((T, nsel), F32)],
        scratch_shapes=[pltpu.VMEM((qd, tm), BF16), pltpu.VMEM((nsel, tm), I32), pltpu.VMEM((nsel, tm), F32)],
        compiler_params=_params(1),
        name="peer_route",
    )(h2, w_peer_q.T.astype(BF16), skb)


TOK = 8


def _gelu_tanh(x):
    return 0.5 * x * (1.0 + jnp.tanh(math.sqrt(2.0 / math.pi) * (x + 0.044715 * (x * x * x))))


def _sublane_sums(ps):
    sub = lax.broadcasted_iota(I32, (SUBLANES, LANES), 0)
    lvl, stride = list(ps), SUBLANES // 2
    while len(lvl) > 1:
        half = len(lvl) // 2
        low = (sub & stride) == 0
        nxt = []
        for n in range(half):
            a, b = lvl[n], lvl[n + half]
            nxt.append(jnp.where(low, a + pltpu.roll(a, SUBLANES - stride, 0), b + pltpu.roll(b, stride, 0)))
        lvl, stride = nxt, stride // 2
    return lvl[0]


def _pack_uv(u, v):
    ne, d = u.shape
    hi = lax.bitcast_convert_type(u.astype(BF16), jnp.uint16).astype(jnp.uint32)
    lo = lax.bitcast_convert_type(v.astype(BF16), jnp.uint16).astype(jnp.uint32)
    return ((hi << 16) | lo).reshape(ne, d // LANES, LANES)


def _u_of(word):
    return pltpu.bitcast(word & jnp.uint32(0xFFFF0000), F32)


def _v_of(word):
    return pltpu.bitcast(word << 16, F32)


def _expert_group(row, h2_ref, x1_ref, gate_ref, o_ref, abc_scr, base, nsel, before_dots=None, before_sum=None):
    dsub = h2_ref.shape[1]
    lane_id = lax.broadcasted_iota(I32, (nsel, LANES), 1)
    st = jnp.zeros((nsel, LANES), F32)
    for t in range(TOK):
        if before_dots is not None:
            before_dots(t)
        xt = h2_ref[base + t]
        qs = []
        for g in range(nsel // SUBLANES):
            ps = [_u_of(row(t, g * SUBLANES + k)) * xt for k in range(SUBLANES)]
            qs.append(_sublane_sums(ps))
        q = jnp.concatenate(qs, axis=0)
        st = jnp.where(lane_id == t, jnp.sum(q, axis=1, keepdims=True), st)
    g8 = gate_ref[base:base + TOK, :]
    gt = jnp.concatenate([g8, jnp.zeros((nsel - TOK, nsel), F32)], axis=0).T
    at = _gelu_tanh(st) * gt

    for t in range(TOK):
        if before_sum is not None:
            before_sum(t)
        abc_scr[t] = jnp.broadcast_to(at[:, t:t + 1], (nsel, LANES))
        accs = [jnp.zeros((dsub, LANES), F32) for _ in range(4)]
        for n in range(nsel):
            accs[n % 4] = accs[n % 4] + abc_scr[t, n:n + 1, :] * _v_of(row(t, n))
        o_ref[base + t] = x1_ref[base + t] + ((accs[0] + accs[1]) + (accs[2] + accs[3]))


def _expert_kernel(idxc_ref, idxn_ref, h2_ref, x1_ref, gate_ref, uv_ref, o_ref, buf, sem, abc_scr, *, nsel):
    i = pl.program_id(0)
    nsteps = pl.num_programs(0)
    rows = TOK * nsel

    def issue_token(idx_ref, tok, s, t, n0=0, n1=nsel):
        for n in range(n0, n1):
            pltpu.make_async_copy(uv_ref.at[idx_ref[tok, n]], buf.at[s, t * nsel + n],
                                  sem.at[s]).start(priority=n % 2)

    def wait_slot(s):
        pltpu.make_async_copy(uv_ref.at[pl.ds(0, rows)], buf.at[s], sem.at[s]).wait()

    @pl.when(i == 0)
    def _():
        for t in range(TOK):
            issue_token(idxc_ref, t, 0, t)

    n_early = (3 * nsel) // 4

    for grp in range(2):
        wait_slot(grp)
        nxt_idx, nxt_tok, nxt_slot = (idxc_ref, TOK, 1) if grp == 0 else (idxn_ref, 0, 0)
        _expert_group(lambda t, n, grp=grp: buf[grp, t * nsel + n], h2_ref, x1_ref, gate_ref, o_ref, abc_scr,
                      grp * TOK, nsel,
                      before_dots=lambda t: issue_token(nxt_idx, nxt_tok + t, nxt_slot, t, 0, n_early),
                      before_sum=lambda t: issue_token(nxt_idx, nxt_tok + t, nxt_slot, t, n_early, nsel))

    @pl.when(i == nsteps - 1)
    def _():
        wait_slot(0)


def _expert_staged_kernel(rows_ref, h2_ref, x1_ref, gate_ref, prev_ref, o_ref, abc_scr, *, nsel):
    del prev_ref
    for grp in range(h2_ref.shape[0] // TOK):
        _expert_group(lambda t, n, grp=grp: rows_ref[(grp * TOK + t) * nsel + n], h2_ref, x1_ref, gate_ref, o_ref,
                      abc_scr, grp * TOK, nsel)


SC_CORES = 2
SC_SUBCORES = 16
SC_CHUNK = 16
SC_NBUF = 4
SC_SLAB = 2048


def _sc_gather(table, idx):
    n = idx.shape[0]
    nw = SC_CORES * SC_SUBCORES
    per_w = n // nw
    nb = SC_NBUF
    assert n % nw == 0 and per_w % SC_SLAB == 0 and SC_SLAB % (nb * SC_CHUNK) == 0
    nslab, nchunk = per_w // SC_SLAB, SC_SLAB // SC_CHUNK
    row_shape = table.shape[1:]
    mesh = plsc.VectorSubcoreMesh(core_axis_name="c", subcore_axis_name="s")

    @functools.partial(
        pl.kernel, mesh=mesh, out_type=jax.ShapeDtypeStruct((n,) + row_shape, table.dtype),
        scratch_types=[pltpu.VMEM((SC_SLAB,), jnp.int32)]
        + [pltpu.VMEM((SC_CHUNK,) + row_shape, table.dtype) for _ in range(nb)]
        + [pltpu.SemaphoreType.DMA for _ in range(2 * nb)])
    def gather(table_hbm, idx_hbm, out_hbm, idx_v, *scr):
        wid = lax.axis_index("s") * SC_CORES + lax.axis_index("c")
        bufs, gsem, wsem = scr[:nb], scr[nb:2 * nb], scr[2 * nb:]

        def gather_copy(c, b):
            return pltpu.make_async_copy(table_hbm.at[idx_v.at[pl.ds(c * SC_CHUNK, SC_CHUNK)]], bufs[b], gsem[b])

        @pl.loop(0, nslab)
        def _(sl):
            base = wid * per_w + sl * SC_SLAB
            pltpu.sync_copy(idx_hbm.at[pl.ds(base, SC_SLAB)], idx_v)

            def write_copy(c, b):
                return pltpu.make_async_copy(bufs[b], out_hbm.at[pl.ds(base + c * SC_CHUNK, SC_CHUNK)], wsem[b])

            for b in range(nb - 1):
                gather_copy(b, b).start()

            @pl.loop(0, nchunk, step=nb)
            def _(c0):
                for b in range(nb):
                    c = c0 + b
                    prev = (b - 1) % nb
                    gather_copy(c, b).wait()
                    write_copy(c, b).start()

                    @pl.when(c >= 1)
                    def _():
                        write_copy(c - 1, prev).wait()

                    @pl.when(c + nb - 1 < nchunk)
                    def _():
                        gather_copy(c + nb - 1, prev).start()

            write_copy(nchunk - 1, (nchunk - 1) % nb).wait()

    return gather(table, idx)


def _experts_dma(eidx, gates, h2, x1, uv, ta):
    T, dsub, _ = x1.shape
    nsel = eidx.shape[1]
    tb = 2 * TOK
    nsteps = ta // tb
    return pl.pallas_call(
        functools.partial(_expert_kernel, nsel=nsel),
        grid=(nsteps,),
        in_specs=[pl.BlockSpec((tb, nsel), lambda i: (i, 0), memory_space=pltpu.SMEM),
                  pl.BlockSpec((tb, nsel), lambda i: (jnp.minimum(i + 1, nsteps - 1), 0), memory_space=pltpu.SMEM),
                  pl.BlockSpec((tb, dsub, LANES), lambda i: (i, 0, 0)),
                  pl.BlockSpec((tb, dsub, LANES), lambda i: (i, 0, 0)),
                  pl.BlockSpec((tb, nsel), lambda i: (i, 0)),
                  pl.BlockSpec(memory_space=pl.ANY)],
        out_specs=pl.BlockSpec((tb, dsub, LANES), lambda i: (i, 0, 0)),
        out_shape=jax.ShapeDtypeStruct((T, dsub, LANES), F32),
        scratch_shapes=[pltpu.VMEM((2, TOK * nsel, dsub, LANES), jnp.uint32), pltpu.SemaphoreType.DMA((2,)),
                        pltpu.VMEM((TOK, nsel, LANES), F32)],
        compiler_params=_params(1),
        name="peer_experts",
    )(eidx, eidx, h2, x1, gates, uv)


def _experts_staged(rows, gates, h2, x1, out, ta):
    T, dsub, _ = x1.shape
    nsel = gates.shape[1]
    tb = 2 * TOK
    first = ta // tb
    tok = lambda i: (first + i, 0, 0)
    return pl.pallas_call(
        functools.partial(_expert_staged_kernel, nsel=nsel),
        grid=((T - ta) // tb,),
        in_specs=[pl.BlockSpec((tb * nsel, dsub, LANES), lambda i: (i, 0, 0)),
                  pl.BlockSpec((tb, dsub, LANES), tok),
                  pl.BlockSpec((tb, dsub, LANES), tok),
                  pl.BlockSpec((tb, nsel), lambda i: (first + i, 0)),
                  pl.BlockSpec(memory_space=pl.ANY)],
        out_specs=pl.BlockSpec((tb, dsub, LANES), tok),
        out_shape=jax.ShapeDtypeStruct((T, dsub, LANES), F32),
        scratch_shapes=[pltpu.VMEM((TOK, nsel, LANES), F32)],
        input_output_aliases={4: 0},
        compiler_params=_params(1),
        name="peer_experts_staged",
    )(rows, h2, x1, gates, out)


def _staged_tokens(T):
    unit = SC_CORES * SC_SUBCORES * SC_SLAB // (PEER_HEADS * PEER_TOPK)
    return (3 * T // 4) // unit * unit


def _peer_gather_start(eidx, uv):
    T, nsel = eidx.shape
    ts = _staged_tokens(T)
    return _sc_gather(uv, eidx[T - ts:].reshape(ts * nsel)) if ts else None


def _peer_dma_part(eidx, gates, h2, x1, uv):
    T, D = x1.shape
    dsub = D // LANES
    ta = T - _staged_tokens(T)
    h3, x3 = h2.reshape(T, dsub, LANES), x1.reshape(T, dsub, LANES)
    return _experts_dma(eidx, gates, h3, x3, uv, ta) if ta else jnp.zeros_like(x3)


def _peer_finish(rows, out, gates, h2, x1):
    T, D = x1.shape
    dsub = D // LANES
    if rows is not None:
        out = _experts_staged(rows, gates, h2.reshape(T, dsub, LANES), x1.reshape(T, dsub, LANES), out,
                              T - _staged_tokens(T))
    return out.reshape(T, D)


def _ple_kernel(x_ref, p_ref, g_ref, wg_ref, wp_ref, o_ref):
    x = x_ref[...]
    h = x * lax.rsqrt(jnp.mean(x * x, axis=-1, keepdims=True) + EPS) * g_ref[...]
    gate = jax.nn.sigmoid(_dot(h.astype(BF16), wg_ref[...]))
    o_ref[...] = x + gate * _dot(p_ref[...].astype(BF16), wp_ref[...])


def _ple(x2, p2, ple_g, w_gate, w_proj, tm=512):
    T, D = x2.shape
    pd = p2.shape[1]
    return pl.pallas_call(
        _ple_kernel,
        grid=(T // tm,),
        in_specs=[pl.BlockSpec((tm, D), lambda i: (i, 0)), pl.BlockSpec((tm, pd), lambda i: (i, 0)),
                  _single((1, D), lambda i: (0, 0)), _single((D, D), lambda i: (0, 0)),
                  _single((pd, D), lambda i: (0, 0))],
        out_specs=pl.BlockSpec((tm, D), lambda i: (i, 0)),
        out_shape=jax.ShapeDtypeStruct((T, D), F32),
        compiler_params=_params(1),
        name="ple",
    )(x2, p2, ple_g.reshape(1, D), w_gate.astype(BF16), w_proj.astype(BF16))


def kernel(x, p, rel_bias, attn_norm_g, w_in, b_gate, q_norm_g, k_norm_g, w_att_out, conv_w, conv_b, conv_ln_g,
           conv_ln_b, w_conv_out, w_out, ffn_norm_g, w_peer_q, peer_sub_keys, peer_u, peer_v, ple_norm_g,
           w_ple_gate, w_ple_proj):
    B, S, D = x.shape
    depth = w_in.shape[0]
    assert S % KQUAD == 0 and D % LANES == 0 and S % 512 == 0
    n_near = _num_near_tiles(S)
    bias = _bias_tiles(rel_bias, n_near)
    xs = [x[b] for b in range(B)]
    for i in range(depth):
        uv = _pack_uv(peer_u[i], peer_v[i])

        def finish(pending, i=i):
            rows, part, gates, h2, x1, b = pending
            x2 = _peer_finish(rows, part, gates, h2, x1)
            return _ple(x2, p[i, b], ple_norm_g[i], w_ple_gate[i], w_ple_proj[i])

        pending, outs = None, []
        for b in range(B):
            q, k, qi, ki, vt, wit, u, gate = _in_proj(xs[b], attn_norm_g[i], w_in[i], b_gate[i], q_norm_g[i],
                                                      k_norm_g[i])
            att = _attention(q, k, qi, ki, vt, wit, bias, 1, S, n_near)
            x1, h2 = _mix(att, u, gate, xs[b], conv_w[i], conv_b[i], conv_ln_g[i], conv_ln_b[i], w_att_out[i],
                          w_conv_out[i], w_out[i], ffn_norm_g[i], S)
            eidx, gates = _peer_route(h2, w_peer_q[i], peer_sub_keys[i])
            rows = _peer_gather_start(eidx, uv)
            if pending is not None:
                outs.append(finish(pending))
            part = _peer_dma_part(eidx, gates, h2, x1, uv)
            pending = (rows, part, gates, h2, x1, b)
        outs.append(finish(pending))
        xs = outs
    return jnp.stack(xs, axis=0)
```

```python
import functools
import math

import numpy as np
import jax
import jax.numpy as jnp
from jax import lax
from jax.experimental import pallas as pl
from jax.experimental.pallas import tpu as pltpu
from jax.experimental.pallas import tpu_sc as plsc

CHUNK = 64
ATT_HEADS = 8
ATT_HEAD_DIM = 64
IDX_HEADS = 8
IDX_DIM = 64
TOPK_MAX = 256
REL_BUCKETS = 32
REL_MAX_DIST = 1024
CONV_CH = 512
CONV_WIDTH = 31
N_BRANCH = 2
PEER_HEADS = 8
N_KEYS = 128
PEER_TOPK = 16
EPS = 1e-6

LANES = 128
SUBLANES = 8
VMEM_LIMIT = 56 * 1024 * 1024

QB = 128
KPAIR = 2 * QB
KQUAD = 4 * QB
INT_MIN = -(2 ** 31)
NEG_BIG = -1e30
LOG2E = math.log2(math.e)

F32 = jnp.float32
BF16 = jnp.bfloat16
I32 = jnp.int32


def _nt_dot(a, b, precision=None):
    return lax.dot_general(a, b, (((1,), (1,)), ((), ())), precision=precision,
                           preferred_element_type=F32)


def _dot(a, b):
    return jnp.dot(a, b, preferred_element_type=F32)


def _single(shape, index_map):
    return pl.BlockSpec(shape, index_map)


def _params(n_grid_dims):
    return pltpu.CompilerParams(dimension_semantics=("arbitrary",) * n_grid_dims,
                                vmem_limit_bytes=VMEM_LIMIT)


def _t5_bucket_np(rel):
    half = REL_BUCKETS // 2
    max_exact = half // 2
    ret = np.where(rel > 0, half, 0)
    n = np.abs(rel)
    nf = np.maximum(n, 1).astype(np.float32)
    large = max_exact + (np.log(nf / np.float32(max_exact)) / np.float32(math.log(REL_MAX_DIST / max_exact))
                         * np.float32(half - max_exact)).astype(np.int32)
    large = np.minimum(large, half - 1)
    return ret + np.where(n < max_exact, n, large)


def _num_near_tiles(seq):
    n = np.arange(1, max(seq, 2 * REL_MAX_DIST) + 1)
    b = _t5_bucket_np(-n)
    sat = REL_BUCKETS // 2 - 1
    unsat = np.nonzero(b != sat)[0]
    n_sat = int(n[unsat[-1]]) + 1 if unsat.size else 1
    return -(-(n_sat + QB - 1) // QB)


def _bias_kernel(rb_ref, o_ref, *, n_near):
    d = pl.program_id(0)
    i = lax.broadcasted_iota(I32, (QB, QB), 0)
    j = lax.broadcasted_iota(I32, (QB, QB), 1)
    rel = i - j - d * QB
    rel = jnp.where(d >= n_near, -8 * REL_MAX_DIST, rel)
    half = REL_BUCKETS // 2
    max_exact = half // 2
    ret = jnp.where(rel > 0, half, 0)
    n = jnp.abs(rel)
    nf = jnp.maximum(n, 1).astype(F32)
    large = max_exact + (jnp.log(nf / max_exact) / math.log(REL_MAX_DIST / max_exact)
                         * (half - max_exact)).astype(I32)
    large = jnp.minimum(large, half - 1)
    bucket = ret + jnp.where(n < max_exact, n, large)
    for h in range(ATT_HEADS):
        acc = jnp.zeros((QB, QB), F32)
        for b in range(REL_BUCKETS):
            acc = jnp.where(bucket == b, rb_ref[b, h], acc)
        o_ref[0, h] = acc * LOG2E


def _bias_tiles(rel_bias, n_near):
    return pl.pallas_call(
        functools.partial(_bias_kernel, n_near=n_near),
        grid=(n_near + 1,),
        in_specs=[pl.BlockSpec(memory_space=pltpu.SMEM)],
        out_specs=pl.BlockSpec((1, ATT_HEADS, QB, QB), lambda d: (d, 0, 0, 0)),
        out_shape=jax.ShapeDtypeStruct((n_near + 1, ATT_HEADS, QB, QB), F32),
        compiler_params=_params(1),
        name="bias_tiles",
    )(rel_bias)


def _inproj_kernel(x_ref, g_ref, wa_ref, wvt_ref, wwit_ref, wglu_ref, wgate_ref, bgate_ref, gq_ref, gk_ref,
                   q_ref, k_ref, qi_ref, ki_ref, vt_ref, wit_ref, u_ref, gate_ref, *, tm):
    x = x_ref[...]
    h = x * lax.rsqrt(jnp.mean(x * x, axis=-1, keepdims=True) + EPS) * g_ref[...]
    hb = h.astype(BF16)
    hp = ATT_HEADS * LANES
    ya = _dot(hb, wa_ref[...])
    for h_i in range(ATT_HEADS):
        sl = slice(h_i * LANES, (h_i + 1) * LANES)
        qh = ya[:, h_i * LANES:(h_i + 1) * LANES]
        ms = jnp.sum(qh * qh, axis=-1, keepdims=True) * (1.0 / ATT_HEAD_DIM)
        q_ref[:, sl] = (qh * lax.rsqrt(ms + EPS) * gq_ref[...]).astype(BF16)
        kh = ya[:, hp + h_i * LANES:hp + (h_i + 1) * LANES]
        ms = jnp.sum(kh * kh, axis=-1, keepdims=True) * (1.0 / ATT_HEAD_DIM)
        k_ref[:, sl] = (kh * lax.rsqrt(ms + EPS) * gk_ref[...]).astype(BF16)
    qi_ref[...] = ya[:, 2 * hp:3 * hp].astype(BF16)
    ki_ref[...] = ya[:, 3 * hp:3 * hp + LANES].astype(BF16)
    vt = _nt_dot(wvt_ref[...], hb).astype(BF16)
    for c in range(tm // KPAIR):
        vt_ref[c] = vt[:, c * KPAIR:(c + 1) * KPAIR]
    wit_ref[...] = _nt_dot(wwit_ref[...], hb)
    glu = _dot(hb, wglu_ref[...])
    u_ref[...] = glu[:, :CONV_CH] * jax.nn.sigmoid(glu[:, CONV_CH:])
    gate_ref[...] = jax.nn.sigmoid(_dot(hb, wgate_ref[...]) + bgate_ref[...]).astype(BF16)


def _pad_heads(w, nh, hd):
    d = w.shape[0]
    w3 = w.reshape(d, nh, hd)
    w3 = jnp.pad(w3, ((0, 0), (0, 0), (0, LANES - hd)))
    return w3.reshape(d, nh * LANES)


def _in_proj(x2, attn_g, w_in, b_gate, q_g, k_g, tm=256):
    T, D = x2.shape
    aw = ATT_HEADS * ATT_HEAD_DIM
    iw = IDX_HEADS * IDX_DIM
    o = 0
    wq = w_in[:, o:o + aw]; o += aw
    wk = w_in[:, o:o + aw]; o += aw
    wv = w_in[:, o:o + aw]; o += aw
    wqi = w_in[:, o:o + iw]; o += iw
    wki = w_in[:, o:o + IDX_DIM]; o += IDX_DIM
    wwi = w_in[:, o:o + IDX_HEADS]; o += IDX_HEADS
    wglu = w_in[:, o:o + 2 * CONV_CH]; o += 2 * CONV_CH
    wgate = w_in[:, o:o + N_BRANCH * D]
    wa = jnp.concatenate([_pad_heads(wq, ATT_HEADS, ATT_HEAD_DIM), _pad_heads(wk, ATT_HEADS, ATT_HEAD_DIM),
                          _pad_heads(wqi, IDX_HEADS, IDX_DIM),
                          jnp.pad(wki, ((0, 0), (0, LANES - IDX_DIM)))], axis=1).astype(BF16)
    na = wa.shape[1]
    hp = ATT_HEADS * LANES
    pad_g = lambda g, s: jnp.pad(g * s, (0, LANES - ATT_HEAD_DIM)).reshape(1, LANES)
    gq = pad_g(q_g, ATT_HEAD_DIM ** -0.5 * LOG2E)
    gk = pad_g(k_g, 1.0)
    const = lambda *shape: _single(shape, lambda i: (0,) * len(shape))
    outs = pl.pallas_call(
        functools.partial(_inproj_kernel, tm=tm),
        grid=(T // tm,),
        in_specs=[pl.BlockSpec((tm, D), lambda i: (i, 0)), const(1, D), const(D, na), const(aw, D),
                  const(IDX_HEADS, D), const(D, 2 * CONV_CH), const(D, N_BRANCH * D), const(1, N_BRANCH * D),
                  const(1, LANES), const(1, LANES)],
        out_specs=[pl.BlockSpec((tm, hp), lambda i: (i, 0)), pl.BlockSpec((tm, hp), lambda i: (i, 0)),
                   pl.BlockSpec((tm, hp), lambda i: (i, 0)), pl.BlockSpec((tm, LANES), lambda i: (i, 0)),
                   pl.BlockSpec((tm // KPAIR, aw, KPAIR), lambda i: (i, 0, 0)),
                   pl.BlockSpec((IDX_HEADS, tm), lambda i: (0, i)),
                   pl.BlockSpec((tm, CONV_CH), lambda i: (i, 0)),
                   pl.BlockSpec((tm, N_BRANCH * D), lambda i: (i, 0))],
        out_shape=[jax.ShapeDtypeStruct((T, hp), BF16), jax.ShapeDtypeStruct((T, hp), BF16),
                   jax.ShapeDtypeStruct((T, hp), BF16), jax.ShapeDtypeStruct((T, LANES), BF16),
                   jax.ShapeDtypeStruct((T // KPAIR, aw, KPAIR), BF16),
                   jax.ShapeDtypeStruct((IDX_HEADS, T), F32),
                   jax.ShapeDtypeStruct((T, CONV_CH), F32),
                   jax.ShapeDtypeStruct((T, N_BRANCH * D), BF16)],
        compiler_params=_params(1),
        name="in_proj",
    )(x2, attn_g.reshape(1, D), wa, wv.T.astype(BF16), wwi.T.astype(BF16), wglu.astype(BF16),
      wgate.astype(BF16), b_gate.reshape(1, N_BRANCH * D), gq, gk)
    return outs


def _attn_kernel(q_ref, qi_ref, wit_ref, k_hbm, ki_hbm, vt_hbm, bias_hbm, o_ref, k_ref, ki_ref, vt_ref, bias_ref,
                 load_sem, key_scr, att_scr, *scr, seq, n_near, topk):
    qb = pl.program_id(1)

    @pl.when(qb == 0)
    def _():
        b = pl.program_id(0)
        loads = [pltpu.make_async_copy(k_hbm.at[b], k_ref, load_sem.at[0]),
                 pltpu.make_async_copy(ki_hbm.at[b], ki_ref, load_sem.at[1]),
                 pltpu.make_async_copy(vt_hbm.at[b], vt_ref, load_sem.at[2]),
                 pltpu.make_async_copy(bias_hbm, bias_ref, load_sem.at[3])]
        for c in loads:
            c.start()
        for c in loads:
            c.wait()

    nquad = (qb + 4) // 4
    lane_t = lax.broadcasted_iota(I32, (1, QB), 1) + qb * QB
    qchunk = lane_t // CHUNK
    sub = lax.broadcasted_iota(I32, (KQUAD, QB), 0)
    idx_scale = (IDX_DIM ** -0.5) * (IDX_HEADS ** -0.5)
    wrow = wit_ref[...] * idx_scale

    qi_all = jnp.concatenate([qi_ref[:, h * LANES:(h + 1) * LANES] for h in range(IDX_HEADS)], axis=0)

    def score_quad(j, carry):
        d = _nt_dot(ki_ref[j], qi_all)
        acc = jnp.zeros((KQUAD, QB), F32)
        for h in range(IDX_HEADS):
            acc = acc + jnp.maximum(d[:, h * QB:(h + 1) * QB], 0.0) * wrow[h:h + 1, :]
        bits = pltpu.bitcast(acc, I32)
        skey = bits ^ ((bits >> 31) & 0x7FFFFFFF)
        visible = (sub + j * KQUAD) // CHUNK <= qchunk
        key_scr[j] = jnp.where(visible, skey, INT_MIN)
        return carry

    lax.fori_loop(0, nquad, score_quad, 0)

    nvis = (qchunk + 1) * CHUNK
    kk = jnp.minimum(topk, nvis)

    def count(pred):
        def body(j, acc8):
            m = jnp.where(pred(key_scr[j], j), 1, 0)
            return acc8 + m.reshape(KQUAD // SUBLANES, SUBLANES, QB).sum(axis=0)
        return lax.fori_loop(0, nquad, body, jnp.zeros((SUBLANES, QB), I32)).sum(axis=0, keepdims=True)

    def bit_step(i, carry):
        ans, cnt = carry
        cand = ans + lax.shift_left(jnp.int32(1), 31 - i)
        c = count(lambda blk, j: blk >= cand)
        take = c >= kk
        return jnp.where(take, cand, ans), jnp.where(take, c, cnt)

    thr, cnt = lax.fori_loop(0, 32, bit_step,
                             (jnp.full((1, QB), INT_MIN, I32), jnp.full((1, QB), 0, I32) + nquad * KQUAD))

    @pl.when(jnp.max(cnt - kk) > 0)
    def _():
        n_gt = count(lambda blk, j: blk > thr)
        need = kk - n_gt

        def idx_step(i, jbound):
            cand = jbound + lax.shift_left(jnp.int32(1), (seq.bit_length() - 1) - i)
            c = count(lambda blk, j: (blk == thr) & (sub + j * KQUAD < cand))
            return jnp.where(c <= need, cand, jbound)

        jbound = lax.fori_loop(0, seq.bit_length(), idx_step, jnp.zeros((1, QB), I32))

        def drop(j, carry):
            blk = key_scr[j]
            key_scr[j] = jnp.where((blk == thr) & (sub + j * KQUAD >= jbound), INT_MIN, blk)
            return carry

        lax.fori_loop(0, nquad, drop, 0)

    acc_refs, s_even, s_odd = scr[:ATT_HEADS], scr[ATT_HEADS], scr[ATT_HEADS + 1]
    for acc_ref in acc_refs:
        acc_ref[...] = jnp.zeros(acc_ref.shape, F32)
    head_row = lax.broadcasted_iota(I32, (ATT_HEADS, QB), 0)
    last_pair = k_ref.shape[0] - 1

    def qk(jp, s_ref):
        jc = jnp.minimum(jp, last_pair)
        for h in range(ATT_HEADS):
            s_ref[h] = _nt_dot(k_ref[jc, :, h * LANES:(h + 1) * LANES], q_ref[:, h * LANES:(h + 1) * LANES])

    def softmax_pv(jq, half, s_ref, carry):
        m_all, l_all = carry
        jp = 2 * jq + half
        keys = key_scr[jq, half * KPAIR:(half + 1) * KPAIR, :]
        masked = jnp.where(keys >= thr, 0.0, -jnp.inf)
        tile0 = jnp.clip(qb - 2 * jp, 0, n_near)
        tile1 = jnp.clip(qb - 2 * jp - 1, 0, n_near)
        for h in range(ATT_HEADS):
            rows = slice(h * ATT_HEAD_DIM, (h + 1) * ATT_HEAD_DIM)
            bias = jnp.concatenate([bias_ref[tile0, h], bias_ref[tile1, h]], axis=0)
            s = s_ref[h] + bias + masked
            m = m_all[h:h + 1, :]
            m_new = jnp.maximum(m, jnp.max(s, axis=0, keepdims=True))
            p = jnp.exp2(s - m_new)
            alpha = jnp.exp2(m - m_new)
            l_new = alpha * l_all[h:h + 1, :] + jnp.sum(p, axis=0, keepdims=True)
            m_all = jnp.where(head_row == h, m_new, m_all)
            l_all = jnp.where(head_row == h, l_new, l_all)
            acc_refs[h][...] = alpha * acc_refs[h][...] + _dot(vt_ref[jp, rows, :], p.astype(BF16))
        return m_all, l_all

    def att_quad(jq, carry):
        qk(2 * jq + 1, s_odd)
        carry = softmax_pv(jq, 0, s_even, carry)
        qk(2 * jq + 2, s_even)
        return softmax_pv(jq, 1, s_odd, carry)

    qk(0, s_even)
    _, l_all = lax.fori_loop(0, nquad, att_quad,
                             (jnp.full((ATT_HEADS, QB), NEG_BIG, F32), jnp.zeros((ATT_HEADS, QB), F32)))
    for h in range(ATT_HEADS):
        rows = slice(h * ATT_HEAD_DIM, (h + 1) * ATT_HEAD_DIM)
        att_scr[rows, :] = acc_refs[h][...] / l_all[h:h + 1, :]
    o_ref[...] = att_scr[...].T.astype(o_ref.dtype)


def _attention(q, k, qi, ki, vt, wit, bias, B, S, n_near):
    T = B * S
    nqb = S // QB
    hp = ATT_HEADS * LANES
    aw = ATT_HEADS * ATT_HEAD_DIM
    topk = min(TOPK_MAX, S // 4)
    npr, nqd = S // KPAIR, S // KQUAD
    k4 = k.reshape(B, npr, KPAIR, hp)
    ki4 = ki.reshape(B, nqd, KQUAD, LANES)
    vt4 = vt.reshape(B, npr, aw, KPAIR)
    return pl.pallas_call(
        functools.partial(_attn_kernel, seq=S, n_near=n_near, topk=topk),
        grid=(B, nqb),
        in_specs=[pl.BlockSpec((QB, hp), lambda b, i: (b * nqb + i, 0)),
                  pl.BlockSpec((QB, hp), lambda b, i: (b * nqb + i, 0)),
                  pl.BlockSpec((IDX_HEADS, QB), lambda b, i: (0, b * nqb + i)),
                  pl.BlockSpec(memory_space=pl.ANY), pl.BlockSpec(memory_space=pl.ANY),
                  pl.BlockSpec(memory_space=pl.ANY), pl.BlockSpec(memory_space=pl.ANY)],
        out_specs=pl.BlockSpec((QB, aw), lambda b, i: (b * nqb + i, 0)),
        out_shape=jax.ShapeDtypeStruct((T, aw), BF16),
        scratch_shapes=[pltpu.VMEM((npr, KPAIR, hp), BF16), pltpu.VMEM((nqd, KQUAD, LANES), BF16),
                        pltpu.VMEM((npr, aw, KPAIR), BF16), pltpu.VMEM((n_near + 1, ATT_HEADS, QB, QB), F32),
                        pltpu.SemaphoreType.DMA((4,)),
                        pltpu.VMEM((nqd, KQUAD, QB), I32), pltpu.VMEM((aw, QB), F32)]
        + [pltpu.VMEM((ATT_HEAD_DIM, QB), F32) for _ in range(ATT_HEADS)]
        + [pltpu.VMEM((ATT_HEADS, KPAIR, QB), F32) for _ in range(2)],
        compiler_params=_params(2),
        name="dsa_attention",
    )(q, qi, wit, k4, ki4, vt4, bias)


HALO = 32


def _mix_kernel(att_ref, u_ref, halo_ref, gate_ref, x_ref, cw_ref, cb_ref, lng_ref, lnb_ref, wao_ref, wco_ref,
                wout_ref, gffn_ref, x1_ref, h2_ref, ext_scr, *, tm, tiles_per_seq):
    i = pl.program_id(0)
    first = (i % tiles_per_seq) == 0
    ext_scr[0:HALO, :] = jnp.where(first, 0.0, halo_ref[...])
    ext_scr[HALO:HALO + tm, :] = u_ref[...]
    y = jnp.zeros((tm, CONV_CH), F32)
    for j in range(CONV_WIDTH):
        y = y + cw_ref[j:j + 1, :] * ext_scr[pl.ds(HALO - (CONV_WIDTH - 1) + j, tm), :]
    y = y + cb_ref[...]
    mu = jnp.mean(y, axis=-1, keepdims=True)
    yc = y - mu
    yn = yc * lax.rsqrt(jnp.mean(yc * yc, axis=-1, keepdims=True) + EPS) * lng_ref[...] + lnb_ref[...]
    z = yn * jax.nn.sigmoid(yn)
    y_conv = _dot(z.astype(BF16), wco_ref[...])
    y_att = _dot(att_ref[...], wao_ref[...])
    d = y_att.shape[1]
    g = gate_ref[...]
    mixed = g[:, :d].astype(F32) * y_att + g[:, d:].astype(F32) * y_conv
    x1 = x_ref[...] + _dot(mixed.astype(BF16), wout_ref[...])
    x1_ref[...] = x1
    h2_ref[...] = x1 * lax.rsqrt(jnp.mean(x1 * x1, axis=-1, keepdims=True) + EPS) * gffn_ref[...]


def _mix(att, u, gate, x2, conv_w, conv_b, ln_g, ln_b, w_att_out, w_conv_out, w_out, ffn_g, S, tm=256):
    T, D = x2.shape
    aw = att.shape[1]
    const = lambda *shape: _single(shape, lambda i: (0,) * len(shape))
    hb = tm // HALO
    return pl.pallas_call(
        functools.partial(_mix_kernel, tm=tm, tiles_per_seq=S // tm),
        grid=(T // tm,),
        in_specs=[pl.BlockSpec((tm, aw), lambda i: (i, 0)),
                  pl.BlockSpec((tm, CONV_CH), lambda i: (i, 0)),
                  pl.BlockSpec((HALO, CONV_CH), lambda i: (jnp.maximum(i * hb - 1, 0), 0)),
                  pl.BlockSpec((tm, N_BRANCH * D), lambda i: (i, 0)),
                  pl.BlockSpec((tm, D), lambda i: (i, 0)),
                  const(CONV_WIDTH, CONV_CH), const(1, CONV_CH), const(1, CONV_CH), const(1, CONV_CH),
                  const(aw, D), const(CONV_CH, D), const(D, D), const(1, D)],
        out_specs=[pl.BlockSpec((tm, D), lambda i: (i, 0)), pl.BlockSpec((tm, D), lambda i: (i, 0))],
        out_shape=[jax.ShapeDtypeStruct((T, D), F32), jax.ShapeDtypeStruct((T, D), F32)],
        scratch_shapes=[pltpu.VMEM((HALO + tm, CONV_CH), F32)],
        compiler_params=_params(1),
        name="mix_out_proj",
    )(att, u, u, gate, x2, conv_w.reshape(CONV_WIDTH, CONV_CH), conv_b.reshape(1, CONV_CH),
      ln_g.reshape(1, CONV_CH), ln_b.reshape(1, CONV_CH), w_att_out.astype(BF16), w_conv_out.astype(BF16),
      w_out.astype(BF16), ffn_g.reshape(1, D))


def _top_rows(sc, k, payload=None):
    rows = sc.shape[0]
    iota = lax.broadcasted_iota(I32, sc.shape, 0)
    out_row = lax.broadcasted_iota(I32, (k, sc.shape[1]), 0)
    vals = jnp.zeros((k, sc.shape[1]), F32)
    idxs = jnp.zeros((k, sc.shape[1]), I32)
    for r in range(k):
        m = jnp.max(sc, axis=0, keepdims=True)
        idx = jnp.min(jnp.where(sc == m, iota, rows), axis=0, keepdims=True)
        hit = iota == idx
        rec = idx if payload is None else jnp.max(jnp.where(hit, payload, -1), axis=0, keepdims=True)
        vals = jnp.where(out_row == r, m, vals)
        idxs = jnp.where(out_row == r, rec, idxs)
        sc = jnp.where(hit, -jnp.inf, sc)
    return vals, idxs


def _route_kernel(h2_ref, wq_ref, sk_ref, e_ref, g_ref, qt_scr, et_scr, gt_scr, *, tm):
    half = N_KEYS
    qt_scr[...] = _nt_dot(wq_ref[...], h2_ref[...].astype(BF16)).astype(BF16)

    def head(h, carry):
        tops = []
        for c in range(2):
            row0 = pl.multiple_of((h * 2 + c) * half, half)
            sc = _dot(sk_ref[h * 2 + c], qt_scr[pl.ds(row0, half), :])
            tops.append(_top_rows(sc, PEER_TOPK))
        (a, ia), (b, ib) = tops
        k = PEER_TOPK
        g = SUBLANES
        assert (k // 2) % g == 0 and k % g == 0
        row = lax.broadcasted_iota(I32, (g, a.shape[1]), 0)
        cand_parts, cidx_parts = [], []

        def add(av, iav, bv, ibv, valid_rows):
            s = av + bv
            if valid_rows < g:
                s = jnp.where(row < valid_rows, s, -jnp.inf)
            cand_parts.append(s)
            cidx_parts.append(iav * N_KEYS + ibv)

        for i in range(k):
            nj = k // (i + 1)
            if nj >= g:
                for j0 in range(0, nj, g):
                    add(a[i:i + 1, :], ia[i:i + 1, :], b[j0:j0 + g, :], ib[j0:j0 + g, :], g)
            elif nj > 1:
                add(a[i:i + 1, :], ia[i:i + 1, :], b[0:g, :], ib[0:g, :], nj)
            elif i % g == 0:
                add(a[i:i + g, :], ia[i:i + g, :], b[0:1, :], ib[0:1, :], g)
        cand = jnp.concatenate(cand_parts, axis=0)
        cidx = jnp.concatenate(cidx_parts, axis=0)
        ts, te = _top_rows(cand, PEER_TOPK, payload=cidx)
        ex = jnp.exp(ts - ts[0:1, :])
        gate = ex / jnp.sum(ex, axis=0, keepdims=True)
        r0 = pl.multiple_of(h * PEER_TOPK, PEER_TOPK)
        et_scr[pl.ds(r0, PEER_TOPK), :] = te
        gt_scr[pl.ds(r0, PEER_TOPK), :] = gate
        return carry

    lax.fori_loop(0, PEER_HEADS, head, 0)
    e_ref[...] = et_scr[...].T
    g_ref[...] = gt_scr[...].T


def _peer_route(h2, w_peer_q, sub_keys, tm=256):
    T, D = h2.shape
    nsel = PEER_HEADS * PEER_TOPK
    qd = w_peer_q.shape[1]
    half = sub_keys.shape[-1]
    skb = sub_keys.reshape(PEER_HEADS * 2, N_KEYS, half).astype(BF16)
    return pl.pallas_call(
        functools.partial(_route_kernel, tm=tm),
        grid=(T // tm,),
        in_specs=[pl.BlockSpec((tm, D), lambda i: (i, 0)),
                  _single((qd, D), lambda i: (0, 0)),
                  _single((PEER_HEADS * 2, N_KEYS, half), lambda i: (0, 0, 0))],
        out_specs=[pl.BlockSpec((tm, nsel), lambda i: (i, 0)), pl.BlockSpec((tm, nsel), lambda i: (i, 0))],
        out_shape=[jax.ShapeDtypeStruct((T, nsel), I32), jax.ShapeDtypeStruct((T, nsel), F32)],
        scratch_shapes=[pltpu.VMEM((qd, tm), BF16), pltpu.VMEM((nsel, tm), I32), pltpu.VMEM((nsel, tm), F32)],
        compiler_params=_params(1),
        name="peer_route",
    )(h2, w_peer_q.T.astype(BF16), skb)


TOK = 8


def _gelu_tanh(x):
    return 0.5 * x * (1.0 + jnp.tanh(math.sqrt(2.0 / math.pi) * (x + 0.044715 * (x * x * x))))


def _sublane_sums(ps):
    sub = lax.broadcasted_iota(I32, (SUBLANES, LANES), 0)
    lvl, stride = list(ps), SUBLANES // 2
    while len(lvl) > 1:
        half = len(lvl) // 2
        low = (sub & stride) == 0
        nxt = []
        for n in range(half):
            a, b = lvl[n], lvl[n + half]
            nxt.append(jnp.where(low, a + pltpu.roll(a, SUBLANES - stride, 0), b + pltpu.roll(b, stride, 0)))
        lvl, stride = nxt, stride // 2
    return lvl[0]


def _pack_uv(u, v):
    ne, d = u.shape
    hi = lax.bitcast_convert_type(u.astype(BF16), jnp.uint16).astype(jnp.uint32)
    lo = lax.bitcast_convert_type(v.astype(BF16), jnp.uint16).astype(jnp.uint32)
    return ((hi << 16) | lo).reshape(ne, d // LANES, LANES)


def _u_of(word):
    return pltpu.bitcast(word & jnp.uint32(0xFFFF0000), F32)


def _v_of(word):
    return pltpu.bitcast(word << 16, F32)


def _expert_group(row, h2_ref, x1_ref, gate_ref, o_ref, abc_scr, base, nsel, before_dots=None, before_sum=None):
    dsub = h2_ref.shape[1]
    lane_id = lax.broadcasted_iota(I32, (nsel, LANES), 1)
    st = jnp.zeros((nsel, LANES), F32)
    for t in range(TOK):
        if before_dots is not None:
            before_dots(t)
        xt = h2_ref[base + t]
        qs = []
        for g in range(nsel // SUBLANES):
            ps = [_u_of(row(t, g * SUBLANES + k)) * xt for k in range(SUBLANES)]
            qs.append(_sublane_sums(ps))
        q = jnp.concatenate(qs, axis=0)
        st = jnp.where(lane_id == t, jnp.sum(q, axis=1, keepdims=True), st)
    g8 = gate_ref[base:base + TOK, :]
    gt = jnp.concatenate([g8, jnp.zeros((nsel - TOK, nsel), F32)], axis=0).T
    at = _gelu_tanh(st) * gt

    for t in range(TOK):
        if before_sum is not None:
            before_sum(t)
        abc_scr[t] = jnp.broadcast_to(at[:, t:t + 1], (nsel, LANES))
        accs = [jnp.zeros((dsub, LANES), F32) for _ in range(4)]
        for n in range(nsel):
            accs[n % 4] = accs[n % 4] + abc_scr[t, n:n + 1, :] * _v_of(row(t, n))
        o_ref[base + t] = x1_ref[base + t] + ((accs[0] + accs[1]) + (accs[2] + accs[3]))


def _expert_kernel(idxc_ref, idxn_ref, h2_ref, x1_ref, gate_ref, uv_ref, o_ref, buf, sem, abc_scr, *, nsel):
    i = pl.program_id(0)
    nsteps = pl.num_programs(0)
    rows = TOK * nsel

    def issue_token(idx_ref, tok, s, t, n0=0, n1=nsel):
        for n in range(n0, n1):
            pltpu.make_async_copy(uv_ref.at[idx_ref[tok, n]], buf.at[s, t * nsel + n],
                                  sem.at[s]).start(priority=n % 2)

    def wait_slot(s):
        pltpu.make_async_copy(uv_ref.at[pl.ds(0, rows)], buf.at[s], sem.at[s]).wait()

    @pl.when(i == 0)
    def _():
        for t in range(TOK):
            issue_token(idxc_ref, t, 0, t)

    n_early = (3 * nsel) // 4

    for grp in range(2):
        wait_slot(grp)
        nxt_idx, nxt_tok, nxt_slot = (idxc_ref, TOK, 1) if grp == 0 else (idxn_ref, 0, 0)
        _expert_group(lambda t, n, grp=grp: buf[grp, t * nsel + n], h2_ref, x1_ref, gate_ref, o_ref, abc_scr,
                      grp * TOK, nsel,
                      before_dots=lambda t: issue_token(nxt_idx, nxt_tok + t, nxt_slot, t, 0, n_early),
                      before_sum=lambda t: issue_token(nxt_idx, nxt_tok + t, nxt_slot, t, n_early, nsel))

    @pl.when(i == nsteps - 1)
    def _():
        wait_slot(0)


def _expert_staged_kernel(rows_ref, h2_ref, x1_ref, gate_ref, prev_ref, o_ref, abc_scr, *, nsel):
    del prev_ref
    for grp in range(h2_ref.shape[0] // TOK):
        _expert_group(lambda t, n, grp=grp: rows_ref[(grp * TOK + t) * nsel + n], h2_ref, x1_ref, gate_ref, o_ref,
                      abc_scr, grp * TOK, nsel)


SC_CORES = 2
SC_SUBCORES = 16
SC_CHUNK = 16
SC_NBUF = 4
SC_SLAB = 2048


def _sc_gather(table, idx):
    n = idx.shape[0]
    nw = SC_CORES * SC_SUBCORES
    per_w = n // nw
    nb = SC_NBUF
    assert n % nw == 0 and per_w % SC_SLAB == 0 and SC_SLAB % (nb * SC_CHUNK) == 0
    nslab, nchunk = per_w // SC_SLAB, SC_SLAB // SC_CHUNK
    row_shape = table.shape[1:]
    mesh = plsc.VectorSubcoreMesh(core_axis_name="c", subcore_axis_name="s")

    @functools.partial(
        pl.kernel, mesh=mesh, out_type=jax.ShapeDtypeStruct((n,) + row_shape, table.dtype),
        scratch_types=[pltpu.VMEM((SC_SLAB,), jnp.int32)]
        + [pltpu.VMEM((SC_CHUNK,) + row_shape, table.dtype) for _ in range(nb)]
        + [pltpu.SemaphoreType.DMA for _ in range(2 * nb)])
    def gather(table_hbm, idx_hbm, out_hbm, idx_v, *scr):
        wid = lax.axis_index("s") * SC_CORES + lax.axis_index("c")
        bufs, gsem, wsem = scr[:nb], scr[nb:2 * nb], scr[2 * nb:]

        def gather_copy(c, b):
            return pltpu.make_async_copy(table_hbm.at[idx_v.at[pl.ds(c * SC_CHUNK, SC_CHUNK)]], bufs[b], gsem[b])

        @pl.loop(0, nslab)
        def _(sl):
            base = wid * per_w + sl * SC_SLAB
            pltpu.sync_copy(idx_hbm.at[pl.ds(base, SC_SLAB)], idx_v)

            def write_copy(c, b):
                return pltpu.make_async_copy(bufs[b], out_hbm.at[pl.ds(base + c * SC_CHUNK, SC_CHUNK)], wsem[b])

            for b in range(nb - 1):
                gather_copy(b, b).start()

            @pl.loop(0, nchunk, step=nb)
            def _(c0):
                for b in range(nb):
                    c = c0 + b
                    prev = (b - 1) % nb
                    gather_copy(c, b).wait()
                    write_copy(c, b).start()

                    @pl.when(c >= 1)
                    def _():
                        write_copy(c - 1, prev).wait()

                    @pl.when(c + nb - 1 < nchunk)
                    def _():
                        gather_copy(c + nb - 1, prev).start()

            write_copy(nchunk - 1, (nchunk - 1) % nb).wait()

    return gather(table, idx)


def _experts_dma(eidx, gates, h2, x1, uv, ta):
    T, dsub, _ = x1.shape
    nsel = eidx.shape[1]
    tb = 2 * TOK
    nsteps = ta // tb
    return pl.pallas_call(
        functools.partial(_expert_kernel, nsel=nsel),
        grid=(nsteps,),
        in_specs=[pl.BlockSpec((tb, nsel), lambda i: (i, 0), memory_space=pltpu.SMEM),
                  pl.BlockSpec((tb, nsel), lambda i: (jnp.minimum(i + 1, nsteps - 1), 0), memory_space=pltpu.SMEM),
                  pl.BlockSpec((tb, dsub, LANES), lambda i: (i, 0, 0)),
                  pl.BlockSpec((tb, dsub, LANES), lambda i: (i, 0, 0)),
                  pl.BlockSpec((tb, nsel), lambda i: (i, 0)),
                  pl.BlockSpec(memory_space=pl.ANY)],
        out_specs=pl.BlockSpec((tb, dsub, LANES), lambda i: (i, 0, 0)),
        out_shape=jax.ShapeDtypeStruct((T, dsub, LANES), F32),
        scratch_shapes=[pltpu.VMEM((2, TOK * nsel, dsub, LANES), jnp.uint32), pltpu.SemaphoreType.DMA((2,)),
                        pltpu.VMEM((TOK, nsel, LANES), F32)],
        compiler_params=_params(1),
        name="peer_experts",
    )(eidx, eidx, h2, x1, gates, uv)


def _experts_staged(rows, gates, h2, x1, out, ta):
    T, dsub, _ = x1.shape
    nsel = gates.shape[1]
    tb = 2 * TOK
    first = ta // tb
    tok = lambda i: (first + i, 0, 0)
    return pl.pallas_call(
        functools.partial(_expert_staged_kernel, nsel=nsel),
        grid=((T - ta) // tb,),
        in_specs=[pl.BlockSpec((tb * nsel, dsub, LANES), lambda i: (i, 0, 0)),
                  pl.BlockSpec((tb, dsub, LANES), tok),
                  pl.BlockSpec((tb, dsub, LANES), tok),
                  pl.BlockSpec((tb, nsel), lambda i: (first + i, 0)),
                  pl.BlockSpec(memory_space=pl.ANY)],
        out_specs=pl.BlockSpec((tb, dsub, LANES), tok),
        out_shape=jax.ShapeDtypeStruct((T, dsub, LANES), F32),
        scratch_shapes=[pltpu.VMEM((TOK, nsel, LANES), F32)],
        input_output_aliases={4: 0},
        compiler_params=_params(1),
        name="peer_experts_staged",
    )(rows, h2, x1, gates, out)


def _staged_tokens(T):
    unit = SC_CORES * SC_SUBCORES * SC_SLAB // (PEER_HEADS * PEER_TOPK)
    return T // unit * unit


def _peer_gather_start(eidx, uv):
    T, nsel = eidx.shape
    ts = _staged_tokens(T)
    return _sc_gather(uv, eidx[T - ts:].reshape(ts * nsel)) if ts else None


def _peer_dma_part(eidx, gates, h2, x1, uv):
    T, D = x1.shape
    dsub = D // LANES
    ta = T - _staged_tokens(T)
    h3, x3 = h2.reshape(T, dsub, LANES), x1.reshape(T, dsub, LANES)
    return _experts_dma(eidx, gates, h3, x3, uv, ta) if ta else jnp.zeros_like(x3)


def _peer_finish(rows, out, gates, h2, x1):
    T, D = x1.shape
    dsub = D // LANES
    if rows is not None:
        out = _experts_staged(rows, gates, h2.reshape(T, dsub, LANES), x1.reshape(T, dsub, LANES), out,
                              T - _staged_tokens(T))
    return out.reshape(T, D)


def _ple_kernel(x_ref, p_ref, g_ref, wg_ref, wp_ref, o_ref):
    x = x_ref[...]
    h = x * lax.rsqrt(jnp.mean(x * x, axis=-1, keepdims=True) + EPS) * g_ref[...]
    gate = jax.nn.sigmoid(_dot(h.astype(BF16), wg_ref[...]))
    o_ref[...] = x + gate * _dot(p_ref[...].astype(BF16), wp_ref[...])


def _ple(x2, p2, ple_g, w_gate, w_proj, tm=512):
    T, D = x2.shape
    pd = p2.shape[1]
    return pl.pallas_call(
        _ple_kernel,
        grid=(T // tm,),
        in_specs=[pl.BlockSpec((tm, D), lambda i: (i, 0)), pl.BlockSpec((tm, pd), lambda i: (i, 0)),
                  _single((1, D), lambda i: (0, 0)), _single((D, D), lambda i: (0, 0)),
                  _single((pd, D), lambda i: (0, 0))],
        out_specs=pl.BlockSpec((tm, D), lambda i: (i, 0)),
        out_shape=jax.ShapeDtypeStruct((T, D), F32),
        compiler_params=_params(1),
        name="ple",
    )(x2, p2, ple_g.reshape(1, D), w_gate.astype(BF16), w_proj.astype(BF16))


def kernel(x, p, rel_bias, attn_norm_g, w_in, b_gate, q_norm_g, k_norm_g, w_att_out, conv_w, conv_b, conv_ln_g,
           conv_ln_b, w_conv_out, w_out, ffn_norm_g, w_peer_q, peer_sub_keys, peer_u, peer_v, ple_norm_g,
           w_ple_gate, w_ple_proj):
    B, S, D = x.shape
    depth = w_in.shape[0]
    assert S % KQUAD == 0 and D % LANES == 0 and S % 512 == 0
    n_near = _num_near_tiles(S)
    bias = _bias_tiles(rel_bias, n_near)
    xs = [x[b] for b in range(B)]
    for i in range(depth):
        uv = _pack_uv(peer_u[i], peer_v[i])

        def finish(pending, i=i):
            rows, part, gates, h2, x1, b = pending
            x2 = _peer_finish(rows, part, gates, h2, x1)
            return _ple(x2, p[i, b], ple_norm_g[i], w_ple_gate[i], w_ple_proj[i])

        pending, outs = None, []
        for b in range(B):
            q, k, qi, ki, vt, wit, u, gate = _in_proj(xs[b], attn_norm_g[i], w_in[i], b_gate[i], q_norm_g[i],
                                                      k_norm_g[i])
            att = _attention(q, k, qi, ki, vt, wit, bias, 1, S, n_near)
            x1, h2 = _mix(att, u, gate, xs[b], conv_w[i], conv_b[i], conv_ln_g[i], conv_ln_b[i], w_att_out[i],
                          w_conv_out[i], w_out[i], ffn_norm_g[i], S)
            eidx, gates = _peer_route(h2, w_peer_q[i], peer_sub_keys[i])
            rows = _peer_gather_start(eidx, uv)
            if pending is not None:
                outs.append(finish(pending))
            part = _peer_dma_part(eidx, gates, h2, x1, uv)
            pending = (rows, part, gates, h2, x1, b)
        outs.append(finish(pending))
        xs = outs
    return jnp.stack(xs, axis=0)
```

```python
import functools
import math

import numpy as np
import jax
import jax.numpy as jnp
from jax import lax
from jax.experimental import pallas as pl
from jax.experimental.pallas import tpu as pltpu
from jax.experimental.pallas import tpu_sc as plsc

CHUNK = 64
ATT_HEADS = 8
ATT_HEAD_DIM = 64
IDX_HEADS = 8
IDX_DIM = 64
TOPK_MAX = 256
REL_BUCKETS = 32
REL_MAX_DIST = 1024
CONV_CH = 512
CONV_WIDTH = 31
N_BRANCH = 2
PEER_HEADS = 8
N_KEYS = 128
PEER_TOPK = 16
EPS = 1e-6

LANES = 128
SUBLANES = 8
VMEM_LIMIT = 56 * 1024 * 1024

QB = 128
KPAIR = 2 * QB
KQUAD = 4 * QB
INT_MIN = -(2 ** 31)
NEG_BIG = -1e30
LOG2E = math.log2(math.e)

F32 = jnp.float32
BF16 = jnp.bfloat16
I32 = jnp.int32


def _nt_dot(a, b, precision=None):
    return lax.dot_general(a, b, (((1,), (1,)), ((), ())), precision=precision,
                           preferred_element_type=F32)


def _dot(a, b):
    return jnp.dot(a, b, preferred_element_type=F32)


def _single(shape, index_map):
    return pl.BlockSpec(shape, index_map)


def _params(n_grid_dims):
    return pltpu.CompilerParams(dimension_semantics=("arbitrary",) * n_grid_dims,
                                vmem_limit_bytes=VMEM_LIMIT)


def _t5_bucket_np(rel):
    half = REL_BUCKETS // 2
    max_exact = half // 2
    ret = np.where(rel > 0, half, 0)
    n = np.abs(rel)
    nf = np.maximum(n, 1).astype(np.float32)
    large = max_exact + (np.log(nf / np.float32(max_exact)) / np.float32(math.log(REL_MAX_DIST / max_exact))
                         * np.float32(half - max_exact)).astype(np.int32)
    large = np.minimum(large, half - 1)
    return ret + np.where(n < max_exact, n, large)


def _num_near_tiles(seq):
    n = np.arange(1, max(seq, 2 * REL_MAX_DIST) + 1)
    b = _t5_bucket_np(-n)
    sat = REL_BUCKETS // 2 - 1
    unsat = np.nonzero(b != sat)[0]
    n_sat = int(n[unsat[-1]]) + 1 if unsat.size else 1
    return -(-(n_sat + QB - 1) // QB)


def _bias_kernel(rb_ref, o_ref, *, n_near):
    d = pl.program_id(0)
    i = lax.broadcasted_iota(I32, (QB, QB), 0)
    j = lax.broadcasted_iota(I32, (QB, QB), 1)
    rel = i - j - d * QB
    rel = jnp.where(d >= n_near, -8 * REL_MAX_DIST, rel)
    half = REL_BUCKETS // 2
    max_exact = half // 2
    ret = jnp.where(rel > 0, half, 0)
    n = jnp.abs(rel)
    nf = jnp.maximum(n, 1).astype(F32)
    large = max_exact + (jnp.log(nf / max_exact) / math.log(REL_MAX_DIST / max_exact)
                         * (half - max_exact)).astype(I32)
    large = jnp.minimum(large, half - 1)
    bucket = ret + jnp.where(n < max_exact, n, large)
    for h in range(ATT_HEADS):
        acc = jnp.zeros((QB, QB), F32)
        for b in range(REL_BUCKETS):
            acc = jnp.where(bucket == b, rb_ref[b, h], acc)
        o_ref[0, h] = acc * LOG2E


def _bias_tiles(rel_bias, n_near):
    return pl.pallas_call(
        functools.partial(_bias_kernel, n_near=n_near),
        grid=(n_near + 1,),
        in_specs=[pl.BlockSpec(memory_space=pltpu.SMEM)],
        out_specs=pl.BlockSpec((1, ATT_HEADS, QB, QB), lambda d: (d, 0, 0, 0)),
        out_shape=jax.ShapeDtypeStruct((n_near + 1, ATT_HEADS, QB, QB), F32),
        compiler_params=_params(1),
        name="bias_tiles",
    )(rel_bias)


def _inproj_kernel(x_ref, g_ref, wa_ref, wvt_ref, wwit_ref, wglu_ref, wgate_ref, bgate_ref, gq_ref, gk_ref,
                   q_ref, k_ref, qi_ref, ki_ref, vt_ref, wit_ref, u_ref, gate_ref, *, tm):
    x = x_ref[...]
    h = x * lax.rsqrt(jnp.mean(x * x, axis=-1, keepdims=True) + EPS) * g_ref[...]
    hb = h.astype(BF16)
    hp = ATT_HEADS * LANES
    ya = _dot(hb, wa_ref[...])
    for h_i in range(ATT_HEADS):
        sl = slice(h_i * LANES, (h_i + 1) * LANES)
        qh = ya[:, h_i * LANES:(h_i + 1) * LANES]
        ms = jnp.sum(qh * qh, axis=-1, keepdims=True) * (1.0 / ATT_HEAD_DIM)
        q_ref[:, sl] = (qh * lax.rsqrt(ms + EPS) * gq_ref[...]).astype(BF16)
        kh = ya[:, hp + h_i * LANES:hp + (h_i + 1) * LANES]
        ms = jnp.sum(kh * kh, axis=-1, keepdims=True) * (1.0 / ATT_HEAD_DIM)
        k_ref[:, sl] = (kh * lax.rsqrt(ms + EPS) * gk_ref[...]).astype(BF16)
    qi_ref[...] = ya[:, 2 * hp:3 * hp].astype(BF16)
    ki_ref[...] = ya[:, 3 * hp:3 * hp + LANES].astype(BF16)
    vt = _nt_dot(wvt_ref[...], hb).astype(BF16)
    for c in range(tm // KPAIR):
        vt_ref[c] = vt[:, c * KPAIR:(c + 1) * KPAIR]
    wit_ref[...] = _nt_dot(wwit_ref[...], hb)
    glu = _dot(hb, wglu_ref[...])
    u_ref[...] = glu[:, :CONV_CH] * jax.nn.sigmoid(glu[:, CONV_CH:])
    gate_ref[...] = jax.nn.sigmoid(_dot(hb, wgate_ref[...]) + bgate_ref[...]).astype(BF16)


def _pad_heads(w, nh, hd):
    d = w.shape[0]
    w3 = w.reshape(d, nh, hd)
    w3 = jnp.pad(w3, ((0, 0), (0, 0), (0, LANES - hd)))
    return w3.reshape(d, nh * LANES)


def _in_proj(x2, attn_g, w_in, b_gate, q_g, k_g, tm=256):
    T, D = x2.shape
    aw = ATT_HEADS * ATT_HEAD_DIM
    iw = IDX_HEADS * IDX_DIM
    o = 0
    wq = w_in[:, o:o + aw]; o += aw
    wk = w_in[:, o:o + aw]; o += aw
    wv = w_in[:, o:o + aw]; o += aw
    wqi = w_in[:, o:o + iw]; o += iw
    wki = w_in[:, o:o + IDX_DIM]; o += IDX_DIM
    wwi = w_in[:, o:o + IDX_HEADS]; o += IDX_HEADS
    wglu = w_in[:, o:o + 2 * CONV_CH]; o += 2 * CONV_CH
    wgate = w_in[:, o:o + N_BRANCH * D]
    wa = jnp.concatenate([_pad_heads(wq, ATT_HEADS, ATT_HEAD_DIM), _pad_heads(wk, ATT_HEADS, ATT_HEAD_DIM),
                          _pad_heads(wqi, IDX_HEADS, IDX_DIM),
                          jnp.pad(wki, ((0, 0), (0, LANES - IDX_DIM)))], axis=1).astype(BF16)
    na = wa.shape[1]
    hp = ATT_HEADS * LANES
    pad_g = lambda g, s: jnp.pad(g * s, (0, LANES - ATT_HEAD_DIM)).reshape(1, LANES)
    gq = pad_g(q_g, ATT_HEAD_DIM ** -0.5 * LOG2E)
    gk = pad_g(k_g, 1.0)
    const = lambda *shape: _single(shape, lambda i: (0,) * len(shape))
    outs = pl.pallas_call(
        functools.partial(_inproj_kernel, tm=tm),
        grid=(T // tm,),
        in_specs=[pl.BlockSpec((tm, D), lambda i: (i, 0)), const(1, D), const(D, na), const(aw, D),
                  const(IDX_HEADS, D), const(D, 2 * CONV_CH), const(D, N_BRANCH * D), const(1, N_BRANCH * D),
                  const(1, LANES), const(1, LANES)],
        out_specs=[pl.BlockSpec((tm, hp), lambda i: (i, 0)), pl.BlockSpec((tm, hp), lambda i: (i, 0)),
                   pl.BlockSpec((tm, hp), lambda i: (i, 0)), pl.BlockSpec((tm, LANES), lambda i: (i, 0)),
                   pl.BlockSpec((tm // KPAIR, aw, KPAIR), lambda i: (i, 0, 0)),
                   pl.BlockSpec((IDX_HEADS, tm), lambda i: (0, i)),
                   pl.BlockSpec((tm, CONV_CH), lambda i: (i, 0)),
                   pl.BlockSpec((tm, N_BRANCH * D), lambda i: (i, 0))],
        out_shape=[jax.ShapeDtypeStruct((T, hp), BF16), jax.ShapeDtypeStruct((T, hp), BF16),
                   jax.ShapeDtypeStruct((T, hp), BF16), jax.ShapeDtypeStruct((T, LANES), BF16),
                   jax.ShapeDtypeStruct((T // KPAIR, aw, KPAIR), BF16),
                   jax.ShapeDtypeStruct((IDX_HEADS, T), F32),
                   jax.ShapeDtypeStruct((T, CONV_CH), F32),
                   jax.ShapeDtypeStruct((T, N_BRANCH * D), BF16)],
        compiler_params=_params(1),
        name="in_proj",
    )(x2, attn_g.reshape(1, D), wa, wv.T.astype(BF16), wwi.T.astype(BF16), wglu.astype(BF16),
      wgate.astype(BF16), b_gate.reshape(1, N_BRANCH * D), gq, gk)
    return outs


def _attn_kernel(q_ref, qi_ref, wit_ref, k_hbm, ki_hbm, vt_hbm, bias_hbm, o_ref, k_ref, ki_ref, vt_ref, bias_ref,
                 load_sem, key_scr, att_scr, *scr, seq, n_near, topk):
    qb = pl.program_id(1)

    @pl.when(qb == 0)
    def _():
        b = pl.program_id(0)
        loads = [pltpu.make_async_copy(k_hbm.at[b], k_ref, load_sem.at[0]),
                 pltpu.make_async_copy(ki_hbm.at[b], ki_ref, load_sem.at[1]),
                 pltpu.make_async_copy(vt_hbm.at[b], vt_ref, load_sem.at[2]),
                 pltpu.make_async_copy(bias_hbm, bias_ref, load_sem.at[3])]
        for c in loads:
            c.start()
        for c in loads:
            c.wait()

    nquad = (qb + 4) // 4
    lane_t = lax.broadcasted_iota(I32, (1, QB), 1) + qb * QB
    qchunk = lane_t // CHUNK
    sub = lax.broadcasted_iota(I32, (KQUAD, QB), 0)
    idx_scale = (IDX_DIM ** -0.5) * (IDX_HEADS ** -0.5)
    wrow = wit_ref[...] * idx_scale

    qi_all = jnp.concatenate([qi_ref[:, h * LANES:(h + 1) * LANES] for h in range(IDX_HEADS)], axis=0)

    def score_quad(j, carry):
        d = _nt_dot(ki_ref[j], qi_all)
        acc = jnp.zeros((KQUAD, QB), F32)
        for h in range(IDX_HEADS):
            acc = acc + jnp.maximum(d[:, h * QB:(h + 1) * QB], 0.0) * wrow[h:h + 1, :]
        bits = pltpu.bitcast(acc, I32)
        skey = bits ^ ((bits >> 31) & 0x7FFFFFFF)
        visible = (sub + j * KQUAD) // CHUNK <= qchunk
        key_scr[j] = jnp.where(visible, skey, INT_MIN)
        return carry

    lax.fori_loop(0, nquad, score_quad, 0)

    nvis = (qchunk + 1) * CHUNK
    kk = jnp.minimum(topk, nvis)

    def count(pred):
        def body(j, acc8):
            m = jnp.where(pred(key_scr[j], j), 1, 0)
            return acc8 + m.reshape(KQUAD // SUBLANES, SUBLANES, QB).sum(axis=0)
        return lax.fori_loop(0, nquad, body, jnp.zeros((SUBLANES, QB), I32)).sum(axis=0, keepdims=True)

    def bit_step(i, carry):
        ans, cnt = carry
        cand = ans + lax.shift_left(jnp.int32(1), 31 - i)
        c = count(lambda blk, j: blk >= cand)
        take = c >= kk
        return jnp.where(take, cand, ans), jnp.where(take, c, cnt)

    thr, cnt = lax.fori_loop(0, 32, bit_step,
                             (jnp.full((1, QB), INT_MIN, I32), jnp.full((1, QB), 0, I32) + nquad * KQUAD))

    @pl.when(jnp.max(cnt - kk) > 0)
    def _():
        n_gt = count(lambda blk, j: blk > thr)
        need = kk - n_gt

        def idx_step(i, jbound):
            cand = jbound + lax.shift_left(jnp.int32(1), (seq.bit_length() - 1) - i)
            c = count(lambda blk, j: (blk == thr) & (sub + j * KQUAD < cand))
            return jnp.where(c <= need, cand, jbound)

        jbound = lax.fori_loop(0, seq.bit_length(), idx_step, jnp.zeros((1, QB), I32))

        def drop(j, carry):
            blk = key_scr[j]
            key_scr[j] = jnp.where((blk == thr) & (sub + j * KQUAD >= jbound), INT_MIN, blk)
            return carry

        lax.fori_loop(0, nquad, drop, 0)

    acc_refs, s_even, s_odd = scr[:ATT_HEADS], scr[ATT_HEADS], scr[ATT_HEADS + 1]
    for acc_ref in acc_refs:
        acc_ref[...] = jnp.zeros(acc_ref.shape, F32)
    head_row = lax.broadcasted_iota(I32, (ATT_HEADS, QB), 0)
    last_pair = k_ref.shape[0] - 1

    def qk(jp, s_ref):
        jc = jnp.minimum(jp, last_pair)
        for h in range(ATT_HEADS):
            s_ref[h] = _nt_dot(k_ref[jc, :, h * LANES:(h + 1) * LANES], q_ref[:, h * LANES:(h + 1) * LANES])

    def softmax_pv(jq, half, s_ref, carry):
        m_all, l_all = carry
        jp = 2 * jq + half
        keys = key_scr[jq, half * KPAIR:(half + 1) * KPAIR, :]
        masked = jnp.where(keys >= thr, 0.0, -jnp.inf)
        tile0 = jnp.clip(qb - 2 * jp, 0, n_near)
        tile1 = jnp.clip(qb - 2 * jp - 1, 0, n_near)
        for h in range(ATT_HEADS):
            rows = slice(h * ATT_HEAD_DIM, (h + 1) * ATT_HEAD_DIM)
            bias = jnp.concatenate([bias_ref[tile0, h], bias_ref[tile1, h]], axis=0)
            s = s_ref[h] + bias + masked
            m = m_all[h:h + 1, :]
            m_new = jnp.maximum(m, jnp.max(s, axis=0, keepdims=True))
            p = jnp.exp2(s - m_new)
            alpha = jnp.exp2(m - m_new)
            l_new = alpha * l_all[h:h + 1, :] + jnp.sum(p, axis=0, keepdims=True)
            m_all = jnp.where(head_row == h, m_new, m_all)
            l_all = jnp.where(head_row == h, l_new, l_all)
            acc_refs[h][...] = alpha * acc_refs[h][...] + _dot(vt_ref[jp, rows, :], p.astype(BF16))
        return m_all, l_all

    def att_quad(jq, carry):
        qk(2 * jq + 1, s_odd)
        carry = softmax_pv(jq, 0, s_even, carry)
        qk(2 * jq + 2, s_even)
        return softmax_pv(jq, 1, s_odd, carry)

    qk(0, s_even)
    _, l_all = lax.fori_loop(0, nquad, att_quad,
                             (jnp.full((ATT_HEADS, QB), NEG_BIG, F32), jnp.zeros((ATT_HEADS, QB), F32)))
    for h in range(ATT_HEADS):
        rows = slice(h * ATT_HEAD_DIM, (h + 1) * ATT_HEAD_DIM)
        att_scr[rows, :] = acc_refs[h][...] / l_all[h:h + 1, :]
    o_ref[...] = att_scr[...].T.astype(o_ref.dtype)


def _attention(q, k, qi, ki, vt, wit, bias, B, S, n_near):
    T = B * S
    nqb = S // QB
    hp = ATT_HEADS * LANES
    aw = ATT_HEADS * ATT_HEAD_DIM
    topk = min(TOPK_MAX, S // 4)
    npr, nqd = S // KPAIR, S // KQUAD
    k4 = k.reshape(B, npr, KPAIR, hp)
    ki4 = ki.reshape(B, nqd, KQUAD, LANES)
    vt4 = vt.reshape(B, npr, aw, KPAIR)
    return pl.pallas_call(
        functools.partial(_attn_kernel, seq=S, n_near=n_near, topk=topk),
        grid=(B, nqb),
        in_specs=[pl.BlockSpec((QB, hp), lambda b, i: (b * nqb + i, 0)),
                  pl.BlockSpec((QB, hp), lambda b, i: (b * nqb + i, 0)),
                  pl.BlockSpec((IDX_HEADS, QB), lambda b, i: (0, b * nqb + i)),
                  pl.BlockSpec(memory_space=pl.ANY), pl.BlockSpec(memory_space=pl.ANY),
                  pl.BlockSpec(memory_space=pl.ANY), pl.BlockSpec(memory_space=pl.ANY)],
        out_specs=pl.BlockSpec((QB, aw), lambda b, i: (b * nqb + i, 0)),
        out_shape=jax.ShapeDtypeStruct((T, aw), BF16),
        scratch_shapes=[pltpu.VMEM((npr, KPAIR, hp), BF16), pltpu.VMEM((nqd, KQUAD, LANES), BF16),
                        pltpu.VMEM((npr, aw, KPAIR), BF16), pltpu.VMEM((n_near + 1, ATT_HEADS, QB, QB), F32),
                        pltpu.SemaphoreType.DMA((4,)),
                        pltpu.VMEM((nqd, KQUAD, QB), I32), pltpu.VMEM((aw, QB), F32)]
        + [pltpu.VMEM((ATT_HEAD_DIM, QB), F32) for _ in range(ATT_HEADS)]
        + [pltpu.VMEM((ATT_HEADS, KPAIR, QB), F32) for _ in range(2)],
        compiler_params=_params(2),
        name="dsa_attention",
    )(q, qi, wit, k4, ki4, vt4, bias)


HALO = 32


def _mix_kernel(att_ref, u_ref, halo_ref, gate_ref, x_ref, cw_ref, cb_ref, lng_ref, lnb_ref, wao_ref, wco_ref,
                wout_ref, gffn_ref, x1_ref, h2_ref, ext_scr, *, tm, tiles_per_seq):
    i = pl.program_id(0)
    first = (i % tiles_per_seq) == 0
    ext_scr[0:HALO, :] = jnp.where(first, 0.0, halo_ref[...])
    ext_scr[HALO:HALO + tm, :] = u_ref[...]
    y = jnp.zeros((tm, CONV_CH), F32)
    for j in range(CONV_WIDTH):
        y = y + cw_ref[j:j + 1, :] * ext_scr[pl.ds(HALO - (CONV_WIDTH - 1) + j, tm), :]
    y = y + cb_ref[...]
    mu = jnp.mean(y, axis=-1, keepdims=True)
    yc = y - mu
    yn = yc * lax.rsqrt(jnp.mean(yc * yc, axis=-1, keepdims=True) + EPS) * lng_ref[...] + lnb_ref[...]
    z = yn * jax.nn.sigmoid(yn)
    y_conv = _dot(z.astype(BF16), wco_ref[...])
    y_att = _dot(att_ref[...], wao_ref[...])
    d = y_att.shape[1]
    g = gate_ref[...]
    mixed = g[:, :d].astype(F32) * y_att + g[:, d:].astype(F32) * y_conv
    x1 = x_ref[...] + _dot(mixed.astype(BF16), wout_ref[...])
    x1_ref[...] = x1
    h2_ref[...] = x1 * lax.rsqrt(jnp.mean(x1 * x1, axis=-1, keepdims=True) + EPS) * gffn_ref[...]


def _mix(att, u, gate, x2, conv_w, conv_b, ln_g, ln_b, w_att_out, w_conv_out, w_out, ffn_g, S, tm=256):
    T, D = x2.shape
    aw = att.shape[1]
    const = lambda *shape: _single(shape, lambda i: (0,) * len(shape))
    hb = tm // HALO
    return pl.pallas_call(
        functools.partial(_mix_kernel, tm=tm, tiles_per_seq=S // tm),
        grid=(T // tm,),
        in_specs=[pl.BlockSpec((tm, aw), lambda i: (i, 0)),
                  pl.BlockSpec((tm, CONV_CH), lambda i: (i, 0)),
                  pl.BlockSpec((HALO, CONV_CH), lambda i: (jnp.maximum(i * hb - 1, 0), 0)),
                  pl.BlockSpec((tm, N_BRANCH * D), lambda i: (i, 0)),
                  pl.BlockSpec((tm, D), lambda i: (i, 0)),
                  const(CONV_WIDTH, CONV_CH), const(1, CONV_CH), const(1, CONV_CH), const(1, CONV_CH),
                  const(aw, D), const(CONV_CH, D), const(D, D), const(1, D)],
        out_specs=[pl.BlockSpec((tm, D), lambda i: (i, 0)), pl.BlockSpec((tm, D), lambda i: (i, 0))],
        out_shape=[jax.ShapeDtypeStruct((T, D), F32), jax.ShapeDtypeStruct((T, D), F32)],
        scratch_shapes=[pltpu.VMEM((HALO + tm, CONV_CH), F32)],
        compiler_params=_params(1),
        name="mix_out_proj",
    )(att, u, u, gate, x2, conv_w.reshape(CONV_WIDTH, CONV_CH), conv_b.reshape(1, CONV_CH),
      ln_g.reshape(1, CONV_CH), ln_b.reshape(1, CONV_CH), w_att_out.astype(BF16), w_conv_out.astype(BF16),
      w_out.astype(BF16), ffn_g.reshape(1, D))


def _top_rows(sc, k, payload=None):
    rows = sc.shape[0]
    iota = lax.broadcasted_iota(I32, sc.shape, 0)
    out_row = lax.broadcasted_iota(I32, (k, sc.shape[1]), 0)
    vals = jnp.zeros((k, sc.shape[1]), F32)
    idxs = jnp.zeros((k, sc.shape[1]), I32)
    for r in range(k):
        m = jnp.max(sc, axis=0, keepdims=True)
        idx = jnp.min(jnp.where(sc == m, iota, rows), axis=0, keepdims=True)
        hit = iota == idx
        rec = idx if payload is None else jnp.max(jnp.where(hit, payload, -1), axis=0, keepdims=True)
        vals = jnp.where(out_row == r, m, vals)
        idxs = jnp.where(out_row == r, rec, idxs)
        sc = jnp.where(hit, -jnp.inf, sc)
    return vals, idxs


def _route_kernel(h2_ref, wq_ref, sk_ref, e_ref, g_ref, qt_scr, et_scr, gt_scr, *, tm):
    half = N_KEYS
    qt_scr[...] = _nt_dot(wq_ref[...], h2_ref[...].astype(BF16)).astype(BF16)

    def head(h, carry):
        tops = []
        for c in range(2):
            row0 = pl.multiple_of((h * 2 + c) * half, half)
            sc = _dot(sk_ref[h * 2 + c], qt_scr[pl.ds(row0, half), :])
            tops.append(_top_rows(sc, PEER_TOPK))
        (a, ia), (b, ib) = tops
        k = PEER_TOPK
        g = SUBLANES
        assert (k // 2) % g == 0 and k % g == 0
        row = lax.broadcasted_iota(I32, (g, a.shape[1]), 0)
        cand_parts, cidx_parts = [], []

        def add(av, iav, bv, ibv, valid_rows):
            s = av + bv
            if valid_rows < g:
                s = jnp.where(row < valid_rows, s, -jnp.inf)
            cand_parts.append(s)
            cidx_parts.append(iav * N_KEYS + ibv)

        for i in range(k):
            nj = k // (i + 1)
            if nj >= g:
                for j0 in range(0, nj, g):
                    add(a[i:i + 1, :], ia[i:i + 1, :], b[j0:j0 + g, :], ib[j0:j0 + g, :], g)
            elif nj > 1:
                add(a[i:i + 1, :], ia[i:i + 1, :], b[0:g, :], ib[0:g, :], nj)
            elif i % g == 0:
                add(a[i:i + g, :], ia[i:i + g, :], b[0:1, :], ib[0:1, :], g)
        cand = jnp.concatenate(cand_parts, axis=0)
        cidx = jnp.concatenate(cidx_parts, axis=0)
        ts, te = _top_rows(cand, PEER_TOPK, payload=cidx)
        ex = jnp.exp(ts - ts[0:1, :])
        gate = ex / jnp.sum(ex, axis=0, keepdims=True)
        r0 = pl.multiple_of(h * PEER_TOPK, PEER_TOPK)
        et_scr[pl.ds(r0, PEER_TOPK), :] = te
        gt_scr[pl.ds(r0, PEER_TOPK), :] = gate
        return carry

    lax.fori_loop(0, PEER_HEADS, head, 0)
    e_ref[...] = et_scr[...].T
    g_ref[...] = gt_scr[...].T


def _peer_route(h2, w_peer_q, sub_keys, tm=256):
    T, D = h2.shape
    nsel = PEER_HEADS * PEER_TOPK
    qd = w_peer_q.shape[1]
    half = sub_keys.shape[-1]
    skb = sub_keys.reshape(PEER_HEADS * 2, N_KEYS, half).astype(BF16)
    return pl.pallas_call(
        functools.partial(_route_kernel, tm=tm),
        grid=(T // tm,),
        in_specs=[pl.BlockSpec((tm, D), lambda i: (i, 0)),
                  _single((qd, D), lambda i: (0, 0)),
                  _single((PEER_HEADS * 2, N_KEYS, half), lambda i: (0, 0, 0))],
        out_specs=[pl.BlockSpec((tm, nsel), lambda i: (i, 0)), pl.BlockSpec((tm, nsel), lambda i: (i, 0))],
        out_shape=[jax.ShapeDtypeStruct((T, nsel), I32), jax.ShapeDtypeStruct((T, nsel), F32)],
        scratch_shapes=[pltpu.VMEM((qd, tm), BF16), pltpu.VMEM((nsel, tm), I32), pltpu.VMEM((nsel, tm), F32)],
        compiler_params=_params(1),
        name="peer_route",
    )(h2, w_peer_q.T.astype(BF16), skb)


TOK = 8


def _gelu_tanh(x):
    return 0.5 * x * (1.0 + jnp.tanh(math.sqrt(2.0 / math.pi) * (x + 0.044715 * (x * x * x))))


def _sublane_sums(ps):
    sub = lax.broadcasted_iota(I32, (SUBLANES, LANES), 0)
    lvl, stride = list(ps), SUBLANES // 2
    while len(lvl) > 1:
        half = len(lvl) // 2
        low = (sub & stride) == 0
        nxt = []
        for n in range(half):
            a, b = lvl[n], lvl[n + half]
            nxt.append(jnp.where(low, a + pltpu.roll(a, SUBLANES - stride, 0), b + pltpu.roll(b, stride, 0)))
        lvl, stride = nxt, stride // 2
    return lvl[0]


def _pack_uv(u, v):
    ne, d = u.shape
    hi = lax.bitcast_convert_type(u.astype(BF16), jnp.uint16).astype(jnp.uint32)
    lo = lax.bitcast_convert_type(v.astype(BF16), jnp.uint16).astype(jnp.uint32)
    return ((hi << 16) | lo).reshape(ne, d // LANES, LANES)


def _u_of(word):
    return pltpu.bitcast(word & jnp.uint32(0xFFFF0000), F32)


def _v_of(word):
    return pltpu.bitcast(word << 16, F32)


def _expert_group(row, h2_ref, x1_ref, gate_ref, o_ref, abc_scr, base, nsel, before_dots=None, before_sum=None):
    dsub = h2_ref.shape[1]
    lane_id = lax.broadcasted_iota(I32, (nsel, LANES), 1)
    st = jnp.zeros((nsel, LANES), F32)
    for t in range(TOK):
        if before_dots is not None:
            before_dots(t)
        xt = h2_ref[base + t]
        qs = []
        for g in range(nsel // SUBLANES):
            ps = [_u_of(row(t, g * SUBLANES + k)) * xt for k in range(SUBLANES)]
            qs.append(_sublane_sums(ps))
        q = jnp.concatenate(qs, axis=0)
        st = jnp.where(lane_id == t, jnp.sum(q, axis=1, keepdims=True), st)
    g8 = gate_ref[base:base + TOK, :]
    gt = jnp.concatenate([g8, jnp.zeros((nsel - TOK, nsel), F32)], axis=0).T
    at = _gelu_tanh(st) * gt

    for t in range(TOK):
        if before_sum is not None:
            before_sum(t)
        abc_scr[t] = jnp.broadcast_to(at[:, t:t + 1], (nsel, LANES))
        accs = [jnp.zeros((dsub, LANES), F32) for _ in range(4)]
        for n in range(nsel):
            accs[n % 4] = accs[n % 4] + abc_scr[t, n:n + 1, :] * _v_of(row(t, n))
        o_ref[base + t] = x1_ref[base + t] + ((accs[0] + accs[1]) + (accs[2] + accs[3]))


def _expert_kernel(idxc_ref, idxn_ref, h2_ref, x1_ref, gate_ref, uv_ref, after_ref, o_ref, buf, sem, abc_scr,
                   *, nsel):
    del after_ref
    i = pl.program_id(0)
    nsteps = pl.num_programs(0)
    rows = TOK * nsel

    def issue_token(idx_ref, tok, s, t, n0=0, n1=nsel):
        for n in range(n0, n1):
            pltpu.make_async_copy(uv_ref.at[idx_ref[tok, n]], buf.at[s, t * nsel + n],
                                  sem.at[s]).start(priority=n % 2)

    def wait_slot(s):
        pltpu.make_async_copy(uv_ref.at[pl.ds(0, rows)], buf.at[s], sem.at[s]).wait()

    @pl.when(i == 0)
    def _():
        for t in range(TOK):
            issue_token(idxc_ref, t, 0, t)

    n_early = (3 * nsel) // 4

    for grp in range(2):
        wait_slot(grp)
        nxt_idx, nxt_tok, nxt_slot = (idxc_ref, TOK, 1) if grp == 0 else (idxn_ref, 0, 0)
        _expert_group(lambda t, n, grp=grp: buf[grp, t * nsel + n], h2_ref, x1_ref, gate_ref, o_ref, abc_scr,
                      grp * TOK, nsel,
                      before_dots=lambda t: issue_token(nxt_idx, nxt_tok + t, nxt_slot, t, 0, n_early),
                      before_sum=lambda t: issue_token(nxt_idx, nxt_tok + t, nxt_slot, t, n_early, nsel))

    @pl.when(i == nsteps - 1)
    def _():
        wait_slot(0)


def _expert_staged_kernel(rows_ref, h2_ref, x1_ref, gate_ref, prev_ref, o_ref, abc_scr, *, nsel):
    del prev_ref
    for grp in range(h2_ref.shape[0] // TOK):
        _expert_group(lambda t, n, grp=grp: rows_ref[(grp * TOK + t) * nsel + n], h2_ref, x1_ref, gate_ref, o_ref,
                      abc_scr, grp * TOK, nsel)


SC_CORES = 2
SC_SUBCORES = 16
SC_CHUNK = 16
SC_NBUF = 7
SC_SLAB = 2048


def _sc_gather(table, idx):
    n = idx.shape[0]
    nw = SC_CORES * SC_SUBCORES
    per_w = n // nw
    nb = SC_NBUF
    assert n % nw == 0 and per_w % SC_SLAB == 0 and SC_SLAB % SC_CHUNK == 0
    nslab, nchunk = per_w // SC_SLAB, SC_SLAB // SC_CHUNK
    assert nchunk >= nb
    row_shape = table.shape[1:]
    mesh = plsc.VectorSubcoreMesh(core_axis_name="c", subcore_axis_name="s")

    @functools.partial(
        pl.kernel, mesh=mesh, out_type=jax.ShapeDtypeStruct((n,) + row_shape, table.dtype),
        scratch_types=[pltpu.VMEM((SC_SLAB,), jnp.int32)]
        + [pltpu.VMEM((SC_CHUNK,) + row_shape, table.dtype) for _ in range(nb)]
        + [pltpu.SemaphoreType.DMA for _ in range(2 * nb)])
    def gather(table_hbm, idx_hbm, out_hbm, idx_v, *scr):
        wid = lax.axis_index("s") * SC_CORES + lax.axis_index("c")
        bufs, gsem, wsem = scr[:nb], scr[nb:2 * nb], scr[2 * nb:]

        def gather_copy(c, b):
            return pltpu.make_async_copy(table_hbm.at[idx_v.at[pl.ds(c * SC_CHUNK, SC_CHUNK)]], bufs[b], gsem[b])

        @pl.loop(0, nslab)
        def _(sl):
            base = wid * per_w + sl * SC_SLAB
            pltpu.sync_copy(idx_hbm.at[pl.ds(base, SC_SLAB)], idx_v)

            def write_copy(c, b):
                return pltpu.make_async_copy(bufs[b], out_hbm.at[pl.ds(base + c * SC_CHUNK, SC_CHUNK)], wsem[b])

            for b in range(nb - 1):
                gather_copy(b, b).start()

            @pl.loop(0, -(-nchunk // nb) * nb, step=nb)
            def _(c0):
                for b in range(nb):
                    c = c0 + b
                    prev = (b - 1) % nb

                    @pl.when(c < nchunk)
                    def _():
                        gather_copy(c, b).wait()
                        write_copy(c, b).start()

                    @pl.when((c >= 1) & (c < nchunk))
                    def _():
                        write_copy(c - 1, prev).wait()

                    @pl.when(c + nb - 1 < nchunk)
                    def _():
                        gather_copy(c + nb - 1, prev).start()

            write_copy(nchunk - 1, (nchunk - 1) % nb).wait()

    return gather(table, idx)


def _experts_dma(eidx, gates, h2, x1, uv, ta, after=None):
    T, dsub, _ = x1.shape
    nsel = eidx.shape[1]
    tb = 2 * TOK
    nsteps = ta // tb
    return pl.pallas_call(
        functools.partial(_expert_kernel, nsel=nsel),
        grid=(nsteps,),
        in_specs=[pl.BlockSpec((tb, nsel), lambda i: (i, 0), memory_space=pltpu.SMEM),
                  pl.BlockSpec((tb, nsel), lambda i: (jnp.minimum(i + 1, nsteps - 1), 0), memory_space=pltpu.SMEM),
                  pl.BlockSpec((tb, dsub, LANES), lambda i: (i, 0, 0)),
                  pl.BlockSpec((tb, dsub, LANES), lambda i: (i, 0, 0)),
                  pl.BlockSpec((tb, nsel), lambda i: (i, 0)),
                  pl.BlockSpec(memory_space=pl.ANY), pl.BlockSpec(memory_space=pl.ANY)],
        out_specs=pl.BlockSpec((tb, dsub, LANES), lambda i: (i, 0, 0)),
        out_shape=jax.ShapeDtypeStruct((T, dsub, LANES), F32),
        scratch_shapes=[pltpu.VMEM((2, TOK * nsel, dsub, LANES), jnp.uint32), pltpu.SemaphoreType.DMA((2,)),
                        pltpu.VMEM((TOK, nsel, LANES), F32)],
        compiler_params=_params(1),
        name="peer_experts",
    )(eidx, eidx, h2, x1, gates, uv, uv if after is None else after)


def _experts_staged(rows, gates, h2, x1, out, ta):
    T, dsub, _ = x1.shape
    nsel = gates.shape[1]
    tb = 2 * TOK
    first = ta // tb
    tok = lambda i: (first + i, 0, 0)
    return pl.pallas_call(
        functools.partial(_expert_staged_kernel, nsel=nsel),
        grid=((T - ta) // tb,),
        in_specs=[pl.BlockSpec((tb * nsel, dsub, LANES), lambda i: (i, 0, 0)),
                  pl.BlockSpec((tb, dsub, LANES), tok),
                  pl.BlockSpec((tb, dsub, LANES), tok),
                  pl.BlockSpec((tb, nsel), lambda i: (first + i, 0)),
                  pl.BlockSpec(memory_space=pl.ANY)],
        out_specs=pl.BlockSpec((tb, dsub, LANES), tok),
        out_shape=jax.ShapeDtypeStruct((T, dsub, LANES), F32),
        scratch_shapes=[pltpu.VMEM((TOK, nsel, LANES), F32)],
        input_output_aliases={4: 0},
        compiler_params=_params(1),
        name="peer_experts_staged",
    )(rows, h2, x1, gates, out)


def _staged_tokens(T, use_sc):
    unit = SC_CORES * SC_SUBCORES * SC_SLAB // (PEER_HEADS * PEER_TOPK)
    return T // unit * unit if use_sc else 0


def _peer_gather_start(eidx, uv, use_sc):
    T, nsel = eidx.shape
    ts = _staged_tokens(T, use_sc)
    return _sc_gather(uv, eidx[T - ts:].reshape(ts * nsel)) if ts else None


def _peer_dma_part(eidx, gates, h2, x1, uv, use_sc, after):
    T, D = x1.shape
    dsub = D // LANES
    ta = T - _staged_tokens(T, use_sc)
    h3, x3 = h2.reshape(T, dsub, LANES), x1.reshape(T, dsub, LANES)
    return _experts_dma(eidx, gates, h3, x3, uv, ta, after) if ta else jnp.zeros_like(x3)


def _peer_finish(rows, out, gates, h2, x1, use_sc):
    T, D = x1.shape
    dsub = D // LANES
    if rows is not None:
        out = _experts_staged(rows, gates, h2.reshape(T, dsub, LANES), x1.reshape(T, dsub, LANES), out,
                              T - _staged_tokens(T, use_sc))
    return out.reshape(T, D)


def _ple_kernel(x_ref, p_ref, g_ref, wg_ref, wp_ref, o_ref):
    x = x_ref[...]
    h = x * lax.rsqrt(jnp.mean(x * x, axis=-1, keepdims=True) + EPS) * g_ref[...]
    gate = jax.nn.sigmoid(_dot(h.astype(BF16), wg_ref[...]))
    o_ref[...] = x + gate * _dot(p_ref[...].astype(BF16), wp_ref[...])


def _ple(x2, p2, ple_g, w_gate, w_proj, tm=512):
    T, D = x2.shape
    pd = p2.shape[1]
    return pl.pallas_call(
        _ple_kernel,
        grid=(T // tm,),
        in_specs=[pl.BlockSpec((tm, D), lambda i: (i, 0)), pl.BlockSpec((tm, pd), lambda i: (i, 0)),
                  _single((1, D), lambda i: (0, 0)), _single((D, D), lambda i: (0, 0)),
                  _single((pd, D), lambda i: (0, 0))],
        out_specs=pl.BlockSpec((tm, D), lambda i: (i, 0)),
        out_shape=jax.ShapeDtypeStruct((T, D), F32),
        compiler_params=_params(1),
        name="ple",
    )(x2, p2, ple_g.reshape(1, D), w_gate.astype(BF16), w_proj.astype(BF16))


def kernel(x, p, rel_bias, attn_norm_g, w_in, b_gate, q_norm_g, k_norm_g, w_att_out, conv_w, conv_b, conv_ln_g,
           conv_ln_b, w_conv_out, w_out, ffn_norm_g, w_peer_q, peer_sub_keys, peer_u, peer_v, ple_norm_g,
           w_ple_gate, w_ple_proj):
    B, S, D = x.shape
    depth = w_in.shape[0]
    assert S % KQUAD == 0 and D % LANES == 0 and S % 512 == 0
    n_near = _num_near_tiles(S)
    bias = _bias_tiles(rel_bias, n_near)
    xs = [x[b] for b in range(B)]
    for i in range(depth):
        uv = _pack_uv(peer_u[i], peer_v[i])

        def finish(pending, i=i):
            rows, part, gates, h2, x1, b, use_sc = pending
            x2 = _peer_finish(rows, part, gates, h2, x1, use_sc)
            return _ple(x2, p[i, b], ple_norm_g[i], w_ple_gate[i], w_ple_proj[i])

        pending, outs, last_rows = None, [], None
        for b in range(B):
            use_sc = b < B - 1
            q, k, qi, ki, vt, wit, u, gate = _in_proj(xs[b], attn_norm_g[i], w_in[i], b_gate[i], q_norm_g[i],
                                                      k_norm_g[i])
            att = _attention(q, k, qi, ki, vt, wit, bias, 1, S, n_near)
            x1, h2 = _mix(att, u, gate, xs[b], conv_w[i], conv_b[i], conv_ln_g[i], conv_ln_b[i], w_att_out[i],
                          w_conv_out[i], w_out[i], ffn_norm_g[i], S)
            eidx, gates = _peer_route(h2, w_peer_q[i], peer_sub_keys[i])
            rows = _peer_gather_start(eidx, uv, use_sc)
            if pending is not None:
                outs.append(finish(pending))
            part = _peer_dma_part(eidx, gates, h2, x1, uv, use_sc, last_rows)
            pending = (rows, part, gates, h2, x1, b, use_sc)
            last_rows = rows if rows is not None else last_rows
        outs.append(finish(pending))
        xs = outs
    return jnp.stack(xs, axis=0)
```

```python
import functools
import math

import numpy as np
import jax
import jax.numpy as jnp
from jax import lax
from jax.experimental import pallas as pl
from jax.experimental.pallas import tpu as pltpu
from jax.experimental.pallas import tpu_sc as plsc

CHUNK = 64
ATT_HEADS = 8
ATT_HEAD_DIM = 64
IDX_HEADS = 8
IDX_DIM = 64
TOPK_MAX = 256
REL_BUCKETS = 32
REL_MAX_DIST = 1024
CONV_CH = 512
CONV_WIDTH = 31
N_BRANCH = 2
PEER_HEADS = 8
N_KEYS = 128
PEER_TOPK = 16
EPS = 1e-6

LANES = 128
SUBLANES = 8
VMEM_LIMIT = 56 * 1024 * 1024

QB = 128
KPAIR = 2 * QB
KQUAD = 4 * QB
INT_MIN = -(2 ** 31)
NEG_BIG = -1e30
LOG2E = math.log2(math.e)

F32 = jnp.float32
BF16 = jnp.bfloat16
I32 = jnp.int32
I16 = jnp.int16
HALF_BIAS = 1 << 15


def _nt_dot(a, b, precision=None):
    return lax.dot_general(a, b, (((1,), (1,)), ((), ())), precision=precision,
                           preferred_element_type=F32)


def _dot(a, b):
    return jnp.dot(a, b, preferred_element_type=F32)


def _single(shape, index_map):
    return pl.BlockSpec(shape, index_map)


def _params(n_grid_dims):
    return pltpu.CompilerParams(dimension_semantics=("arbitrary",) * n_grid_dims,
                                vmem_limit_bytes=VMEM_LIMIT)


def _t5_bucket_np(rel):
    half = REL_BUCKETS // 2
    max_exact = half // 2
    ret = np.where(rel > 0, half, 0)
    n = np.abs(rel)
    nf = np.maximum(n, 1).astype(np.float32)
    large = max_exact + (np.log(nf / np.float32(max_exact)) / np.float32(math.log(REL_MAX_DIST / max_exact))
                         * np.float32(half - max_exact)).astype(np.int32)
    large = np.minimum(large, half - 1)
    return ret + np.where(n < max_exact, n, large)


def _num_near_tiles(seq):
    n = np.arange(1, max(seq, 2 * REL_MAX_DIST) + 1)
    b = _t5_bucket_np(-n)
    sat = REL_BUCKETS // 2 - 1
    unsat = np.nonzero(b != sat)[0]
    n_sat = int(n[unsat[-1]]) + 1 if unsat.size else 1
    return -(-(n_sat + QB - 1) // QB)


def _bias_kernel(rb_ref, o_ref, *, n_near):
    d = pl.program_id(0)
    i = lax.broadcasted_iota(I32, (QB, QB), 0)
    j = lax.broadcasted_iota(I32, (QB, QB), 1)
    rel = i - j - d * QB
    rel = jnp.where(d >= n_near, -8 * REL_MAX_DIST, rel)
    half = REL_BUCKETS // 2
    max_exact = half // 2
    ret = jnp.where(rel > 0, half, 0)
    n = jnp.abs(rel)
    nf = jnp.maximum(n, 1).astype(F32)
    large = max_exact + (jnp.log(nf / max_exact) / math.log(REL_MAX_DIST / max_exact)
                         * (half - max_exact)).astype(I32)
    large = jnp.minimum(large, half - 1)
    bucket = ret + jnp.where(n < max_exact, n, large)
    for h in range(ATT_HEADS):
        acc = jnp.zeros((QB, QB), F32)
        for b in range(REL_BUCKETS):
            acc = jnp.where(bucket == b, rb_ref[b, h], acc)
        o_ref[0, h] = acc * LOG2E


def _bias_tiles(rel_bias, n_near):
    return pl.pallas_call(
        functools.partial(_bias_kernel, n_near=n_near),
        grid=(n_near + 1,),
        in_specs=[pl.BlockSpec(memory_space=pltpu.SMEM)],
        out_specs=pl.BlockSpec((1, ATT_HEADS, QB, QB), lambda d: (d, 0, 0, 0)),
        out_shape=jax.ShapeDtypeStruct((n_near + 1, ATT_HEADS, QB, QB), F32),
        compiler_params=_params(1),
        name="bias_tiles",
    )(rel_bias)


def _inproj_kernel(x_ref, g_ref, wa_ref, wvt_ref, wwit_ref, wglu_ref, wgate_ref, bgate_ref, gq_ref, gk_ref,
                   q_ref, k_ref, qi_ref, ki_ref, vt_ref, wit_ref, u_ref, gate_ref, *, tm):
    x = x_ref[...]
    h = x * lax.rsqrt(jnp.mean(x * x, axis=-1, keepdims=True) + EPS) * g_ref[...]
    hb = h.astype(BF16)
    hp = ATT_HEADS * LANES
    ya = _dot(hb, wa_ref[...])
    for h_i in range(ATT_HEADS):
        sl = slice(h_i * LANES, (h_i + 1) * LANES)
        qh = ya[:, h_i * LANES:(h_i + 1) * LANES]
        ms = jnp.sum(qh * qh, axis=-1, keepdims=True) * (1.0 / ATT_HEAD_DIM)
        q_ref[:, sl] = (qh * lax.rsqrt(ms + EPS) * gq_ref[...]).astype(BF16)
        kh = ya[:, hp + h_i * LANES:hp + (h_i + 1) * LANES]
        ms = jnp.sum(kh * kh, axis=-1, keepdims=True) * (1.0 / ATT_HEAD_DIM)
        k_ref[:, sl] = (kh * lax.rsqrt(ms + EPS) * gk_ref[...]).astype(BF16)
    qi_ref[...] = ya[:, 2 * hp:3 * hp].astype(BF16)
    ki_ref[...] = ya[:, 3 * hp:3 * hp + LANES].astype(BF16)
    vt = _nt_dot(wvt_ref[...], hb).astype(BF16)
    for c in range(tm // KPAIR):
        vt_ref[c] = vt[:, c * KPAIR:(c + 1) * KPAIR]
    wit_ref[...] = _nt_dot(wwit_ref[...], hb)
    glu = _dot(hb, wglu_ref[...])
    u_ref[...] = glu[:, :CONV_CH] * jax.nn.sigmoid(glu[:, CONV_CH:])
    gate_ref[...] = jax.nn.sigmoid(_dot(hb, wgate_ref[...]) + bgate_ref[...]).astype(BF16)


def _pad_heads(w, nh, hd):
    d = w.shape[0]
    w3 = w.reshape(d, nh, hd)
    w3 = jnp.pad(w3, ((0, 0), (0, 0), (0, LANES - hd)))
    return w3.reshape(d, nh * LANES)


def _in_proj(x2, attn_g, w_in, b_gate, q_g, k_g, tm=256):
    T, D = x2.shape
    aw = ATT_HEADS * ATT_HEAD_DIM
    iw = IDX_HEADS * IDX_DIM
    o = 0
    wq = w_in[:, o:o + aw]; o += aw
    wk = w_in[:, o:o + aw]; o += aw
    wv = w_in[:, o:o + aw]; o += aw
    wqi = w_in[:, o:o + iw]; o += iw
    wki = w_in[:, o:o + IDX_DIM]; o += IDX_DIM
    wwi = w_in[:, o:o + IDX_HEADS]; o += IDX_HEADS
    wglu = w_in[:, o:o + 2 * CONV_CH]; o += 2 * CONV_CH
    wgate = w_in[:, o:o + N_BRANCH * D]
    wa = jnp.concatenate([_pad_heads(wq, ATT_HEADS, ATT_HEAD_DIM), _pad_heads(wk, ATT_HEADS, ATT_HEAD_DIM),
                          _pad_heads(wqi, IDX_HEADS, IDX_DIM),
                          jnp.pad(wki, ((0, 0), (0, LANES - IDX_DIM)))], axis=1).astype(BF16)
    na = wa.shape[1]
    hp = ATT_HEADS * LANES
    pad_g = lambda g, s: jnp.pad(g * s, (0, LANES - ATT_HEAD_DIM)).reshape(1, LANES)
    gq = pad_g(q_g, ATT_HEAD_DIM ** -0.5 * LOG2E)
    gk = pad_g(k_g, 1.0)
    const = lambda *shape: _single(shape, lambda i: (0,) * len(shape))
    outs = pl.pallas_call(
        functools.partial(_inproj_kernel, tm=tm),
        grid=(T // tm,),
        in_specs=[pl.BlockSpec((tm, D), lambda i: (i, 0)), const(1, D), const(D, na), const(aw, D),
                  const(IDX_HEADS, D), const(D, 2 * CONV_CH), const(D, N_BRANCH * D), const(1, N_BRANCH * D),
                  const(1, LANES), const(1, LANES)],
        out_specs=[pl.BlockSpec((tm, hp), lambda i: (i, 0)), pl.BlockSpec((tm, hp), lambda i: (i, 0)),
                   pl.BlockSpec((tm, hp), lambda i: (i, 0)), pl.BlockSpec((tm, LANES), lambda i: (i, 0)),
                   pl.BlockSpec((tm // KPAIR, aw, KPAIR), lambda i: (i, 0, 0)),
                   pl.BlockSpec((IDX_HEADS, tm), lambda i: (0, i)),
                   pl.BlockSpec((tm, CONV_CH), lambda i: (i, 0)),
                   pl.BlockSpec((tm, N_BRANCH * D), lambda i: (i, 0))],
        out_shape=[jax.ShapeDtypeStruct((T, hp), BF16), jax.ShapeDtypeStruct((T, hp), BF16),
                   jax.ShapeDtypeStruct((T, hp), BF16), jax.ShapeDtypeStruct((T, LANES), BF16),
                   jax.ShapeDtypeStruct((T // KPAIR, aw, KPAIR), BF16),
                   jax.ShapeDtypeStruct((IDX_HEADS, T), F32),
                   jax.ShapeDtypeStruct((T, CONV_CH), F32),
                   jax.ShapeDtypeStruct((T, N_BRANCH * D), BF16)],
        compiler_params=_params(1),
        name="in_proj",
    )(x2, attn_g.reshape(1, D), wa, wv.T.astype(BF16), wwi.T.astype(BF16), wglu.astype(BF16),
      wgate.astype(BF16), b_gate.reshape(1, N_BRANCH * D), gq, gk)
    return outs


def _attn_kernel(q_ref, qi_ref, wit_ref, k_hbm, ki_hbm, vt_hbm, bias_hbm, o_ref, k_ref, ki_ref, vt_ref, bias_ref,
                 load_sem, key_scr, hi_scr, lo_scr, eq_scr, att_scr, *scr, seq, n_near, topk):
    qb = pl.program_id(1)

    @pl.when(qb == 0)
    def _():
        b = pl.program_id(0)
        loads = [pltpu.make_async_copy(k_hbm.at[b], k_ref, load_sem.at[0]),
                 pltpu.make_async_copy(ki_hbm.at[b], ki_ref, load_sem.at[1]),
                 pltpu.make_async_copy(vt_hbm.at[b], vt_ref, load_sem.at[2]),
                 pltpu.make_async_copy(bias_hbm, bias_ref, load_sem.at[3])]
        for c in loads:
            c.start()
        for c in loads:
            c.wait()

    nquad = (qb + 4) // 4
    lane_t = lax.broadcasted_iota(I32, (1, QB), 1) + qb * QB
    qchunk = lane_t // CHUNK
    sub = lax.broadcasted_iota(I32, (KQUAD, QB), 0)
    idx_scale = (IDX_DIM ** -0.5) * (IDX_HEADS ** -0.5)
    wrow = wit_ref[...] * idx_scale

    qi_all = jnp.concatenate([qi_ref[:, h * LANES:(h + 1) * LANES] for h in range(IDX_HEADS)], axis=0)

    def score_quad(j, carry):
        d = _nt_dot(ki_ref[j], qi_all)
        acc = jnp.zeros((KQUAD, QB), F32)
        for h in range(IDX_HEADS):
            acc = acc + jnp.maximum(d[:, h * QB:(h + 1) * QB], 0.0) * wrow[h:h + 1, :]
        bits = pltpu.bitcast(acc, I32)
        skey = bits ^ ((bits >> 31) & 0x7FFFFFFF)
        visible = (sub + j * KQUAD) // CHUNK <= qchunk
        key = jnp.where(visible, skey, INT_MIN)
        key_scr[j] = key
        hi_scr[j] = (key >> 16).astype(I16)
        lo_scr[j] = ((key & 0xFFFF) - HALF_BIAS).astype(I16)
        return carry

    lax.fori_loop(0, nquad, score_quad, 0)

    nvis = (qchunk + 1) * CHUNK
    kk = jnp.minimum(topk, nvis)
    rows16 = 2 * SUBLANES

    def count(pred):
        def body(j, acc8):
            m = jnp.where(pred(key_scr[j], j), 1, 0)
            return acc8 + m.reshape(KQUAD // SUBLANES, SUBLANES, QB).sum(axis=0)
        return lax.fori_loop(0, nquad, body, jnp.zeros((SUBLANES, QB), I32)).sum(axis=0, keepdims=True)

    def count16(weight):
        def body(j, acc):
            w = weight(j)
            parts = [w[r:r + rows16, :] for r in range(0, KQUAD, rows16)]
            while len(parts) > 1:
                parts = [a + b for a, b in zip(parts[0::2], parts[1::2])]
            return acc + parts[0]
        acc = lax.fori_loop(0, nquad, body, jnp.zeros((rows16, QB), I16))
        return acc.astype(I32).sum(axis=0, keepdims=True)

    one16, zero16 = jnp.int16(1), jnp.int16(0)

    def search16(weight_ge, target):
        def step(i, carry):
            ans, cnt = carry
            cand = ans + lax.shift_left(jnp.int32(1), 15 - i)
            c = count16(lambda j: weight_ge(j, cand.astype(I16)))
            take = c >= target
            return jnp.where(take, cand, ans), jnp.where(take, c, cnt)
        lowest = jnp.full((1, QB), -HALF_BIAS, I32)
        return lax.fori_loop(0, 16, step, (lowest, count16(lambda j: weight_ge(j, lowest.astype(I16)))))

    hi_k, cnt_hi = search16(lambda j, c: jnp.where(hi_scr[j] >= c, one16, zero16), kk)
    above = jnp.minimum(hi_k + 1, HALF_BIAS - 1).astype(I16)
    n_above = jnp.where(hi_k == HALF_BIAS - 1, 0,
                        count16(lambda j: jnp.where(hi_scr[j] >= above, one16, zero16)))
    hi_k16 = hi_k.astype(I16)

    def mark_equal(j, carry):
        eq_scr[j] = jnp.where(hi_scr[j] == hi_k16, one16, zero16)
        return carry

    lax.fori_loop(0, nquad, mark_equal, 0)
    lo_k, cnt_lo = search16(lambda j, c: jnp.where(lo_scr[j] >= c, eq_scr[j], zero16), kk - n_above)
    thr = hi_k * (2 * HALF_BIAS) + (lo_k + HALF_BIAS)
    cnt = n_above + cnt_lo
    del cnt_hi

    @pl.when(jnp.max(cnt - kk) > 0)
    def _():
        n_gt = count(lambda blk, j: blk > thr)
        need = kk - n_gt

        def idx_step(i, jbound):
            cand = jbound + lax.shift_left(jnp.int32(1), (seq.bit_length() - 1) - i)
            c = count(lambda blk, j: (blk == thr) & (sub + j * KQUAD < cand))
            return jnp.where(c <= need, cand, jbound)

        jbound = lax.fori_loop(0, seq.bit_length(), idx_step, jnp.zeros((1, QB), I32))

        def drop(j, carry):
            blk = key_scr[j]
            key_scr[j] = jnp.where((blk == thr) & (sub + j * KQUAD >= jbound), INT_MIN, blk)
            return carry

        lax.fori_loop(0, nquad, drop, 0)

    acc_refs, s_even, s_odd = scr[:ATT_HEADS], scr[ATT_HEADS], scr[ATT_HEADS + 1]
    for acc_ref in acc_refs:
        acc_ref[...] = jnp.zeros(acc_ref.shape, F32)
    head_row = lax.broadcasted_iota(I32, (ATT_HEADS, QB), 0)
    last_pair = k_ref.shape[0] - 1

    def qk(jp, s_ref):
        jc = jnp.minimum(jp, last_pair)
        for h in range(ATT_HEADS):
            s_ref[h] = _nt_dot(k_ref[jc, :, h * LANES:(h + 1) * LANES], q_ref[:, h * LANES:(h + 1) * LANES])

    def softmax_pv(jq, half, s_ref, carry):
        m_all, l_all = carry
        jp = 2 * jq + half
        keys = key_scr[jq, half * KPAIR:(half + 1) * KPAIR, :]
        masked = jnp.where(keys >= thr, 0.0, -jnp.inf)
        tile0 = jnp.clip(qb - 2 * jp, 0, n_near)
        tile1 = jnp.clip(qb - 2 * jp - 1, 0, n_near)
        for h in range(ATT_HEADS):
            rows = slice(h * ATT_HEAD_DIM, (h + 1) * ATT_HEAD_DIM)
            bias = jnp.concatenate([bias_ref[tile0, h], bias_ref[tile1, h]], axis=0)
            s = s_ref[h] + bias + masked
            m = m_all[h:h + 1, :]
            m_new = jnp.maximum(m, jnp.max(s, axis=0, keepdims=True))
            p = jnp.exp2(s - m_new)
            alpha = jnp.exp2(m - m_new)
            l_new = alpha * l_all[h:h + 1, :] + jnp.sum(p, axis=0, keepdims=True)
            m_all = jnp.where(head_row == h, m_new, m_all)
            l_all = jnp.where(head_row == h, l_new, l_all)
            acc_refs[h][...] = alpha * acc_refs[h][...] + _dot(vt_ref[jp, rows, :], p.astype(BF16))
        return m_all, l_all

    def att_quad(jq, carry):
        qk(2 * jq + 1, s_odd)
        carry = softmax_pv(jq, 0, s_even, carry)
        qk(2 * jq + 2, s_even)
        return softmax_pv(jq, 1, s_odd, carry)

    qk(0, s_even)
    _, l_all = lax.fori_loop(0, nquad, att_quad,
                             (jnp.full((ATT_HEADS, QB), NEG_BIG, F32), jnp.zeros((ATT_HEADS, QB), F32)))
    for h in range(ATT_HEADS):
        rows = slice(h * ATT_HEAD_DIM, (h + 1) * ATT_HEAD_DIM)
        att_scr[rows, :] = acc_refs[h][...] / l_all[h:h + 1, :]
    o_ref[...] = att_scr[...].T.astype(o_ref.dtype)


def _attention(q, k, qi, ki, vt, wit, bias, B, S, n_near):
    T = B * S
    nqb = S // QB
    hp = ATT_HEADS * LANES
    aw = ATT_HEADS * ATT_HEAD_DIM
    topk = min(TOPK_MAX, S // 4)
    npr, nqd = S // KPAIR, S // KQUAD
    k4 = k.reshape(B, npr, KPAIR, hp)
    ki4 = ki.reshape(B, nqd, KQUAD, LANES)
    vt4 = vt.reshape(B, npr, aw, KPAIR)
    return pl.pallas_call(
        functools.partial(_attn_kernel, seq=S, n_near=n_near, topk=topk),
        grid=(B, nqb),
        in_specs=[pl.BlockSpec((QB, hp), lambda b, i: (b * nqb + i, 0)),
                  pl.BlockSpec((QB, hp), lambda b, i: (b * nqb + i, 0)),
                  pl.BlockSpec((IDX_HEADS, QB), lambda b, i: (0, b * nqb + i)),
                  pl.BlockSpec(memory_space=pl.ANY), pl.BlockSpec(memory_space=pl.ANY),
                  pl.BlockSpec(memory_space=pl.ANY), pl.BlockSpec(memory_space=pl.ANY)],
        out_specs=pl.BlockSpec((QB, aw), lambda b, i: (b * nqb + i, 0)),
        out_shape=jax.ShapeDtypeStruct((T, aw), BF16),
        scratch_shapes=[pltpu.VMEM((npr, KPAIR, hp), BF16), pltpu.VMEM((nqd, KQUAD, LANES), BF16),
                        pltpu.VMEM((npr, aw, KPAIR), BF16), pltpu.VMEM((n_near + 1, ATT_HEADS, QB, QB), F32),
                        pltpu.SemaphoreType.DMA((4,)),
                        pltpu.VMEM((nqd, KQUAD, QB), I32), pltpu.VMEM((nqd, KQUAD, QB), I16),
                        pltpu.VMEM((nqd, KQUAD, QB), I16), pltpu.VMEM((nqd, KQUAD, QB), I16),
                        pltpu.VMEM((aw, QB), F32)]
        + [pltpu.VMEM((ATT_HEAD_DIM, QB), F32) for _ in range(ATT_HEADS)]
        + [pltpu.VMEM((ATT_HEADS, KPAIR, QB), F32) for _ in range(2)],
        compiler_params=_params(2),
        name="dsa_attention",
    )(q, qi, wit, k4, ki4, vt4, bias)


HALO = 32


def _mix_kernel(att_ref, u_ref, halo_ref, gate_ref, x_ref, cw_ref, cb_ref, lng_ref, lnb_ref, wao_ref, wco_ref,
                wout_ref, gffn_ref, x1_ref, h2_ref, ext_scr, *, tm, tiles_per_seq):
    i = pl.program_id(0)
    first = (i % tiles_per_seq) == 0
    ext_scr[0:HALO, :] = jnp.where(first, 0.0, halo_ref[...])
    ext_scr[HALO:HALO + tm, :] = u_ref[...]
    y = jnp.zeros((tm, CONV_CH), F32)
    for j in range(CONV_WIDTH):
        y = y + cw_ref[j:j + 1, :] * ext_scr[pl.ds(HALO - (CONV_WIDTH - 1) + j, tm), :]
    y = y + cb_ref[...]
    mu = jnp.mean(y, axis=-1, keepdims=True)
    yc = y - mu
    yn = yc * lax.rsqrt(jnp.mean(yc * yc, axis=-1, keepdims=True) + EPS) * lng_ref[...] + lnb_ref[...]
    z = yn * jax.nn.sigmoid(yn)
    y_conv = _dot(z.astype(BF16), wco_ref[...])
    y_att = _dot(att_ref[...], wao_ref[...])
    d = y_att.shape[1]
    g = gate_ref[...]
    mixed = g[:, :d].astype(F32) * y_att + g[:, d:].astype(F32) * y_conv
    x1 = x_ref[...] + _dot(mixed.astype(BF16), wout_ref[...])
    x1_ref[...] = x1
    h2_ref[...] = x1 * lax.rsqrt(jnp.mean(x1 * x1, axis=-1, keepdims=True) + EPS) * gffn_ref[...]


def _mix(att, u, gate, x2, conv_w, conv_b, ln_g, ln_b, w_att_out, w_conv_out, w_out, ffn_g, S, tm=256):
    T, D = x2.shape
    aw = att.shape[1]
    const = lambda *shape: _single(shape, lambda i: (0,) * len(shape))
    hb = tm // HALO
    return pl.pallas_call(
        functools.partial(_mix_kernel, tm=tm, tiles_per_seq=S // tm),
        grid=(T // tm,),
        in_specs=[pl.BlockSpec((tm, aw), lambda i: (i, 0)),
                  pl.BlockSpec((tm, CONV_CH), lambda i: (i, 0)),
                  pl.BlockSpec((HALO, CONV_CH), lambda i: (jnp.maximum(i * hb - 1, 0), 0)),
                  pl.BlockSpec((tm, N_BRANCH * D), lambda i: (i, 0)),
                  pl.BlockSpec((tm, D), lambda i: (i, 0)),
                  const(CONV_WIDTH, CONV_CH), const(1, CONV_CH), const(1, CONV_CH), const(1, CONV_CH),
                  const(aw, D), const(CONV_CH, D), const(D, D), const(1, D)],
        out_specs=[pl.BlockSpec((tm, D), lambda i: (i, 0)), pl.BlockSpec((tm, D), lambda i: (i, 0))],
        out_shape=[jax.ShapeDtypeStruct((T, D), F32), jax.ShapeDtypeStruct((T, D), F32)],
        scratch_shapes=[pltpu.VMEM((HALO + tm, CONV_CH), F32)],
        compiler_params=_params(1),
        name="mix_out_proj",
    )(att, u, u, gate, x2, conv_w.reshape(CONV_WIDTH, CONV_CH), conv_b.reshape(1, CONV_CH),
      ln_g.reshape(1, CONV_CH), ln_b.reshape(1, CONV_CH), w_att_out.astype(BF16), w_conv_out.astype(BF16),
      w_out.astype(BF16), ffn_g.reshape(1, D))


def _top_rows(sc, k, payload=None):
    rows = sc.shape[0]
    iota = lax.broadcasted_iota(I32, sc.shape, 0)
    out_row = lax.broadcasted_iota(I32, (k, sc.shape[1]), 0)
    vals = jnp.zeros((k, sc.shape[1]), F32)
    idxs = jnp.zeros((k, sc.shape[1]), I32)
    for r in range(k):
        m = jnp.max(sc, axis=0, keepdims=True)
        idx = jnp.min(jnp.where(sc == m, iota, rows), axis=0, keepdims=True)
        hit = iota == idx
        rec = idx if payload is None else jnp.max(jnp.where(hit, payload, -1), axis=0, keepdims=True)
        vals = jnp.where(out_row == r, m, vals)
        idxs = jnp.where(out_row == r, rec, idxs)
        sc = jnp.where(hit, -jnp.inf, sc)
    return vals, idxs


def _route_kernel(h2_ref, wq_ref, sk_ref, e_ref, g_ref, qt_scr, et_scr, gt_scr, *, tm):
    half = N_KEYS
    qt_scr[...] = _nt_dot(wq_ref[...], h2_ref[...].astype(BF16)).astype(BF16)

    def head(h, carry):
        tops = []
        for c in range(2):
            row0 = pl.multiple_of((h * 2 + c) * half, half)
            sc = _dot(sk_ref[h * 2 + c], qt_scr[pl.ds(row0, half), :])
            tops.append(_top_rows(sc, PEER_TOPK))
        (a, ia), (b, ib) = tops
        k = PEER_TOPK
        g = SUBLANES
        assert (k // 2) % g == 0 and k % g == 0
        row = lax.broadcasted_iota(I32, (g, a.shape[1]), 0)
        cand_parts, cidx_parts = [], []

        def add(av, iav, bv, ibv, valid_rows):
            s = av + bv
            if valid_rows < g:
                s = jnp.where(row < valid_rows, s, -jnp.inf)
            cand_parts.append(s)
            cidx_parts.append(iav * N_KEYS + ibv)

        for i in range(k):
            nj = k // (i + 1)
            if nj >= g:
                for j0 in range(0, nj, g):
                    add(a[i:i + 1, :], ia[i:i + 1, :], b[j0:j0 + g, :], ib[j0:j0 + g, :], g)
            elif nj > 1:
                add(a[i:i + 1, :], ia[i:i + 1, :], b[0:g, :], ib[0:g, :], nj)
            elif i % g == 0:
                add(a[i:i + g, :], ia[i:i + g, :], b[0:1, :], ib[0:1, :], g)
        cand = jnp.concatenate(cand_parts, axis=0)
        cidx = jnp.concatenate(cidx_parts, axis=0)
        ts, te = _top_rows(cand, PEER_TOPK, payload=cidx)
        ex = jnp.exp(ts - ts[0:1, :])
        gate = ex / jnp.sum(ex, axis=0, keepdims=True)
        r0 = pl.multiple_of(h * PEER_TOPK, PEER_TOPK)
        et_scr[pl.ds(r0, PEER_TOPK), :] = te
        gt_scr[pl.ds(r0, PEER_TOPK), :] = gate
        return carry

    lax.fori_loop(0, PEER_HEADS, head, 0)
    e_ref[...] = et_scr[...].T
    g_ref[...] = gt_scr[...].T


def _peer_route(h2, w_peer_q, sub_keys, tm=256):
    T, D = h2.shape
    nsel = PEER_HEADS * PEER_TOPK
    qd = w_peer_q.shape[1]
    half = sub_keys.shape[-1]
    skb = sub_keys.reshape(PEER_HEADS * 2, N_KEYS, half).astype(BF16)
    return pl.pallas_call(
        functools.partial(_route_kernel, tm=tm),
        grid=(T // tm,),
        in_specs=[pl.BlockSpec((tm, D), lambda i: (i, 0)),
                  _single((qd, D), lambda i: (0, 0)),
                  _single((PEER_HEADS * 2, N_KEYS, half), lambda i: (0, 0, 0))],
        out_specs=[pl.BlockSpec((tm, nsel), lambda i: (i, 0)), pl.BlockSpec((tm, nsel), lambda i: (i, 0))],
        out_shape=[jax.ShapeDtypeStruct((T, nsel), I32), jax.ShapeDtypeStruct((T, nsel), F32)],
        scratch_shapes=[pltpu.VMEM((qd, tm), BF16), pltpu.VMEM((nsel, tm), I32), pltpu.VMEM((nsel, tm), F32)],
        compiler_params=_params(1),
        name="peer_route",
    )(h2, w_peer_q.T.astype(BF16), skb)


TOK = 8


def _gelu_tanh(x):
    return 0.5 * x * (1.0 + jnp.tanh(math.sqrt(2.0 / math.pi) * (x + 0.044715 * (x * x * x))))


def _sublane_sums(ps):
    sub = lax.broadcasted_iota(I32, (SUBLANES, LANES), 0)
    lvl, stride = list(ps), SUBLANES // 2
    while len(lvl) > 1:
        half = len(lvl) // 2
        low = (sub & stride) == 0
        nxt = []
        for n in range(half):
            a, b = lvl[n], lvl[n + half]
            nxt.append(jnp.where(low, a + pltpu.roll(a, SUBLANES - stride, 0), b + pltpu.roll(b, stride, 0)))
        lvl, stride = nxt, stride // 2
    return lvl[0]


def _pack_uv(u, v):
    ne, d = u.shape
    hi = lax.bitcast_convert_type(u.astype(BF16), jnp.uint16).astype(jnp.uint32)
    lo = lax.bitcast_convert_type(v.astype(BF16), jnp.uint16).astype(jnp.uint32)
    return ((hi << 16) | lo).reshape(ne, d // LANES, LANES)


def _u_of(word):
    return pltpu.bitcast(word & jnp.uint32(0xFFFF0000), F32)


def _v_of(word):
    return pltpu.bitcast(word << 16, F32)


def _expert_group(row, h2_ref, x1_ref, gate_ref, o_ref, abc_scr, base, nsel, before_dots=None, before_sum=None):
    dsub = h2_ref.shape[1]
    lane_id = lax.broadcasted_iota(I32, (nsel, LANES), 1)
    st = jnp.zeros((nsel, LANES), F32)
    for t in range(TOK):
        if before_dots is not None:
            before_dots(t)
        xt = h2_ref[base + t]
        qs = []
        for g in range(nsel // SUBLANES):
            ps = [_u_of(row(t, g * SUBLANES + k)) * xt for k in range(SUBLANES)]
            qs.append(_sublane_sums(ps))
        q = jnp.concatenate(qs, axis=0)
        st = jnp.where(lane_id == t, jnp.sum(q, axis=1, keepdims=True), st)
    g8 = gate_ref[base:base + TOK, :]
    gt = jnp.concatenate([g8, jnp.zeros((nsel - TOK, nsel), F32)], axis=0).T
    at = _gelu_tanh(st) * gt

    for t in range(TOK):
        if before_sum is not None:
            before_sum(t)
        abc_scr[t] = jnp.broadcast_to(at[:, t:t + 1], (nsel, LANES))
        accs = [jnp.zeros((dsub, LANES), F32) for _ in range(4)]
        for n in range(nsel):
            accs[n % 4] = accs[n % 4] + abc_scr[t, n:n + 1, :] * _v_of(row(t, n))
        o_ref[base + t] = x1_ref[base + t] + ((accs[0] + accs[1]) + (accs[2] + accs[3]))


def _expert_kernel(idxc_ref, idxn_ref, h2_ref, x1_ref, gate_ref, uv_ref, after_ref, o_ref, buf, sem, abc_scr,
                   *, nsel):
    del after_ref
    i = pl.program_id(0)
    nsteps = pl.num_programs(0)
    rows = TOK * nsel

    def issue_token(idx_ref, tok, s, t, n0=0, n1=nsel):
        for n in range(n0, n1):
            pltpu.make_async_copy(uv_ref.at[idx_ref[tok, n]], buf.at[s, t * nsel + n],
                                  sem.at[s]).start(priority=n % 2)

    def wait_slot(s):
        pltpu.make_async_copy(uv_ref.at[pl.ds(0, rows)], buf.at[s], sem.at[s]).wait()

    @pl.when(i == 0)
    def _():
        for t in range(TOK):
            issue_token(idxc_ref, t, 0, t)

    n_early = (3 * nsel) // 4

    for grp in range(2):
        wait_slot(grp)
        nxt_idx, nxt_tok, nxt_slot = (idxc_ref, TOK, 1) if grp == 0 else (idxn_ref, 0, 0)
        _expert_group(lambda t, n, grp=grp: buf[grp, t * nsel + n], h2_ref, x1_ref, gate_ref, o_ref, abc_scr,
                      grp * TOK, nsel,
                      before_dots=lambda t: issue_token(nxt_idx, nxt_tok + t, nxt_slot, t, 0, n_early),
                      before_sum=lambda t: issue_token(nxt_idx, nxt_tok + t, nxt_slot, t, n_early, nsel))

    @pl.when(i == nsteps - 1)
    def _():
        wait_slot(0)


def _expert_staged_kernel(rows_ref, h2_ref, x1_ref, gate_ref, prev_ref, o_ref, abc_scr, *, nsel):
    del prev_ref
    for grp in range(h2_ref.shape[0] // TOK):
        _expert_group(lambda t, n, grp=grp: rows_ref[(grp * TOK + t) * nsel + n], h2_ref, x1_ref, gate_ref, o_ref,
                      abc_scr, grp * TOK, nsel)


SC_CORES = 2
SC_SUBCORES = 16
SC_CHUNK = 16
SC_NBUF = 7
SC_SLAB = 2048


def _sc_gather(table, idx):
    n = idx.shape[0]
    nw = SC_CORES * SC_SUBCORES
    per_w = n // nw
    nb = SC_NBUF
    assert n % nw == 0 and per_w % SC_SLAB == 0 and SC_SLAB % SC_CHUNK == 0
    nslab, nchunk = per_w // SC_SLAB, SC_SLAB // SC_CHUNK
    assert nchunk >= nb
    row_shape = table.shape[1:]
    mesh = plsc.VectorSubcoreMesh(core_axis_name="c", subcore_axis_name="s")

    @functools.partial(
        pl.kernel, mesh=mesh, out_type=jax.ShapeDtypeStruct((n,) + row_shape, table.dtype),
        scratch_types=[pltpu.VMEM((SC_SLAB,), jnp.int32)]
        + [pltpu.VMEM((SC_CHUNK,) + row_shape, table.dtype) for _ in range(nb)]
        + [pltpu.SemaphoreType.DMA for _ in range(2 * nb)])
    def gather(table_hbm, idx_hbm, out_hbm, idx_v, *scr):
        wid = lax.axis_index("s") * SC_CORES + lax.axis_index("c")
        bufs, gsem, wsem = scr[:nb], scr[nb:2 * nb], scr[2 * nb:]

        def gather_copy(c, b):
            return pltpu.make_async_copy(table_hbm.at[idx_v.at[pl.ds(c * SC_CHUNK, SC_CHUNK)]], bufs[b], gsem[b])

        @pl.loop(0, nslab)
        def _(sl):
            base = wid * per_w + sl * SC_SLAB
            pltpu.sync_copy(idx_hbm.at[pl.ds(base, SC_SLAB)], idx_v)

            def write_copy(c, b):
                return pltpu.make_async_copy(bufs[b], out_hbm.at[pl.ds(base + c * SC_CHUNK, SC_CHUNK)], wsem[b])

            for b in range(nb - 1):
                gather_copy(b, b).start()

            @pl.loop(0, -(-nchunk // nb) * nb, step=nb)
            def _(c0):
                for b in range(nb):
                    c = c0 + b
                    prev = (b - 1) % nb

                    @pl.when(c < nchunk)
                    def _():
                        gather_copy(c, b).wait()
                        write_copy(c, b).start()

                    @pl.when((c >= 1) & (c < nchunk))
                    def _():
                        write_copy(c - 1, prev).wait()

                    @pl.when(c + nb - 1 < nchunk)
                    def _():
                        gather_copy(c + nb - 1, prev).start()

            write_copy(nchunk - 1, (nchunk - 1) % nb).wait()

    return gather(table, idx)


def _experts_dma(eidx, gates, h2, x1, uv, ta, after=None):
    T, dsub, _ = x1.shape
    nsel = eidx.shape[1]
    tb = 2 * TOK
    nsteps = ta // tb
    return pl.pallas_call(
        functools.partial(_expert_kernel, nsel=nsel),
        grid=(nsteps,),
        in_specs=[pl.BlockSpec((tb, nsel), lambda i: (i, 0), memory_space=pltpu.SMEM),
                  pl.BlockSpec((tb, nsel), lambda i: (jnp.minimum(i + 1, nsteps - 1), 0), memory_space=pltpu.SMEM),
                  pl.BlockSpec((tb, dsub, LANES), lambda i: (i, 0, 0)),
                  pl.BlockSpec((tb, dsub, LANES), lambda i: (i, 0, 0)),
                  pl.BlockSpec((tb, nsel), lambda i: (i, 0)),
                  pl.BlockSpec(memory_space=pl.ANY), pl.BlockSpec(memory_space=pl.ANY)],
        out_specs=pl.BlockSpec((tb, dsub, LANES), lambda i: (i, 0, 0)),
        out_shape=jax.ShapeDtypeStruct((T, dsub, LANES), F32),
        scratch_shapes=[pltpu.VMEM((2, TOK * nsel, dsub, LANES), jnp.uint32), pltpu.SemaphoreType.DMA((2,)),
                        pltpu.VMEM((TOK, nsel, LANES), F32)],
        compiler_params=_params(1),
        name="peer_experts",
    )(eidx, eidx, h2, x1, gates, uv, uv if after is None else after)


def _experts_staged(rows, gates, h2, x1, out, ta):
    T, dsub, _ = x1.shape
    nsel = gates.shape[1]
    tb = 2 * TOK
    first = ta // tb
    tok = lambda i: (first + i, 0, 0)
    return pl.pallas_call(
        functools.partial(_expert_staged_kernel, nsel=nsel),
        grid=((T - ta) // tb,),
        in_specs=[pl.BlockSpec((tb * nsel, dsub, LANES), lambda i: (i, 0, 0)),
                  pl.BlockSpec((tb, dsub, LANES), tok),
                  pl.BlockSpec((tb, dsub, LANES), tok),
                  pl.BlockSpec((tb, nsel), lambda i: (first + i, 0)),
                  pl.BlockSpec(memory_space=pl.ANY)],
        out_specs=pl.BlockSpec((tb, dsub, LANES), tok),
        out_shape=jax.ShapeDtypeStruct((T, dsub, LANES), F32),
        scratch_shapes=[pltpu.VMEM((TOK, nsel, LANES), F32)],
        input_output_aliases={4: 0},
        compiler_params=_params(1),
        name="peer_experts_staged",
    )(rows, h2, x1, gates, out)


def _staged_tokens(T, use_sc):
    unit = SC_CORES * SC_SUBCORES * SC_SLAB // (PEER_HEADS * PEER_TOPK)
    return T // unit * unit if use_sc else 0


def _peer_gather_start(eidx, uv, use_sc):
    T, nsel = eidx.shape
    ts = _staged_tokens(T, use_sc)
    return _sc_gather(uv, eidx[T - ts:].reshape(ts * nsel)) if ts else None


def _peer_dma_part(eidx, gates, h2, x1, uv, use_sc, after):
    T, D = x1.shape
    dsub = D // LANES
    ta = T - _staged_tokens(T, use_sc)
    h3, x3 = h2.reshape(T, dsub, LANES), x1.reshape(T, dsub, LANES)
    return _experts_dma(eidx, gates, h3, x3, uv, ta, after) if ta else jnp.zeros_like(x3)


def _peer_finish(rows, out, gates, h2, x1, use_sc):
    T, D = x1.shape
    dsub = D // LANES
    if rows is not None:
        out = _experts_staged(rows, gates, h2.reshape(T, dsub, LANES), x1.reshape(T, dsub, LANES), out,
                              T - _staged_tokens(T, use_sc))
    return out.reshape(T, D)


def _ple_kernel(x_ref, p_ref, g_ref, wg_ref, wp_ref, o_ref):
    x = x_ref[...]
    h = x * lax.rsqrt(jnp.mean(x * x, axis=-1, keepdims=True) + EPS) * g_ref[...]
    gate = jax.nn.sigmoid(_dot(h.astype(BF16), wg_ref[...]))
    o_ref[...] = x + gate * _dot(p_ref[...].astype(BF16), wp_ref[...])


def _ple(x2, p2, ple_g, w_gate, w_proj, tm=512):
    T, D = x2.shape
    pd = p2.shape[1]
    return pl.pallas_call(
        _ple_kernel,
        grid=(T // tm,),
        in_specs=[pl.BlockSpec((tm, D), lambda i: (i, 0)), pl.BlockSpec((tm, pd), lambda i: (i, 0)),
                  _single((1, D), lambda i: (0, 0)), _single((D, D), lambda i: (0, 0)),
                  _single((pd, D), lambda i: (0, 0))],
        out_specs=pl.BlockSpec((tm, D), lambda i: (i, 0)),
        out_shape=jax.ShapeDtypeStruct((T, D), F32),
        compiler_params=_params(1),
        name="ple",
    )(x2, p2, ple_g.reshape(1, D), w_gate.astype(BF16), w_proj.astype(BF16))


def kernel(x, p, rel_bias, attn_norm_g, w_in, b_gate, q_norm_g, k_norm_g, w_att_out, conv_w, conv_b, conv_ln_g,
           conv_ln_b, w_conv_out, w_out, ffn_norm_g, w_peer_q, peer_sub_keys, peer_u, peer_v, ple_norm_g,
           w_ple_gate, w_ple_proj):
    B, S, D = x.shape
    depth = w_in.shape[0]
    assert S % KQUAD == 0 and D % LANES == 0 and S % 512 == 0
    n_near = _num_near_tiles(S)
    bias = _bias_tiles(rel_bias, n_near)
    xs = [x[b] for b in range(B)]
    for i in range(depth):
        uv = _pack_uv(peer_u[i], peer_v[i])

        def finish(pending, i=i):
            rows, part, gates, h2, x1, b, use_sc = pending
            x2 = _peer_finish(rows, part, gates, h2, x1, use_sc)
            return _ple(x2, p[i, b], ple_norm_g[i], w_ple_gate[i], w_ple_proj[i])

        pending, outs, last_rows = None, [], None
        for b in range(B):
            use_sc = b < B - 1
            q, k, qi, ki, vt, wit, u, gate = _in_proj(xs[b], attn_norm_g[i], w_in[i], b_gate[i], q_norm_g[i],
                                                      k_norm_g[i])
            att = _attention(q, k, qi, ki, vt, wit, bias, 1, S, n_near)
            x1, h2 = _mix(att, u, gate, xs[b], conv_w[i], conv_b[i], conv_ln_g[i], conv_ln_b[i], w_att_out[i],
                          w_conv_out[i], w_out[i], ffn_norm_g[i], S)
            eidx, gates = _peer_route(h2, w_peer_q[i], peer_sub_keys[i])
            rows = _peer_gather_start(eidx, uv, use_sc)
            if pending is not None:
                outs.append(finish(pending))
            part = _peer_dma_part(eidx, gates, h2, x1, uv, use_sc, last_rows)
            pending = (rows, part, gates, h2, x1, b, use_sc)
            last_rows = rows if rows is not None else last_rows
        outs.append(finish(pending))
        xs = outs
    return jnp.stack(xs, axis=0)
```

```python
import functools
import math

import numpy as np
import jax
import jax.numpy as jnp
from jax import lax
from jax.experimental import pallas as pl
from jax.experimental.pallas import tpu as pltpu
from jax.experimental.pallas import tpu_sc as plsc

CHUNK = 64
ATT_HEADS = 8
ATT_HEAD_DIM = 64
IDX_HEADS = 8
IDX_DIM = 64
TOPK_MAX = 256
REL_BUCKETS = 32
REL_MAX_DIST = 1024
CONV_CH = 512
CONV_WIDTH = 31
N_BRANCH = 2
PEER_HEADS = 8
N_KEYS = 128
PEER_TOPK = 16
EPS = 1e-6

LANES = 128
SUBLANES = 8
VMEM_LIMIT = 56 * 1024 * 1024

QB = 128
KPAIR = 2 * QB
KQUAD = 4 * QB
INT_MIN = -(2 ** 31)
NEG_BIG = -1e30
LOG2E = math.log2(math.e)

F32 = jnp.float32
BF16 = jnp.bfloat16
I32 = jnp.int32


def _nt_dot(a, b, precision=None):
    return lax.dot_general(a, b, (((1,), (1,)), ((), ())), precision=precision,
                           preferred_element_type=F32)


def _dot(a, b):
    return jnp.dot(a, b, preferred_element_type=F32)


def _single(shape, index_map):
    return pl.BlockSpec(shape, index_map)


def _params(n_grid_dims):
    return pltpu.CompilerParams(dimension_semantics=("arbitrary",) * n_grid_dims,
                                vmem_limit_bytes=VMEM_LIMIT)


def _t5_bucket_np(rel):
    half = REL_BUCKETS // 2
    max_exact = half // 2
    ret = np.where(rel > 0, half, 0)
    n = np.abs(rel)
    nf = np.maximum(n, 1).astype(np.float32)
    large = max_exact + (np.log(nf / np.float32(max_exact)) / np.float32(math.log(REL_MAX_DIST / max_exact))
                         * np.float32(half - max_exact)).astype(np.int32)
    large = np.minimum(large, half - 1)
    return ret + np.where(n < max_exact, n, large)


def _num_near_tiles(seq):
    n = np.arange(1, max(seq, 2 * REL_MAX_DIST) + 1)
    b = _t5_bucket_np(-n)
    sat = REL_BUCKETS // 2 - 1
    unsat = np.nonzero(b != sat)[0]
    n_sat = int(n[unsat[-1]]) + 1 if unsat.size else 1
    return -(-(n_sat + QB - 1) // QB)


def _bias_kernel(rb_ref, o_ref, *, n_near):
    d = pl.program_id(0)
    i = lax.broadcasted_iota(I32, (QB, QB), 0)
    j = lax.broadcasted_iota(I32, (QB, QB), 1)
    rel = i - j - d * QB
    rel = jnp.where(d >= n_near, -8 * REL_MAX_DIST, rel)
    half = REL_BUCKETS // 2
    max_exact = half // 2
    ret = jnp.where(rel > 0, half, 0)
    n = jnp.abs(rel)
    nf = jnp.maximum(n, 1).astype(F32)
    large = max_exact + (jnp.log(nf / max_exact) / math.log(REL_MAX_DIST / max_exact)
                         * (half - max_exact)).astype(I32)
    large = jnp.minimum(large, half - 1)
    bucket = ret + jnp.where(n < max_exact, n, large)
    for h in range(ATT_HEADS):
        acc = jnp.zeros((QB, QB), F32)
        for b in range(REL_BUCKETS):
            acc = jnp.where(bucket == b, rb_ref[b, h], acc)
        o_ref[0, h] = acc * LOG2E


def _bias_tiles(rel_bias, n_near):
    return pl.pallas_call(
        functools.partial(_bias_kernel, n_near=n_near),
        grid=(n_near + 1,),
        in_specs=[pl.BlockSpec(memory_space=pltpu.SMEM)],
        out_specs=pl.BlockSpec((1, ATT_HEADS, QB, QB), lambda d: (d, 0, 0, 0)),
        out_shape=jax.ShapeDtypeStruct((n_near + 1, ATT_HEADS, QB, QB), F32),
        compiler_params=_params(1),
        name="bias_tiles",
    )(rel_bias)


def _inproj_kernel(x_ref, g_ref, wa_ref, wvt_ref, wwit_ref, wglu_ref, wgate_ref, bgate_ref, gq_ref, gk_ref,
                   q_ref, k_ref, qi_ref, ki_ref, vt_ref, wit_ref, u_ref, gate_ref, *, tm):
    x = x_ref[...]
    h = x * lax.rsqrt(jnp.mean(x * x, axis=-1, keepdims=True) + EPS) * g_ref[...]
    hb = h.astype(BF16)
    hp = ATT_HEADS * LANES
    ya = _dot(hb, wa_ref[...])
    for h_i in range(ATT_HEADS):
        sl = slice(h_i * LANES, (h_i + 1) * LANES)
        qh = ya[:, h_i * LANES:(h_i + 1) * LANES]
        ms = jnp.sum(qh * qh, axis=-1, keepdims=True) * (1.0 / ATT_HEAD_DIM)
        q_ref[:, sl] = (qh * lax.rsqrt(ms + EPS) * gq_ref[...]).astype(BF16)
        kh = ya[:, hp + h_i * LANES:hp + (h_i + 1) * LANES]
        ms = jnp.sum(kh * kh, axis=-1, keepdims=True) * (1.0 / ATT_HEAD_DIM)
        k_ref[:, sl] = (kh * lax.rsqrt(ms + EPS) * gk_ref[...]).astype(BF16)
    qi_ref[...] = ya[:, 2 * hp:3 * hp].astype(BF16)
    ki_ref[...] = ya[:, 3 * hp:3 * hp + LANES].astype(BF16)
    vt = _nt_dot(wvt_ref[...], hb).astype(BF16)
    for c in range(tm // KPAIR):
        vt_ref[c] = vt[:, c * KPAIR:(c + 1) * KPAIR]
    wit_ref[...] = _nt_dot(wwit_ref[...], hb)
    glu = _dot(hb, wglu_ref[...])
    u_ref[...] = glu[:, :CONV_CH] * jax.nn.sigmoid(glu[:, CONV_CH:])
    gate_ref[...] = jax.nn.sigmoid(_dot(hb, wgate_ref[...]) + bgate_ref[...]).astype(BF16)


def _pad_heads(w, nh, hd):
    d = w.shape[0]
    w3 = w.reshape(d, nh, hd)
    w3 = jnp.pad(w3, ((0, 0), (0, 0), (0, LANES - hd)))
    return w3.reshape(d, nh * LANES)


def _in_proj(x2, attn_g, w_in, b_gate, q_g, k_g, tm=256):
    T, D = x2.shape
    aw = ATT_HEADS * ATT_HEAD_DIM
    iw = IDX_HEADS * IDX_DIM
    o = 0
    wq = w_in[:, o:o + aw]; o += aw
    wk = w_in[:, o:o + aw]; o += aw
    wv = w_in[:, o:o + aw]; o += aw
    wqi = w_in[:, o:o + iw]; o += iw
    wki = w_in[:, o:o + IDX_DIM]; o += IDX_DIM
    wwi = w_in[:, o:o + IDX_HEADS]; o += IDX_HEADS
    wglu = w_in[:, o:o + 2 * CONV_CH]; o += 2 * CONV_CH
    wgate = w_in[:, o:o + N_BRANCH * D]
    wa = jnp.concatenate([_pad_heads(wq, ATT_HEADS, ATT_HEAD_DIM), _pad_heads(wk, ATT_HEADS, ATT_HEAD_DIM),
                          _pad_heads(wqi, IDX_HEADS, IDX_DIM),
                          jnp.pad(wki, ((0, 0), (0, LANES - IDX_DIM)))], axis=1).astype(BF16)
    na = wa.shape[1]
    hp = ATT_HEADS * LANES
    pad_g = lambda g, s: jnp.pad(g * s, (0, LANES - ATT_HEAD_DIM)).reshape(1, LANES)
    gq = pad_g(q_g, ATT_HEAD_DIM ** -0.5 * LOG2E)
    gk = pad_g(k_g, 1.0)
    const = lambda *shape: _single(shape, lambda i: (0,) * len(shape))
    outs = pl.pallas_call(
        functools.partial(_inproj_kernel, tm=tm),
        grid=(T // tm,),
        in_specs=[pl.BlockSpec((tm, D), lambda i: (i, 0)), const(1, D), const(D, na), const(aw, D),
                  const(IDX_HEADS, D), const(D, 2 * CONV_CH), const(D, N_BRANCH * D), const(1, N_BRANCH * D),
                  const(1, LANES), const(1, LANES)],
        out_specs=[pl.BlockSpec((tm, hp), lambda i: (i, 0)), pl.BlockSpec((tm, hp), lambda i: (i, 0)),
                   pl.BlockSpec((tm, hp), lambda i: (i, 0)), pl.BlockSpec((tm, LANES), lambda i: (i, 0)),
                   pl.BlockSpec((tm // KPAIR, aw, KPAIR), lambda i: (i, 0, 0)),
                   pl.BlockSpec((IDX_HEADS, tm), lambda i: (0, i)),
                   pl.BlockSpec((tm, CONV_CH), lambda i: (i, 0)),
                   pl.BlockSpec((tm, N_BRANCH * D), lambda i: (i, 0))],
        out_shape=[jax.ShapeDtypeStruct((T, hp), BF16), jax.ShapeDtypeStruct((T, hp), BF16),
                   jax.ShapeDtypeStruct((T, hp), BF16), jax.ShapeDtypeStruct((T, LANES), BF16),
                   jax.ShapeDtypeStruct((T // KPAIR, aw, KPAIR), BF16),
                   jax.ShapeDtypeStruct((IDX_HEADS, T), F32),
                   jax.ShapeDtypeStruct((T, CONV_CH), F32),
                   jax.ShapeDtypeStruct((T, N_BRANCH * D), BF16)],
        compiler_params=_params(1),
        name="in_proj",
    )(x2, attn_g.reshape(1, D), wa, wv.T.astype(BF16), wwi.T.astype(BF16), wglu.astype(BF16),
      wgate.astype(BF16), b_gate.reshape(1, N_BRANCH * D), gq, gk)
    return outs


def _attn_kernel(q_ref, qi_ref, wit_ref, *rest, seq, n_near, topk, n_parts, qb0):
    k_parts, ki_parts, vt_parts = rest[:n_parts], rest[n_parts:2 * n_parts], rest[2 * n_parts:3 * n_parts]
    (bias_hbm, o_ref, k_ref, ki_ref, vt_ref, bias_ref, load_sem, key_scr, att_scr), scr = (
        rest[3 * n_parts:3 * n_parts + 9], rest[3 * n_parts + 9:])
    qb = qb0 + pl.program_id(0)

    @pl.when(pl.program_id(0) == 0)
    def _():
        loads = [pltpu.make_async_copy(bias_hbm, bias_ref, load_sem.at[3 * n_parts])]
        kp = kq = 0
        for n in range(n_parts):
            npair, nqd = k_parts[n].shape[0], ki_parts[n].shape[0]
            loads += [pltpu.make_async_copy(k_parts[n], k_ref.at[pl.ds(kp, npair)], load_sem.at[3 * n]),
                      pltpu.make_async_copy(ki_parts[n], ki_ref.at[pl.ds(kq, nqd)], load_sem.at[3 * n + 1]),
                      pltpu.make_async_copy(vt_parts[n], vt_ref.at[pl.ds(kp, npair)], load_sem.at[3 * n + 2])]
            kp, kq = kp + npair, kq + nqd
        for c in loads:
            c.start()
        for c in loads:
            c.wait()

    nquad = (qb + 4) // 4
    lane_t = lax.broadcasted_iota(I32, (1, QB), 1) + qb * QB
    qchunk = lane_t // CHUNK
    sub = lax.broadcasted_iota(I32, (KQUAD, QB), 0)
    idx_scale = (IDX_DIM ** -0.5) * (IDX_HEADS ** -0.5)
    wrow = wit_ref[...] * idx_scale

    qi_all = jnp.concatenate([qi_ref[:, h * LANES:(h + 1) * LANES] for h in range(IDX_HEADS)], axis=0)

    def score_quad(j, carry):
        d = _nt_dot(ki_ref[j], qi_all)
        acc = jnp.zeros((KQUAD, QB), F32)
        for h in range(IDX_HEADS):
            acc = acc + jnp.maximum(d[:, h * QB:(h + 1) * QB], 0.0) * wrow[h:h + 1, :]
        bits = pltpu.bitcast(acc, I32)
        skey = bits ^ ((bits >> 31) & 0x7FFFFFFF)
        visible = (sub + j * KQUAD) // CHUNK <= qchunk
        key_scr[j] = jnp.where(visible, skey, INT_MIN)
        return carry

    lax.fori_loop(0, nquad, score_quad, 0)

    nvis = (qchunk + 1) * CHUNK
    kk = jnp.minimum(topk, nvis)

    def count(pred):
        def body(j, acc8):
            m = jnp.where(pred(key_scr[j], j), 1, 0)
            return acc8 + m.reshape(KQUAD // SUBLANES, SUBLANES, QB).sum(axis=0)
        return lax.fori_loop(0, nquad, body, jnp.zeros((SUBLANES, QB), I32)).sum(axis=0, keepdims=True)

    def bit_step(i, carry):
        ans, cnt = carry
        cand = ans + lax.shift_left(jnp.int32(1), 31 - i)
        c = count(lambda blk, j: blk >= cand)
        take = c >= kk
        return jnp.where(take, cand, ans), jnp.where(take, c, cnt)

    thr, cnt = lax.fori_loop(0, 32, bit_step,
                             (jnp.full((1, QB), INT_MIN, I32), jnp.full((1, QB), 0, I32) + nquad * KQUAD))

    @pl.when(jnp.max(cnt - kk) > 0)
    def _():
        n_gt = count(lambda blk, j: blk > thr)
        need = kk - n_gt

        def idx_step(i, jbound):
            cand = jbound + lax.shift_left(jnp.int32(1), (seq.bit_length() - 1) - i)
            c = count(lambda blk, j: (blk == thr) & (sub + j * KQUAD < cand))
            return jnp.where(c <= need, cand, jbound)

        jbound = lax.fori_loop(0, seq.bit_length(), idx_step, jnp.zeros((1, QB), I32))

        def drop(j, carry):
            blk = key_scr[j]
            key_scr[j] = jnp.where((blk == thr) & (sub + j * KQUAD >= jbound), INT_MIN, blk)
            return carry

        lax.fori_loop(0, nquad, drop, 0)

    acc_refs, s_even, s_odd = scr[:ATT_HEADS], scr[ATT_HEADS], scr[ATT_HEADS + 1]
    for acc_ref in acc_refs:
        acc_ref[...] = jnp.zeros(acc_ref.shape, F32)
    head_row = lax.broadcasted_iota(I32, (ATT_HEADS, QB), 0)
    last_pair = k_ref.shape[0] - 1

    def qk(jp, s_ref):
        jc = jnp.minimum(jp, last_pair)
        for h in range(ATT_HEADS):
            s_ref[h] = _nt_dot(k_ref[jc, :, h * LANES:(h + 1) * LANES], q_ref[:, h * LANES:(h + 1) * LANES])

    def softmax_pv(jq, half, s_ref, carry):
        m_all, l_all = carry
        jp = 2 * jq + half
        keys = key_scr[jq, half * KPAIR:(half + 1) * KPAIR, :]
        masked = jnp.where(keys >= thr, 0.0, -jnp.inf)
        tile0 = jnp.clip(qb - 2 * jp, 0, n_near)
        tile1 = jnp.clip(qb - 2 * jp - 1, 0, n_near)
        for h in range(ATT_HEADS):
            rows = slice(h * ATT_HEAD_DIM, (h + 1) * ATT_HEAD_DIM)
            bias = jnp.concatenate([bias_ref[tile0, h], bias_ref[tile1, h]], axis=0)
            s = s_ref[h] + bias + masked
            m = m_all[h:h + 1, :]
            m_new = jnp.maximum(m, jnp.max(s, axis=0, keepdims=True))
            p = jnp.exp2(s - m_new)
            alpha = jnp.exp2(m - m_new)
            l_new = alpha * l_all[h:h + 1, :] + jnp.sum(p, axis=0, keepdims=True)
            m_all = jnp.where(head_row == h, m_new, m_all)
            l_all = jnp.where(head_row == h, l_new, l_all)
            acc_refs[h][...] = alpha * acc_refs[h][...] + _dot(vt_ref[jp, rows, :], p.astype(BF16))
        return m_all, l_all

    def att_quad(jq, carry):
        qk(2 * jq + 1, s_odd)
        carry = softmax_pv(jq, 0, s_even, carry)
        qk(2 * jq + 2, s_even)
        return softmax_pv(jq, 1, s_odd, carry)

    qk(0, s_even)
    _, l_all = lax.fori_loop(0, nquad, att_quad,
                             (jnp.full((ATT_HEADS, QB), NEG_BIG, F32), jnp.zeros((ATT_HEADS, QB), F32)))
    for h in range(ATT_HEADS):
        rows = slice(h * ATT_HEAD_DIM, (h + 1) * ATT_HEAD_DIM)
        att_scr[rows, :] = acc_refs[h][...] / l_all[h:h + 1, :]
    o_ref[...] = att_scr[...].T.astype(o_ref.dtype)


def _attention(q, qi, wit, k_parts, ki_parts, vt_parts, bias, S, n_near):
    Tq = q.shape[0]
    hp = ATT_HEADS * LANES
    aw = ATT_HEADS * ATT_HEAD_DIM
    topk = min(TOPK_MAX, S // 4)
    n_parts = len(k_parts)
    k4 = [k.reshape(-1, KPAIR, hp) for k in k_parts]
    ki4 = [ki.reshape(-1, KQUAD, LANES) for ki in ki_parts]
    npr = sum(k.shape[0] for k in k4)
    nqd = sum(ki.shape[0] for ki in ki4)
    assert npr * KPAIR == nqd * KQUAD and vt_parts[0].shape[1:] == (aw, KPAIR)
    qb0 = (npr * KPAIR - Tq) // QB
    return pl.pallas_call(
        functools.partial(_attn_kernel, seq=S, n_near=n_near, topk=topk, n_parts=n_parts, qb0=qb0),
        grid=(Tq // QB,),
        in_specs=[pl.BlockSpec((QB, hp), lambda i: (i, 0)),
                  pl.BlockSpec((QB, hp), lambda i: (i, 0)),
                  pl.BlockSpec((IDX_HEADS, QB), lambda i: (0, i))]
        + [pl.BlockSpec(memory_space=pl.ANY)] * (3 * n_parts + 1),
        out_specs=pl.BlockSpec((QB, aw), lambda i: (i, 0)),
        out_shape=jax.ShapeDtypeStruct((Tq, aw), BF16),
        scratch_shapes=[pltpu.VMEM((npr, KPAIR, hp), BF16), pltpu.VMEM((nqd, KQUAD, LANES), BF16),
                        pltpu.VMEM((npr, aw, KPAIR), BF16), pltpu.VMEM((n_near + 1, ATT_HEADS, QB, QB), F32),
                        pltpu.SemaphoreType.DMA((3 * n_parts + 1,)),
                        pltpu.VMEM((nqd, KQUAD, QB), I32), pltpu.VMEM((aw, QB), F32)]
        + [pltpu.VMEM((ATT_HEAD_DIM, QB), F32) for _ in range(ATT_HEADS)]
        + [pltpu.VMEM((ATT_HEADS, KPAIR, QB), F32) for _ in range(2)],
        compiler_params=_params(1),
        name="dsa_attention",
    )(q, qi, wit, *k4, *ki4, *vt_parts, bias)


HALO = 32


def _mix_kernel(att_ref, u_ref, halo_ref, hist_ref, gate_ref, x_ref, cw_ref, cb_ref, lng_ref, lnb_ref, wao_ref,
                wco_ref, wout_ref, gffn_ref, x1_ref, h2_ref, ext_scr, *, tm):
    i = pl.program_id(0)
    ext_scr[0:HALO, :] = jnp.where(i == 0, hist_ref[...], halo_ref[...])
    ext_scr[HALO:HALO + tm, :] = u_ref[...]
    y = jnp.zeros((tm, CONV_CH), F32)
    for j in range(CONV_WIDTH):
        y = y + cw_ref[j:j + 1, :] * ext_scr[pl.ds(HALO - (CONV_WIDTH - 1) + j, tm), :]
    y = y + cb_ref[...]
    mu = jnp.mean(y, axis=-1, keepdims=True)
    yc = y - mu
    yn = yc * lax.rsqrt(jnp.mean(yc * yc, axis=-1, keepdims=True) + EPS) * lng_ref[...] + lnb_ref[...]
    z = yn * jax.nn.sigmoid(yn)
    y_conv = _dot(z.astype(BF16), wco_ref[...])
    y_att = _dot(att_ref[...], wao_ref[...])
    d = y_att.shape[1]
    g = gate_ref[...]
    mixed = g[:, :d].astype(F32) * y_att + g[:, d:].astype(F32) * y_conv
    x1 = x_ref[...] + _dot(mixed.astype(BF16), wout_ref[...])
    x1_ref[...] = x1
    h2_ref[...] = x1 * lax.rsqrt(jnp.mean(x1 * x1, axis=-1, keepdims=True) + EPS) * gffn_ref[...]


def _mix(att, u, history, gate, x2, conv_w, conv_b, ln_g, ln_b, w_att_out, w_conv_out, w_out, ffn_g, tm=256):
    T, D = x2.shape
    aw = att.shape[1]
    const = lambda *shape: _single(shape, lambda i: (0,) * len(shape))
    hb = tm // HALO
    return pl.pallas_call(
        functools.partial(_mix_kernel, tm=tm),
        grid=(T // tm,),
        in_specs=[pl.BlockSpec((tm, aw), lambda i: (i, 0)),
                  pl.BlockSpec((tm, CONV_CH), lambda i: (i, 0)),
                  pl.BlockSpec((HALO, CONV_CH), lambda i: (jnp.maximum(i * hb - 1, 0), 0)),
                  const(HALO, CONV_CH),
                  pl.BlockSpec((tm, N_BRANCH * D), lambda i: (i, 0)),
                  pl.BlockSpec((tm, D), lambda i: (i, 0)),
                  const(CONV_WIDTH, CONV_CH), const(1, CONV_CH), const(1, CONV_CH), const(1, CONV_CH),
                  const(aw, D), const(CONV_CH, D), const(D, D), const(1, D)],
        out_specs=[pl.BlockSpec((tm, D), lambda i: (i, 0)), pl.BlockSpec((tm, D), lambda i: (i, 0))],
        out_shape=[jax.ShapeDtypeStruct((T, D), F32), jax.ShapeDtypeStruct((T, D), F32)],
        scratch_shapes=[pltpu.VMEM((HALO + tm, CONV_CH), F32)],
        compiler_params=_params(1),
        name="mix_out_proj",
    )(att, u, u, history, gate, x2, conv_w.reshape(CONV_WIDTH, CONV_CH), conv_b.reshape(1, CONV_CH),
      ln_g.reshape(1, CONV_CH), ln_b.reshape(1, CONV_CH), w_att_out.astype(BF16), w_conv_out.astype(BF16),
      w_out.astype(BF16), ffn_g.reshape(1, D))


def _top_rows(sc, k, payload=None):
    rows = sc.shape[0]
    iota = lax.broadcasted_iota(I32, sc.shape, 0)
    out_row = lax.broadcasted_iota(I32, (k, sc.shape[1]), 0)
    vals = jnp.zeros((k, sc.shape[1]), F32)
    idxs = jnp.zeros((k, sc.shape[1]), I32)
    for r in range(k):
        m = jnp.max(sc, axis=0, keepdims=True)
        idx = jnp.min(jnp.where(sc == m, iota, rows), axis=0, keepdims=True)
        hit = iota == idx
        rec = idx if payload is None else jnp.max(jnp.where(hit, payload, -1), axis=0, keepdims=True)
        vals = jnp.where(out_row == r, m, vals)
        idxs = jnp.where(out_row == r, rec, idxs)
        sc = jnp.where(hit, -jnp.inf, sc)
    return vals, idxs


def _route_kernel(h2_ref, wq_ref, sk_ref, e_ref, g_ref, qt_scr, et_scr, gt_scr, *, tm):
    half = N_KEYS
    qt_scr[...] = _nt_dot(wq_ref[...], h2_ref[...].astype(BF16)).astype(BF16)

    def head(h, carry):
        tops = []
        for c in range(2):
            row0 = pl.multiple_of((h * 2 + c) * half, half)
            sc = _dot(sk_ref[h * 2 + c], qt_scr[pl.ds(row0, half), :])
            tops.append(_top_rows(sc, PEER_TOPK))
        (a, ia), (b, ib) = tops
        k = PEER_TOPK
        g = SUBLANES
        assert (k // 2) % g == 0 and k % g == 0
        row = lax.broadcasted_iota(I32, (g, a.shape[1]), 0)
        cand_parts, cidx_parts = [], []

        def add(av, iav, bv, ibv, valid_rows):
            s = av + bv
            if valid_rows < g:
                s = jnp.where(row < valid_rows, s, -jnp.inf)
            cand_parts.append(s)
            cidx_parts.append(iav * N_KEYS + ibv)

        for i in range(k):
            nj = k // (i + 1)
            if nj >= g:
                for j0 in range(0, nj, g):
                    add(a[i:i + 1, :], ia[i:i + 1, :], b[j0:j0 + g, :], ib[j0:j0 + g, :], g)
            elif nj > 1:
                add(a[i:i + 1, :], ia[i:i + 1, :], b[0:g, :], ib[0:g, :], nj)
            elif i % g == 0:
                add(a[i:i + g, :], ia[i:i + g, :], b[0:1, :], ib[0:1, :], g)
        cand = jnp.concatenate(cand_parts, axis=0)
        cidx = jnp.concatenate(cidx_parts, axis=0)
        ts, te = _top_rows(cand, PEER_TOPK, payload=cidx)
        ex = jnp.exp(ts - ts[0:1, :])
        gate = ex / jnp.sum(ex, axis=0, keepdims=True)
        r0 = pl.multiple_of(h * PEER_TOPK, PEER_TOPK)
        et_scr[pl.ds(r0, PEER_TOPK), :] = te
        gt_scr[pl.ds(r0, PEER_TOPK), :] = gate
        return carry

    lax.fori_loop(0, PEER_HEADS, head, 0)
    e_ref[...] = et_scr[...].T
    g_ref[...] = gt_scr[...].T


def _peer_route(h2, w_peer_q, sub_keys, tm=256):
    T, D = h2.shape
    nsel = PEER_HEADS * PEER_TOPK
    qd = w_peer_q.shape[1]
    half = sub_keys.shape[-1]
    skb = sub_keys.reshape(PEER_HEADS * 2, N_KEYS, half).astype(BF16)
    return pl.pallas_call(
        functools.partial(_route_kernel, tm=tm),
        grid=(T // tm,),
        in_specs=[pl.BlockSpec((tm, D), lambda i: (i, 0)),
                  _single((qd, D), lambda i: (0, 0)),
                  _single((PEER_HEADS * 2, N_KEYS, half), lambda i: (0, 0, 0))],
        out_specs=[pl.BlockSpec((tm, nsel), lambda i: (i, 0)), pl.BlockSpec((tm, nsel), lambda i: (i, 0))],
        out_shape=[jax.ShapeDtypeStruct((T, nsel), I32), jax.ShapeDtypeStruct((T, nsel), F32)],
        scratch_shapes=[pltpu.VMEM((qd, tm), BF16), pltpu.VMEM((nsel, tm), I32), pltpu.VMEM((nsel, tm), F32)],
        compiler_params=_params(1),
        name="peer_route",
    )(h2, w_peer_q.T.astype(BF16), skb)


TOK = 8


def _gelu_tanh(x):
    return 0.5 * x * (1.0 + jnp.tanh(math.sqrt(2.0 / math.pi) * (x + 0.044715 * (x * x * x))))


def _sublane_sums(ps):
    sub = lax.broadcasted_iota(I32, (SUBLANES, LANES), 0)
    lvl, stride = list(ps), SUBLANES // 2
    while len(lvl) > 1:
        half = len(lvl) // 2
        low = (sub & stride) == 0
        nxt = []
        for n in range(half):
            a, b = lvl[n], lvl[n + half]
            nxt.append(jnp.where(low, a + pltpu.roll(a, SUBLANES - stride, 0), b + pltpu.roll(b, stride, 0)))
        lvl, stride = nxt, stride // 2
    return lvl[0]


def _pack_uv(u, v):
    ne, d = u.shape
    hi = lax.bitcast_convert_type(u.astype(BF16), jnp.uint16).astype(jnp.uint32)
    lo = lax.bitcast_convert_type(v.astype(BF16), jnp.uint16).astype(jnp.uint32)
    return ((hi << 16) | lo).reshape(ne, d // LANES, LANES)


def _u_of(word):
    return pltpu.bitcast(word & jnp.uint32(0xFFFF0000), F32)


def _v_of(word):
    return pltpu.bitcast(word << 16, F32)


def _expert_group(row, h2_ref, x1_ref, gate_ref, o_ref, abc_scr, base, nsel, before_dots=None, before_sum=None):
    dsub = h2_ref.shape[1]
    lane_id = lax.broadcasted_iota(I32, (nsel, LANES), 1)
    st = jnp.zeros((nsel, LANES), F32)
    for t in range(TOK):
        if before_dots is not None:
            before_dots(t)
        xt = h2_ref[base + t]
        qs = []
        for g in range(nsel // SUBLANES):
            ps = [_u_of(row(t, g * SUBLANES + k)) * xt for k in range(SUBLANES)]
            qs.append(_sublane_sums(ps))
        q = jnp.concatenate(qs, axis=0)
        st = jnp.where(lane_id == t, jnp.sum(q, axis=1, keepdims=True), st)
    g8 = gate_ref[base:base + TOK, :]
    gt = jnp.concatenate([g8, jnp.zeros((nsel - TOK, nsel), F32)], axis=0).T
    at = _gelu_tanh(st) * gt

    for t in range(TOK):
        if before_sum is not None:
            before_sum(t)
        abc_scr[t] = jnp.broadcast_to(at[:, t:t + 1], (nsel, LANES))
        accs = [jnp.zeros((dsub, LANES), F32) for _ in range(4)]
        for n in range(nsel):
            accs[n % 4] = accs[n % 4] + abc_scr[t, n:n + 1, :] * _v_of(row(t, n))
        o_ref[base + t] = x1_ref[base + t] + ((accs[0] + accs[1]) + (accs[2] + accs[3]))


def _expert_kernel(idxc_ref, idxn_ref, h2_ref, x1_ref, gate_ref, uv_ref, after_ref, o_ref, buf, sem, abc_scr,
                   *, nsel):
    del after_ref
    i = pl.program_id(0)
    nsteps = pl.num_programs(0)
    rows = TOK * nsel

    def issue_token(idx_ref, tok, s, t, n0=0, n1=nsel):
        for n in range(n0, n1):
            pltpu.make_async_copy(uv_ref.at[idx_ref[tok, n]], buf.at[s, t * nsel + n],
                                  sem.at[s]).start(priority=n % 2)

    def wait_slot(s):
        pltpu.make_async_copy(uv_ref.at[pl.ds(0, rows)], buf.at[s], sem.at[s]).wait()

    @pl.when(i == 0)
    def _():
        for t in range(TOK):
            issue_token(idxc_ref, t, 0, t)

    n_early = (3 * nsel) // 4

    for grp in range(2):
        wait_slot(grp)
        nxt_idx, nxt_tok, nxt_slot = (idxc_ref, TOK, 1) if grp == 0 else (idxn_ref, 0, 0)
        _expert_group(lambda t, n, grp=grp: buf[grp, t * nsel + n], h2_ref, x1_ref, gate_ref, o_ref, abc_scr,
                      grp * TOK, nsel,
                      before_dots=lambda t: issue_token(nxt_idx, nxt_tok + t, nxt_slot, t, 0, n_early),
                      before_sum=lambda t: issue_token(nxt_idx, nxt_tok + t, nxt_slot, t, n_early, nsel))

    @pl.when(i == nsteps - 1)
    def _():
        wait_slot(0)


def _expert_staged_kernel(rows_ref, h2_ref, x1_ref, gate_ref, prev_ref, o_ref, abc_scr, *, nsel):
    del prev_ref
    for grp in range(h2_ref.shape[0] // TOK):
        _expert_group(lambda t, n, grp=grp: rows_ref[(grp * TOK + t) * nsel + n], h2_ref, x1_ref, gate_ref, o_ref,
                      abc_scr, grp * TOK, nsel)


SC_CORES = 2
SC_SUBCORES = 16
SC_CHUNK = 16
SC_NBUF = 7
SC_SLAB = 2048


def _sc_gather(table, idx):
    n = idx.shape[0]
    nw = SC_CORES * SC_SUBCORES
    per_w = n // nw
    nb = SC_NBUF
    assert n % nw == 0 and per_w % SC_SLAB == 0 and SC_SLAB % SC_CHUNK == 0
    nslab, nchunk = per_w // SC_SLAB, SC_SLAB // SC_CHUNK
    assert nchunk >= nb
    row_shape = table.shape[1:]
    mesh = plsc.VectorSubcoreMesh(core_axis_name="c", subcore_axis_name="s")

    @functools.partial(
        pl.kernel, mesh=mesh, out_type=jax.ShapeDtypeStruct((n,) + row_shape, table.dtype),
        scratch_types=[pltpu.VMEM((SC_SLAB,), jnp.int32)]
        + [pltpu.VMEM((SC_CHUNK,) + row_shape, table.dtype) for _ in range(nb)]
        + [pltpu.SemaphoreType.DMA for _ in range(2 * nb)])
    def gather(table_hbm, idx_hbm, out_hbm, idx_v, *scr):
        wid = lax.axis_index("s") * SC_CORES + lax.axis_index("c")
        bufs, gsem, wsem = scr[:nb], scr[nb:2 * nb], scr[2 * nb:]

        def gather_copy(c, b):
            return pltpu.make_async_copy(table_hbm.at[idx_v.at[pl.ds(c * SC_CHUNK, SC_CHUNK)]], bufs[b], gsem[b])

        @pl.loop(0, nslab)
        def _(sl):
            base = wid * per_w + sl * SC_SLAB
            pltpu.sync_copy(idx_hbm.at[pl.ds(base, SC_SLAB)], idx_v)

            def write_copy(c, b):
                return pltpu.make_async_copy(bufs[b], out_hbm.at[pl.ds(base + c * SC_CHUNK, SC_CHUNK)], wsem[b])

            for b in range(nb - 1):
                gather_copy(b, b).start()

            @pl.loop(0, -(-nchunk // nb) * nb, step=nb)
            def _(c0):
                for b in range(nb):
                    c = c0 + b
                    prev = (b - 1) % nb

                    @pl.when(c < nchunk)
                    def _():
                        gather_copy(c, b).wait()
                        write_copy(c, b).start()

                    @pl.when((c >= 1) & (c < nchunk))
                    def _():
                        write_copy(c - 1, prev).wait()

                    @pl.when(c + nb - 1 < nchunk)
                    def _():
                        gather_copy(c + nb - 1, prev).start()

            write_copy(nchunk - 1, (nchunk - 1) % nb).wait()

    return gather(table, idx)


def _experts_dma(eidx, gates, h2, x1, uv, ta, after=None):
    T, dsub, _ = x1.shape
    nsel = eidx.shape[1]
    tb = 2 * TOK
    nsteps = ta // tb
    return pl.pallas_call(
        functools.partial(_expert_kernel, nsel=nsel),
        grid=(nsteps,),
        in_specs=[pl.BlockSpec((tb, nsel), lambda i: (i, 0), memory_space=pltpu.SMEM),
                  pl.BlockSpec((tb, nsel), lambda i: (jnp.minimum(i + 1, nsteps - 1), 0), memory_space=pltpu.SMEM),
                  pl.BlockSpec((tb, dsub, LANES), lambda i: (i, 0, 0)),
                  pl.BlockSpec((tb, dsub, LANES), lambda i: (i, 0, 0)),
                  pl.BlockSpec((tb, nsel), lambda i: (i, 0)),
                  pl.BlockSpec(memory_space=pl.ANY), pl.BlockSpec(memory_space=pl.ANY)],
        out_specs=pl.BlockSpec((tb, dsub, LANES), lambda i: (i, 0, 0)),
        out_shape=jax.ShapeDtypeStruct((T, dsub, LANES), F32),
        scratch_shapes=[pltpu.VMEM((2, TOK * nsel, dsub, LANES), jnp.uint32), pltpu.SemaphoreType.DMA((2,)),
                        pltpu.VMEM((TOK, nsel, LANES), F32)],
        compiler_params=_params(1),
        name="peer_experts",
    )(eidx, eidx, h2, x1, gates, uv, uv if after is None else after)


def _experts_staged(rows, gates, h2, x1, out, ta):
    T, dsub, _ = x1.shape
    nsel = gates.shape[1]
    tb = 2 * TOK
    first = ta // tb
    tok = lambda i: (first + i, 0, 0)
    return pl.pallas_call(
        functools.partial(_expert_staged_kernel, nsel=nsel),
        grid=((T - ta) // tb,),
        in_specs=[pl.BlockSpec((tb * nsel, dsub, LANES), lambda i: (i, 0, 0)),
                  pl.BlockSpec((tb, dsub, LANES), tok),
                  pl.BlockSpec((tb, dsub, LANES), tok),
                  pl.BlockSpec((tb, nsel), lambda i: (first + i, 0)),
                  pl.BlockSpec(memory_space=pl.ANY)],
        out_specs=pl.BlockSpec((tb, dsub, LANES), tok),
        out_shape=jax.ShapeDtypeStruct((T, dsub, LANES), F32),
        scratch_shapes=[pltpu.VMEM((TOK, nsel, LANES), F32)],
        input_output_aliases={4: 0},
        compiler_params=_params(1),
        name="peer_experts_staged",
    )(rows, h2, x1, gates, out)


def _staged_tokens(T, use_sc):
    unit = SC_CORES * SC_SUBCORES * SC_SLAB // (PEER_HEADS * PEER_TOPK)
    return T // unit * unit if use_sc else 0


def _peer_gather_start(eidx, uv, use_sc):
    T, nsel = eidx.shape
    ts = _staged_tokens(T, use_sc)
    return _sc_gather(uv, eidx[T - ts:].reshape(ts * nsel)) if ts else None


def _peer_dma_part(eidx, gates, h2, x1, uv, use_sc, after):
    T, D = x1.shape
    dsub = D // LANES
    ta = T - _staged_tokens(T, use_sc)
    h3, x3 = h2.reshape(T, dsub, LANES), x1.reshape(T, dsub, LANES)
    return _experts_dma(eidx, gates, h3, x3, uv, ta, after) if ta else jnp.zeros_like(x3)


def _peer_finish(rows, out, gates, h2, x1, use_sc):
    T, D = x1.shape
    dsub = D // LANES
    if rows is not None:
        out = _experts_staged(rows, gates, h2.reshape(T, dsub, LANES), x1.reshape(T, dsub, LANES), out,
                              T - _staged_tokens(T, use_sc))
    return out.reshape(T, D)


def _ple_kernel(x_ref, p_ref, g_ref, wg_ref, wp_ref, o_ref):
    x = x_ref[...]
    h = x * lax.rsqrt(jnp.mean(x * x, axis=-1, keepdims=True) + EPS) * g_ref[...]
    gate = jax.nn.sigmoid(_dot(h.astype(BF16), wg_ref[...]))
    o_ref[...] = x + gate * _dot(p_ref[...].astype(BF16), wp_ref[...])


def _ple(x2, p2, ple_g, w_gate, w_proj, tm=512):
    T, D = x2.shape
    pd = p2.shape[1]
    return pl.pallas_call(
        _ple_kernel,
        grid=(T // tm,),
        in_specs=[pl.BlockSpec((tm, D), lambda i: (i, 0)), pl.BlockSpec((tm, pd), lambda i: (i, 0)),
                  _single((1, D), lambda i: (0, 0)), _single((D, D), lambda i: (0, 0)),
                  _single((pd, D), lambda i: (0, 0))],
        out_specs=pl.BlockSpec((tm, D), lambda i: (i, 0)),
        out_shape=jax.ShapeDtypeStruct((T, D), F32),
        compiler_params=_params(1),
        name="ple",
    )(x2, p2, ple_g.reshape(1, D), w_gate.astype(BF16), w_proj.astype(BF16))


def kernel(x, p, rel_bias, attn_norm_g, w_in, b_gate, q_norm_g, k_norm_g, w_att_out, conv_w, conv_b, conv_ln_g,
           conv_ln_b, w_conv_out, w_out, ffn_norm_g, w_peer_q, peer_sub_keys, peer_u, peer_v, ple_norm_g,
           w_ple_gate, w_ple_proj):
    B, S, D = x.shape
    depth = w_in.shape[0]
    assert S % KQUAD == 0 and D % LANES == 0 and S % 512 == 0
    n_near = _num_near_tiles(S)
    bias = _bias_tiles(rel_bias, n_near)
    nsl = 2 if S % (2 * 2048) == 0 else 1
    sl = S // nsl
    xs = [x[b, s * sl:(s + 1) * sl] for b in range(B) for s in range(nsl)]
    units = [(b, s) for b in range(B) for s in range(nsl)]
    for i in range(depth):
        uv = _pack_uv(peer_u[i], peer_v[i])

        def finish(pending, i=i):
            rows, part, gates, h2, x1, (b, s), use_sc = pending
            x2 = _peer_finish(rows, part, gates, h2, x1, use_sc)
            return _ple(x2, p[i, b, s * sl:(s + 1) * sl], ple_norm_g[i], w_ple_gate[i], w_ple_proj[i])

        pending, outs, last_rows = None, [], None
        keys, history = [], None
        for n, (b, s) in enumerate(units):
            use_sc = n < len(units) - 1
            q, k, qi, ki, vt, wit, u, gate = _in_proj(xs[n], attn_norm_g[i], w_in[i], b_gate[i], q_norm_g[i],
                                                      k_norm_g[i])
            if s == 0:
                keys, history = [], jnp.zeros((HALO, CONV_CH), F32)
            keys.append((k, ki, vt))
            att = _attention(q, qi, wit, [kp[0] for kp in keys], [kp[1] for kp in keys], [kp[2] for kp in keys],
                             bias, S, n_near)
            x1, h2 = _mix(att, u, history, gate, xs[n], conv_w[i], conv_b[i], conv_ln_g[i], conv_ln_b[i],
                          w_att_out[i], w_conv_out[i], w_out[i], ffn_norm_g[i])
            history = u[sl - HALO:]
            eidx, gates = _peer_route(h2, w_peer_q[i], peer_sub_keys[i])
            rows = _peer_gather_start(eidx, uv, use_sc)
            if pending is not None:
                outs.append(finish(pending))
            part = _peer_dma_part(eidx, gates, h2, x1, uv, use_sc, last_rows)
            pending = (rows, part, gates, h2, x1, (b, s), use_sc)
            last_rows = rows if rows is not None else last_rows
        outs.append(finish(pending))
        xs = outs
    return jnp.concatenate(xs, axis=0).reshape(B, S, D)
```

```python
import functools
import math

import numpy as np
import jax
import jax.numpy as jnp
from jax import lax
from jax.experimental import pallas as pl
from jax.experimental.pallas import tpu as pltpu
from jax.experimental.pallas import tpu_sc as plsc

CHUNK = 64
ATT_HEADS = 8
ATT_HEAD_DIM = 64
IDX_HEADS = 8
IDX_DIM = 64
TOPK_MAX = 256
REL_BUCKETS = 32
REL_MAX_DIST = 1024
CONV_CH = 512
CONV_WIDTH = 31
N_BRANCH = 2
PEER_HEADS = 8
N_KEYS = 128
PEER_TOPK = 16
EPS = 1e-6

LANES = 128
SUBLANES = 8
VMEM_LIMIT = 56 * 1024 * 1024

QB = 128
KPAIR = 2 * QB
KQUAD = 4 * QB
INT_MIN = -(2 ** 31)
NEG_BIG = -1e30
LOG2E = math.log2(math.e)

F32 = jnp.float32
BF16 = jnp.bfloat16
I32 = jnp.int32


def _nt_dot(a, b, precision=None):
    return lax.dot_general(a, b, (((1,), (1,)), ((), ())), precision=precision,
                           preferred_element_type=F32)


def _dot(a, b):
    return jnp.dot(a, b, preferred_element_type=F32)


def _single(shape, index_map):
    return pl.BlockSpec(shape, index_map)


def _params(n_grid_dims):
    return pltpu.CompilerParams(dimension_semantics=("arbitrary",) * n_grid_dims,
                                vmem_limit_bytes=VMEM_LIMIT)


def _t5_bucket_np(rel):
    half = REL_BUCKETS // 2
    max_exact = half // 2
    ret = np.where(rel > 0, half, 0)
    n = np.abs(rel)
    nf = np.maximum(n, 1).astype(np.float32)
    large = max_exact + (np.log(nf / np.float32(max_exact)) / np.float32(math.log(REL_MAX_DIST / max_exact))
                         * np.float32(half - max_exact)).astype(np.int32)
    large = np.minimum(large, half - 1)
    return ret + np.where(n < max_exact, n, large)


def _num_near_tiles(seq):
    n = np.arange(1, max(seq, 2 * REL_MAX_DIST) + 1)
    b = _t5_bucket_np(-n)
    sat = REL_BUCKETS // 2 - 1
    unsat = np.nonzero(b != sat)[0]
    n_sat = int(n[unsat[-1]]) + 1 if unsat.size else 1
    return -(-(n_sat + QB - 1) // QB)


def _bias_kernel(rb_ref, o_ref, *, n_near):
    d = pl.program_id(0)
    i = lax.broadcasted_iota(I32, (QB, QB), 0)
    j = lax.broadcasted_iota(I32, (QB, QB), 1)
    rel = i - j - d * QB
    rel = jnp.where(d >= n_near, -8 * REL_MAX_DIST, rel)
    half = REL_BUCKETS // 2
    max_exact = half // 2
    ret = jnp.where(rel > 0, half, 0)
    n = jnp.abs(rel)
    nf = jnp.maximum(n, 1).astype(F32)
    large = max_exact + (jnp.log(nf / max_exact) / math.log(REL_MAX_DIST / max_exact)
                         * (half - max_exact)).astype(I32)
    large = jnp.minimum(large, half - 1)
    bucket = ret + jnp.where(n < max_exact, n, large)
    for h in range(ATT_HEADS):
        acc = jnp.zeros((QB, QB), F32)
        for b in range(REL_BUCKETS):
            acc = jnp.where(bucket == b, rb_ref[b, h], acc)
        o_ref[0, h] = acc * LOG2E


def _bias_tiles(rel_bias, n_near):
    return pl.pallas_call(
        functools.partial(_bias_kernel, n_near=n_near),
        grid=(n_near + 1,),
        in_specs=[pl.BlockSpec(memory_space=pltpu.SMEM)],
        out_specs=pl.BlockSpec((1, ATT_HEADS, QB, QB), lambda d: (d, 0, 0, 0)),
        out_shape=jax.ShapeDtypeStruct((n_near + 1, ATT_HEADS, QB, QB), F32),
        compiler_params=_params(1),
        name="bias_tiles",
    )(rel_bias)


def _inproj_kernel(x_ref, g_ref, wa_ref, wvt_ref, wwit_ref, wglu_ref, wgate_ref, bgate_ref, gq_ref, gk_ref,
                   q_ref, k_ref, qi_ref, ki_ref, vt_ref, wit_ref, u_ref, gate_ref, *, tm):
    x = x_ref[...]
    h = x * lax.rsqrt(jnp.mean(x * x, axis=-1, keepdims=True) + EPS) * g_ref[...]
    hb = h.astype(BF16)
    hp = ATT_HEADS * LANES
    ya = _dot(hb, wa_ref[...])
    for h_i in range(ATT_HEADS):
        sl = slice(h_i * LANES, (h_i + 1) * LANES)
        qh = ya[:, h_i * LANES:(h_i + 1) * LANES]
        ms = jnp.sum(qh * qh, axis=-1, keepdims=True) * (1.0 / ATT_HEAD_DIM)
        q_ref[:, sl] = (qh * lax.rsqrt(ms + EPS) * gq_ref[...]).astype(BF16)
        kh = ya[:, hp + h_i * LANES:hp + (h_i + 1) * LANES]
        ms = jnp.sum(kh * kh, axis=-1, keepdims=True) * (1.0 / ATT_HEAD_DIM)
        k_ref[:, sl] = (kh * lax.rsqrt(ms + EPS) * gk_ref[...]).astype(BF16)
    qi_ref[...] = ya[:, 2 * hp:3 * hp].astype(BF16)
    ki_ref[...] = ya[:, 3 * hp:3 * hp + LANES].astype(BF16)
    vt = _nt_dot(wvt_ref[...], hb).astype(BF16)
    for c in range(tm // KPAIR):
        vt_ref[c] = vt[:, c * KPAIR:(c + 1) * KPAIR]
    wit_ref[...] = _nt_dot(wwit_ref[...], hb)
    glu = _dot(hb, wglu_ref[...])
    u_ref[...] = glu[:, :CONV_CH] * jax.nn.sigmoid(glu[:, CONV_CH:])
    gate_ref[...] = jax.nn.sigmoid(_dot(hb, wgate_ref[...]) + bgate_ref[...]).astype(BF16)


def _pad_heads(w, nh, hd):
    d = w.shape[0]
    w3 = w.reshape(d, nh, hd)
    w3 = jnp.pad(w3, ((0, 0), (0, 0), (0, LANES - hd)))
    return w3.reshape(d, nh * LANES)


def _in_proj(x2, attn_g, w_in, b_gate, q_g, k_g, tm=256):
    T, D = x2.shape
    aw = ATT_HEADS * ATT_HEAD_DIM
    iw = IDX_HEADS * IDX_DIM
    o = 0
    wq = w_in[:, o:o + aw]; o += aw
    wk = w_in[:, o:o + aw]; o += aw
    wv = w_in[:, o:o + aw]; o += aw
    wqi = w_in[:, o:o + iw]; o += iw
    wki = w_in[:, o:o + IDX_DIM]; o += IDX_DIM
    wwi = w_in[:, o:o + IDX_HEADS]; o += IDX_HEADS
    wglu = w_in[:, o:o + 2 * CONV_CH]; o += 2 * CONV_CH
    wgate = w_in[:, o:o + N_BRANCH * D]
    wa = jnp.concatenate([_pad_heads(wq, ATT_HEADS, ATT_HEAD_DIM), _pad_heads(wk, ATT_HEADS, ATT_HEAD_DIM),
                          _pad_heads(wqi, IDX_HEADS, IDX_DIM),
                          jnp.pad(wki, ((0, 0), (0, LANES - IDX_DIM)))], axis=1).astype(BF16)
    na = wa.shape[1]
    hp = ATT_HEADS * LANES
    pad_g = lambda g, s: jnp.pad(g * s, (0, LANES - ATT_HEAD_DIM)).reshape(1, LANES)
    gq = pad_g(q_g, ATT_HEAD_DIM ** -0.5 * LOG2E)
    gk = pad_g(k_g, 1.0)
    const = lambda *shape: _single(shape, lambda i: (0,) * len(shape))
    outs = pl.pallas_call(
        functools.partial(_inproj_kernel, tm=tm),
        grid=(T // tm,),
        in_specs=[pl.BlockSpec((tm, D), lambda i: (i, 0)), const(1, D), const(D, na), const(aw, D),
                  const(IDX_HEADS, D), const(D, 2 * CONV_CH), const(D, N_BRANCH * D), const(1, N_BRANCH * D),
                  const(1, LANES), const(1, LANES)],
        out_specs=[pl.BlockSpec((tm, hp), lambda i: (i, 0)), pl.BlockSpec((tm, hp), lambda i: (i, 0)),
                   pl.BlockSpec((tm, hp), lambda i: (i, 0)), pl.BlockSpec((tm, LANES), lambda i: (i, 0)),
                   pl.BlockSpec((tm // KPAIR, aw, KPAIR), lambda i: (i, 0, 0)),
                   pl.BlockSpec((IDX_HEADS, tm), lambda i: (0, i)),
                   pl.BlockSpec((tm, CONV_CH), lambda i: (i, 0)),
                   pl.BlockSpec((tm, N_BRANCH * D), lambda i: (i, 0))],
        out_shape=[jax.ShapeDtypeStruct((T, hp), BF16), jax.ShapeDtypeStruct((T, hp), BF16),
                   jax.ShapeDtypeStruct((T, hp), BF16), jax.ShapeDtypeStruct((T, LANES), BF16),
                   jax.ShapeDtypeStruct((T // KPAIR, aw, KPAIR), BF16),
                   jax.ShapeDtypeStruct((IDX_HEADS, T), F32),
                   jax.ShapeDtypeStruct((T, CONV_CH), F32),
                   jax.ShapeDtypeStruct((T, N_BRANCH * D), BF16)],
        compiler_params=_params(1),
        name="in_proj",
    )(x2, attn_g.reshape(1, D), wa, wv.T.astype(BF16), wwi.T.astype(BF16), wglu.astype(BF16),
      wgate.astype(BF16), b_gate.reshape(1, N_BRANCH * D), gq, gk)
    return outs


def _attn_kernel(q_ref, qi_ref, wit_ref, *rest, seq, n_near, topk, n_parts, qb0):
    k_parts, ki_parts, vt_parts = rest[:n_parts], rest[n_parts:2 * n_parts], rest[2 * n_parts:3 * n_parts]
    (bias_hbm, o_ref, k_ref, ki_ref, vt_ref, bias_ref, load_sem, key_scr, att_scr), scr = (
        rest[3 * n_parts:3 * n_parts + 9], rest[3 * n_parts + 9:])
    qb = qb0 + pl.program_id(0)

    @pl.when(pl.program_id(0) == 0)
    def _():
        loads = [pltpu.make_async_copy(bias_hbm, bias_ref, load_sem.at[3 * n_parts])]
        kp = kq = 0
        for n in range(n_parts):
            npair, nqd = k_parts[n].shape[0], ki_parts[n].shape[0]
            loads += [pltpu.make_async_copy(k_parts[n], k_ref.at[pl.ds(kp, npair)], load_sem.at[3 * n]),
                      pltpu.make_async_copy(ki_parts[n], ki_ref.at[pl.ds(kq, nqd)], load_sem.at[3 * n + 1]),
                      pltpu.make_async_copy(vt_parts[n], vt_ref.at[pl.ds(kp, npair)], load_sem.at[3 * n + 2])]
            kp, kq = kp + npair, kq + nqd
        for c in loads:
            c.start()
        for c in loads:
            c.wait()

    nquad = (qb + 4) // 4
    lane_t = lax.broadcasted_iota(I32, (1, QB), 1) + qb * QB
    qchunk = lane_t // CHUNK
    sub = lax.broadcasted_iota(I32, (KQUAD, QB), 0)
    idx_scale = (IDX_DIM ** -0.5) * (IDX_HEADS ** -0.5)
    wrow = wit_ref[...] * idx_scale

    qi_all = jnp.concatenate([qi_ref[:, h * LANES:(h + 1) * LANES] for h in range(IDX_HEADS)], axis=0)

    def score_quad(j, carry):
        d = _nt_dot(ki_ref[j], qi_all)
        acc = jnp.zeros((KQUAD, QB), F32)
        for h in range(IDX_HEADS):
            acc = acc + jnp.maximum(d[:, h * QB:(h + 1) * QB], 0.0) * wrow[h:h + 1, :]
        bits = pltpu.bitcast(acc, I32)
        skey = bits ^ ((bits >> 31) & 0x7FFFFFFF)
        visible = (sub + j * KQUAD) // CHUNK <= qchunk
        key_scr[j] = jnp.where(visible, skey, INT_MIN)
        return carry

    lax.fori_loop(0, nquad, score_quad, 0)

    nvis = (qchunk + 1) * CHUNK
    kk = jnp.minimum(topk, nvis)

    def count(pred):
        def body(j, acc8):
            m = jnp.where(pred(key_scr[j], j), 1, 0)
            return acc8 + m.reshape(KQUAD // SUBLANES, SUBLANES, QB).sum(axis=0)
        return lax.fori_loop(0, nquad, body, jnp.zeros((SUBLANES, QB), I32)).sum(axis=0, keepdims=True)

    def bit_step(i, carry):
        ans, cnt = carry
        cand = ans + lax.shift_left(jnp.int32(1), 31 - i)
        c = count(lambda blk, j: blk >= cand)
        take = c >= kk
        return jnp.where(take, cand, ans), jnp.where(take, c, cnt)

    thr, cnt = lax.fori_loop(0, 32, bit_step,
                             (jnp.full((1, QB), INT_MIN, I32), jnp.full((1, QB), 0, I32) + nquad * KQUAD))

    @pl.when(jnp.max(cnt - kk) > 0)
    def _():
        n_gt = count(lambda blk, j: blk > thr)
        need = kk - n_gt

        def idx_step(i, jbound):
            cand = jbound + lax.shift_left(jnp.int32(1), (seq.bit_length() - 1) - i)
            c = count(lambda blk, j: (blk == thr) & (sub + j * KQUAD < cand))
            return jnp.where(c <= need, cand, jbound)

        jbound = lax.fori_loop(0, seq.bit_length(), idx_step, jnp.zeros((1, QB), I32))

        def drop(j, carry):
            blk = key_scr[j]
            key_scr[j] = jnp.where((blk == thr) & (sub + j * KQUAD >= jbound), INT_MIN, blk)
            return carry

        lax.fori_loop(0, nquad, drop, 0)

    acc_refs, s_even, s_odd = scr[:ATT_HEADS], scr[ATT_HEADS], scr[ATT_HEADS + 1]
    for acc_ref in acc_refs:
        acc_ref[...] = jnp.zeros(acc_ref.shape, F32)
    head_row = lax.broadcasted_iota(I32, (ATT_HEADS, QB), 0)
    last_pair = k_ref.shape[0] - 1

    def qk(jp, s_ref):
        jc = jnp.minimum(jp, last_pair)
        for h in range(ATT_HEADS):
            s_ref[h] = _nt_dot(k_ref[jc, :, h * LANES:(h + 1) * LANES], q_ref[:, h * LANES:(h + 1) * LANES])

    def softmax_pv(jq, half, s_ref, carry):
        m_all, l_all = carry
        jp = 2 * jq + half
        keys = key_scr[jq, half * KPAIR:(half + 1) * KPAIR, :]
        masked = jnp.where(keys >= thr, 0.0, -jnp.inf)
        tile0 = jnp.clip(qb - 2 * jp, 0, n_near)
        tile1 = jnp.clip(qb - 2 * jp - 1, 0, n_near)
        for h in range(ATT_HEADS):
            rows = slice(h * ATT_HEAD_DIM, (h + 1) * ATT_HEAD_DIM)
            bias = jnp.concatenate([bias_ref[tile0, h], bias_ref[tile1, h]], axis=0)
            s = s_ref[h] + bias + masked
            m = m_all[h:h + 1, :]
            m_new = jnp.maximum(m, jnp.max(s, axis=0, keepdims=True))
            p = jnp.exp2(s - m_new)
            alpha = jnp.exp2(m - m_new)
            l_new = alpha * l_all[h:h + 1, :] + jnp.sum(p, axis=0, keepdims=True)
            m_all = jnp.where(head_row == h, m_new, m_all)
            l_all = jnp.where(head_row == h, l_new, l_all)
            acc_refs[h][...] = alpha * acc_refs[h][...] + _dot(vt_ref[jp, rows, :], p.astype(BF16))
        return m_all, l_all

    def att_quad(jq, carry):
        qk(2 * jq + 1, s_odd)
        carry = softmax_pv(jq, 0, s_even, carry)
        qk(2 * jq + 2, s_even)
        return softmax_pv(jq, 1, s_odd, carry)

    qk(0, s_even)
    _, l_all = lax.fori_loop(0, nquad, att_quad,
                             (jnp.full((ATT_HEADS, QB), NEG_BIG, F32), jnp.zeros((ATT_HEADS, QB), F32)))
    for h in range(ATT_HEADS):
        rows = slice(h * ATT_HEAD_DIM, (h + 1) * ATT_HEAD_DIM)
        att_scr[rows, :] = acc_refs[h][...] / l_all[h:h + 1, :]
    o_ref[...] = att_scr[...].T.astype(o_ref.dtype)


def _attention(q, qi, wit, k_parts, ki_parts, vt_parts, bias, S, n_near):
    Tq = q.shape[0]
    hp = ATT_HEADS * LANES
    aw = ATT_HEADS * ATT_HEAD_DIM
    topk = min(TOPK_MAX, S // 4)
    n_parts = len(k_parts)
    k4 = [k.reshape(-1, KPAIR, hp) for k in k_parts]
    ki4 = [ki.reshape(-1, KQUAD, LANES) for ki in ki_parts]
    npr = sum(k.shape[0] for k in k4)
    nqd = sum(ki.shape[0] for ki in ki4)
    assert npr * KPAIR == nqd * KQUAD and vt_parts[0].shape[1:] == (aw, KPAIR)
    qb0 = (npr * KPAIR - Tq) // QB
    return pl.pallas_call(
        functools.partial(_attn_kernel, seq=S, n_near=n_near, topk=topk, n_parts=n_parts, qb0=qb0),
        grid=(Tq // QB,),
        in_specs=[pl.BlockSpec((QB, hp), lambda i: (i, 0)),
                  pl.BlockSpec((QB, hp), lambda i: (i, 0)),
                  pl.BlockSpec((IDX_HEADS, QB), lambda i: (0, i))]
        + [pl.BlockSpec(memory_space=pl.ANY)] * (3 * n_parts + 1),
        out_specs=pl.BlockSpec((QB, aw), lambda i: (i, 0)),
        out_shape=jax.ShapeDtypeStruct((Tq, aw), BF16),
        scratch_shapes=[pltpu.VMEM((npr, KPAIR, hp), BF16), pltpu.VMEM((nqd, KQUAD, LANES), BF16),
                        pltpu.VMEM((npr, aw, KPAIR), BF16), pltpu.VMEM((n_near + 1, ATT_HEADS, QB, QB), F32),
                        pltpu.SemaphoreType.DMA((3 * n_parts + 1,)),
                        pltpu.VMEM((nqd, KQUAD, QB), I32), pltpu.VMEM((aw, QB), F32)]
        + [pltpu.VMEM((ATT_HEAD_DIM, QB), F32) for _ in range(ATT_HEADS)]
        + [pltpu.VMEM((ATT_HEADS, KPAIR, QB), F32) for _ in range(2)],
        compiler_params=_params(1),
        name="dsa_attention",
    )(q, qi, wit, *k4, *ki4, *vt_parts, bias)


HALO = 32


def _mix_kernel(att_ref, u_ref, halo_ref, hist_ref, gate_ref, x_ref, cw_ref, cb_ref, lng_ref, lnb_ref, wao_ref,
                wco_ref, wout_ref, gffn_ref, x1_ref, h2_ref, ext_scr, *, tm):
    i = pl.program_id(0)
    ext_scr[0:HALO, :] = jnp.where(i == 0, hist_ref[...], halo_ref[...])
    ext_scr[HALO:HALO + tm, :] = u_ref[...]
    y = jnp.zeros((tm, CONV_CH), F32)
    for j in range(CONV_WIDTH):
        y = y + cw_ref[j:j + 1, :] * ext_scr[pl.ds(HALO - (CONV_WIDTH - 1) + j, tm), :]
    y = y + cb_ref[...]
    mu = jnp.mean(y, axis=-1, keepdims=True)
    yc = y - mu
    yn = yc * lax.rsqrt(jnp.mean(yc * yc, axis=-1, keepdims=True) + EPS) * lng_ref[...] + lnb_ref[...]
    z = yn * jax.nn.sigmoid(yn)
    y_conv = _dot(z.astype(BF16), wco_ref[...])
    y_att = _dot(att_ref[...], wao_ref[...])
    d = y_att.shape[1]
    g = gate_ref[...]
    mixed = g[:, :d].astype(F32) * y_att + g[:, d:].astype(F32) * y_conv
    x1 = x_ref[...] + _dot(mixed.astype(BF16), wout_ref[...])
    x1_ref[...] = x1
    h2_ref[...] = x1 * lax.rsqrt(jnp.mean(x1 * x1, axis=-1, keepdims=True) + EPS) * gffn_ref[...]


def _mix(att, u, history, gate, x2, conv_w, conv_b, ln_g, ln_b, w_att_out, w_conv_out, w_out, ffn_g, tm=256):
    T, D = x2.shape
    aw = att.shape[1]
    const = lambda *shape: _single(shape, lambda i: (0,) * len(shape))
    hb = tm // HALO
    return pl.pallas_call(
        functools.partial(_mix_kernel, tm=tm),
        grid=(T // tm,),
        in_specs=[pl.BlockSpec((tm, aw), lambda i: (i, 0)),
                  pl.BlockSpec((tm, CONV_CH), lambda i: (i, 0)),
                  pl.BlockSpec((HALO, CONV_CH), lambda i: (jnp.maximum(i * hb - 1, 0), 0)),
                  const(HALO, CONV_CH),
                  pl.BlockSpec((tm, N_BRANCH * D), lambda i: (i, 0)),
                  pl.BlockSpec((tm, D), lambda i: (i, 0)),
                  const(CONV_WIDTH, CONV_CH), const(1, CONV_CH), const(1, CONV_CH), const(1, CONV_CH),
                  const(aw, D), const(CONV_CH, D), const(D, D), const(1, D)],
        out_specs=[pl.BlockSpec((tm, D), lambda i: (i, 0)), pl.BlockSpec((tm, D), lambda i: (i, 0))],
        out_shape=[jax.ShapeDtypeStruct((T, D), F32), jax.ShapeDtypeStruct((T, D), F32)],
        scratch_shapes=[pltpu.VMEM((HALO + tm, CONV_CH), F32)],
        compiler_params=_params(1),
        name="mix_out_proj",
    )(att, u, u, history, gate, x2, conv_w.reshape(CONV_WIDTH, CONV_CH), conv_b.reshape(1, CONV_CH),
      ln_g.reshape(1, CONV_CH), ln_b.reshape(1, CONV_CH), w_att_out.astype(BF16), w_conv_out.astype(BF16),
      w_out.astype(BF16), ffn_g.reshape(1, D))


def _top_rows(sc, k, payload=None):
    rows = sc.shape[0]
    iota = lax.broadcasted_iota(I32, sc.shape, 0)
    out_row = lax.broadcasted_iota(I32, (k, sc.shape[1]), 0)
    vals = jnp.zeros((k, sc.shape[1]), F32)
    idxs = jnp.zeros((k, sc.shape[1]), I32)
    for r in range(k):
        m = jnp.max(sc, axis=0, keepdims=True)
        idx = jnp.min(jnp.where(sc == m, iota, rows), axis=0, keepdims=True)
        hit = iota == idx
        rec = idx if payload is None else jnp.max(jnp.where(hit, payload, -1), axis=0, keepdims=True)
        vals = jnp.where(out_row == r, m, vals)
        idxs = jnp.where(out_row == r, rec, idxs)
        sc = jnp.where(hit, -jnp.inf, sc)
    return vals, idxs


def _route_kernel(h2_ref, wq_ref, sk_ref, e_ref, g_ref, qt_scr, et_scr, gt_scr, *, tm):
    half = N_KEYS
    qt_scr[...] = _nt_dot(wq_ref[...], h2_ref[...].astype(BF16)).astype(BF16)

    def head(h, carry):
        tops = []
        for c in range(2):
            row0 = pl.multiple_of((h * 2 + c) * half, half)
            sc = _dot(sk_ref[h * 2 + c], qt_scr[pl.ds(row0, half), :])
            tops.append(_top_rows(sc, PEER_TOPK))
        (a, ia), (b, ib) = tops
        k = PEER_TOPK
        g = SUBLANES
        assert (k // 2) % g == 0 and k % g == 0
        row = lax.broadcasted_iota(I32, (g, a.shape[1]), 0)
        cand_parts, cidx_parts = [], []

        def add(av, iav, bv, ibv, valid_rows):
            s = av + bv
            if valid_rows < g:
                s = jnp.where(row < valid_rows, s, -jnp.inf)
            cand_parts.append(s)
            cidx_parts.append(iav * N_KEYS + ibv)

        for i in range(k):
            nj = k // (i + 1)
            if nj >= g:
                for j0 in range(0, nj, g):
                    add(a[i:i + 1, :], ia[i:i + 1, :], b[j0:j0 + g, :], ib[j0:j0 + g, :], g)
            elif nj > 1:
                add(a[i:i + 1, :], ia[i:i + 1, :], b[0:g, :], ib[0:g, :], nj)
            elif i % g == 0:
                add(a[i:i + g, :], ia[i:i + g, :], b[0:1, :], ib[0:1, :], g)
        cand = jnp.concatenate(cand_parts, axis=0)
        cidx = jnp.concatenate(cidx_parts, axis=0)
        ts, te = _top_rows(cand, PEER_TOPK, payload=cidx)
        ex = jnp.exp(ts - ts[0:1, :])
        gate = ex / jnp.sum(ex, axis=0, keepdims=True)
        r0 = pl.multiple_of(h * PEER_TOPK, PEER_TOPK)
        et_scr[pl.ds(r0, PEER_TOPK), :] = te
        gt_scr[pl.ds(r0, PEER_TOPK), :] = gate
        return carry

    lax.fori_loop(0, PEER_HEADS, head, 0)
    e_ref[...] = et_scr[...].T
    g_ref[...] = gt_scr[...].T


def _peer_route(h2, w_peer_q, sub_keys, tm=256):
    T, D = h2.shape
    nsel = PEER_HEADS * PEER_TOPK
    qd = w_peer_q.shape[1]
    half = sub_keys.shape[-1]
    skb = sub_keys.reshape(PEER_HEADS * 2, N_KEYS, half).astype(BF16)
    return pl.pallas_call(
        functools.partial(_route_kernel, tm=tm),
        grid=(T // tm,),
        in_specs=[pl.BlockSpec((tm, D), lambda i: (i, 0)),
                  _single((qd, D), lambda i: (0, 0)),
                  _single((PEER_HEADS * 2, N_KEYS, half), lambda i: (0, 0, 0))],
        out_specs=[pl.BlockSpec((tm, nsel), lambda i: (i, 0)), pl.BlockSpec((tm, nsel), lambda i: (i, 0))],
        out_shape=[jax.ShapeDtypeStruct((T, nsel), I32), jax.ShapeDtypeStruct((T, nsel), F32)],
        scratch_shapes=[pltpu.VMEM((qd, tm), BF16), pltpu.VMEM((nsel, tm), I32), pltpu.VMEM((nsel, tm), F32)],
        compiler_params=_params(1),
        name="peer_route",
    )(h2, w_peer_q.T.astype(BF16), skb)


TOK = 8


def _gelu_tanh(x):
    return 0.5 * x * (1.0 + jnp.tanh(math.sqrt(2.0 / math.pi) * (x + 0.044715 * (x * x * x))))


def _sublane_sums(ps):
    sub = lax.broadcasted_iota(I32, (SUBLANES, LANES), 0)
    lvl, stride = list(ps), SUBLANES // 2
    while len(lvl) > 1:
        half = len(lvl) // 2
        low = (sub & stride) == 0
        nxt = []
        for n in range(half):
            a, b = lvl[n], lvl[n + half]
            nxt.append(jnp.where(low, a + pltpu.roll(a, SUBLANES - stride, 0), b + pltpu.roll(b, stride, 0)))
        lvl, stride = nxt, stride // 2
    return lvl[0]


def _pack_uv(u, v):
    ne, d = u.shape
    hi = lax.bitcast_convert_type(u.astype(BF16), jnp.uint16).astype(jnp.uint32)
    lo = lax.bitcast_convert_type(v.astype(BF16), jnp.uint16).astype(jnp.uint32)
    return ((hi << 16) | lo).reshape(ne, d // LANES, LANES)


def _u_of(word):
    return pltpu.bitcast(word & jnp.uint32(0xFFFF0000), F32)


def _v_of(word):
    return pltpu.bitcast(word << 16, F32)


def _expert_group(row, h2_ref, x1_ref, gate_ref, o_ref, abc_scr, base, nsel, before_dots=None, before_sum=None):
    dsub = h2_ref.shape[1]
    lane_id = lax.broadcasted_iota(I32, (nsel, LANES), 1)
    st = jnp.zeros((nsel, LANES), F32)
    for t in range(TOK):
        if before_dots is not None:
            before_dots(t)
        xt = h2_ref[base + t]
        qs = []
        for g in range(nsel // SUBLANES):
            ps = [_u_of(row(t, g * SUBLANES + k)) * xt for k in range(SUBLANES)]
            qs.append(_sublane_sums(ps))
        q = jnp.concatenate(qs, axis=0)
        st = jnp.where(lane_id == t, jnp.sum(q, axis=1, keepdims=True), st)
    g8 = gate_ref[base:base + TOK, :]
    gt = jnp.concatenate([g8, jnp.zeros((nsel - TOK, nsel), F32)], axis=0).T
    at = _gelu_tanh(st) * gt

    for t in range(TOK):
        if before_sum is not None:
            before_sum(t)
        abc_scr[t] = jnp.broadcast_to(at[:, t:t + 1], (nsel, LANES))
        accs = [jnp.zeros((dsub, LANES), F32) for _ in range(4)]
        for n in range(nsel):
            accs[n % 4] = accs[n % 4] + abc_scr[t, n:n + 1, :] * _v_of(row(t, n))
        o_ref[base + t] = x1_ref[base + t] + ((accs[0] + accs[1]) + (accs[2] + accs[3]))


def _expert_kernel(idxc_ref, idxn_ref, h2_ref, x1_ref, gate_ref, uv_ref, *rest, nsel):
    o_ref, buf, sem, abc_scr = rest[-4:]
    i = pl.program_id(0)
    nsteps = pl.num_programs(0)
    rows = TOK * nsel

    def issue_token(idx_ref, tok, s, t, n0=0, n1=nsel):
        for n in range(n0, n1):
            pltpu.make_async_copy(uv_ref.at[idx_ref[tok, n]], buf.at[s, t * nsel + n],
                                  sem.at[s]).start(priority=n % 2)

    def wait_slot(s):
        pltpu.make_async_copy(uv_ref.at[pl.ds(0, rows)], buf.at[s], sem.at[s]).wait()

    @pl.when(i == 0)
    def _():
        for t in range(TOK):
            issue_token(idxc_ref, t, 0, t)

    n_early = (3 * nsel) // 4

    for grp in range(2):
        wait_slot(grp)
        nxt_idx, nxt_tok, nxt_slot = (idxc_ref, TOK, 1) if grp == 0 else (idxn_ref, 0, 0)
        _expert_group(lambda t, n, grp=grp: buf[grp, t * nsel + n], h2_ref, x1_ref, gate_ref, o_ref, abc_scr,
                      grp * TOK, nsel,
                      before_dots=lambda t: issue_token(nxt_idx, nxt_tok + t, nxt_slot, t, 0, n_early),
                      before_sum=lambda t: issue_token(nxt_idx, nxt_tok + t, nxt_slot, t, n_early, nsel))

    @pl.when(i == nsteps - 1)
    def _():
        wait_slot(0)


def _expert_staged_kernel(rows_ref, h2_ref, x1_ref, gate_ref, prev_ref, o_ref, abc_scr, *, nsel):
    del prev_ref
    for grp in range(h2_ref.shape[0] // TOK):
        _expert_group(lambda t, n, grp=grp: rows_ref[(grp * TOK + t) * nsel + n], h2_ref, x1_ref, gate_ref, o_ref,
                      abc_scr, grp * TOK, nsel)


SC_CORES = 2
SC_SUBCORES = 16
SC_CHUNK = 16
SC_NBUF = 7
SC_SLAB = 2048


def _sc_gather(table, idx):
    n = idx.shape[0]
    nw = SC_CORES * SC_SUBCORES
    per_w = n // nw
    nb = SC_NBUF
    assert n % nw == 0 and per_w % SC_SLAB == 0 and SC_SLAB % SC_CHUNK == 0
    nslab, nchunk = per_w // SC_SLAB, SC_SLAB // SC_CHUNK
    assert nchunk >= nb
    row_shape = table.shape[1:]
    mesh = plsc.VectorSubcoreMesh(core_axis_name="c", subcore_axis_name="s")

    @functools.partial(
        pl.kernel, mesh=mesh, out_type=jax.ShapeDtypeStruct((n,) + row_shape, table.dtype),
        scratch_types=[pltpu.VMEM((SC_SLAB,), jnp.int32)]
        + [pltpu.VMEM((SC_CHUNK,) + row_shape, table.dtype) for _ in range(nb)]
        + [pltpu.SemaphoreType.DMA for _ in range(2 * nb)])
    def gather(table_hbm, idx_hbm, out_hbm, idx_v, *scr):
        wid = lax.axis_index("s") * SC_CORES + lax.axis_index("c")
        bufs, gsem, wsem = scr[:nb], scr[nb:2 * nb], scr[2 * nb:]

        def gather_copy(c, b):
            return pltpu.make_async_copy(table_hbm.at[idx_v.at[pl.ds(c * SC_CHUNK, SC_CHUNK)]], bufs[b], gsem[b])

        @pl.loop(0, nslab)
        def _(sl):
            base = wid * per_w + sl * SC_SLAB
            pltpu.sync_copy(idx_hbm.at[pl.ds(base, SC_SLAB)], idx_v)

            def write_copy(c, b):
                return pltpu.make_async_copy(bufs[b], out_hbm.at[pl.ds(base + c * SC_CHUNK, SC_CHUNK)], wsem[b])

            for b in range(nb - 1):
                gather_copy(b, b).start()

            @pl.loop(0, -(-nchunk // nb) * nb, step=nb)
            def _(c0):
                for b in range(nb):
                    c = c0 + b
                    prev = (b - 1) % nb

                    @pl.when(c < nchunk)
                    def _():
                        gather_copy(c, b).wait()
                        write_copy(c, b).start()

                    @pl.when((c >= 1) & (c < nchunk))
                    def _():
                        write_copy(c - 1, prev).wait()

                    @pl.when(c + nb - 1 < nchunk)
                    def _():
                        gather_copy(c + nb - 1, prev).start()

            write_copy(nchunk - 1, (nchunk - 1) % nb).wait()

    return gather(table, idx)


def _experts_dma(eidx, gates, h2, x1, uv, ta, after=()):
    T, dsub, _ = x1.shape
    nsel = eidx.shape[1]
    tb = 2 * TOK
    nsteps = ta // tb
    return pl.pallas_call(
        functools.partial(_expert_kernel, nsel=nsel),
        grid=(nsteps,),
        in_specs=[pl.BlockSpec((tb, nsel), lambda i: (i, 0), memory_space=pltpu.SMEM),
                  pl.BlockSpec((tb, nsel), lambda i: (jnp.minimum(i + 1, nsteps - 1), 0), memory_space=pltpu.SMEM),
                  pl.BlockSpec((tb, dsub, LANES), lambda i: (i, 0, 0)),
                  pl.BlockSpec((tb, dsub, LANES), lambda i: (i, 0, 0)),
                  pl.BlockSpec((tb, nsel), lambda i: (i, 0)),
                  pl.BlockSpec(memory_space=pl.ANY)] + [pl.BlockSpec(memory_space=pl.ANY)] * len(after),
        out_specs=pl.BlockSpec((tb, dsub, LANES), lambda i: (i, 0, 0)),
        out_shape=jax.ShapeDtypeStruct((T, dsub, LANES), F32),
        scratch_shapes=[pltpu.VMEM((2, TOK * nsel, dsub, LANES), jnp.uint32), pltpu.SemaphoreType.DMA((2,)),
                        pltpu.VMEM((TOK, nsel, LANES), F32)],
        compiler_params=_params(1),
        name="peer_experts",
    )(eidx, eidx, h2, x1, gates, uv, *after)


def _experts_staged(rows, gates, h2, x1, out, ta):
    T, dsub, _ = x1.shape
    nsel = gates.shape[1]
    tb = 2 * TOK
    first = ta // tb
    tok = lambda i: (first + i, 0, 0)
    return pl.pallas_call(
        functools.partial(_expert_staged_kernel, nsel=nsel),
        grid=((T - ta) // tb,),
        in_specs=[pl.BlockSpec((tb * nsel, dsub, LANES), lambda i: (i, 0, 0)),
                  pl.BlockSpec((tb, dsub, LANES), tok),
                  pl.BlockSpec((tb, dsub, LANES), tok),
                  pl.BlockSpec((tb, nsel), lambda i: (first + i, 0)),
                  pl.BlockSpec(memory_space=pl.ANY)],
        out_specs=pl.BlockSpec((tb, dsub, LANES), tok),
        out_shape=jax.ShapeDtypeStruct((T, dsub, LANES), F32),
        scratch_shapes=[pltpu.VMEM((TOK, nsel, LANES), F32)],
        input_output_aliases={4: 0},
        compiler_params=_params(1),
        name="peer_experts_staged",
    )(rows, h2, x1, gates, out)


def _staged_tokens(T, use_sc):
    unit = SC_CORES * SC_SUBCORES * SC_SLAB // (PEER_HEADS * PEER_TOPK)
    return T // unit * unit if use_sc else 0


def _peer_gather_start(eidx, uv, use_sc):
    T, nsel = eidx.shape
    ts = _staged_tokens(T, use_sc)
    return _sc_gather(uv, eidx[T - ts:].reshape(ts * nsel)) if ts else None


def _peer_dma_part(eidx, gates, h2, x1, uv, use_sc, after):
    T, D = x1.shape
    dsub = D // LANES
    ta = T - _staged_tokens(T, use_sc)
    h3, x3 = h2.reshape(T, dsub, LANES), x1.reshape(T, dsub, LANES)
    return _experts_dma(eidx, gates, h3, x3, uv, ta, after) if ta else jnp.zeros_like(x3)


def _peer_finish(rows, out, gates, h2, x1, use_sc):
    T, D = x1.shape
    dsub = D // LANES
    if rows is not None:
        out = _experts_staged(rows, gates, h2.reshape(T, dsub, LANES), x1.reshape(T, dsub, LANES), out,
                              T - _staged_tokens(T, use_sc))
    return out.reshape(T, D)


def _ple_kernel(x_ref, p_ref, g_ref, wg_ref, wp_ref, o_ref):
    x = x_ref[...]
    h = x * lax.rsqrt(jnp.mean(x * x, axis=-1, keepdims=True) + EPS) * g_ref[...]
    gate = jax.nn.sigmoid(_dot(h.astype(BF16), wg_ref[...]))
    o_ref[...] = x + gate * _dot(p_ref[...].astype(BF16), wp_ref[...])


def _ple(x2, p2, ple_g, w_gate, w_proj, tm=512):
    T, D = x2.shape
    pd = p2.shape[1]
    return pl.pallas_call(
        _ple_kernel,
        grid=(T // tm,),
        in_specs=[pl.BlockSpec((tm, D), lambda i: (i, 0)), pl.BlockSpec((tm, pd), lambda i: (i, 0)),
                  _single((1, D), lambda i: (0, 0)), _single((D, D), lambda i: (0, 0)),
                  _single((pd, D), lambda i: (0, 0))],
        out_specs=pl.BlockSpec((tm, D), lambda i: (i, 0)),
        out_shape=jax.ShapeDtypeStruct((T, D), F32),
        compiler_params=_params(1),
        name="ple",
    )(x2, p2, ple_g.reshape(1, D), w_gate.astype(BF16), w_proj.astype(BF16))


def kernel(x, p, rel_bias, attn_norm_g, w_in, b_gate, q_norm_g, k_norm_g, w_att_out, conv_w, conv_b, conv_ln_g,
           conv_ln_b, w_conv_out, w_out, ffn_norm_g, w_peer_q, peer_sub_keys, peer_u, peer_v, ple_norm_g,
           w_ple_gate, w_ple_proj):
    B, S, D = x.shape
    depth = w_in.shape[0]
    assert S % KQUAD == 0 and D % LANES == 0 and S % 512 == 0
    n_near = _num_near_tiles(S)
    bias = _bias_tiles(rel_bias, n_near)
    nsl = 2 if S % (2 * 2048) == 0 else 1
    sl = S // nsl
    xs = [x[b, s * sl:(s + 1) * sl] for b in range(B) for s in range(nsl)]
    units = [(b, s) for b in range(B) for s in range(nsl)]
    for i in range(depth):
        uv = _pack_uv(peer_u[i], peer_v[i])

        def finish(pending, i=i):
            rows, part, gates, h2, x1, (b, s), use_sc = pending
            x2 = _peer_finish(rows, part, gates, h2, x1, use_sc)
            return _ple(x2, p[i, b, s * sl:(s + 1) * sl], ple_norm_g[i], w_ple_gate[i], w_ple_proj[i])

        pending, outs, sc_rows = None, [], []
        keys, history = [], None
        for n, (b, s) in enumerate(units):
            use_sc = n < len(units) - 1
            q, k, qi, ki, vt, wit, u, gate = _in_proj(xs[n], attn_norm_g[i], w_in[i], b_gate[i], q_norm_g[i],
                                                      k_norm_g[i])
            if s == 0:
                keys, history = [], jnp.zeros((HALO, CONV_CH), F32)
            keys.append((k, ki, vt))
            att = _attention(q, qi, wit, [kp[0] for kp in keys], [kp[1] for kp in keys], [kp[2] for kp in keys],
                             bias, S, n_near)
            x1, h2 = _mix(att, u, history, gate, xs[n], conv_w[i], conv_b[i], conv_ln_g[i], conv_ln_b[i],
                          w_att_out[i], w_conv_out[i], w_out[i], ffn_norm_g[i])
            history = u[sl - HALO:]
            eidx, gates = _peer_route(h2, w_peer_q[i], peer_sub_keys[i])
            rows = _peer_gather_start(eidx, uv, use_sc)
            if pending is not None:
                outs.append(finish(pending))
            part = _peer_dma_part(eidx, gates, h2, x1, uv, use_sc, tuple(sc_rows))
            pending = (rows, part, gates, h2, x1, (b, s), use_sc)
            if rows is not None:
                sc_rows.append(rows)
        outs.append(finish(pending))
        xs = outs
    return jnp.concatenate(xs, axis=0).reshape(B, S, D)
```

```python
import functools
import math

import numpy as np
import jax
import jax.numpy as jnp
from jax import lax
from jax.experimental import pallas as pl
from jax.experimental.pallas import tpu as pltpu
from jax.experimental.pallas import tpu_sc as plsc

CHUNK = 64
ATT_HEADS = 8
ATT_HEAD_DIM = 64
IDX_HEADS = 8
IDX_DIM = 64
TOPK_MAX = 256
REL_BUCKETS = 32
REL_MAX_DIST = 1024
CONV_CH = 512
CONV_WIDTH = 31
N_BRANCH = 2
PEER_HEADS = 8
N_KEYS = 128
PEER_TOPK = 16
EPS = 1e-6

LANES = 128
SUBLANES = 8
VMEM_LIMIT = 56 * 1024 * 1024

QB = 128
KPAIR = 2 * QB
KQUAD = 4 * QB
INT_MIN = -(2 ** 31)
NEG_BIG = -1e30
LOG2E = math.log2(math.e)

F32 = jnp.float32
BF16 = jnp.bfloat16
I32 = jnp.int32


def _nt_dot(a, b, precision=None):
    return lax.dot_general(a, b, (((1,), (1,)), ((), ())), precision=precision,
                           preferred_element_type=F32)


def _dot(a, b):
    return jnp.dot(a, b, preferred_element_type=F32)


def _single(shape, index_map):
    return pl.BlockSpec(shape, index_map)


def _params(n_grid_dims):
    return pltpu.CompilerParams(dimension_semantics=("arbitrary",) * n_grid_dims,
                                vmem_limit_bytes=VMEM_LIMIT)


def _t5_bucket_np(rel):
    half = REL_BUCKETS // 2
    max_exact = half // 2
    ret = np.where(rel > 0, half, 0)
    n = np.abs(rel)
    nf = np.maximum(n, 1).astype(np.float32)
    large = max_exact + (np.log(nf / np.float32(max_exact)) / np.float32(math.log(REL_MAX_DIST / max_exact))
                         * np.float32(half - max_exact)).astype(np.int32)
    large = np.minimum(large, half - 1)
    return ret + np.where(n < max_exact, n, large)


def _num_near_tiles(seq):
    n = np.arange(1, max(seq, 2 * REL_MAX_DIST) + 1)
    b = _t5_bucket_np(-n)
    sat = REL_BUCKETS // 2 - 1
    unsat = np.nonzero(b != sat)[0]
    n_sat = int(n[unsat[-1]]) + 1 if unsat.size else 1
    return -(-(n_sat + QB - 1) // QB)


def _bias_kernel(rb_ref, o_ref, *, n_near):
    d = pl.program_id(0)
    i = lax.broadcasted_iota(I32, (QB, QB), 0)
    j = lax.broadcasted_iota(I32, (QB, QB), 1)
    rel = i - j - d * QB
    rel = jnp.where(d >= n_near, -8 * REL_MAX_DIST, rel)
    half = REL_BUCKETS // 2
    max_exact = half // 2
    ret = jnp.where(rel > 0, half, 0)
    n = jnp.abs(rel)
    nf = jnp.maximum(n, 1).astype(F32)
    large = max_exact + (jnp.log(nf / max_exact) / math.log(REL_MAX_DIST / max_exact)
                         * (half - max_exact)).astype(I32)
    large = jnp.minimum(large, half - 1)
    bucket = ret + jnp.where(n < max_exact, n, large)
    for h in range(ATT_HEADS):
        acc = jnp.zeros((QB, QB), F32)
        for b in range(REL_BUCKETS):
            acc = jnp.where(bucket == b, rb_ref[b, h], acc)
        o_ref[0, h] = acc * LOG2E


def _bias_tiles(rel_bias, n_near):
    return pl.pallas_call(
        functools.partial(_bias_kernel, n_near=n_near),
        grid=(n_near + 1,),
        in_specs=[pl.BlockSpec(memory_space=pltpu.SMEM)],
        out_specs=pl.BlockSpec((1, ATT_HEADS, QB, QB), lambda d: (d, 0, 0, 0)),
        out_shape=jax.ShapeDtypeStruct((n_near + 1, ATT_HEADS, QB, QB), F32),
        compiler_params=_params(1),
        name="bias_tiles",
    )(rel_bias)


def _inproj_kernel(x_ref, g_ref, wa_ref, wvt_ref, wwit_ref, wglu_ref, wgate_ref, bgate_ref, gq_ref, gk_ref,
                   q_ref, k_ref, qi_ref, ki_ref, vt_ref, wit_ref, u_ref, gate_ref, *, tm):
    x = x_ref[...]
    h = x * lax.rsqrt(jnp.mean(x * x, axis=-1, keepdims=True) + EPS) * g_ref[...]
    hb = h.astype(BF16)
    hp = ATT_HEADS * LANES
    ya = _dot(hb, wa_ref[...])
    for h_i in range(ATT_HEADS):
        sl = slice(h_i * LANES, (h_i + 1) * LANES)
        qh = ya[:, h_i * LANES:(h_i + 1) * LANES]
        ms = jnp.sum(qh * qh, axis=-1, keepdims=True) * (1.0 / ATT_HEAD_DIM)
        q_ref[:, sl] = (qh * lax.rsqrt(ms + EPS) * gq_ref[...]).astype(BF16)
        kh = ya[:, hp + h_i * LANES:hp + (h_i + 1) * LANES]
        ms = jnp.sum(kh * kh, axis=-1, keepdims=True) * (1.0 / ATT_HEAD_DIM)
        k_ref[:, sl] = (kh * lax.rsqrt(ms + EPS) * gk_ref[...]).astype(BF16)
    qi_ref[...] = ya[:, 2 * hp:3 * hp].astype(BF16)
    ki_ref[...] = ya[:, 3 * hp:3 * hp + LANES].astype(BF16)
    vt = _nt_dot(wvt_ref[...], hb).astype(BF16)
    for c in range(tm // KPAIR):
        vt_ref[c] = vt[:, c * KPAIR:(c + 1) * KPAIR]
    wit_ref[...] = _nt_dot(wwit_ref[...], hb)
    glu = _dot(hb, wglu_ref[...])
    u_ref[...] = glu[:, :CONV_CH] * jax.nn.sigmoid(glu[:, CONV_CH:])
    gate_ref[...] = jax.nn.sigmoid(_dot(hb, wgate_ref[...]) + bgate_ref[...]).astype(BF16)


def _pad_heads(w, nh, hd):
    d = w.shape[0]
    w3 = w.reshape(d, nh, hd)
    w3 = jnp.pad(w3, ((0, 0), (0, 0), (0, LANES - hd)))
    return w3.reshape(d, nh * LANES)


def _in_proj(x2, attn_g, w_in, b_gate, q_g, k_g, tm=256):
    T, D = x2.shape
    aw = ATT_HEADS * ATT_HEAD_DIM
    iw = IDX_HEADS * IDX_DIM
    o = 0
    wq = w_in[:, o:o + aw]; o += aw
    wk = w_in[:, o:o + aw]; o += aw
    wv = w_in[:, o:o + aw]; o += aw
    wqi = w_in[:, o:o + iw]; o += iw
    wki = w_in[:, o:o + IDX_DIM]; o += IDX_DIM
    wwi = w_in[:, o:o + IDX_HEADS]; o += IDX_HEADS
    wglu = w_in[:, o:o + 2 * CONV_CH]; o += 2 * CONV_CH
    wgate = w_in[:, o:o + N_BRANCH * D]
    wa = jnp.concatenate([_pad_heads(wq, ATT_HEADS, ATT_HEAD_DIM), _pad_heads(wk, ATT_HEADS, ATT_HEAD_DIM),
                          _pad_heads(wqi, IDX_HEADS, IDX_DIM),
                          jnp.pad(wki, ((0, 0), (0, LANES - IDX_DIM)))], axis=1).astype(BF16)
    na = wa.shape[1]
    hp = ATT_HEADS * LANES
    pad_g = lambda g, s: jnp.pad(g * s, (0, LANES - ATT_HEAD_DIM)).reshape(1, LANES)
    gq = pad_g(q_g, ATT_HEAD_DIM ** -0.5 * LOG2E)
    gk = pad_g(k_g, 1.0)
    const = lambda *shape: _single(shape, lambda i: (0,) * len(shape))
    outs = pl.pallas_call(
        functools.partial(_inproj_kernel, tm=tm),
        grid=(T // tm,),
        in_specs=[pl.BlockSpec((tm, D), lambda i: (i, 0)), const(1, D), const(D, na), const(aw, D),
                  const(IDX_HEADS, D), const(D, 2 * CONV_CH), const(D, N_BRANCH * D), const(1, N_BRANCH * D),
                  const(1, LANES), const(1, LANES)],
        out_specs=[pl.BlockSpec((tm, hp), lambda i: (i, 0)), pl.BlockSpec((tm, hp), lambda i: (i, 0)),
                   pl.BlockSpec((tm, hp), lambda i: (i, 0)), pl.BlockSpec((tm, LANES), lambda i: (i, 0)),
                   pl.BlockSpec((tm // KPAIR, aw, KPAIR), lambda i: (i, 0, 0)),
                   pl.BlockSpec((IDX_HEADS, tm), lambda i: (0, i)),
                   pl.BlockSpec((tm, CONV_CH), lambda i: (i, 0)),
                   pl.BlockSpec((tm, N_BRANCH * D), lambda i: (i, 0))],
        out_shape=[jax.ShapeDtypeStruct((T, hp), BF16), jax.ShapeDtypeStruct((T, hp), BF16),
                   jax.ShapeDtypeStruct((T, hp), BF16), jax.ShapeDtypeStruct((T, LANES), BF16),
                   jax.ShapeDtypeStruct((T // KPAIR, aw, KPAIR), BF16),
                   jax.ShapeDtypeStruct((IDX_HEADS, T), F32),
                   jax.ShapeDtypeStruct((T, CONV_CH), F32),
                   jax.ShapeDtypeStruct((T, N_BRANCH * D), BF16)],
        compiler_params=_params(1),
        name="in_proj",
    )(x2, attn_g.reshape(1, D), wa, wv.T.astype(BF16), wwi.T.astype(BF16), wglu.astype(BF16),
      wgate.astype(BF16), b_gate.reshape(1, N_BRANCH * D), gq, gk)
    return outs


def _attn_kernel(q_ref, qi_ref, wit_ref, *rest, seq, n_near, topk, n_parts, qb0):
    k_parts, ki_parts, vt_parts = rest[:n_parts], rest[n_parts:2 * n_parts], rest[2 * n_parts:3 * n_parts]
    (bias_hbm, o_ref, k_ref, ki_ref, vt_ref, bias_ref, load_sem, key_scr, att_scr), scr = (
        rest[3 * n_parts:3 * n_parts + 9], rest[3 * n_parts + 9:])
    qb = qb0 + pl.program_id(0)

    @pl.when(pl.program_id(0) == 0)
    def _():
        loads = [pltpu.make_async_copy(bias_hbm, bias_ref, load_sem.at[3 * n_parts])]
        kp = kq = 0
        for n in range(n_parts):
            npair, nqd = k_parts[n].shape[0], ki_parts[n].shape[0]
            loads += [pltpu.make_async_copy(k_parts[n], k_ref.at[pl.ds(kp, npair)], load_sem.at[3 * n]),
                      pltpu.make_async_copy(ki_parts[n], ki_ref.at[pl.ds(kq, nqd)], load_sem.at[3 * n + 1]),
                      pltpu.make_async_copy(vt_parts[n], vt_ref.at[pl.ds(kp, npair)], load_sem.at[3 * n + 2])]
            kp, kq = kp + npair, kq + nqd
        for c in loads:
            c.start()
        for c in loads:
            c.wait()

    nquad = (qb + 4) // 4
    lane_t = lax.broadcasted_iota(I32, (1, QB), 1) + qb * QB
    qchunk = lane_t // CHUNK
    sub = lax.broadcasted_iota(I32, (KQUAD, QB), 0)
    idx_scale = (IDX_DIM ** -0.5) * (IDX_HEADS ** -0.5)
    wrow = wit_ref[...] * idx_scale

    qi_all = jnp.concatenate([qi_ref[:, h * LANES:(h + 1) * LANES] for h in range(IDX_HEADS)], axis=0)

    def score_quad(j, carry):
        d = _nt_dot(ki_ref[j], qi_all)
        acc = jnp.zeros((KQUAD, QB), F32)
        for h in range(IDX_HEADS):
            acc = acc + jnp.maximum(d[:, h * QB:(h + 1) * QB], 0.0) * wrow[h:h + 1, :]
        bits = pltpu.bitcast(acc, I32)
        skey = bits ^ ((bits >> 31) & 0x7FFFFFFF)
        visible = (sub + j * KQUAD) // CHUNK <= qchunk
        key_scr[j] = jnp.where(visible, skey, INT_MIN)
        return carry

    lax.fori_loop(0, nquad, score_quad, 0)

    nvis = (qchunk + 1) * CHUNK
    kk = jnp.minimum(topk, nvis)

    def count(pred):
        def body(j, acc8):
            m = jnp.where(pred(key_scr[j], j), 1, 0)
            return acc8 + m.reshape(KQUAD // SUBLANES, SUBLANES, QB).sum(axis=0)
        return lax.fori_loop(0, nquad, body, jnp.zeros((SUBLANES, QB), I32)).sum(axis=0, keepdims=True)

    def bit_step(i, carry):
        ans, cnt = carry
        cand = ans + lax.shift_left(jnp.int32(1), 31 - i)
        c = count(lambda blk, j: blk >= cand)
        take = c >= kk
        return jnp.where(take, cand, ans), jnp.where(take, c, cnt)

    thr, cnt = lax.fori_loop(0, 32, bit_step,
                             (jnp.full((1, QB), INT_MIN, I32), jnp.full((1, QB), 0, I32) + nquad * KQUAD))

    @pl.when(jnp.max(cnt - kk) > 0)
    def _():
        n_gt = count(lambda blk, j: blk > thr)
        need = kk - n_gt

        def idx_step(i, jbound):
            cand = jbound + lax.shift_left(jnp.int32(1), (seq.bit_length() - 1) - i)
            c = count(lambda blk, j: (blk == thr) & (sub + j * KQUAD < cand))
            return jnp.where(c <= need, cand, jbound)

        jbound = lax.fori_loop(0, seq.bit_length(), idx_step, jnp.zeros((1, QB), I32))

        def drop(j, carry):
            blk = key_scr[j]
            key_scr[j] = jnp.where((blk == thr) & (sub + j * KQUAD >= jbound), INT_MIN, blk)
            return carry

        lax.fori_loop(0, nquad, drop, 0)

    acc_refs, s_even, s_odd = scr[:ATT_HEADS], scr[ATT_HEADS], scr[ATT_HEADS + 1]
    for acc_ref in acc_refs:
        acc_ref[...] = jnp.zeros(acc_ref.shape, F32)
    head_row = lax.broadcasted_iota(I32, (ATT_HEADS, QB), 0)
    last_pair = k_ref.shape[0] - 1

    def qk(jp, s_ref):
        jc = jnp.minimum(jp, last_pair)
        for h in range(ATT_HEADS):
            s_ref[h] = _nt_dot(k_ref[jc, :, h * LANES:(h + 1) * LANES], q_ref[:, h * LANES:(h + 1) * LANES])

    def softmax_pv(jq, half, s_ref, carry):
        m_all, l_all = carry
        jp = 2 * jq + half
        keys = key_scr[jq, half * KPAIR:(half + 1) * KPAIR, :]
        masked = jnp.where(keys >= thr, 0.0, -jnp.inf)
        tile0 = jnp.clip(qb - 2 * jp, 0, n_near)
        tile1 = jnp.clip(qb - 2 * jp - 1, 0, n_near)
        for h in range(ATT_HEADS):
            rows = slice(h * ATT_HEAD_DIM, (h + 1) * ATT_HEAD_DIM)
            bias = jnp.concatenate([bias_ref[tile0, h], bias_ref[tile1, h]], axis=0)
            s = s_ref[h] + bias + masked
            m = m_all[h:h + 1, :]
            m_new = jnp.maximum(m, jnp.max(s, axis=0, keepdims=True))
            p = jnp.exp2(s - m_new)
            alpha = jnp.exp2(m - m_new)
            l_new = alpha * l_all[h:h + 1, :] + jnp.sum(p, axis=0, keepdims=True)
            m_all = jnp.where(head_row == h, m_new, m_all)
            l_all = jnp.where(head_row == h, l_new, l_all)
            acc_refs[h][...] = alpha * acc_refs[h][...] + _dot(vt_ref[jp, rows, :], p.astype(BF16))
        return m_all, l_all

    def att_quad(jq, carry):
        qk(2 * jq + 1, s_odd)
        carry = softmax_pv(jq, 0, s_even, carry)
        qk(2 * jq + 2, s_even)
        return softmax_pv(jq, 1, s_odd, carry)

    qk(0, s_even)
    _, l_all = lax.fori_loop(0, nquad, att_quad,
                             (jnp.full((ATT_HEADS, QB), NEG_BIG, F32), jnp.zeros((ATT_HEADS, QB), F32)))
    for h in range(ATT_HEADS):
        rows = slice(h * ATT_HEAD_DIM, (h + 1) * ATT_HEAD_DIM)
        att_scr[rows, :] = acc_refs[h][...] / l_all[h:h + 1, :]
    o_ref[...] = att_scr[...].T.astype(o_ref.dtype)


def _attention(q, qi, wit, k_parts, ki_parts, vt_parts, bias, S, n_near):
    Tq = q.shape[0]
    hp = ATT_HEADS * LANES
    aw = ATT_HEADS * ATT_HEAD_DIM
    topk = min(TOPK_MAX, S // 4)
    n_parts = len(k_parts)
    k4 = [k.reshape(-1, KPAIR, hp) for k in k_parts]
    ki4 = [ki.reshape(-1, KQUAD, LANES) for ki in ki_parts]
    npr = sum(k.shape[0] for k in k4)
    nqd = sum(ki.shape[0] for ki in ki4)
    assert npr * KPAIR == nqd * KQUAD and vt_parts[0].shape[1:] == (aw, KPAIR)
    qb0 = (npr * KPAIR - Tq) // QB
    return pl.pallas_call(
        functools.partial(_attn_kernel, seq=S, n_near=n_near, topk=topk, n_parts=n_parts, qb0=qb0),
        grid=(Tq // QB,),
        in_specs=[pl.BlockSpec((QB, hp), lambda i: (i, 0)),
                  pl.BlockSpec((QB, hp), lambda i: (i, 0)),
                  pl.BlockSpec((IDX_HEADS, QB), lambda i: (0, i))]
        + [pl.BlockSpec(memory_space=pl.ANY)] * (3 * n_parts + 1),
        out_specs=pl.BlockSpec((QB, aw), lambda i: (i, 0)),
        out_shape=jax.ShapeDtypeStruct((Tq, aw), BF16),
        scratch_shapes=[pltpu.VMEM((npr, KPAIR, hp), BF16), pltpu.VMEM((nqd, KQUAD, LANES), BF16),
                        pltpu.VMEM((npr, aw, KPAIR), BF16), pltpu.VMEM((n_near + 1, ATT_HEADS, QB, QB), F32),
                        pltpu.SemaphoreType.DMA((3 * n_parts + 1,)),
                        pltpu.VMEM((nqd, KQUAD, QB), I32), pltpu.VMEM((aw, QB), F32)]
        + [pltpu.VMEM((ATT_HEAD_DIM, QB), F32) for _ in range(ATT_HEADS)]
        + [pltpu.VMEM((ATT_HEADS, KPAIR, QB), F32) for _ in range(2)],
        compiler_params=_params(1),
        name="dsa_attention",
    )(q, qi, wit, *k4, *ki4, *vt_parts, bias)


HALO = 32


def _mix_kernel(att_ref, u_ref, halo_ref, hist_ref, gate_ref, x_ref, cw_ref, cb_ref, lng_ref, lnb_ref, wao_ref,
                wco_ref, wout_ref, gffn_ref, x1_ref, h2_ref, ext_scr, *, tm):
    i = pl.program_id(0)
    ext_scr[0:HALO, :] = jnp.where(i == 0, hist_ref[...], halo_ref[...])
    ext_scr[HALO:HALO + tm, :] = u_ref[...]
    y = jnp.zeros((tm, CONV_CH), F32)
    for j in range(CONV_WIDTH):
        y = y + cw_ref[j:j + 1, :] * ext_scr[pl.ds(HALO - (CONV_WIDTH - 1) + j, tm), :]
    y = y + cb_ref[...]
    mu = jnp.mean(y, axis=-1, keepdims=True)
    yc = y - mu
    yn = yc * lax.rsqrt(jnp.mean(yc * yc, axis=-1, keepdims=True) + EPS) * lng_ref[...] + lnb_ref[...]
    z = yn * jax.nn.sigmoid(yn)
    y_conv = _dot(z.astype(BF16), wco_ref[...])
    y_att = _dot(att_ref[...], wao_ref[...])
    d = y_att.shape[1]
    g = gate_ref[...]
    mixed = g[:, :d].astype(F32) * y_att + g[:, d:].astype(F32) * y_conv
    x1 = x_ref[...] + _dot(mixed.astype(BF16), wout_ref[...])
    x1_ref[...] = x1
    h2_ref[...] = x1 * lax.rsqrt(jnp.mean(x1 * x1, axis=-1, keepdims=True) + EPS) * gffn_ref[...]


def _mix(att, u, history, gate, x2, conv_w, conv_b, ln_g, ln_b, w_att_out, w_conv_out, w_out, ffn_g, tm=256):
    T, D = x2.shape
    aw = att.shape[1]
    const = lambda *shape: _single(shape, lambda i: (0,) * len(shape))
    hb = tm // HALO
    return pl.pallas_call(
        functools.partial(_mix_kernel, tm=tm),
        grid=(T // tm,),
        in_specs=[pl.BlockSpec((tm, aw), lambda i: (i, 0)),
                  pl.BlockSpec((tm, CONV_CH), lambda i: (i, 0)),
                  pl.BlockSpec((HALO, CONV_CH), lambda i: (jnp.maximum(i * hb - 1, 0), 0)),
                  const(HALO, CONV_CH),
                  pl.BlockSpec((tm, N_BRANCH * D), lambda i: (i, 0)),
                  pl.BlockSpec((tm, D), lambda i: (i, 0)),
                  const(CONV_WIDTH, CONV_CH), const(1, CONV_CH), const(1, CONV_CH), const(1, CONV_CH),
                  const(aw, D), const(CONV_CH, D), const(D, D), const(1, D)],
        out_specs=[pl.BlockSpec((tm, D), lambda i: (i, 0)), pl.BlockSpec((tm, D), lambda i: (i, 0))],
        out_shape=[jax.ShapeDtypeStruct((T, D), F32), jax.ShapeDtypeStruct((T, D), F32)],
        scratch_shapes=[pltpu.VMEM((HALO + tm, CONV_CH), F32)],
        compiler_params=_params(1),
        name="mix_out_proj",
    )(att, u, u, history, gate, x2, conv_w.reshape(CONV_WIDTH, CONV_CH), conv_b.reshape(1, CONV_CH),
      ln_g.reshape(1, CONV_CH), ln_b.reshape(1, CONV_CH), w_att_out.astype(BF16), w_conv_out.astype(BF16),
      w_out.astype(BF16), ffn_g.reshape(1, D))


def _top_rows(sc, k, payload=None):
    rows = sc.shape[0]
    iota = lax.broadcasted_iota(I32, sc.shape, 0)
    out_row = lax.broadcasted_iota(I32, (k, sc.shape[1]), 0)
    vals = jnp.zeros((k, sc.shape[1]), F32)
    idxs = jnp.zeros((k, sc.shape[1]), I32)
    for r in range(k):
        m = jnp.max(sc, axis=0, keepdims=True)
        idx = jnp.min(jnp.where(sc == m, iota, rows), axis=0, keepdims=True)
        hit = iota == idx
        rec = idx if payload is None else jnp.max(jnp.where(hit, payload, -1), axis=0, keepdims=True)
        vals = jnp.where(out_row == r, m, vals)
        idxs = jnp.where(out_row == r, rec, idxs)
        sc = jnp.where(hit, -jnp.inf, sc)
    return vals, idxs


def _route_kernel(h2_ref, wq_ref, sk_ref, e_ref, g_ref, qt_scr, et_scr, gt_scr, *, tm):
    half = N_KEYS
    qt_scr[...] = _nt_dot(wq_ref[...], h2_ref[...].astype(BF16)).astype(BF16)

    def head(h, carry):
        tops = []
        for c in range(2):
            row0 = pl.multiple_of((h * 2 + c) * half, half)
            sc = _dot(sk_ref[h * 2 + c], qt_scr[pl.ds(row0, half), :])
            tops.append(_top_rows(sc, PEER_TOPK))
        (a, ia), (b, ib) = tops
        k = PEER_TOPK
        g = SUBLANES
        assert (k // 2) % g == 0 and k % g == 0
        row = lax.broadcasted_iota(I32, (g, a.shape[1]), 0)
        cand_parts, cidx_parts = [], []

        def add(av, iav, bv, ibv, valid_rows):
            s = av + bv
            if valid_rows < g:
                s = jnp.where(row < valid_rows, s, -jnp.inf)
            cand_parts.append(s)
            cidx_parts.append(iav * N_KEYS + ibv)

        for i in range(k):
            nj = k // (i + 1)
            if nj >= g:
                for j0 in range(0, nj, g):
                    add(a[i:i + 1, :], ia[i:i + 1, :], b[j0:j0 + g, :], ib[j0:j0 + g, :], g)
            elif nj > 1:
                add(a[i:i + 1, :], ia[i:i + 1, :], b[0:g, :], ib[0:g, :], nj)
            elif i % g == 0:
                add(a[i:i + g, :], ia[i:i + g, :], b[0:1, :], ib[0:1, :], g)
        cand = jnp.concatenate(cand_parts, axis=0)
        cidx = jnp.concatenate(cidx_parts, axis=0)
        ts, te = _top_rows(cand, PEER_TOPK, payload=cidx)
        ex = jnp.exp(ts - ts[0:1, :])
        gate = ex / jnp.sum(ex, axis=0, keepdims=True)
        r0 = pl.multiple_of(h * PEER_TOPK, PEER_TOPK)
        et_scr[pl.ds(r0, PEER_TOPK), :] = te
        gt_scr[pl.ds(r0, PEER_TOPK), :] = gate
        return carry

    lax.fori_loop(0, PEER_HEADS, head, 0)
    e_ref[...] = et_scr[...].T
    g_ref[...] = gt_scr[...].T


def _peer_route(h2, w_peer_q, sub_keys, tm=256):
    T, D = h2.shape
    nsel = PEER_HEADS * PEER_TOPK
    qd = w_peer_q.shape[1]
    half = sub_keys.shape[-1]
    skb = sub_keys.reshape(PEER_HEADS * 2, N_KEYS, half).astype(BF16)
    return pl.pallas_call(
        functools.partial(_route_kernel, tm=tm),
        grid=(T // tm,),
        in_specs=[pl.BlockSpec((tm, D), lambda i: (i, 0)),
                  _single((qd, D), lambda i: (0, 0)),
                  _single((PEER_HEADS * 2, N_KEYS, half), lambda i: (0, 0, 0))],
        out_specs=[pl.BlockSpec((tm, nsel), lambda i: (i, 0)), pl.BlockSpec((tm, nsel), lambda i: (i, 0))],
        out_shape=[jax.ShapeDtypeStruct((T, nsel), I32), jax.ShapeDtypeStruct((T, nsel), F32)],
        scratch_shapes=[pltpu.VMEM((qd, tm), BF16), pltpu.VMEM((nsel, tm), I32), pltpu.VMEM((nsel, tm), F32)],
        compiler_params=_params(1),
        name="peer_route",
    )(h2, w_peer_q.T.astype(BF16), skb)


TOK = 8


def _gelu_tanh(x):
    return 0.5 * x * (1.0 + jnp.tanh(math.sqrt(2.0 / math.pi) * (x + 0.044715 * (x * x * x))))


def _sublane_sums(ps):
    sub = lax.broadcasted_iota(I32, (SUBLANES, LANES), 0)
    lvl, stride = list(ps), SUBLANES // 2
    while len(lvl) > 1:
        half = len(lvl) // 2
        low = (sub & stride) == 0
        nxt = []
        for n in range(half):
            a, b = lvl[n], lvl[n + half]
            nxt.append(jnp.where(low, a + pltpu.roll(a, SUBLANES - stride, 0), b + pltpu.roll(b, stride, 0)))
        lvl, stride = nxt, stride // 2
    return lvl[0]


def _pack_uv(u, v):
    ne, d = u.shape
    hi = lax.bitcast_convert_type(u.astype(BF16), jnp.uint16).astype(jnp.uint32)
    lo = lax.bitcast_convert_type(v.astype(BF16), jnp.uint16).astype(jnp.uint32)
    return ((hi << 16) | lo).reshape(ne, d // LANES, LANES)


def _u_of(word):
    return pltpu.bitcast(word & jnp.uint32(0xFFFF0000), F32)


def _v_of(word):
    return pltpu.bitcast(word << 16, F32)


def _expert_group(row, h2_ref, x1_ref, gate_ref, o_ref, abc_scr, base, nsel, before_dots=None, before_sum=None):
    dsub = h2_ref.shape[1]
    lane_id = lax.broadcasted_iota(I32, (nsel, LANES), 1)
    st = jnp.zeros((nsel, LANES), F32)
    for t in range(TOK):
        if before_dots is not None:
            before_dots(t)
        xt = h2_ref[base + t]
        qs = []
        for g in range(nsel // SUBLANES):
            ps = [_u_of(row(t, g * SUBLANES + k)) * xt for k in range(SUBLANES)]
            qs.append(_sublane_sums(ps))
        q = jnp.concatenate(qs, axis=0)
        st = jnp.where(lane_id == t, jnp.sum(q, axis=1, keepdims=True), st)
    g8 = gate_ref[base:base + TOK, :]
    gt = jnp.concatenate([g8, jnp.zeros((nsel - TOK, nsel), F32)], axis=0).T
    at = _gelu_tanh(st) * gt

    for t in range(TOK):
        if before_sum is not None:
            before_sum(t)
        abc_scr[t] = jnp.broadcast_to(at[:, t:t + 1], (nsel, LANES))
        accs = [jnp.zeros((dsub, LANES), F32) for _ in range(4)]
        for n in range(nsel):
            accs[n % 4] = accs[n % 4] + abc_scr[t, n:n + 1, :] * _v_of(row(t, n))
        o_ref[base + t] = x1_ref[base + t] + ((accs[0] + accs[1]) + (accs[2] + accs[3]))


def _expert_kernel(idxc_ref, idxn_ref, h2_ref, x1_ref, gate_ref, uv_ref, *rest, nsel):
    o_ref, buf, sem, abc_scr = rest[-4:]
    i = pl.program_id(0)
    nsteps = pl.num_programs(0)
    rows = TOK * nsel

    def issue_token(idx_ref, tok, s, t, n0=0, n1=nsel):
        for n in range(n0, n1):
            pltpu.make_async_copy(uv_ref.at[idx_ref[tok, n]], buf.at[s, t * nsel + n],
                                  sem.at[s]).start(priority=n % 2)

    def wait_slot(s):
        pltpu.make_async_copy(uv_ref.at[pl.ds(0, rows)], buf.at[s], sem.at[s]).wait()

    @pl.when(i == 0)
    def _():
        for t in range(TOK):
            issue_token(idxc_ref, t, 0, t)

    n_early = (3 * nsel) // 4

    for grp in range(2):
        wait_slot(grp)
        nxt_idx, nxt_tok, nxt_slot = (idxc_ref, TOK, 1) if grp == 0 else (idxn_ref, 0, 0)
        _expert_group(lambda t, n, grp=grp: buf[grp, t * nsel + n], h2_ref, x1_ref, gate_ref, o_ref, abc_scr,
                      grp * TOK, nsel,
                      before_dots=lambda t: issue_token(nxt_idx, nxt_tok + t, nxt_slot, t, 0, n_early),
                      before_sum=lambda t: issue_token(nxt_idx, nxt_tok + t, nxt_slot, t, n_early, nsel))

    @pl.when(i == nsteps - 1)
    def _():
        wait_slot(0)


def _expert_staged_kernel(rows_ref, h2_ref, x1_ref, gate_ref, prev_ref, o_ref, abc_scr, *, nsel):
    del prev_ref
    for grp in range(h2_ref.shape[0] // TOK):
        _expert_group(lambda t, n, grp=grp: rows_ref[(grp * TOK + t) * nsel + n], h2_ref, x1_ref, gate_ref, o_ref,
                      abc_scr, grp * TOK, nsel)


SC_CORES = 2
SC_SUBCORES = 16
SC_CHUNK = 16
SC_NBUF = 7
SC_SLAB = 2048


def _sc_gather(table, idx):
    n = idx.shape[0]
    nw = SC_CORES * SC_SUBCORES
    per_w = n // nw
    nb = SC_NBUF
    assert n % nw == 0 and per_w % SC_SLAB == 0 and SC_SLAB % SC_CHUNK == 0
    nslab, nchunk = per_w // SC_SLAB, SC_SLAB // SC_CHUNK
    assert nchunk >= nb
    row_shape = table.shape[1:]
    mesh = plsc.VectorSubcoreMesh(core_axis_name="c", subcore_axis_name="s")

    @functools.partial(
        pl.kernel, mesh=mesh, out_type=jax.ShapeDtypeStruct((n,) + row_shape, table.dtype),
        scratch_types=[pltpu.VMEM((SC_SLAB,), jnp.int32)]
        + [pltpu.VMEM((SC_CHUNK,) + row_shape, table.dtype) for _ in range(nb)]
        + [pltpu.SemaphoreType.DMA for _ in range(2 * nb)])
    def gather(table_hbm, idx_hbm, out_hbm, idx_v, *scr):
        wid = lax.axis_index("s") * SC_CORES + lax.axis_index("c")
        bufs, gsem, wsem = scr[:nb], scr[nb:2 * nb], scr[2 * nb:]

        def gather_copy(c, b):
            return pltpu.make_async_copy(table_hbm.at[idx_v.at[pl.ds(c * SC_CHUNK, SC_CHUNK)]], bufs[b], gsem[b])

        @pl.loop(0, nslab)
        def _(sl):
            base = wid * per_w + sl * SC_SLAB
            pltpu.sync_copy(idx_hbm.at[pl.ds(base, SC_SLAB)], idx_v)

            def write_copy(c, b):
                return pltpu.make_async_copy(bufs[b], out_hbm.at[pl.ds(base + c * SC_CHUNK, SC_CHUNK)], wsem[b])

            for b in range(nb - 1):
                gather_copy(b, b).start()

            @pl.loop(0, -(-nchunk // nb) * nb, step=nb)
            def _(c0):
                for b in range(nb):
                    c = c0 + b
                    prev = (b - 1) % nb

                    @pl.when(c < nchunk)
                    def _():
                        gather_copy(c, b).wait()
                        write_copy(c, b).start()

                    @pl.when((c >= 1) & (c < nchunk))
                    def _():
                        write_copy(c - 1, prev).wait()

                    @pl.when(c + nb - 1 < nchunk)
                    def _():
                        gather_copy(c + nb - 1, prev).start()

            write_copy(nchunk - 1, (nchunk - 1) % nb).wait()

    return gather(table, idx)


def _experts_dma(eidx, gates, h2, x1, uv, ta, after=()):
    T, dsub, _ = x1.shape
    nsel = eidx.shape[1]
    tb = 2 * TOK
    nsteps = ta // tb
    return pl.pallas_call(
        functools.partial(_expert_kernel, nsel=nsel),
        grid=(nsteps,),
        in_specs=[pl.BlockSpec((tb, nsel), lambda i: (i, 0), memory_space=pltpu.SMEM),
                  pl.BlockSpec((tb, nsel), lambda i: (jnp.minimum(i + 1, nsteps - 1), 0), memory_space=pltpu.SMEM),
                  pl.BlockSpec((tb, dsub, LANES), lambda i: (i, 0, 0)),
                  pl.BlockSpec((tb, dsub, LANES), lambda i: (i, 0, 0)),
                  pl.BlockSpec((tb, nsel), lambda i: (i, 0)),
                  pl.BlockSpec(memory_space=pl.ANY)] + [pl.BlockSpec(memory_space=pl.ANY)] * len(after),
        out_specs=pl.BlockSpec((tb, dsub, LANES), lambda i: (i, 0, 0)),
        out_shape=jax.ShapeDtypeStruct((T, dsub, LANES), F32),
        scratch_shapes=[pltpu.VMEM((2, TOK * nsel, dsub, LANES), jnp.uint32), pltpu.SemaphoreType.DMA((2,)),
                        pltpu.VMEM((TOK, nsel, LANES), F32)],
        compiler_params=_params(1),
        name="peer_experts",
    )(eidx, eidx, h2, x1, gates, uv, *after)


def _experts_staged(rows, gates, h2, x1, out, ta):
    T, dsub, _ = x1.shape
    nsel = gates.shape[1]
    tb = 2 * TOK
    first = ta // tb
    tok = lambda i: (first + i, 0, 0)
    return pl.pallas_call(
        functools.partial(_expert_staged_kernel, nsel=nsel),
        grid=((T - ta) // tb,),
        in_specs=[pl.BlockSpec((tb * nsel, dsub, LANES), lambda i: (i, 0, 0)),
                  pl.BlockSpec((tb, dsub, LANES), tok),
                  pl.BlockSpec((tb, dsub, LANES), tok),
                  pl.BlockSpec((tb, nsel), lambda i: (first + i, 0)),
                  pl.BlockSpec(memory_space=pl.ANY)],
        out_specs=pl.BlockSpec((tb, dsub, LANES), tok),
        out_shape=jax.ShapeDtypeStruct((T, dsub, LANES), F32),
        scratch_shapes=[pltpu.VMEM((TOK, nsel, LANES), F32)],
        input_output_aliases={4: 0},
        compiler_params=_params(1),
        name="peer_experts_staged",
    )(rows, h2, x1, gates, out)


def _staged_tokens(T, use_sc):
    unit = SC_CORES * SC_SUBCORES * SC_SLAB // (PEER_HEADS * PEER_TOPK)
    return T // unit * unit if use_sc else 0


def _peer_gather_start(eidx, uv, use_sc):
    T, nsel = eidx.shape
    ts = _staged_tokens(T, use_sc)
    return _sc_gather(uv, eidx[T - ts:].reshape(ts * nsel)) if ts else None


def _peer_dma_part(eidx, gates, h2, x1, uv, use_sc, after):
    T, D = x1.shape
    dsub = D // LANES
    ta = T - _staged_tokens(T, use_sc)
    h3, x3 = h2.reshape(T, dsub, LANES), x1.reshape(T, dsub, LANES)
    return _experts_dma(eidx, gates, h3, x3, uv, ta, after) if ta else jnp.zeros_like(x3)


def _peer_finish(rows, out, gates, h2, x1, use_sc):
    T, D = x1.shape
    dsub = D // LANES
    if rows is not None:
        out = _experts_staged(rows, gates, h2.reshape(T, dsub, LANES), x1.reshape(T, dsub, LANES), out,
                              T - _staged_tokens(T, use_sc))
    return out.reshape(T, D)


def _ple_kernel(x_ref, p_ref, g_ref, wg_ref, wp_ref, o_ref):
    x = x_ref[...]
    h = x * lax.rsqrt(jnp.mean(x * x, axis=-1, keepdims=True) + EPS) * g_ref[...]
    gate = jax.nn.sigmoid(_dot(h.astype(BF16), wg_ref[...]))
    o_ref[...] = x + gate * _dot(p_ref[...].astype(BF16), wp_ref[...])


def _ple(x2, p2, ple_g, w_gate, w_proj, tm=512):
    T, D = x2.shape
    pd = p2.shape[1]
    return pl.pallas_call(
        _ple_kernel,
        grid=(T // tm,),
        in_specs=[pl.BlockSpec((tm, D), lambda i: (i, 0)), pl.BlockSpec((tm, pd), lambda i: (i, 0)),
                  _single((1, D), lambda i: (0, 0)), _single((D, D), lambda i: (0, 0)),
                  _single((pd, D), lambda i: (0, 0))],
        out_specs=pl.BlockSpec((tm, D), lambda i: (i, 0)),
        out_shape=jax.ShapeDtypeStruct((T, D), F32),
        compiler_params=_params(1),
        name="ple",
    )(x2, p2, ple_g.reshape(1, D), w_gate.astype(BF16), w_proj.astype(BF16))


def kernel(x, p, rel_bias, attn_norm_g, w_in, b_gate, q_norm_g, k_norm_g, w_att_out, conv_w, conv_b, conv_ln_g,
           conv_ln_b, w_conv_out, w_out, ffn_norm_g, w_peer_q, peer_sub_keys, peer_u, peer_v, ple_norm_g,
           w_ple_gate, w_ple_proj):
    B, S, D = x.shape
    depth = w_in.shape[0]
    assert S % KQUAD == 0 and D % LANES == 0 and S % 512 == 0
    n_near = _num_near_tiles(S)
    bias = _bias_tiles(rel_bias, n_near)
    nsl = 4 if S % (4 * 2048) == 0 else 1
    sl = S // nsl
    xs = [x[b, s * sl:(s + 1) * sl] for b in range(B) for s in range(nsl)]
    units = [(b, s) for b in range(B) for s in range(nsl)]
    for i in range(depth):
        uv = _pack_uv(peer_u[i], peer_v[i])

        def finish(pending, i=i):
            rows, part, gates, h2, x1, (b, s), use_sc = pending
            x2 = _peer_finish(rows, part, gates, h2, x1, use_sc)
            return _ple(x2, p[i, b, s * sl:(s + 1) * sl], ple_norm_g[i], w_ple_gate[i], w_ple_proj[i])

        pending, outs, sc_rows = None, [], []
        keys, history = [], None
        for n, (b, s) in enumerate(units):
            use_sc = n < len(units) - 1
            q, k, qi, ki, vt, wit, u, gate = _in_proj(xs[n], attn_norm_g[i], w_in[i], b_gate[i], q_norm_g[i],
                                                      k_norm_g[i])
            if s == 0:
                keys, history = [], jnp.zeros((HALO, CONV_CH), F32)
            keys.append((k, ki, vt))
            att = _attention(q, qi, wit, [kp[0] for kp in keys], [kp[1] for kp in keys], [kp[2] for kp in keys],
                             bias, S, n_near)
            x1, h2 = _mix(att, u, history, gate, xs[n], conv_w[i], conv_b[i], conv_ln_g[i], conv_ln_b[i],
                          w_att_out[i], w_conv_out[i], w_out[i], ffn_norm_g[i])
            history = u[sl - HALO:]
            eidx, gates = _peer_route(h2, w_peer_q[i], peer_sub_keys[i])
            rows = _peer_gather_start(eidx, uv, use_sc)
            if pending is not None:
                outs.append(finish(pending))
            part = _peer_dma_part(eidx, gates, h2, x1, uv, use_sc, tuple(sc_rows))
            pending = (rows, part, gates, h2, x1, (b, s), use_sc)
            if rows is not None:
                sc_rows.append(rows)
        outs.append(finish(pending))
        xs = outs
    return jnp.concatenate(xs, axis=0).reshape(B, S, D)
```

```python
import functools
import math

import numpy as np
import jax
import jax.numpy as jnp
from jax import lax
from jax.experimental import pallas as pl
from jax.experimental.pallas import tpu as pltpu
from jax.experimental.pallas import tpu_sc as plsc

CHUNK = 64
ATT_HEADS = 8
ATT_HEAD_DIM = 64
IDX_HEADS = 8
IDX_DIM = 64
TOPK_MAX = 256
REL_BUCKETS = 32
REL_MAX_DIST = 1024
CONV_CH = 512
CONV_WIDTH = 31
N_BRANCH = 2
PEER_HEADS = 8
N_KEYS = 128
PEER_TOPK = 16
EPS = 1e-6

LANES = 128
SUBLANES = 8
VMEM_LIMIT = 56 * 1024 * 1024

QB = 128
KPAIR = 2 * QB
KQUAD = 4 * QB
INT_MIN = -(2 ** 31)
NEG_BIG = -1e30
LOG2E = math.log2(math.e)

F32 = jnp.float32
BF16 = jnp.bfloat16
I32 = jnp.int32


def _nt_dot(a, b, precision=None):
    return lax.dot_general(a, b, (((1,), (1,)), ((), ())), precision=precision,
                           preferred_element_type=F32)


def _dot(a, b):
    return jnp.dot(a, b, preferred_element_type=F32)


def _single(shape, index_map):
    return pl.BlockSpec(shape, index_map)


def _params(n_grid_dims):
    return pltpu.CompilerParams(dimension_semantics=("arbitrary",) * n_grid_dims,
                                vmem_limit_bytes=VMEM_LIMIT)


def _t5_bucket_np(rel):
    half = REL_BUCKETS // 2
    max_exact = half // 2
    ret = np.where(rel > 0, half, 0)
    n = np.abs(rel)
    nf = np.maximum(n, 1).astype(np.float32)
    large = max_exact + (np.log(nf / np.float32(max_exact)) / np.float32(math.log(REL_MAX_DIST / max_exact))
                         * np.float32(half - max_exact)).astype(np.int32)
    large = np.minimum(large, half - 1)
    return ret + np.where(n < max_exact, n, large)


def _num_near_tiles(seq):
    n = np.arange(1, max(seq, 2 * REL_MAX_DIST) + 1)
    b = _t5_bucket_np(-n)
    sat = REL_BUCKETS // 2 - 1
    unsat = np.nonzero(b != sat)[0]
    n_sat = int(n[unsat[-1]]) + 1 if unsat.size else 1
    return -(-(n_sat + QB - 1) // QB)


def _bias_kernel(rb_ref, o_ref, *, n_near):
    d = pl.program_id(0)
    i = lax.broadcasted_iota(I32, (QB, QB), 0)
    j = lax.broadcasted_iota(I32, (QB, QB), 1)
    rel = i - j - d * QB
    rel = jnp.where(d >= n_near, -8 * REL_MAX_DIST, rel)
    half = REL_BUCKETS // 2
    max_exact = half // 2
    ret = jnp.where(rel > 0, half, 0)
    n = jnp.abs(rel)
    nf = jnp.maximum(n, 1).astype(F32)
    large = max_exact + (jnp.log(nf / max_exact) / math.log(REL_MAX_DIST / max_exact)
                         * (half - max_exact)).astype(I32)
    large = jnp.minimum(large, half - 1)
    bucket = ret + jnp.where(n < max_exact, n, large)
    for h in range(ATT_HEADS):
        acc = jnp.zeros((QB, QB), F32)
        for b in range(REL_BUCKETS):
            acc = jnp.where(bucket == b, rb_ref[b, h], acc)
        o_ref[0, h] = acc * LOG2E


def _bias_tiles(rel_bias, n_near):
    return pl.pallas_call(
        functools.partial(_bias_kernel, n_near=n_near),
        grid=(n_near + 1,),
        in_specs=[pl.BlockSpec(memory_space=pltpu.SMEM)],
        out_specs=pl.BlockSpec((1, ATT_HEADS, QB, QB), lambda d: (d, 0, 0, 0)),
        out_shape=jax.ShapeDtypeStruct((n_near + 1, ATT_HEADS, QB, QB), F32),
        compiler_params=_params(1),
        name="bias_tiles",
    )(rel_bias)


def _inproj_kernel(x_ref, g_ref, wa_ref, wvt_ref, wwit_ref, wglu_ref, wgate_ref, bgate_ref, gq_ref, gk_ref,
                   q_ref, k_ref, qi_ref, ki_ref, vt_ref, wit_ref, u_ref, gate_ref, *, tm):
    x = x_ref[...]
    h = x * lax.rsqrt(jnp.mean(x * x, axis=-1, keepdims=True) + EPS) * g_ref[...]
    hb = h.astype(BF16)
    hp = ATT_HEADS * LANES
    ya = _dot(hb, wa_ref[...])
    for h_i in range(ATT_HEADS):
        sl = slice(h_i * LANES, (h_i + 1) * LANES)
        qh = ya[:, h_i * LANES:(h_i + 1) * LANES]
        ms = jnp.sum(qh * qh, axis=-1, keepdims=True) * (1.0 / ATT_HEAD_DIM)
        q_ref[:, sl] = (qh * lax.rsqrt(ms + EPS) * gq_ref[...]).astype(BF16)
        kh = ya[:, hp + h_i * LANES:hp + (h_i + 1) * LANES]
        ms = jnp.sum(kh * kh, axis=-1, keepdims=True) * (1.0 / ATT_HEAD_DIM)
        k_ref[:, sl] = (kh * lax.rsqrt(ms + EPS) * gk_ref[...]).astype(BF16)
    qi_ref[...] = ya[:, 2 * hp:3 * hp].astype(BF16)
    ki_ref[...] = ya[:, 3 * hp:3 * hp + LANES].astype(BF16)
    vt = _nt_dot(wvt_ref[...], hb).astype(BF16)
    for c in range(tm // KPAIR):
        vt_ref[c] = vt[:, c * KPAIR:(c + 1) * KPAIR]
    wit_ref[...] = _nt_dot(wwit_ref[...], hb)
    glu = _dot(hb, wglu_ref[...])
    u_ref[...] = glu[:, :CONV_CH] * jax.nn.sigmoid(glu[:, CONV_CH:])
    gate_ref[...] = jax.nn.sigmoid(_dot(hb, wgate_ref[...]) + bgate_ref[...]).astype(BF16)


def _pad_heads(w, nh, hd):
    d = w.shape[0]
    w3 = w.reshape(d, nh, hd)
    w3 = jnp.pad(w3, ((0, 0), (0, 0), (0, LANES - hd)))
    return w3.reshape(d, nh * LANES)


def _in_proj(x2, attn_g, w_in, b_gate, q_g, k_g, tm=256):
    T, D = x2.shape
    aw = ATT_HEADS * ATT_HEAD_DIM
    iw = IDX_HEADS * IDX_DIM
    o = 0
    wq = w_in[:, o:o + aw]; o += aw
    wk = w_in[:, o:o + aw]; o += aw
    wv = w_in[:, o:o + aw]; o += aw
    wqi = w_in[:, o:o + iw]; o += iw
    wki = w_in[:, o:o + IDX_DIM]; o += IDX_DIM
    wwi = w_in[:, o:o + IDX_HEADS]; o += IDX_HEADS
    wglu = w_in[:, o:o + 2 * CONV_CH]; o += 2 * CONV_CH
    wgate = w_in[:, o:o + N_BRANCH * D]
    wa = jnp.concatenate([_pad_heads(wq, ATT_HEADS, ATT_HEAD_DIM), _pad_heads(wk, ATT_HEADS, ATT_HEAD_DIM),
                          _pad_heads(wqi, IDX_HEADS, IDX_DIM),
                          jnp.pad(wki, ((0, 0), (0, LANES - IDX_DIM)))], axis=1).astype(BF16)
    na = wa.shape[1]
    hp = ATT_HEADS * LANES
    pad_g = lambda g, s: jnp.pad(g * s, (0, LANES - ATT_HEAD_DIM)).reshape(1, LANES)
    gq = pad_g(q_g, ATT_HEAD_DIM ** -0.5 * LOG2E)
    gk = pad_g(k_g, 1.0)
    const = lambda *shape: _single(shape, lambda i: (0,) * len(shape))
    outs = pl.pallas_call(
        functools.partial(_inproj_kernel, tm=tm),
        grid=(T // tm,),
        in_specs=[pl.BlockSpec((tm, D), lambda i: (i, 0)), const(1, D), const(D, na), const(aw, D),
                  const(IDX_HEADS, D), const(D, 2 * CONV_CH), const(D, N_BRANCH * D), const(1, N_BRANCH * D),
                  const(1, LANES), const(1, LANES)],
        out_specs=[pl.BlockSpec((tm, hp), lambda i: (i, 0)), pl.BlockSpec((tm, hp), lambda i: (i, 0)),
                   pl.BlockSpec((tm, hp), lambda i: (i, 0)), pl.BlockSpec((tm, LANES), lambda i: (i, 0)),
                   pl.BlockSpec((tm // KPAIR, aw, KPAIR), lambda i: (i, 0, 0)),
                   pl.BlockSpec((IDX_HEADS, tm), lambda i: (0, i)),
                   pl.BlockSpec((tm, CONV_CH), lambda i: (i, 0)),
                   pl.BlockSpec((tm, N_BRANCH * D), lambda i: (i, 0))],
        out_shape=[jax.ShapeDtypeStruct((T, hp), BF16), jax.ShapeDtypeStruct((T, hp), BF16),
                   jax.ShapeDtypeStruct((T, hp), BF16), jax.ShapeDtypeStruct((T, LANES), BF16),
                   jax.ShapeDtypeStruct((T // KPAIR, aw, KPAIR), BF16),
                   jax.ShapeDtypeStruct((IDX_HEADS, T), F32),
                   jax.ShapeDtypeStruct((T, CONV_CH), F32),
                   jax.ShapeDtypeStruct((T, N_BRANCH * D), BF16)],
        compiler_params=_params(1),
        name="in_proj",
    )(x2, attn_g.reshape(1, D), wa, wv.T.astype(BF16), wwi.T.astype(BF16), wglu.astype(BF16),
      wgate.astype(BF16), b_gate.reshape(1, N_BRANCH * D), gq, gk)
    return outs


def _attn_kernel(q_ref, qi_ref, wit_ref, *rest, seq, n_near, topk, n_parts, qb0):
    k_parts, ki_parts, vt_parts = rest[:n_parts], rest[n_parts:2 * n_parts], rest[2 * n_parts:3 * n_parts]
    (bias_hbm, o_ref, k_ref, ki_ref, vt_ref, bias_ref, load_sem, key_scr, att_scr), scr = (
        rest[3 * n_parts:3 * n_parts + 9], rest[3 * n_parts + 9:])
    qb = qb0 + pl.program_id(0)

    @pl.when(pl.program_id(0) == 0)
    def _():
        loads = [pltpu.make_async_copy(bias_hbm, bias_ref, load_sem.at[3 * n_parts])]
        kp = kq = 0
        for n in range(n_parts):
            npair, nqd = k_parts[n].shape[0], ki_parts[n].shape[0]
            loads += [pltpu.make_async_copy(k_parts[n], k_ref.at[pl.ds(kp, npair)], load_sem.at[3 * n]),
                      pltpu.make_async_copy(ki_parts[n], ki_ref.at[pl.ds(kq, nqd)], load_sem.at[3 * n + 1]),
                      pltpu.make_async_copy(vt_parts[n], vt_ref.at[pl.ds(kp, npair)], load_sem.at[3 * n + 2])]
            kp, kq = kp + npair, kq + nqd
        for c in loads:
            c.start()
        for c in loads:
            c.wait()

    nquad = (qb + 4) // 4
    lane_t = lax.broadcasted_iota(I32, (1, QB), 1) + qb * QB
    qchunk = lane_t // CHUNK
    sub = lax.broadcasted_iota(I32, (KQUAD, QB), 0)
    idx_scale = (IDX_DIM ** -0.5) * (IDX_HEADS ** -0.5)
    wrow = wit_ref[...] * idx_scale

    qi_all = jnp.concatenate([qi_ref[:, h * LANES:(h + 1) * LANES] for h in range(IDX_HEADS)], axis=0)

    def score_quad(j, carry, masked=False):
        d = _nt_dot(ki_ref[j], qi_all)
        acc = jnp.zeros((KQUAD, QB), F32)
        for h in range(IDX_HEADS):
            acc = acc + jnp.maximum(d[:, h * QB:(h + 1) * QB], 0.0) * wrow[h:h + 1, :]
        bits = pltpu.bitcast(acc, I32)
        skey = bits ^ ((bits >> 31) & 0x7FFFFFFF)
        if masked:
            skey = jnp.where((sub + j * KQUAD) // CHUNK <= qchunk, skey, INT_MIN)
        key_scr[j] = skey
        return carry

    lax.fori_loop(0, nquad - 1, score_quad, 0)
    score_quad(nquad - 1, 0, masked=True)

    nvis = (qchunk + 1) * CHUNK
    kk = jnp.minimum(topk, nvis)

    def count(pred):
        def body(j, acc8):
            m = jnp.where(pred(key_scr[j], j), 1, 0)
            return acc8 + m.reshape(KQUAD // SUBLANES, SUBLANES, QB).sum(axis=0)
        return lax.fori_loop(0, nquad, body, jnp.zeros((SUBLANES, QB), I32)).sum(axis=0, keepdims=True)

    def bit_step(i, carry):
        ans, cnt = carry
        cand = ans + lax.shift_left(jnp.int32(1), 31 - i)
        c = count(lambda blk, j: blk >= cand)
        take = c >= kk
        return jnp.where(take, cand, ans), jnp.where(take, c, cnt)

    thr, cnt = lax.fori_loop(0, 32, bit_step,
                             (jnp.full((1, QB), INT_MIN, I32), jnp.full((1, QB), 0, I32) + nquad * KQUAD))

    @pl.when(jnp.max(cnt - kk) > 0)
    def _():
        n_gt = count(lambda blk, j: blk > thr)
        need = kk - n_gt

        def idx_step(i, jbound):
            cand = jbound + lax.shift_left(jnp.int32(1), (seq.bit_length() - 1) - i)
            c = count(lambda blk, j: (blk == thr) & (sub + j * KQUAD < cand))
            return jnp.where(c <= need, cand, jbound)

        jbound = lax.fori_loop(0, seq.bit_length(), idx_step, jnp.zeros((1, QB), I32))

        def drop(j, carry):
            blk = key_scr[j]
            key_scr[j] = jnp.where((blk == thr) & (sub + j * KQUAD >= jbound), INT_MIN, blk)
            return carry

        lax.fori_loop(0, nquad, drop, 0)

    acc_refs, s_even, s_odd = scr[:ATT_HEADS], scr[ATT_HEADS], scr[ATT_HEADS + 1]
    for acc_ref in acc_refs:
        acc_ref[...] = jnp.zeros(acc_ref.shape, F32)
    head_row = lax.broadcasted_iota(I32, (ATT_HEADS, QB), 0)
    last_pair = k_ref.shape[0] - 1

    def qk(jp, s_ref):
        jc = jnp.minimum(jp, last_pair)
        for h in range(ATT_HEADS):
            s_ref[h] = _nt_dot(k_ref[jc, :, h * LANES:(h + 1) * LANES], q_ref[:, h * LANES:(h + 1) * LANES])

    def softmax_pv(jq, half, s_ref, carry):
        m_all, l_all = carry
        jp = 2 * jq + half
        keys = key_scr[jq, half * KPAIR:(half + 1) * KPAIR, :]
        masked = jnp.where(keys >= thr, 0.0, -jnp.inf)
        tile0 = jnp.clip(qb - 2 * jp, 0, n_near)
        tile1 = jnp.clip(qb - 2 * jp - 1, 0, n_near)
        for h in range(ATT_HEADS):
            rows = slice(h * ATT_HEAD_DIM, (h + 1) * ATT_HEAD_DIM)
            bias = jnp.concatenate([bias_ref[tile0, h], bias_ref[tile1, h]], axis=0)
            s = s_ref[h] + bias + masked
            m = m_all[h:h + 1, :]
            m_new = jnp.maximum(m, jnp.max(s, axis=0, keepdims=True))
            p = jnp.exp2(s - m_new)
            alpha = jnp.exp2(m - m_new)
            l_new = alpha * l_all[h:h + 1, :] + jnp.sum(p, axis=0, keepdims=True)
            m_all = jnp.where(head_row == h, m_new, m_all)
            l_all = jnp.where(head_row == h, l_new, l_all)
            acc_refs[h][...] = alpha * acc_refs[h][...] + _dot(vt_ref[jp, rows, :], p.astype(BF16))
        return m_all, l_all

    def att_quad(jq, carry):
        qk(2 * jq + 1, s_odd)
        carry = softmax_pv(jq, 0, s_even, carry)
        qk(2 * jq + 2, s_even)
        return softmax_pv(jq, 1, s_odd, carry)

    qk(0, s_even)
    _, l_all = lax.fori_loop(0, nquad, att_quad,
                             (jnp.full((ATT_HEADS, QB), NEG_BIG, F32), jnp.zeros((ATT_HEADS, QB), F32)))
    for h in range(ATT_HEADS):
        rows = slice(h * ATT_HEAD_DIM, (h + 1) * ATT_HEAD_DIM)
        att_scr[rows, :] = acc_refs[h][...] / l_all[h:h + 1, :]
    o_ref[...] = att_scr[...].T.astype(o_ref.dtype)


def _attention(q, qi, wit, k_parts, ki_parts, vt_parts, bias, S, n_near):
    Tq = q.shape[0]
    hp = ATT_HEADS * LANES
    aw = ATT_HEADS * ATT_HEAD_DIM
    topk = min(TOPK_MAX, S // 4)
    n_parts = len(k_parts)
    k4 = [k.reshape(-1, KPAIR, hp) for k in k_parts]
    ki4 = [ki.reshape(-1, KQUAD, LANES) for ki in ki_parts]
    npr = sum(k.shape[0] for k in k4)
    nqd = sum(ki.shape[0] for ki in ki4)
    assert npr * KPAIR == nqd * KQUAD and vt_parts[0].shape[1:] == (aw, KPAIR)
    qb0 = (npr * KPAIR - Tq) // QB
    return pl.pallas_call(
        functools.partial(_attn_kernel, seq=S, n_near=n_near, topk=topk, n_parts=n_parts, qb0=qb0),
        grid=(Tq // QB,),
        in_specs=[pl.BlockSpec((QB, hp), lambda i: (i, 0)),
                  pl.BlockSpec((QB, hp), lambda i: (i, 0)),
                  pl.BlockSpec((IDX_HEADS, QB), lambda i: (0, i))]
        + [pl.BlockSpec(memory_space=pl.ANY)] * (3 * n_parts + 1),
        out_specs=pl.BlockSpec((QB, aw), lambda i: (i, 0)),
        out_shape=jax.ShapeDtypeStruct((Tq, aw), BF16),
        scratch_shapes=[pltpu.VMEM((npr, KPAIR, hp), BF16), pltpu.VMEM((nqd, KQUAD, LANES), BF16),
                        pltpu.VMEM((npr, aw, KPAIR), BF16), pltpu.VMEM((n_near + 1, ATT_HEADS, QB, QB), F32),
                        pltpu.SemaphoreType.DMA((3 * n_parts + 1,)),
                        pltpu.VMEM((nqd, KQUAD, QB), I32), pltpu.VMEM((aw, QB), F32)]
        + [pltpu.VMEM((ATT_HEAD_DIM, QB), F32) for _ in range(ATT_HEADS)]
        + [pltpu.VMEM((ATT_HEADS, KPAIR, QB), F32) for _ in range(2)],
        compiler_params=_params(1),
        name="dsa_attention",
    )(q, qi, wit, *k4, *ki4, *vt_parts, bias)


HALO = 32


def _mix_kernel(att_ref, u_ref, halo_ref, hist_ref, gate_ref, x_ref, cw_ref, cb_ref, lng_ref, lnb_ref, wao_ref,
                wco_ref, wout_ref, gffn_ref, x1_ref, h2_ref, ext_scr, *, tm):
    i = pl.program_id(0)
    ext_scr[0:HALO, :] = jnp.where(i == 0, hist_ref[...], halo_ref[...])
    ext_scr[HALO:HALO + tm, :] = u_ref[...]
    y = jnp.zeros((tm, CONV_CH), F32)
    for j in range(CONV_WIDTH):
        y = y + cw_ref[j:j + 1, :] * ext_scr[pl.ds(HALO - (CONV_WIDTH - 1) + j, tm), :]
    y = y + cb_ref[...]
    mu = jnp.mean(y, axis=-1, keepdims=True)
    yc = y - mu
    yn = yc * lax.rsqrt(jnp.mean(yc * yc, axis=-1, keepdims=True) + EPS) * lng_ref[...] + lnb_ref[...]
    z = yn * jax.nn.sigmoid(yn)
    y_conv = _dot(z.astype(BF16), wco_ref[...])
    y_att = _dot(att_ref[...], wao_ref[...])
    d = y_att.shape[1]
    g = gate_ref[...]
    mixed = g[:, :d].astype(F32) * y_att + g[:, d:].astype(F32) * y_conv
    x1 = x_ref[...] + _dot(mixed.astype(BF16), wout_ref[...])
    x1_ref[...] = x1
    h2_ref[...] = x1 * lax.rsqrt(jnp.mean(x1 * x1, axis=-1, keepdims=True) + EPS) * gffn_ref[...]


def _mix(att, u, history, gate, x2, conv_w, conv_b, ln_g, ln_b, w_att_out, w_conv_out, w_out, ffn_g, tm=256):
    T, D = x2.shape
    aw = att.shape[1]
    const = lambda *shape: _single(shape, lambda i: (0,) * len(shape))
    hb = tm // HALO
    return pl.pallas_call(
        functools.partial(_mix_kernel, tm=tm),
        grid=(T // tm,),
        in_specs=[pl.BlockSpec((tm, aw), lambda i: (i, 0)),
                  pl.BlockSpec((tm, CONV_CH), lambda i: (i, 0)),
                  pl.BlockSpec((HALO, CONV_CH), lambda i: (jnp.maximum(i * hb - 1, 0), 0)),
                  const(HALO, CONV_CH),
                  pl.BlockSpec((tm, N_BRANCH * D), lambda i: (i, 0)),
                  pl.BlockSpec((tm, D), lambda i: (i, 0)),
                  const(CONV_WIDTH, CONV_CH), const(1, CONV_CH), const(1, CONV_CH), const(1, CONV_CH),
                  const(aw, D), const(CONV_CH, D), const(D, D), const(1, D)],
        out_specs=[pl.BlockSpec((tm, D), lambda i: (i, 0)), pl.BlockSpec((tm, D), lambda i: (i, 0))],
        out_shape=[jax.ShapeDtypeStruct((T, D), F32), jax.ShapeDtypeStruct((T, D), F32)],
        scratch_shapes=[pltpu.VMEM((HALO + tm, CONV_CH), F32)],
        compiler_params=_params(1),
        name="mix_out_proj",
    )(att, u, u, history, gate, x2, conv_w.reshape(CONV_WIDTH, CONV_CH), conv_b.reshape(1, CONV_CH),
      ln_g.reshape(1, CONV_CH), ln_b.reshape(1, CONV_CH), w_att_out.astype(BF16), w_conv_out.astype(BF16),
      w_out.astype(BF16), ffn_g.reshape(1, D))


def _top_rows(sc, k, payload=None):
    rows = sc.shape[0]
    iota = lax.broadcasted_iota(I32, sc.shape, 0)
    out_row = lax.broadcasted_iota(I32, (k, sc.shape[1]), 0)
    vals = jnp.zeros((k, sc.shape[1]), F32)
    idxs = jnp.zeros((k, sc.shape[1]), I32)
    for r in range(k):
        m = jnp.max(sc, axis=0, keepdims=True)
        idx = jnp.min(jnp.where(sc == m, iota, rows), axis=0, keepdims=True)
        hit = iota == idx
        rec = idx if payload is None else jnp.max(jnp.where(hit, payload, -1), axis=0, keepdims=True)
        vals = jnp.where(out_row == r, m, vals)
        idxs = jnp.where(out_row == r, rec, idxs)
        sc = jnp.where(hit, -jnp.inf, sc)
    return vals, idxs


def _route_kernel(h2_ref, wq_ref, sk_ref, e_ref, g_ref, qt_scr, et_scr, gt_scr, *, tm):
    half = N_KEYS
    qt_scr[...] = _nt_dot(wq_ref[...], h2_ref[...].astype(BF16)).astype(BF16)

    def head(h, carry):
        tops = []
        for c in range(2):
            row0 = pl.multiple_of((h * 2 + c) * half, half)
            sc = _dot(sk_ref[h * 2 + c], qt_scr[pl.ds(row0, half), :])
            tops.append(_top_rows(sc, PEER_TOPK))
        (a, ia), (b, ib) = tops
        k = PEER_TOPK
        g = SUBLANES
        assert (k // 2) % g == 0 and k % g == 0
        row = lax.broadcasted_iota(I32, (g, a.shape[1]), 0)
        cand_parts, cidx_parts = [], []

        def add(av, iav, bv, ibv, valid_rows):
            s = av + bv
            if valid_rows < g:
                s = jnp.where(row < valid_rows, s, -jnp.inf)
            cand_parts.append(s)
            cidx_parts.append(iav * N_KEYS + ibv)

        for i in range(k):
            nj = k // (i + 1)
            if nj >= g:
                for j0 in range(0, nj, g):
                    add(a[i:i + 1, :], ia[i:i + 1, :], b[j0:j0 + g, :], ib[j0:j0 + g, :], g)
            elif nj > 1:
                add(a[i:i + 1, :], ia[i:i + 1, :], b[0:g, :], ib[0:g, :], nj)
            elif i % g == 0:
                add(a[i:i + g, :], ia[i:i + g, :], b[0:1, :], ib[0:1, :], g)
        cand = jnp.concatenate(cand_parts, axis=0)
        cidx = jnp.concatenate(cidx_parts, axis=0)
        ts, te = _top_rows(cand, PEER_TOPK, payload=cidx)
        ex = jnp.exp(ts - ts[0:1, :])
        gate = ex / jnp.sum(ex, axis=0, keepdims=True)
        r0 = pl.multiple_of(h * PEER_TOPK, PEER_TOPK)
        et_scr[pl.ds(r0, PEER_TOPK), :] = te
        gt_scr[pl.ds(r0, PEER_TOPK), :] = gate
        return carry

    lax.fori_loop(0, PEER_HEADS, head, 0)
    e_ref[...] = et_scr[...].T
    g_ref[...] = gt_scr[...].T


def _peer_route(h2, w_peer_q, sub_keys, tm=256):
    T, D = h2.shape
    nsel = PEER_HEADS * PEER_TOPK
    qd = w_peer_q.shape[1]
    half = sub_keys.shape[-1]
    skb = sub_keys.reshape(PEER_HEADS * 2, N_KEYS, half).astype(BF16)
    return pl.pallas_call(
        functools.partial(_route_kernel, tm=tm),
        grid=(T // tm,),
        in_specs=[pl.BlockSpec((tm, D), lambda i: (i, 0)),
                  _single((qd, D), lambda i: (0, 0)),
                  _single((PEER_HEADS * 2, N_KEYS, half), lambda i: (0, 0, 0))],
        out_specs=[pl.BlockSpec((tm, nsel), lambda i: (i, 0)), pl.BlockSpec((tm, nsel), lambda i: (i, 0))],
        out_shape=[jax.ShapeDtypeStruct((T, nsel), I32), jax.ShapeDtypeStruct((T, nsel), F32)],
        scratch_shapes=[pltpu.VMEM((qd, tm), BF16), pltpu.VMEM((nsel, tm), I32), pltpu.VMEM((nsel, tm), F32)],
        compiler_params=_params(1),
        name="peer_route",
    )(h2, w_peer_q.T.astype(BF16), skb)


TOK = 8


def _gelu_tanh(x):
    return 0.5 * x * (1.0 + jnp.tanh(math.sqrt(2.0 / math.pi) * (x + 0.044715 * (x * x * x))))


def _sublane_sums(ps):
    sub = lax.broadcasted_iota(I32, (SUBLANES, LANES), 0)
    lvl, stride = list(ps), SUBLANES // 2
    while len(lvl) > 1:
        half = len(lvl) // 2
        low = (sub & stride) == 0
        nxt = []
        for n in range(half):
            a, b = lvl[n], lvl[n + half]
            nxt.append(jnp.where(low, a + pltpu.roll(a, SUBLANES - stride, 0), b + pltpu.roll(b, stride, 0)))
        lvl, stride = nxt, stride // 2
    return lvl[0]


def _pack_uv(u, v):
    ne, d = u.shape
    hi = lax.bitcast_convert_type(u.astype(BF16), jnp.uint16).astype(jnp.uint32)
    lo = lax.bitcast_convert_type(v.astype(BF16), jnp.uint16).astype(jnp.uint32)
    return ((hi << 16) | lo).reshape(ne, d // LANES, LANES)


def _u_of(word):
    return pltpu.bitcast(word & jnp.uint32(0xFFFF0000), F32)


def _v_of(word):
    return pltpu.bitcast(word << 16, F32)


def _expert_group(row, h2_ref, x1_ref, gate_ref, o_ref, abc_scr, base, nsel, before_dots=None, before_sum=None):
    dsub = h2_ref.shape[1]
    lane_id = lax.broadcasted_iota(I32, (nsel, LANES), 1)
    st = jnp.zeros((nsel, LANES), F32)
    for t in range(TOK):
        if before_dots is not None:
            before_dots(t)
        xt = h2_ref[base + t]
        qs = []
        for g in range(nsel // SUBLANES):
            ps = [_u_of(row(t, g * SUBLANES + k)) * xt for k in range(SUBLANES)]
            qs.append(_sublane_sums(ps))
        q = jnp.concatenate(qs, axis=0)
        st = jnp.where(lane_id == t, jnp.sum(q, axis=1, keepdims=True), st)
    g8 = gate_ref[base:base + TOK, :]
    gt = jnp.concatenate([g8, jnp.zeros((nsel - TOK, nsel), F32)], axis=0).T
    at = _gelu_tanh(st) * gt

    for t in range(TOK):
        if before_sum is not None:
            before_sum(t)
        abc_scr[t] = jnp.broadcast_to(at[:, t:t + 1], (nsel, LANES))
        accs = [jnp.zeros((dsub, LANES), F32) for _ in range(4)]
        for n in range(nsel):
            accs[n % 4] = accs[n % 4] + abc_scr[t, n:n + 1, :] * _v_of(row(t, n))
        o_ref[base + t] = x1_ref[base + t] + ((accs[0] + accs[1]) + (accs[2] + accs[3]))


def _expert_kernel(idxc_ref, idxn_ref, h2_ref, x1_ref, gate_ref, uv_ref, *rest, nsel):
    o_ref, buf, sem, abc_scr = rest[-4:]
    i = pl.program_id(0)
    nsteps = pl.num_programs(0)
    rows = TOK * nsel

    def issue_token(idx_ref, tok, s, t, n0=0, n1=nsel):
        for n in range(n0, n1):
            pltpu.make_async_copy(uv_ref.at[idx_ref[tok, n]], buf.at[s, t * nsel + n],
                                  sem.at[s]).start(priority=n % 2)

    def wait_slot(s):
        pltpu.make_async_copy(uv_ref.at[pl.ds(0, rows)], buf.at[s], sem.at[s]).wait()

    @pl.when(i == 0)
    def _():
        for t in range(TOK):
            issue_token(idxc_ref, t, 0, t)

    n_early = (3 * nsel) // 4

    for grp in range(2):
        wait_slot(grp)
        nxt_idx, nxt_tok, nxt_slot = (idxc_ref, TOK, 1) if grp == 0 else (idxn_ref, 0, 0)
        _expert_group(lambda t, n, grp=grp: buf[grp, t * nsel + n], h2_ref, x1_ref, gate_ref, o_ref, abc_scr,
                      grp * TOK, nsel,
                      before_dots=lambda t: issue_token(nxt_idx, nxt_tok + t, nxt_slot, t, 0, n_early),
                      before_sum=lambda t: issue_token(nxt_idx, nxt_tok + t, nxt_slot, t, n_early, nsel))

    @pl.when(i == nsteps - 1)
    def _():
        wait_slot(0)


def _expert_staged_kernel(rows_ref, h2_ref, x1_ref, gate_ref, prev_ref, o_ref, abc_scr, *, nsel):
    del prev_ref
    for grp in range(h2_ref.shape[0] // TOK):
        _expert_group(lambda t, n, grp=grp: rows_ref[(grp * TOK + t) * nsel + n], h2_ref, x1_ref, gate_ref, o_ref,
                      abc_scr, grp * TOK, nsel)


SC_CORES = 2
SC_SUBCORES = 16
SC_CHUNK = 16
SC_NBUF = 7
SC_SLAB = 2048


def _sc_gather(table, idx):
    n = idx.shape[0]
    nw = SC_CORES * SC_SUBCORES
    per_w = n // nw
    nb = SC_NBUF
    assert n % nw == 0 and per_w % SC_SLAB == 0 and SC_SLAB % SC_CHUNK == 0
    nslab, nchunk = per_w // SC_SLAB, SC_SLAB // SC_CHUNK
    assert nchunk >= nb
    row_shape = table.shape[1:]
    mesh = plsc.VectorSubcoreMesh(core_axis_name="c", subcore_axis_name="s")

    @functools.partial(
        pl.kernel, mesh=mesh, out_type=jax.ShapeDtypeStruct((n,) + row_shape, table.dtype),
        scratch_types=[pltpu.VMEM((SC_SLAB,), jnp.int32)]
        + [pltpu.VMEM((SC_CHUNK,) + row_shape, table.dtype) for _ in range(nb)]
        + [pltpu.SemaphoreType.DMA for _ in range(2 * nb)])
    def gather(table_hbm, idx_hbm, out_hbm, idx_v, *scr):
        wid = lax.axis_index("s") * SC_CORES + lax.axis_index("c")
        bufs, gsem, wsem = scr[:nb], scr[nb:2 * nb], scr[2 * nb:]

        def gather_copy(c, b):
            return pltpu.make_async_copy(table_hbm.at[idx_v.at[pl.ds(c * SC_CHUNK, SC_CHUNK)]], bufs[b], gsem[b])

        @pl.loop(0, nslab)
        def _(sl):
            base = wid * per_w + sl * SC_SLAB
            pltpu.sync_copy(idx_hbm.at[pl.ds(base, SC_SLAB)], idx_v)

            def write_copy(c, b):
                return pltpu.make_async_copy(bufs[b], out_hbm.at[pl.ds(base + c * SC_CHUNK, SC_CHUNK)], wsem[b])

            for b in range(nb - 1):
                gather_copy(b, b).start()

            @pl.loop(0, -(-nchunk // nb) * nb, step=nb)
            def _(c0):
                for b in range(nb):
                    c = c0 + b
                    prev = (b - 1) % nb

                    @pl.when(c < nchunk)
                    def _():
                        gather_copy(c, b).wait()
                        write_copy(c, b).start()

                    @pl.when((c >= 1) & (c < nchunk))
                    def _():
                        write_copy(c - 1, prev).wait()

                    @pl.when(c + nb - 1 < nchunk)
                    def _():
                        gather_copy(c + nb - 1, prev).start()

            write_copy(nchunk - 1, (nchunk - 1) % nb).wait()

    return gather(table, idx)


def _experts_dma(eidx, gates, h2, x1, uv, ta, after=()):
    T, dsub, _ = x1.shape
    nsel = eidx.shape[1]
    tb = 2 * TOK
    nsteps = ta // tb
    return pl.pallas_call(
        functools.partial(_expert_kernel, nsel=nsel),
        grid=(nsteps,),
        in_specs=[pl.BlockSpec((tb, nsel), lambda i: (i, 0), memory_space=pltpu.SMEM),
                  pl.BlockSpec((tb, nsel), lambda i: (jnp.minimum(i + 1, nsteps - 1), 0), memory_space=pltpu.SMEM),
                  pl.BlockSpec((tb, dsub, LANES), lambda i: (i, 0, 0)),
                  pl.BlockSpec((tb, dsub, LANES), lambda i: (i, 0, 0)),
                  pl.BlockSpec((tb, nsel), lambda i: (i, 0)),
                  pl.BlockSpec(memory_space=pl.ANY)] + [pl.BlockSpec(memory_space=pl.ANY)] * len(after),
        out_specs=pl.BlockSpec((tb, dsub, LANES), lambda i: (i, 0, 0)),
        out_shape=jax.ShapeDtypeStruct((T, dsub, LANES), F32),
        scratch_shapes=[pltpu.VMEM((2, TOK * nsel, dsub, LANES), jnp.uint32), pltpu.SemaphoreType.DMA((2,)),
                        pltpu.VMEM((TOK, nsel, LANES), F32)],
        compiler_params=_params(1),
        name="peer_experts",
    )(eidx, eidx, h2, x1, gates, uv, *after)


def _experts_staged(rows, gates, h2, x1, out, ta):
    T, dsub, _ = x1.shape
    nsel = gates.shape[1]
    tb = 2 * TOK
    first = ta // tb
    tok = lambda i: (first + i, 0, 0)
    return pl.pallas_call(
        functools.partial(_expert_staged_kernel, nsel=nsel),
        grid=((T - ta) // tb,),
        in_specs=[pl.BlockSpec((tb * nsel, dsub, LANES), lambda i: (i, 0, 0)),
                  pl.BlockSpec((tb, dsub, LANES), tok),
                  pl.BlockSpec((tb, dsub, LANES), tok),
                  pl.BlockSpec((tb, nsel), lambda i: (first + i, 0)),
                  pl.BlockSpec(memory_space=pl.ANY)],
        out_specs=pl.BlockSpec((tb, dsub, LANES), tok),
        out_shape=jax.ShapeDtypeStruct((T, dsub, LANES), F32),
        scratch_shapes=[pltpu.VMEM((TOK, nsel, LANES), F32)],
        input_output_aliases={4: 0},
        compiler_params=_params(1),
        name="peer_experts_staged",
    )(rows, h2, x1, gates, out)


def _staged_tokens(T, use_sc):
    unit = SC_CORES * SC_SUBCORES * SC_SLAB // (PEER_HEADS * PEER_TOPK)
    return int(T * use_sc) // unit * unit


def _peer_gather_start(eidx, uv, use_sc):
    T, nsel = eidx.shape
    ts = _staged_tokens(T, use_sc)
    return _sc_gather(uv, eidx[T - ts:].reshape(ts * nsel)) if ts else None


def _peer_dma_part(eidx, gates, h2, x1, uv, use_sc, after):
    T, D = x1.shape
    dsub = D // LANES
    ta = T - _staged_tokens(T, use_sc)
    h3, x3 = h2.reshape(T, dsub, LANES), x1.reshape(T, dsub, LANES)
    return _experts_dma(eidx, gates, h3, x3, uv, ta, after) if ta else jnp.zeros_like(x3)


def _peer_finish(rows, out, gates, h2, x1, use_sc):
    T, D = x1.shape
    dsub = D // LANES
    if rows is not None:
        out = _experts_staged(rows, gates, h2.reshape(T, dsub, LANES), x1.reshape(T, dsub, LANES), out,
                              T - _staged_tokens(T, use_sc))
    return out.reshape(T, D)


def _ple_kernel(x_ref, p_ref, g_ref, wg_ref, wp_ref, o_ref):
    x = x_ref[...]
    h = x * lax.rsqrt(jnp.mean(x * x, axis=-1, keepdims=True) + EPS) * g_ref[...]
    gate = jax.nn.sigmoid(_dot(h.astype(BF16), wg_ref[...]))
    o_ref[...] = x + gate * _dot(p_ref[...].astype(BF16), wp_ref[...])


def _ple(x2, p2, ple_g, w_gate, w_proj, tm=512):
    T, D = x2.shape
    pd = p2.shape[1]
    return pl.pallas_call(
        _ple_kernel,
        grid=(T // tm,),
        in_specs=[pl.BlockSpec((tm, D), lambda i: (i, 0)), pl.BlockSpec((tm, pd), lambda i: (i, 0)),
                  _single((1, D), lambda i: (0, 0)), _single((D, D), lambda i: (0, 0)),
                  _single((pd, D), lambda i: (0, 0))],
        out_specs=pl.BlockSpec((tm, D), lambda i: (i, 0)),
        out_shape=jax.ShapeDtypeStruct((T, D), F32),
        compiler_params=_params(1),
        name="ple",
    )(x2, p2, ple_g.reshape(1, D), w_gate.astype(BF16), w_proj.astype(BF16))


def kernel(x, p, rel_bias, attn_norm_g, w_in, b_gate, q_norm_g, k_norm_g, w_att_out, conv_w, conv_b, conv_ln_g,
           conv_ln_b, w_conv_out, w_out, ffn_norm_g, w_peer_q, peer_sub_keys, peer_u, peer_v, ple_norm_g,
           w_ple_gate, w_ple_proj):
    B, S, D = x.shape
    depth = w_in.shape[0]
    assert S % KQUAD == 0 and D % LANES == 0 and S % 512 == 0
    n_near = _num_near_tiles(S)
    bias = _bias_tiles(rel_bias, n_near)
    nsl = 2 if S % (2 * 2048) == 0 else 1
    sl = S // nsl
    xs = [x[b, s * sl:(s + 1) * sl] for b in range(B) for s in range(nsl)]
    units = [(b, s) for b in range(B) for s in range(nsl)]
    for i in range(depth):
        uv = _pack_uv(peer_u[i], peer_v[i])

        def finish(pending, i=i):
            rows, part, gates, h2, x1, (b, s), use_sc = pending
            x2 = _peer_finish(rows, part, gates, h2, x1, use_sc)
            return _ple(x2, p[i, b, s * sl:(s + 1) * sl], ple_norm_g[i], w_ple_gate[i], w_ple_proj[i])

        pending, outs, sc_rows = None, [], []
        keys, history = [], None
        for n, (b, s) in enumerate(units):
            use_sc = (1.0, 0.5, 0.0)[max(0, n - (len(units) - 3))]
            q, k, qi, ki, vt, wit, u, gate = _in_proj(xs[n], attn_norm_g[i], w_in[i], b_gate[i], q_norm_g[i],
                                                      k_norm_g[i])
            if s == 0:
                keys, history = [], jnp.zeros((HALO, CONV_CH), F32)
            keys.append((k, ki, vt))
            att = _attention(q, qi, wit, [kp[0] for kp in keys], [kp[1] for kp in keys], [kp[2] for kp in keys],
                             bias, S, n_near)
            x1, h2 = _mix(att, u, history, gate, xs[n], conv_w[i], conv_b[i], conv_ln_g[i], conv_ln_b[i],
                          w_att_out[i], w_conv_out[i], w_out[i], ffn_norm_g[i])
            history = u[sl - HALO:]
            eidx, gates = _peer_route(h2, w_peer_q[i], peer_sub_keys[i])
            rows = _peer_gather_start(eidx, uv, use_sc)
            if pending is not None:
                outs.append(finish(pending))
            if rows is not None:
                sc_rows.append(rows)
            part = _peer_dma_part(eidx, gates, h2, x1, uv, use_sc, tuple(sc_rows))
            pending = (rows, part, gates, h2, x1, (b, s), use_sc)
        outs.append(finish(pending))
        xs = outs
    return jnp.concatenate(xs, axis=0).reshape(B, S, D)
```

```python
import functools
import math

import numpy as np
import jax
import jax.numpy as jnp
from jax import lax
from jax.experimental import pallas as pl
from jax.experimental.pallas import tpu as pltpu
from jax.experimental.pallas import tpu_sc as plsc

CHUNK = 64
ATT_HEADS = 8
ATT_HEAD_DIM = 64
IDX_HEADS = 8
IDX_DIM = 64
TOPK_MAX = 256
REL_BUCKETS = 32
REL_MAX_DIST = 1024
CONV_CH = 512
CONV_WIDTH = 31
N_BRANCH = 2
PEER_HEADS = 8
N_KEYS = 128
PEER_TOPK = 16
EPS = 1e-6

LANES = 128
SUBLANES = 8
VMEM_LIMIT = 56 * 1024 * 1024

QB = 128
KPAIR = 2 * QB
KQUAD = 4 * QB
INT_MIN = -(2 ** 31)
NEG_BIG = -1e30
LOG2E = math.log2(math.e)

F32 = jnp.float32
BF16 = jnp.bfloat16
I32 = jnp.int32


def _nt_dot(a, b, precision=None):
    return lax.dot_general(a, b, (((1,), (1,)), ((), ())), precision=precision,
                           preferred_element_type=F32)


def _dot(a, b):
    return jnp.dot(a, b, preferred_element_type=F32)


def _single(shape, index_map):
    return pl.BlockSpec(shape, index_map)


def _params(n_grid_dims):
    return pltpu.CompilerParams(dimension_semantics=("arbitrary",) * n_grid_dims,
                                vmem_limit_bytes=VMEM_LIMIT)


def _t5_bucket_np(rel):
    half = REL_BUCKETS // 2
    max_exact = half // 2
    ret = np.where(rel > 0, half, 0)
    n = np.abs(rel)
    nf = np.maximum(n, 1).astype(np.float32)
    large = max_exact + (np.log(nf / np.float32(max_exact)) / np.float32(math.log(REL_MAX_DIST / max_exact))
                         * np.float32(half - max_exact)).astype(np.int32)
    large = np.minimum(large, half - 1)
    return ret + np.where(n < max_exact, n, large)


def _num_near_tiles(seq):
    n = np.arange(1, max(seq, 2 * REL_MAX_DIST) + 1)
    b = _t5_bucket_np(-n)
    sat = REL_BUCKETS // 2 - 1
    unsat = np.nonzero(b != sat)[0]
    n_sat = int(n[unsat[-1]]) + 1 if unsat.size else 1
    return -(-(n_sat + QB - 1) // QB)


def _bias_kernel(rb_ref, o_ref, *, n_near):
    d = pl.program_id(0)
    i = lax.broadcasted_iota(I32, (QB, QB), 0)
    j = lax.broadcasted_iota(I32, (QB, QB), 1)
    rel = i - j - d * QB
    rel = jnp.where(d >= n_near, -8 * REL_MAX_DIST, rel)
    half = REL_BUCKETS // 2
    max_exact = half // 2
    ret = jnp.where(rel > 0, half, 0)
    n = jnp.abs(rel)
    nf = jnp.maximum(n, 1).astype(F32)
    large = max_exact + (jnp.log(nf / max_exact) / math.log(REL_MAX_DIST / max_exact)
                         * (half - max_exact)).astype(I32)
    large = jnp.minimum(large, half - 1)
    bucket = ret + jnp.where(n < max_exact, n, large)
    for h in range(ATT_HEADS):
        acc = jnp.zeros((QB, QB), F32)
        for b in range(REL_BUCKETS):
            acc = jnp.where(bucket == b, rb_ref[b, h], acc)
        o_ref[0, h] = acc * LOG2E


def _bias_tiles(rel_bias, n_near):
    return pl.pallas_call(
        functools.partial(_bias_kernel, n_near=n_near),
        grid=(n_near + 1,),
        in_specs=[pl.BlockSpec(memory_space=pltpu.SMEM)],
        out_specs=pl.BlockSpec((1, ATT_HEADS, QB, QB), lambda d: (d, 0, 0, 0)),
        out_shape=jax.ShapeDtypeStruct((n_near + 1, ATT_HEADS, QB, QB), F32),
        compiler_params=_params(1),
        name="bias_tiles",
    )(rel_bias)


def _inproj_kernel(x_ref, g_ref, wa_ref, wvt_ref, wwit_ref, wglu_ref, wgate_ref, bgate_ref, gq_ref, gk_ref,
                   q_ref, k_ref, qi_ref, ki_ref, vt_ref, wit_ref, u_ref, gate_ref, *, tm):
    x = x_ref[...]
    h = x * lax.rsqrt(jnp.mean(x * x, axis=-1, keepdims=True) + EPS) * g_ref[...]
    hb = h.astype(BF16)
    hp = ATT_HEADS * LANES
    ya = _dot(hb, wa_ref[...])
    for h_i in range(ATT_HEADS):
        sl = slice(h_i * LANES, (h_i + 1) * LANES)
        qh = ya[:, h_i * LANES:(h_i + 1) * LANES]
        ms = jnp.sum(qh * qh, axis=-1, keepdims=True) * (1.0 / ATT_HEAD_DIM)
        q_ref[:, sl] = (qh * lax.rsqrt(ms + EPS) * gq_ref[...]).astype(BF16)
        kh = ya[:, hp + h_i * LANES:hp + (h_i + 1) * LANES]
        ms = jnp.sum(kh * kh, axis=-1, keepdims=True) * (1.0 / ATT_HEAD_DIM)
        k_ref[:, sl] = (kh * lax.rsqrt(ms + EPS) * gk_ref[...]).astype(BF16)
    qi_ref[...] = ya[:, 2 * hp:3 * hp].astype(BF16)
    ki_ref[...] = ya[:, 3 * hp:3 * hp + LANES].astype(BF16)
    vt = _nt_dot(wvt_ref[...], hb).astype(BF16)
    for c in range(tm // KPAIR):
        vt_ref[c] = vt[:, c * KPAIR:(c + 1) * KPAIR]
    wit_ref[...] = _nt_dot(wwit_ref[...], hb)
    glu = _dot(hb, wglu_ref[...])
    u_ref[...] = glu[:, :CONV_CH] * jax.nn.sigmoid(glu[:, CONV_CH:])
    gate_ref[...] = jax.nn.sigmoid(_dot(hb, wgate_ref[...]) + bgate_ref[...]).astype(BF16)


def _pad_heads(w, nh, hd):
    d = w.shape[0]
    w3 = w.reshape(d, nh, hd)
    w3 = jnp.pad(w3, ((0, 0), (0, 0), (0, LANES - hd)))
    return w3.reshape(d, nh * LANES)


def _in_proj(x2, attn_g, w_in, b_gate, q_g, k_g, tm=256):
    T, D = x2.shape
    aw = ATT_HEADS * ATT_HEAD_DIM
    iw = IDX_HEADS * IDX_DIM
    o = 0
    wq = w_in[:, o:o + aw]; o += aw
    wk = w_in[:, o:o + aw]; o += aw
    wv = w_in[:, o:o + aw]; o += aw
    wqi = w_in[:, o:o + iw]; o += iw
    wki = w_in[:, o:o + IDX_DIM]; o += IDX_DIM
    wwi = w_in[:, o:o + IDX_HEADS]; o += IDX_HEADS
    wglu = w_in[:, o:o + 2 * CONV_CH]; o += 2 * CONV_CH
    wgate = w_in[:, o:o + N_BRANCH * D]
    wa = jnp.concatenate([_pad_heads(wq, ATT_HEADS, ATT_HEAD_DIM), _pad_heads(wk, ATT_HEADS, ATT_HEAD_DIM),
                          _pad_heads(wqi, IDX_HEADS, IDX_DIM),
                          jnp.pad(wki, ((0, 0), (0, LANES - IDX_DIM)))], axis=1).astype(BF16)
    na = wa.shape[1]
    hp = ATT_HEADS * LANES
    pad_g = lambda g, s: jnp.pad(g * s, (0, LANES - ATT_HEAD_DIM)).reshape(1, LANES)
    gq = pad_g(q_g, ATT_HEAD_DIM ** -0.5 * LOG2E)
    gk = pad_g(k_g, 1.0)
    const = lambda *shape: _single(shape, lambda i: (0,) * len(shape))
    outs = pl.pallas_call(
        functools.partial(_inproj_kernel, tm=tm),
        grid=(T // tm,),
        in_specs=[pl.BlockSpec((tm, D), lambda i: (i, 0)), const(1, D), const(D, na), const(aw, D),
                  const(IDX_HEADS, D), const(D, 2 * CONV_CH), const(D, N_BRANCH * D), const(1, N_BRANCH * D),
                  const(1, LANES), const(1, LANES)],
        out_specs=[pl.BlockSpec((tm, hp), lambda i: (i, 0)), pl.BlockSpec((tm, hp), lambda i: (i, 0)),
                   pl.BlockSpec((tm, hp), lambda i: (i, 0)), pl.BlockSpec((tm, LANES), lambda i: (i, 0)),
                   pl.BlockSpec((tm // KPAIR, aw, KPAIR), lambda i: (i, 0, 0)),
                   pl.BlockSpec((IDX_HEADS, tm), lambda i: (0, i)),
                   pl.BlockSpec((tm, CONV_CH), lambda i: (i, 0)),
                   pl.BlockSpec((tm, N_BRANCH * D), lambda i: (i, 0))],
        out_shape=[jax.ShapeDtypeStruct((T, hp), BF16), jax.ShapeDtypeStruct((T, hp), BF16),
                   jax.ShapeDtypeStruct((T, hp), BF16), jax.ShapeDtypeStruct((T, LANES), BF16),
                   jax.ShapeDtypeStruct((T // KPAIR, aw, KPAIR), BF16),
                   jax.ShapeDtypeStruct((IDX_HEADS, T), F32),
                   jax.ShapeDtypeStruct((T, CONV_CH), F32),
                   jax.ShapeDtypeStruct((T, N_BRANCH * D), BF16)],
        compiler_params=_params(1),
        name="in_proj",
    )(x2, attn_g.reshape(1, D), wa, wv.T.astype(BF16), wwi.T.astype(BF16), wglu.astype(BF16),
      wgate.astype(BF16), b_gate.reshape(1, N_BRANCH * D), gq, gk)
    return outs


def _attn_kernel(q_ref, qi_ref, wit_ref, *rest, seq, n_near, topk, n_parts, qb0):
    k_parts, ki_parts, vt_parts = rest[:n_parts], rest[n_parts:2 * n_parts], rest[2 * n_parts:3 * n_parts]
    (bias_hbm, o_ref, k_ref, ki_ref, vt_ref, bias_ref, load_sem, key_scr, att_scr), scr = (
        rest[3 * n_parts:3 * n_parts + 9], rest[3 * n_parts + 9:])
    qb = qb0 + pl.program_id(0)

    @pl.when(pl.program_id(0) == 0)
    def _():
        loads = [pltpu.make_async_copy(bias_hbm, bias_ref, load_sem.at[3 * n_parts])]
        kp = kq = 0
        for n in range(n_parts):
            npair, nqd = k_parts[n].shape[0], ki_parts[n].shape[0]
            loads += [pltpu.make_async_copy(k_parts[n], k_ref.at[pl.ds(kp, npair)], load_sem.at[3 * n]),
                      pltpu.make_async_copy(ki_parts[n], ki_ref.at[pl.ds(kq, nqd)], load_sem.at[3 * n + 1]),
                      pltpu.make_async_copy(vt_parts[n], vt_ref.at[pl.ds(kp, npair)], load_sem.at[3 * n + 2])]
            kp, kq = kp + npair, kq + nqd
        for c in loads:
            c.start()
        for c in loads:
            c.wait()

    nquad = (qb + 4) // 4
    lane_t = lax.broadcasted_iota(I32, (1, QB), 1) + qb * QB
    qchunk = lane_t // CHUNK
    sub = lax.broadcasted_iota(I32, (KQUAD, QB), 0)
    idx_scale = (IDX_DIM ** -0.5) * (IDX_HEADS ** -0.5)
    wrow = wit_ref[...] * idx_scale

    qi_all = jnp.concatenate([qi_ref[:, h * LANES:(h + 1) * LANES] for h in range(IDX_HEADS)], axis=0)

    def score_quad(j, carry, masked=False):
        d = _nt_dot(ki_ref[j], qi_all)
        acc = jnp.zeros((KQUAD, QB), F32)
        for h in range(IDX_HEADS):
            acc = acc + jnp.maximum(d[:, h * QB:(h + 1) * QB], 0.0) * wrow[h:h + 1, :]
        bits = pltpu.bitcast(acc, I32)
        skey = bits ^ ((bits >> 31) & 0x7FFFFFFF)
        if masked:
            skey = jnp.where((sub + j * KQUAD) // CHUNK <= qchunk, skey, INT_MIN)
        key_scr[j] = skey
        return carry

    lax.fori_loop(0, nquad - 1, score_quad, 0)
    score_quad(nquad - 1, 0, masked=True)

    nvis = (qchunk + 1) * CHUNK
    kk = jnp.minimum(topk, nvis)

    def count(pred):
        def body(j, acc8):
            m = jnp.where(pred(key_scr[j], j), 1, 0)
            return acc8 + m.reshape(KQUAD // SUBLANES, SUBLANES, QB).sum(axis=0)
        return lax.fori_loop(0, nquad, body, jnp.zeros((SUBLANES, QB), I32)).sum(axis=0, keepdims=True)

    def bit_step(i, carry):
        ans, cnt = carry
        cand = ans + lax.shift_left(jnp.int32(1), 31 - i)
        c = count(lambda blk, j: blk >= cand)
        take = c >= kk
        return jnp.where(take, cand, ans), jnp.where(take, c, cnt)

    thr, cnt = lax.fori_loop(0, 32, bit_step,
                             (jnp.full((1, QB), INT_MIN, I32), jnp.full((1, QB), 0, I32) + nquad * KQUAD))

    @pl.when(jnp.max(cnt - kk) > 0)
    def _():
        n_gt = count(lambda blk, j: blk > thr)
        need = kk - n_gt

        def idx_step(i, jbound):
            cand = jbound + lax.shift_left(jnp.int32(1), (seq.bit_length() - 1) - i)
            c = count(lambda blk, j: (blk == thr) & (sub + j * KQUAD < cand))
            return jnp.where(c <= need, cand, jbound)

        jbound = lax.fori_loop(0, seq.bit_length(), idx_step, jnp.zeros((1, QB), I32))

        def drop(j, carry):
            blk = key_scr[j]
            key_scr[j] = jnp.where((blk == thr) & (sub + j * KQUAD >= jbound), INT_MIN, blk)
            return carry

        lax.fori_loop(0, nquad, drop, 0)

    acc_refs, s_even, s_odd = scr[:ATT_HEADS], scr[ATT_HEADS], scr[ATT_HEADS + 1]
    for acc_ref in acc_refs:
        acc_ref[...] = jnp.zeros(acc_ref.shape, F32)
    head_row = lax.broadcasted_iota(I32, (ATT_HEADS, QB), 0)
    last_pair = k_ref.shape[0] - 1

    def qk(jp, s_ref):
        jc = jnp.minimum(jp, last_pair)
        for h in range(ATT_HEADS):
            s_ref[h] = _nt_dot(k_ref[jc, :, h * LANES:(h + 1) * LANES], q_ref[:, h * LANES:(h + 1) * LANES])

    def softmax_pv(jq, half, s_ref, carry):
        m_all, l_all = carry
        jp = 2 * jq + half
        keys = key_scr[jq, half * KPAIR:(half + 1) * KPAIR, :]
        masked = jnp.where(keys >= thr, 0.0, -jnp.inf)
        tile0 = jnp.clip(qb - 2 * jp, 0, n_near)
        tile1 = jnp.clip(qb - 2 * jp - 1, 0, n_near)
        for h in range(ATT_HEADS):
            rows = slice(h * ATT_HEAD_DIM, (h + 1) * ATT_HEAD_DIM)
            bias = jnp.concatenate([bias_ref[tile0, h], bias_ref[tile1, h]], axis=0)
            s = s_ref[h] + bias + masked
            m = m_all[h:h + 1, :]
            m_new = jnp.maximum(m, jnp.max(s, axis=0, keepdims=True))
            p = jnp.exp2(s - m_new)
            alpha = jnp.exp2(m - m_new)
            l_new = alpha * l_all[h:h + 1, :] + jnp.sum(p, axis=0, keepdims=True)
            m_all = jnp.where(head_row == h, m_new, m_all)
            l_all = jnp.where(head_row == h, l_new, l_all)
            acc_refs[h][...] = alpha * acc_refs[h][...] + _dot(vt_ref[jp, rows, :], p.astype(BF16))
        return m_all, l_all

    def att_quad(jq, carry):
        qk(2 * jq + 1, s_odd)
        carry = softmax_pv(jq, 0, s_even, carry)
        qk(2 * jq + 2, s_even)
        return softmax_pv(jq, 1, s_odd, carry)

    qk(0, s_even)
    _, l_all = lax.fori_loop(0, nquad, att_quad,
                             (jnp.full((ATT_HEADS, QB), NEG_BIG, F32), jnp.zeros((ATT_HEADS, QB), F32)))
    for h in range(ATT_HEADS):
        rows = slice(h * ATT_HEAD_DIM, (h + 1) * ATT_HEAD_DIM)
        att_scr[rows, :] = acc_refs[h][...] / l_all[h:h + 1, :]
    o_ref[...] = att_scr[...].T.astype(o_ref.dtype)


def _attention(q, qi, wit, k_parts, ki_parts, vt_parts, bias, S, n_near):
    Tq = q.shape[0]
    hp = ATT_HEADS * LANES
    aw = ATT_HEADS * ATT_HEAD_DIM
    topk = min(TOPK_MAX, S // 4)
    n_parts = len(k_parts)
    k4 = [k.reshape(-1, KPAIR, hp) for k in k_parts]
    ki4 = [ki.reshape(-1, KQUAD, LANES) for ki in ki_parts]
    npr = sum(k.shape[0] for k in k4)
    nqd = sum(ki.shape[0] for ki in ki4)
    assert npr * KPAIR == nqd * KQUAD and vt_parts[0].shape[1:] == (aw, KPAIR)
    qb0 = (npr * KPAIR - Tq) // QB
    return pl.pallas_call(
        functools.partial(_attn_kernel, seq=S, n_near=n_near, topk=topk, n_parts=n_parts, qb0=qb0),
        grid=(Tq // QB,),
        in_specs=[pl.BlockSpec((QB, hp), lambda i: (i, 0)),
                  pl.BlockSpec((QB, hp), lambda i: (i, 0)),
                  pl.BlockSpec((IDX_HEADS, QB), lambda i: (0, i))]
        + [pl.BlockSpec(memory_space=pl.ANY)] * (3 * n_parts + 1),
        out_specs=pl.BlockSpec((QB, aw), lambda i: (i, 0)),
        out_shape=jax.ShapeDtypeStruct((Tq, aw), BF16),
        scratch_shapes=[pltpu.VMEM((npr, KPAIR, hp), BF16), pltpu.VMEM((nqd, KQUAD, LANES), BF16),
                        pltpu.VMEM((npr, aw, KPAIR), BF16), pltpu.VMEM((n_near + 1, ATT_HEADS, QB, QB), F32),
                        pltpu.SemaphoreType.DMA((3 * n_parts + 1,)),
                        pltpu.VMEM((nqd, KQUAD, QB), I32), pltpu.VMEM((aw, QB), F32)]
        + [pltpu.VMEM((ATT_HEAD_DIM, QB), F32) for _ in range(ATT_HEADS)]
        + [pltpu.VMEM((ATT_HEADS, KPAIR, QB), F32) for _ in range(2)],
        compiler_params=_params(1),
        name="dsa_attention",
    )(q, qi, wit, *k4, *ki4, *vt_parts, bias)


HALO = 32


def _mix_kernel(att_ref, u_ref, halo_ref, hist_ref, gate_ref, x_ref, cw_ref, cb_ref, lng_ref, lnb_ref, wao_ref,
                wco_ref, wout_ref, gffn_ref, x1_ref, h2_ref, ext_scr, *, tm):
    i = pl.program_id(0)
    ext_scr[0:HALO, :] = jnp.where(i == 0, hist_ref[...], halo_ref[...])
    ext_scr[HALO:HALO + tm, :] = u_ref[...]
    y = jnp.zeros((tm, CONV_CH), F32)
    for j in range(CONV_WIDTH):
        y = y + cw_ref[j:j + 1, :] * ext_scr[pl.ds(HALO - (CONV_WIDTH - 1) + j, tm), :]
    y = y + cb_ref[...]
    mu = jnp.mean(y, axis=-1, keepdims=True)
    yc = y - mu
    yn = yc * lax.rsqrt(jnp.mean(yc * yc, axis=-1, keepdims=True) + EPS) * lng_ref[...] + lnb_ref[...]
    z = yn * jax.nn.sigmoid(yn)
    y_conv = _dot(z.astype(BF16), wco_ref[...])
    y_att = _dot(att_ref[...], wao_ref[...])
    d = y_att.shape[1]
    g = gate_ref[...]
    mixed = g[:, :d].astype(F32) * y_att + g[:, d:].astype(F32) * y_conv
    x1 = x_ref[...] + _dot(mixed.astype(BF16), wout_ref[...])
    x1_ref[...] = x1
    h2_ref[...] = x1 * lax.rsqrt(jnp.mean(x1 * x1, axis=-1, keepdims=True) + EPS) * gffn_ref[...]


def _mix(att, u, history, gate, x2, conv_w, conv_b, ln_g, ln_b, w_att_out, w_conv_out, w_out, ffn_g, tm=256):
    T, D = x2.shape
    aw = att.shape[1]
    const = lambda *shape: _single(shape, lambda i: (0,) * len(shape))
    hb = tm // HALO
    return pl.pallas_call(
        functools.partial(_mix_kernel, tm=tm),
        grid=(T // tm,),
        in_specs=[pl.BlockSpec((tm, aw), lambda i: (i, 0)),
                  pl.BlockSpec((tm, CONV_CH), lambda i: (i, 0)),
                  pl.BlockSpec((HALO, CONV_CH), lambda i: (jnp.maximum(i * hb - 1, 0), 0)),
                  const(HALO, CONV_CH),
                  pl.BlockSpec((tm, N_BRANCH * D), lambda i: (i, 0)),
                  pl.BlockSpec((tm, D), lambda i: (i, 0)),
                  const(CONV_WIDTH, CONV_CH), const(1, CONV_CH), const(1, CONV_CH), const(1, CONV_CH),
                  const(aw, D), const(CONV_CH, D), const(D, D), const(1, D)],
        out_specs=[pl.BlockSpec((tm, D), lambda i: (i, 0)), pl.BlockSpec((tm, D), lambda i: (i, 0))],
        out_shape=[jax.ShapeDtypeStruct((T, D), F32), jax.ShapeDtypeStruct((T, D), F32)],
        scratch_shapes=[pltpu.VMEM((HALO + tm, CONV_CH), F32)],
        compiler_params=_params(1),
        name="mix_out_proj",
    )(att, u, u, history, gate, x2, conv_w.reshape(CONV_WIDTH, CONV_CH), conv_b.reshape(1, CONV_CH),
      ln_g.reshape(1, CONV_CH), ln_b.reshape(1, CONV_CH), w_att_out.astype(BF16), w_conv_out.astype(BF16),
      w_out.astype(BF16), ffn_g.reshape(1, D))


def _top_rows(sc, k, payload=None):
    rows = sc.shape[0]
    iota = lax.broadcasted_iota(I32, sc.shape, 0)
    out_row = lax.broadcasted_iota(I32, (k, sc.shape[1]), 0)
    vals = jnp.zeros((k, sc.shape[1]), F32)
    idxs = jnp.zeros((k, sc.shape[1]), I32)
    for r in range(k):
        m = jnp.max(sc, axis=0, keepdims=True)
        idx = jnp.min(jnp.where(sc == m, iota, rows), axis=0, keepdims=True)
        hit = iota == idx
        rec = idx if payload is None else jnp.max(jnp.where(hit, payload, -1), axis=0, keepdims=True)
        vals = jnp.where(out_row == r, m, vals)
        idxs = jnp.where(out_row == r, rec, idxs)
        sc = jnp.where(hit, -jnp.inf, sc)
    return vals, idxs


def _route_kernel(h2_ref, wq_ref, sk_ref, e_ref, g_ref, qt_scr, et_scr, gt_scr, *, tm):
    half = N_KEYS
    qt_scr[...] = _nt_dot(wq_ref[...], h2_ref[...].astype(BF16)).astype(BF16)

    def head(h, carry):
        tops = []
        for c in range(2):
            row0 = pl.multiple_of((h * 2 + c) * half, half)
            sc = _dot(sk_ref[h * 2 + c], qt_scr[pl.ds(row0, half), :])
            tops.append(_top_rows(sc, PEER_TOPK))
        (a, ia), (b, ib) = tops
        k = PEER_TOPK
        g = SUBLANES
        assert (k // 2) % g == 0 and k % g == 0
        row = lax.broadcasted_iota(I32, (g, a.shape[1]), 0)
        cand_parts, cidx_parts = [], []

        def add(av, iav, bv, ibv, valid_rows):
            s = av + bv
            if valid_rows < g:
                s = jnp.where(row < valid_rows, s, -jnp.inf)
            cand_parts.append(s)
            cidx_parts.append(iav * N_KEYS + ibv)

        for i in range(k):
            nj = k // (i + 1)
            if nj >= g:
                for j0 in range(0, nj, g):
                    add(a[i:i + 1, :], ia[i:i + 1, :], b[j0:j0 + g, :], ib[j0:j0 + g, :], g)
            elif nj > 1:
                add(a[i:i + 1, :], ia[i:i + 1, :], b[0:g, :], ib[0:g, :], nj)
            elif i % g == 0:
                add(a[i:i + g, :], ia[i:i + g, :], b[0:1, :], ib[0:1, :], g)
        cand = jnp.concatenate(cand_parts, axis=0)
        cidx = jnp.concatenate(cidx_parts, axis=0)
        ts, te = _top_rows(cand, PEER_TOPK, payload=cidx)
        ex = jnp.exp(ts - ts[0:1, :])
        gate = ex / jnp.sum(ex, axis=0, keepdims=True)
        r0 = pl.multiple_of(h * PEER_TOPK, PEER_TOPK)
        et_scr[pl.ds(r0, PEER_TOPK), :] = te
        gt_scr[pl.ds(r0, PEER_TOPK), :] = gate
        return carry

    lax.fori_loop(0, PEER_HEADS, head, 0)
    e_ref[...] = et_scr[...].T
    g_ref[...] = gt_scr[...].T


def _peer_route(h2, w_peer_q, sub_keys, tm=256):
    T, D = h2.shape
    nsel = PEER_HEADS * PEER_TOPK
    qd = w_peer_q.shape[1]
    half = sub_keys.shape[-1]
    skb = sub_keys.reshape(PEER_HEADS * 2, N_KEYS, half).astype(BF16)
    return pl.pallas_call(
        functools.partial(_route_kernel, tm=tm),
        grid=(T // tm,),
        in_specs=[pl.BlockSpec((tm, D), lambda i: (i, 0)),
                  _single((qd, D), lambda i: (0, 0)),
                  _single((PEER_HEADS * 2, N_KEYS, half), lambda i: (0, 0, 0))],
        out_specs=[pl.BlockSpec((tm, nsel), lambda i: (i, 0)), pl.BlockSpec((tm, nsel), lambda i: (i, 0))],
        out_shape=[jax.ShapeDtypeStruct((T, nsel), I32), jax.ShapeDtypeStruct((T, nsel), F32)],
        scratch_shapes=[pltpu.VMEM((qd, tm), BF16), pltpu.VMEM((nsel, tm), I32), pltpu.VMEM((nsel, tm), F32)],
        compiler_params=_params(1),
        name="peer_route",
    )(h2, w_peer_q.T.astype(BF16), skb)


TOK = 8


def _gelu_tanh(x):
    return 0.5 * x * (1.0 + jnp.tanh(math.sqrt(2.0 / math.pi) * (x + 0.044715 * (x * x * x))))


def _sublane_sums(ps):
    sub = lax.broadcasted_iota(I32, (SUBLANES, LANES), 0)
    lvl, stride = list(ps), SUBLANES // 2
    while len(lvl) > 1:
        half = len(lvl) // 2
        low = (sub & stride) == 0
        nxt = []
        for n in range(half):
            a, b = lvl[n], lvl[n + half]
            nxt.append(jnp.where(low, a + pltpu.roll(a, SUBLANES - stride, 0), b + pltpu.roll(b, stride, 0)))
        lvl, stride = nxt, stride // 2
    return lvl[0]


def _pack_uv(u, v):
    ne, d = u.shape
    hi = lax.bitcast_convert_type(u.astype(BF16), jnp.uint16).astype(jnp.uint32)
    lo = lax.bitcast_convert_type(v.astype(BF16), jnp.uint16).astype(jnp.uint32)
    return ((hi << 16) | lo).reshape(ne, d // LANES, LANES)


def _u_of(word):
    return pltpu.bitcast(word & jnp.uint32(0xFFFF0000), F32)


def _v_of(word):
    return pltpu.bitcast(word << 16, F32)


def _expert_group(row, h2_ref, x1_ref, gate_ref, o_ref, abc_scr, base, nsel, before_dots=None, before_sum=None):
    dsub = h2_ref.shape[1]
    lane_id = lax.broadcasted_iota(I32, (nsel, LANES), 1)
    st = jnp.zeros((nsel, LANES), F32)
    for t in range(TOK):
        if before_dots is not None:
            before_dots(t)
        xt = h2_ref[base + t]
        qs = []
        for g in range(nsel // SUBLANES):
            ps = [_u_of(row(t, g * SUBLANES + k)) * xt for k in range(SUBLANES)]
            qs.append(_sublane_sums(ps))
        q = jnp.concatenate(qs, axis=0)
        st = jnp.where(lane_id == t, jnp.sum(q, axis=1, keepdims=True), st)
    g8 = gate_ref[base:base + TOK, :]
    gt = jnp.concatenate([g8, jnp.zeros((nsel - TOK, nsel), F32)], axis=0).T
    at = _gelu_tanh(st) * gt

    for t in range(TOK):
        if before_sum is not None:
            before_sum(t)
        abc_scr[t] = jnp.broadcast_to(at[:, t:t + 1], (nsel, LANES))
        accs = [jnp.zeros((dsub, LANES), F32) for _ in range(4)]
        for n in range(nsel):
            accs[n % 4] = accs[n % 4] + abc_scr[t, n:n + 1, :] * _v_of(row(t, n))
        o_ref[base + t] = x1_ref[base + t] + ((accs[0] + accs[1]) + (accs[2] + accs[3]))


def _expert_kernel(idxc_ref, idxn_ref, h2_ref, x1_ref, gate_ref, uv_ref, *rest, nsel):
    o_ref, buf, sem, abc_scr = rest[-4:]
    i = pl.program_id(0)
    nsteps = pl.num_programs(0)
    rows = TOK * nsel

    def issue_token(idx_ref, tok, s, t, n0=0, n1=nsel):
        for n in range(n0, n1):
            pltpu.make_async_copy(uv_ref.at[idx_ref[tok, n]], buf.at[s, t * nsel + n],
                                  sem.at[s]).start(priority=n % 2)

    def wait_slot(s):
        pltpu.make_async_copy(uv_ref.at[pl.ds(0, rows)], buf.at[s], sem.at[s]).wait()

    @pl.when(i == 0)
    def _():
        for t in range(TOK):
            issue_token(idxc_ref, t, 0, t)

    n_early = (3 * nsel) // 4

    for grp in range(2):
        wait_slot(grp)
        nxt_idx, nxt_tok, nxt_slot = (idxc_ref, TOK, 1) if grp == 0 else (idxn_ref, 0, 0)
        _expert_group(lambda t, n, grp=grp: buf[grp, t * nsel + n], h2_ref, x1_ref, gate_ref, o_ref, abc_scr,
                      grp * TOK, nsel,
                      before_dots=lambda t: issue_token(nxt_idx, nxt_tok + t, nxt_slot, t, 0, n_early),
                      before_sum=lambda t: issue_token(nxt_idx, nxt_tok + t, nxt_slot, t, n_early, nsel))

    @pl.when(i == nsteps - 1)
    def _():
        wait_slot(0)


def _expert_staged_kernel(rows_ref, h2_ref, x1_ref, gate_ref, prev_ref, o_ref, abc_scr, *, nsel):
    del prev_ref
    for grp in range(h2_ref.shape[0] // TOK):
        _expert_group(lambda t, n, grp=grp: rows_ref[(grp * TOK + t) * nsel + n], h2_ref, x1_ref, gate_ref, o_ref,
                      abc_scr, grp * TOK, nsel)


SC_CORES = 2
SC_SUBCORES = 16
SC_CHUNK = 16
SC_NBUF = 7
SC_SLAB = 2048


def _sc_gather(table, idx):
    n = idx.shape[0]
    nw = SC_CORES * SC_SUBCORES
    per_w = n // nw
    nb = SC_NBUF
    assert n % nw == 0 and per_w % SC_SLAB == 0 and SC_SLAB % SC_CHUNK == 0
    nslab, nchunk = per_w // SC_SLAB, SC_SLAB // SC_CHUNK
    assert nchunk >= nb
    row_shape = table.shape[1:]
    mesh = plsc.VectorSubcoreMesh(core_axis_name="c", subcore_axis_name="s")

    @functools.partial(
        pl.kernel, mesh=mesh, out_type=jax.ShapeDtypeStruct((n,) + row_shape, table.dtype),
        scratch_types=[pltpu.VMEM((SC_SLAB,), jnp.int32)]
        + [pltpu.VMEM((SC_CHUNK,) + row_shape, table.dtype) for _ in range(nb)]
        + [pltpu.SemaphoreType.DMA for _ in range(2 * nb)])
    def gather(table_hbm, idx_hbm, out_hbm, idx_v, *scr):
        wid = lax.axis_index("s") * SC_CORES + lax.axis_index("c")
        bufs, gsem, wsem = scr[:nb], scr[nb:2 * nb], scr[2 * nb:]

        def gather_copy(c, b):
            return pltpu.make_async_copy(table_hbm.at[idx_v.at[pl.ds(c * SC_CHUNK, SC_CHUNK)]], bufs[b], gsem[b])

        @pl.loop(0, nslab)
        def _(sl):
            base = wid * per_w + sl * SC_SLAB
            pltpu.sync_copy(idx_hbm.at[pl.ds(base, SC_SLAB)], idx_v)

            def write_copy(c, b):
                return pltpu.make_async_copy(bufs[b], out_hbm.at[pl.ds(base + c * SC_CHUNK, SC_CHUNK)], wsem[b])

            for b in range(nb - 1):
                gather_copy(b, b).start()

            @pl.loop(0, -(-nchunk // nb) * nb, step=nb)
            def _(c0):
                for b in range(nb):
                    c = c0 + b
                    prev = (b - 1) % nb

                    @pl.when(c < nchunk)
                    def _():
                        gather_copy(c, b).wait()
                        write_copy(c, b).start()

                    @pl.when((c >= 1) & (c < nchunk))
                    def _():
                        write_copy(c - 1, prev).wait()

                    @pl.when(c + nb - 1 < nchunk)
                    def _():
                        gather_copy(c + nb - 1, prev).start()

            write_copy(nchunk - 1, (nchunk - 1) % nb).wait()

    return gather(table, idx)


def _experts_dma(eidx, gates, h2, x1, uv, ta, after=()):
    T, dsub, _ = x1.shape
    nsel = eidx.shape[1]
    tb = 2 * TOK
    nsteps = ta // tb
    return pl.pallas_call(
        functools.partial(_expert_kernel, nsel=nsel),
        grid=(nsteps,),
        in_specs=[pl.BlockSpec((tb, nsel), lambda i: (i, 0), memory_space=pltpu.SMEM),
                  pl.BlockSpec((tb, nsel), lambda i: (jnp.minimum(i + 1, nsteps - 1), 0), memory_space=pltpu.SMEM),
                  pl.BlockSpec((tb, dsub, LANES), lambda i: (i, 0, 0)),
                  pl.BlockSpec((tb, dsub, LANES), lambda i: (i, 0, 0)),
                  pl.BlockSpec((tb, nsel), lambda i: (i, 0)),
                  pl.BlockSpec(memory_space=pl.ANY)] + [pl.BlockSpec(memory_space=pl.ANY)] * len(after),
        out_specs=pl.BlockSpec((tb, dsub, LANES), lambda i: (i, 0, 0)),
        out_shape=jax.ShapeDtypeStruct((T, dsub, LANES), F32),
        scratch_shapes=[pltpu.VMEM((2, TOK * nsel, dsub, LANES), jnp.uint32), pltpu.SemaphoreType.DMA((2,)),
                        pltpu.VMEM((TOK, nsel, LANES), F32)],
        compiler_params=_params(1),
        name="peer_experts",
    )(eidx, eidx, h2, x1, gates, uv, *after)


def _experts_staged(rows, gates, h2, x1, out, ta):
    T, dsub, _ = x1.shape
    nsel = gates.shape[1]
    tb = 2 * TOK
    first = ta // tb
    tok = lambda i: (first + i, 0, 0)
    return pl.pallas_call(
        functools.partial(_expert_staged_kernel, nsel=nsel),
        grid=((T - ta) // tb,),
        in_specs=[pl.BlockSpec((tb * nsel, dsub, LANES), lambda i: (i, 0, 0)),
                  pl.BlockSpec((tb, dsub, LANES), tok),
                  pl.BlockSpec((tb, dsub, LANES), tok),
                  pl.BlockSpec((tb, nsel), lambda i: (first + i, 0)),
                  pl.BlockSpec(memory_space=pl.ANY)],
        out_specs=pl.BlockSpec((tb, dsub, LANES), tok),
        out_shape=jax.ShapeDtypeStruct((T, dsub, LANES), F32),
        scratch_shapes=[pltpu.VMEM((TOK, nsel, LANES), F32)],
        input_output_aliases={4: 0},
        compiler_params=_params(1),
        name="peer_experts_staged",
    )(rows, h2, x1, gates, out)


def _staged_tokens(T, use_sc):
    unit = SC_CORES * SC_SUBCORES * SC_SLAB // (PEER_HEADS * PEER_TOPK)
    return int(T * use_sc) // unit * unit


def _peer_gather_start(eidx, uv, use_sc):
    T, nsel = eidx.shape
    ts = _staged_tokens(T, use_sc)
    return _sc_gather(uv, eidx[T - ts:].reshape(ts * nsel)) if ts else None


def _peer_dma_part(eidx, gates, h2, x1, uv, use_sc, after):
    T, D = x1.shape
    dsub = D // LANES
    ta = T - _staged_tokens(T, use_sc)
    h3, x3 = h2.reshape(T, dsub, LANES), x1.reshape(T, dsub, LANES)
    return _experts_dma(eidx, gates, h3, x3, uv, ta, after) if ta else jnp.zeros_like(x3)


def _peer_finish(rows, out, gates, h2, x1, use_sc):
    T, D = x1.shape
    dsub = D // LANES
    if rows is not None:
        out = _experts_staged(rows, gates, h2.reshape(T, dsub, LANES), x1.reshape(T, dsub, LANES), out,
                              T - _staged_tokens(T, use_sc))
    return out.reshape(T, D)


def _ple_kernel(x_ref, p_ref, g_ref, wg_ref, wp_ref, o_ref):
    x = x_ref[...]
    h = x * lax.rsqrt(jnp.mean(x * x, axis=-1, keepdims=True) + EPS) * g_ref[...]
    gate = jax.nn.sigmoid(_dot(h.astype(BF16), wg_ref[...]))
    o_ref[...] = x + gate * _dot(p_ref[...].astype(BF16), wp_ref[...])


def _ple(x2, p2, ple_g, w_gate, w_proj, tm=512):
    T, D = x2.shape
    pd = p2.shape[1]
    return pl.pallas_call(
        _ple_kernel,
        grid=(T // tm,),
        in_specs=[pl.BlockSpec((tm, D), lambda i: (i, 0)), pl.BlockSpec((tm, pd), lambda i: (i, 0)),
                  _single((1, D), lambda i: (0, 0)), _single((D, D), lambda i: (0, 0)),
                  _single((pd, D), lambda i: (0, 0))],
        out_specs=pl.BlockSpec((tm, D), lambda i: (i, 0)),
        out_shape=jax.ShapeDtypeStruct((T, D), F32),
        compiler_params=_params(1),
        name="ple",
    )(x2, p2, ple_g.reshape(1, D), w_gate.astype(BF16), w_proj.astype(BF16))


def kernel(x, p, rel_bias, attn_norm_g, w_in, b_gate, q_norm_g, k_norm_g, w_att_out, conv_w, conv_b, conv_ln_g,
           conv_ln_b, w_conv_out, w_out, ffn_norm_g, w_peer_q, peer_sub_keys, peer_u, peer_v, ple_norm_g,
           w_ple_gate, w_ple_proj):
    B, S, D = x.shape
    depth = w_in.shape[0]
    assert S % KQUAD == 0 and D % LANES == 0 and S % 512 == 0
    n_near = _num_near_tiles(S)
    bias = _bias_tiles(rel_bias, n_near)
    nsl = 2 if S % (2 * 2048) == 0 else 1
    sl = S // nsl
    xs = [x[b, s * sl:(s + 1) * sl] for b in range(B) for s in range(nsl)]
    units = [(b, s) for b in range(B) for s in range(nsl)]
    for i in range(depth):
        uv = _pack_uv(peer_u[i], peer_v[i])

        def finish(pending, i=i):
            rows, part, gates, h2, x1, (b, s), use_sc = pending
            x2 = _peer_finish(rows, part, gates, h2, x1, use_sc)
            return _ple(x2, p[i, b, s * sl:(s + 1) * sl], ple_norm_g[i], w_ple_gate[i], w_ple_proj[i])

        pending, outs, sc_rows = None, [], []
        keys, history = [], None
        for n, (b, s) in enumerate(units):
            use_sc = (1.0, 0.5, 0.5, 0.0)[max(0, n - (len(units) - 4))]
            q, k, qi, ki, vt, wit, u, gate = _in_proj(xs[n], attn_norm_g[i], w_in[i], b_gate[i], q_norm_g[i],
                                                      k_norm_g[i])
            if s == 0:
                keys, history = [], jnp.zeros((HALO, CONV_CH), F32)
            keys.append((k, ki, vt))
            att = _attention(q, qi, wit, [kp[0] for kp in keys], [kp[1] for kp in keys], [kp[2] for kp in keys],
                             bias, S, n_near)
            x1, h2 = _mix(att, u, history, gate, xs[n], conv_w[i], conv_b[i], conv_ln_g[i], conv_ln_b[i],
                          w_att_out[i], w_conv_out[i], w_out[i], ffn_norm_g[i])
            history = u[sl - HALO:]
            eidx, gates = _peer_route(h2, w_peer_q[i], peer_sub_keys[i])
            rows = _peer_gather_start(eidx, uv, use_sc)
            if pending is not None:
                outs.append(finish(pending))
            if rows is not None:
                sc_rows.append(rows)
            part = _peer_dma_part(eidx, gates, h2, x1, uv, use_sc, tuple(sc_rows))
            pending = (rows, part, gates, h2, x1, (b, s), use_sc)
        outs.append(finish(pending))
        xs = outs
    return jnp.concatenate(xs, axis=0).reshape(B, S, D)
```

```python
import functools
import math

import numpy as np
import jax
import jax.numpy as jnp
from jax import lax
from jax.experimental import pallas as pl
from jax.experimental.pallas import tpu as pltpu
from jax.experimental.pallas import tpu_sc as plsc

CHUNK = 64
ATT_HEADS = 8
ATT_HEAD_DIM = 64
IDX_HEADS = 8
IDX_DIM = 64
TOPK_MAX = 256
REL_BUCKETS = 32
REL_MAX_DIST = 1024
CONV_CH = 512
CONV_WIDTH = 31
N_BRANCH = 2
PEER_HEADS = 8
N_KEYS = 128
PEER_TOPK = 16
EPS = 1e-6

LANES = 128
SUBLANES = 8
VMEM_LIMIT = 56 * 1024 * 1024

QB = 128
KPAIR = 2 * QB
KQUAD = 4 * QB
INT_MIN = -(2 ** 31)
NEG_BIG = -1e30
LOG2E = math.log2(math.e)

F32 = jnp.float32
BF16 = jnp.bfloat16
I32 = jnp.int32


def _nt_dot(a, b, precision=None):
    return lax.dot_general(a, b, (((1,), (1,)), ((), ())), precision=precision,
                           preferred_element_type=F32)


def _dot(a, b):
    return jnp.dot(a, b, preferred_element_type=F32)


def _single(shape, index_map):
    return pl.BlockSpec(shape, index_map)


def _params(n_grid_dims):
    return pltpu.CompilerParams(dimension_semantics=("arbitrary",) * n_grid_dims,
                                vmem_limit_bytes=VMEM_LIMIT)


def _t5_bucket_np(rel):
    half = REL_BUCKETS // 2
    max_exact = half // 2
    ret = np.where(rel > 0, half, 0)
    n = np.abs(rel)
    nf = np.maximum(n, 1).astype(np.float32)
    large = max_exact + (np.log(nf / np.float32(max_exact)) / np.float32(math.log(REL_MAX_DIST / max_exact))
                         * np.float32(half - max_exact)).astype(np.int32)
    large = np.minimum(large, half - 1)
    return ret + np.where(n < max_exact, n, large)


def _num_near_tiles(seq):
    n = np.arange(1, max(seq, 2 * REL_MAX_DIST) + 1)
    b = _t5_bucket_np(-n)
    sat = REL_BUCKETS // 2 - 1
    unsat = np.nonzero(b != sat)[0]
    n_sat = int(n[unsat[-1]]) + 1 if unsat.size else 1
    return -(-(n_sat + QB - 1) // QB)


def _bias_kernel(rb_ref, o_ref, *, n_near):
    d = pl.program_id(0)
    i = lax.broadcasted_iota(I32, (QB, QB), 0)
    j = lax.broadcasted_iota(I32, (QB, QB), 1)
    rel = i - j - d * QB
    rel = jnp.where(d >= n_near, -8 * REL_MAX_DIST, rel)
    half = REL_BUCKETS // 2
    max_exact = half // 2
    ret = jnp.where(rel > 0, half, 0)
    n = jnp.abs(rel)
    nf = jnp.maximum(n, 1).astype(F32)
    large = max_exact + (jnp.log(nf / max_exact) / math.log(REL_MAX_DIST / max_exact)
                         * (half - max_exact)).astype(I32)
    large = jnp.minimum(large, half - 1)
    bucket = ret + jnp.where(n < max_exact, n, large)
    for h in range(ATT_HEADS):
        acc = jnp.zeros((QB, QB), F32)
        for b in range(REL_BUCKETS):
            acc = jnp.where(bucket == b, rb_ref[b, h], acc)
        o_ref[0, h] = (acc - rb_ref[half - 1, h]) * LOG2E


def _bias_tiles(rel_bias, n_near):
    return pl.pallas_call(
        functools.partial(_bias_kernel, n_near=n_near),
        grid=(n_near + 1,),
        in_specs=[pl.BlockSpec(memory_space=pltpu.SMEM)],
        out_specs=pl.BlockSpec((1, ATT_HEADS, QB, QB), lambda d: (d, 0, 0, 0)),
        out_shape=jax.ShapeDtypeStruct((n_near + 1, ATT_HEADS, QB, QB), F32),
        compiler_params=_params(1),
        name="bias_tiles",
    )(rel_bias)


def _inproj_kernel(x_ref, g_ref, wa_ref, wvt_ref, wwit_ref, wglu_ref, wgate_ref, bgate_ref, gq_ref, gk_ref,
                   q_ref, k_ref, qi_ref, ki_ref, vt_ref, wit_ref, u_ref, gate_ref, *, tm):
    x = x_ref[...]
    h = x * lax.rsqrt(jnp.mean(x * x, axis=-1, keepdims=True) + EPS) * g_ref[...]
    hb = h.astype(BF16)
    hp = ATT_HEADS * LANES
    ya = _dot(hb, wa_ref[...])
    for h_i in range(ATT_HEADS):
        sl = slice(h_i * LANES, (h_i + 1) * LANES)
        qh = ya[:, h_i * LANES:(h_i + 1) * LANES]
        ms = jnp.sum(qh * qh, axis=-1, keepdims=True) * (1.0 / ATT_HEAD_DIM)
        q_ref[:, sl] = (qh * lax.rsqrt(ms + EPS) * gq_ref[...]).astype(BF16)
        kh = ya[:, hp + h_i * LANES:hp + (h_i + 1) * LANES]
        ms = jnp.sum(kh * kh, axis=-1, keepdims=True) * (1.0 / ATT_HEAD_DIM)
        k_ref[:, sl] = (kh * lax.rsqrt(ms + EPS) * gk_ref[...]).astype(BF16)
    qi_ref[...] = ya[:, 2 * hp:3 * hp].astype(BF16)
    ki_ref[...] = ya[:, 3 * hp:3 * hp + LANES].astype(BF16)
    vt = _nt_dot(wvt_ref[...], hb).astype(BF16)
    for c in range(tm // KPAIR):
        vt_ref[c] = vt[:, c * KPAIR:(c + 1) * KPAIR]
    wit_ref[...] = _nt_dot(wwit_ref[...], hb)
    glu = _dot(hb, wglu_ref[...])
    u_ref[...] = glu[:, :CONV_CH] * jax.nn.sigmoid(glu[:, CONV_CH:])
    gate_ref[...] = jax.nn.sigmoid(_dot(hb, wgate_ref[...]) + bgate_ref[...]).astype(BF16)


def _pad_heads(w, nh, hd):
    d = w.shape[0]
    w3 = w.reshape(d, nh, hd)
    w3 = jnp.pad(w3, ((0, 0), (0, 0), (0, LANES - hd)))
    return w3.reshape(d, nh * LANES)


def _in_proj(x2, attn_g, w_in, b_gate, q_g, k_g, tm=256):
    T, D = x2.shape
    aw = ATT_HEADS * ATT_HEAD_DIM
    iw = IDX_HEADS * IDX_DIM
    o = 0
    wq = w_in[:, o:o + aw]; o += aw
    wk = w_in[:, o:o + aw]; o += aw
    wv = w_in[:, o:o + aw]; o += aw
    wqi = w_in[:, o:o + iw]; o += iw
    wki = w_in[:, o:o + IDX_DIM]; o += IDX_DIM
    wwi = w_in[:, o:o + IDX_HEADS]; o += IDX_HEADS
    wglu = w_in[:, o:o + 2 * CONV_CH]; o += 2 * CONV_CH
    wgate = w_in[:, o:o + N_BRANCH * D]
    wa = jnp.concatenate([_pad_heads(wq, ATT_HEADS, ATT_HEAD_DIM), _pad_heads(wk, ATT_HEADS, ATT_HEAD_DIM),
                          _pad_heads(wqi, IDX_HEADS, IDX_DIM),
                          jnp.pad(wki, ((0, 0), (0, LANES - IDX_DIM)))], axis=1).astype(BF16)
    na = wa.shape[1]
    hp = ATT_HEADS * LANES
    pad_g = lambda g, s: jnp.pad(g * s, (0, LANES - ATT_HEAD_DIM)).reshape(1, LANES)
    gq = pad_g(q_g, ATT_HEAD_DIM ** -0.5 * LOG2E)
    gk = pad_g(k_g, 1.0)
    const = lambda *shape: _single(shape, lambda i: (0,) * len(shape))
    outs = pl.pallas_call(
        functools.partial(_inproj_kernel, tm=tm),
        grid=(T // tm,),
        in_specs=[pl.BlockSpec((tm, D), lambda i: (i, 0)), const(1, D), const(D, na), const(aw, D),
                  const(IDX_HEADS, D), const(D, 2 * CONV_CH), const(D, N_BRANCH * D), const(1, N_BRANCH * D),
                  const(1, LANES), const(1, LANES)],
        out_specs=[pl.BlockSpec((tm, hp), lambda i: (i, 0)), pl.BlockSpec((tm, hp), lambda i: (i, 0)),
                   pl.BlockSpec((tm, hp), lambda i: (i, 0)), pl.BlockSpec((tm, LANES), lambda i: (i, 0)),
                   pl.BlockSpec((tm // KPAIR, aw, KPAIR), lambda i: (i, 0, 0)),
                   pl.BlockSpec((IDX_HEADS, tm), lambda i: (0, i)),
                   pl.BlockSpec((tm, CONV_CH), lambda i: (i, 0)),
                   pl.BlockSpec((tm, N_BRANCH * D), lambda i: (i, 0))],
        out_shape=[jax.ShapeDtypeStruct((T, hp), BF16), jax.ShapeDtypeStruct((T, hp), BF16),
                   jax.ShapeDtypeStruct((T, hp), BF16), jax.ShapeDtypeStruct((T, LANES), BF16),
                   jax.ShapeDtypeStruct((T // KPAIR, aw, KPAIR), BF16),
                   jax.ShapeDtypeStruct((IDX_HEADS, T), F32),
                   jax.ShapeDtypeStruct((T, CONV_CH), F32),
                   jax.ShapeDtypeStruct((T, N_BRANCH * D), BF16)],
        compiler_params=_params(1),
        name="in_proj",
    )(x2, attn_g.reshape(1, D), wa, wv.T.astype(BF16), wwi.T.astype(BF16), wglu.astype(BF16),
      wgate.astype(BF16), b_gate.reshape(1, N_BRANCH * D), gq, gk)
    return outs


def _attn_kernel(q_ref, qi_ref, wit_ref, *rest, seq, n_near, topk, n_parts, qb0):
    k_parts, ki_parts, vt_parts = rest[:n_parts], rest[n_parts:2 * n_parts], rest[2 * n_parts:3 * n_parts]
    (bias_hbm, o_ref, k_ref, ki_ref, vt_ref, bias_ref, load_sem, key_scr, att_scr), scr = (
        rest[3 * n_parts:3 * n_parts + 9], rest[3 * n_parts + 9:])
    qb = qb0 + pl.program_id(0)

    @pl.when(pl.program_id(0) == 0)
    def _():
        loads = [pltpu.make_async_copy(bias_hbm, bias_ref, load_sem.at[3 * n_parts])]
        kp = kq = 0
        for n in range(n_parts):
            npair, nqd = k_parts[n].shape[0], ki_parts[n].shape[0]
            loads += [pltpu.make_async_copy(k_parts[n], k_ref.at[pl.ds(kp, npair)], load_sem.at[3 * n]),
                      pltpu.make_async_copy(ki_parts[n], ki_ref.at[pl.ds(kq, nqd)], load_sem.at[3 * n + 1]),
                      pltpu.make_async_copy(vt_parts[n], vt_ref.at[pl.ds(kp, npair)], load_sem.at[3 * n + 2])]
            kp, kq = kp + npair, kq + nqd
        for c in loads:
            c.start()
        for c in loads:
            c.wait()

    nquad = (qb + 4) // 4
    lane_t = lax.broadcasted_iota(I32, (1, QB), 1) + qb * QB
    qchunk = lane_t // CHUNK
    sub = lax.broadcasted_iota(I32, (KQUAD, QB), 0)
    idx_scale = (IDX_DIM ** -0.5) * (IDX_HEADS ** -0.5)
    wrow = wit_ref[...] * idx_scale

    qi_all = jnp.concatenate([qi_ref[:, h * LANES:(h + 1) * LANES] for h in range(IDX_HEADS)], axis=0)

    def score_quad(j, carry, masked=False):
        d = _nt_dot(ki_ref[j], qi_all)
        acc = jnp.zeros((KQUAD, QB), F32)
        for h in range(IDX_HEADS):
            acc = acc + jnp.maximum(d[:, h * QB:(h + 1) * QB], 0.0) * wrow[h:h + 1, :]
        bits = pltpu.bitcast(acc, I32)
        skey = bits ^ ((bits >> 31) & 0x7FFFFFFF)
        if masked:
            skey = jnp.where((sub + j * KQUAD) // CHUNK <= qchunk, skey, INT_MIN)
        key_scr[j] = skey
        return carry

    lax.fori_loop(0, nquad - 1, score_quad, 0)
    score_quad(nquad - 1, 0, masked=True)

    nvis = (qchunk + 1) * CHUNK
    kk = jnp.minimum(topk, nvis)

    def count(pred):
        def body(j, acc8):
            m = jnp.where(pred(key_scr[j], j), 1, 0)
            return acc8 + m.reshape(KQUAD // SUBLANES, SUBLANES, QB).sum(axis=0)
        return lax.fori_loop(0, nquad, body, jnp.zeros((SUBLANES, QB), I32)).sum(axis=0, keepdims=True)

    def bit_step(i, carry):
        ans, cnt = carry
        cand = ans + lax.shift_left(jnp.int32(1), 31 - i)
        c = count(lambda blk, j: blk >= cand)
        take = c >= kk
        return jnp.where(take, cand, ans), jnp.where(take, c, cnt)

    thr, cnt = lax.fori_loop(0, 32, bit_step,
                             (jnp.full((1, QB), INT_MIN, I32), jnp.full((1, QB), 0, I32) + nquad * KQUAD))

    @pl.when(jnp.max(cnt - kk) > 0)
    def _():
        n_gt = count(lambda blk, j: blk > thr)
        need = kk - n_gt

        def idx_step(i, jbound):
            cand = jbound + lax.shift_left(jnp.int32(1), (seq.bit_length() - 1) - i)
            c = count(lambda blk, j: (blk == thr) & (sub + j * KQUAD < cand))
            return jnp.where(c <= need, cand, jbound)

        jbound = lax.fori_loop(0, seq.bit_length(), idx_step, jnp.zeros((1, QB), I32))

        def drop(j, carry):
            blk = key_scr[j]
            key_scr[j] = jnp.where((blk == thr) & (sub + j * KQUAD >= jbound), INT_MIN, blk)
            return carry

        lax.fori_loop(0, nquad, drop, 0)

    acc_refs, s_even, s_odd = scr[:ATT_HEADS], scr[ATT_HEADS], scr[ATT_HEADS + 1]
    for acc_ref in acc_refs:
        acc_ref[...] = jnp.zeros(acc_ref.shape, F32)
    head_row = lax.broadcasted_iota(I32, (ATT_HEADS, QB), 0)
    last_pair = k_ref.shape[0] - 1

    def qk(jp, s_ref):
        jc = jnp.minimum(jp, last_pair)
        for h in range(ATT_HEADS):
            s_ref[h] = _nt_dot(k_ref[jc, :, h * LANES:(h + 1) * LANES], q_ref[:, h * LANES:(h + 1) * LANES])

    def softmax_pv(jq, half, s_ref, carry, far):
        m_all, l_all = carry
        jp = 2 * jq + half
        keys = key_scr[jq, half * KPAIR:(half + 1) * KPAIR, :]
        masked = jnp.where(keys >= thr, 0.0, -jnp.inf)
        tile0 = jnp.clip(qb - 2 * jp, 0, n_near)
        tile1 = jnp.clip(qb - 2 * jp - 1, 0, n_near)
        for h in range(ATT_HEADS):
            rows = slice(h * ATT_HEAD_DIM, (h + 1) * ATT_HEAD_DIM)
            if far:
                s = s_ref[h] + masked
            else:
                s = s_ref[h] + jnp.concatenate([bias_ref[tile0, h], bias_ref[tile1, h]], axis=0) + masked
            m = m_all[h:h + 1, :]
            m_new = jnp.maximum(m, jnp.max(s, axis=0, keepdims=True))
            p = jnp.exp2(s - m_new)
            alpha = jnp.exp2(m - m_new)
            l_new = alpha * l_all[h:h + 1, :] + jnp.sum(p, axis=0, keepdims=True)
            m_all = jnp.where(head_row == h, m_new, m_all)
            l_all = jnp.where(head_row == h, l_new, l_all)
            acc_refs[h][...] = alpha * acc_refs[h][...] + _dot(vt_ref[jp, rows, :], p.astype(BF16))
        return m_all, l_all

    def att_quad(jq, carry, far):
        qk(2 * jq + 1, s_odd)
        carry = softmax_pv(jq, 0, s_even, carry, far)
        qk(2 * jq + 2, s_even)
        return softmax_pv(jq, 1, s_odd, carry, far)

    nfar = jnp.maximum(qb - n_near + 1, 0) // 4
    qk(0, s_even)
    carry = lax.fori_loop(0, nfar, functools.partial(att_quad, far=True),
                          (jnp.full((ATT_HEADS, QB), NEG_BIG, F32), jnp.zeros((ATT_HEADS, QB), F32)))
    _, l_all = lax.fori_loop(nfar, nquad, functools.partial(att_quad, far=False), carry)
    for h in range(ATT_HEADS):
        rows = slice(h * ATT_HEAD_DIM, (h + 1) * ATT_HEAD_DIM)
        att_scr[rows, :] = acc_refs[h][...] / l_all[h:h + 1, :]
    o_ref[...] = att_scr[...].T.astype(o_ref.dtype)


def _attention(q, qi, wit, k_parts, ki_parts, vt_parts, bias, S, n_near):
    Tq = q.shape[0]
    hp = ATT_HEADS * LANES
    aw = ATT_HEADS * ATT_HEAD_DIM
    topk = min(TOPK_MAX, S // 4)
    n_parts = len(k_parts)
    k4 = [k.reshape(-1, KPAIR, hp) for k in k_parts]
    ki4 = [ki.reshape(-1, KQUAD, LANES) for ki in ki_parts]
    npr = sum(k.shape[0] for k in k4)
    nqd = sum(ki.shape[0] for ki in ki4)
    assert npr * KPAIR == nqd * KQUAD and vt_parts[0].shape[1:] == (aw, KPAIR)
    qb0 = (npr * KPAIR - Tq) // QB
    return pl.pallas_call(
        functools.partial(_attn_kernel, seq=S, n_near=n_near, topk=topk, n_parts=n_parts, qb0=qb0),
        grid=(Tq // QB,),
        in_specs=[pl.BlockSpec((QB, hp), lambda i: (i, 0)),
                  pl.BlockSpec((QB, hp), lambda i: (i, 0)),
                  pl.BlockSpec((IDX_HEADS, QB), lambda i: (0, i))]
        + [pl.BlockSpec(memory_space=pl.ANY)] * (3 * n_parts + 1),
        out_specs=pl.BlockSpec((QB, aw), lambda i: (i, 0)),
        out_shape=jax.ShapeDtypeStruct((Tq, aw), BF16),
        scratch_shapes=[pltpu.VMEM((npr, KPAIR, hp), BF16), pltpu.VMEM((nqd, KQUAD, LANES), BF16),
                        pltpu.VMEM((npr, aw, KPAIR), BF16), pltpu.VMEM((n_near + 1, ATT_HEADS, QB, QB), F32),
                        pltpu.SemaphoreType.DMA((3 * n_parts + 1,)),
                        pltpu.VMEM((nqd, KQUAD, QB), I32), pltpu.VMEM((aw, QB), F32)]
        + [pltpu.VMEM((ATT_HEAD_DIM, QB), F32) for _ in range(ATT_HEADS)]
        + [pltpu.VMEM((ATT_HEADS, KPAIR, QB), F32) for _ in range(2)],
        compiler_params=_params(1),
        name="dsa_attention",
    )(q, qi, wit, *k4, *ki4, *vt_parts, bias)


HALO = 32


def _mix_kernel(att_ref, u_ref, halo_ref, hist_ref, gate_ref, x_ref, cw_ref, cb_ref, lng_ref, lnb_ref, wao_ref,
                wco_ref, wout_ref, gffn_ref, x1_ref, h2_ref, ext_scr, *, tm):
    i = pl.program_id(0)
    ext_scr[0:HALO, :] = jnp.where(i == 0, hist_ref[...], halo_ref[...])
    ext_scr[HALO:HALO + tm, :] = u_ref[...]
    y = jnp.zeros((tm, CONV_CH), F32)
    for j in range(CONV_WIDTH):
        y = y + cw_ref[j:j + 1, :] * ext_scr[pl.ds(HALO - (CONV_WIDTH - 1) + j, tm), :]
    y = y + cb_ref[...]
    mu = jnp.mean(y, axis=-1, keepdims=True)
    yc = y - mu
    yn = yc * lax.rsqrt(jnp.mean(yc * yc, axis=-1, keepdims=True) + EPS) * lng_ref[...] + lnb_ref[...]
    z = yn * jax.nn.sigmoid(yn)
    y_conv = _dot(z.astype(BF16), wco_ref[...])
    y_att = _dot(att_ref[...], wao_ref[...])
    d = y_att.shape[1]
    g = gate_ref[...]
    mixed = g[:, :d].astype(F32) * y_att + g[:, d:].astype(F32) * y_conv
    x1 = x_ref[...] + _dot(mixed.astype(BF16), wout_ref[...])
    x1_ref[...] = x1
    h2_ref[...] = x1 * lax.rsqrt(jnp.mean(x1 * x1, axis=-1, keepdims=True) + EPS) * gffn_ref[...]


def _mix(att, u, history, gate, x2, conv_w, conv_b, ln_g, ln_b, w_att_out, w_conv_out, w_out, ffn_g, tm=256):
    T, D = x2.shape
    aw = att.shape[1]
    const = lambda *shape: _single(shape, lambda i: (0,) * len(shape))
    hb = tm // HALO
    return pl.pallas_call(
        functools.partial(_mix_kernel, tm=tm),
        grid=(T // tm,),
        in_specs=[pl.BlockSpec((tm, aw), lambda i: (i, 0)),
                  pl.BlockSpec((tm, CONV_CH), lambda i: (i, 0)),
                  pl.BlockSpec((HALO, CONV_CH), lambda i: (jnp.maximum(i * hb - 1, 0), 0)),
                  const(HALO, CONV_CH),
                  pl.BlockSpec((tm, N_BRANCH * D), lambda i: (i, 0)),
                  pl.BlockSpec((tm, D), lambda i: (i, 0)),
                  const(CONV_WIDTH, CONV_CH), const(1, CONV_CH), const(1, CONV_CH), const(1, CONV_CH),
                  const(aw, D), const(CONV_CH, D), const(D, D), const(1, D)],
        out_specs=[pl.BlockSpec((tm, D), lambda i: (i, 0)), pl.BlockSpec((tm, D), lambda i: (i, 0))],
        out_shape=[jax.ShapeDtypeStruct((T, D), F32), jax.ShapeDtypeStruct((T, D), F32)],
        scratch_shapes=[pltpu.VMEM((HALO + tm, CONV_CH), F32)],
        compiler_params=_params(1),
        name="mix_out_proj",
    )(att, u, u, history, gate, x2, conv_w.reshape(CONV_WIDTH, CONV_CH), conv_b.reshape(1, CONV_CH),
      ln_g.reshape(1, CONV_CH), ln_b.reshape(1, CONV_CH), w_att_out.astype(BF16), w_conv_out.astype(BF16),
      w_out.astype(BF16), ffn_g.reshape(1, D))


def _top_rows(sc, k, payload=None):
    rows = sc.shape[0]
    iota = lax.broadcasted_iota(I32, sc.shape, 0)
    out_row = lax.broadcasted_iota(I32, (k, sc.shape[1]), 0)
    vals = jnp.zeros((k, sc.shape[1]), F32)
    idxs = jnp.zeros((k, sc.shape[1]), I32)
    for r in range(k):
        m = jnp.max(sc, axis=0, keepdims=True)
        idx = jnp.min(jnp.where(sc == m, iota, rows), axis=0, keepdims=True)
        hit = iota == idx
        rec = idx if payload is None else jnp.max(jnp.where(hit, payload, -1), axis=0, keepdims=True)
        vals = jnp.where(out_row == r, m, vals)
        idxs = jnp.where(out_row == r, rec, idxs)
        sc = jnp.where(hit, -jnp.inf, sc)
    return vals, idxs


def _route_kernel(h2_ref, wq_ref, sk_ref, e_ref, g_ref, qt_scr, et_scr, gt_scr, *, tm):
    half = N_KEYS
    qt_scr[...] = _nt_dot(wq_ref[...], h2_ref[...].astype(BF16)).astype(BF16)

    def head(h, carry):
        tops = []
        for c in range(2):
            row0 = pl.multiple_of((h * 2 + c) * half, half)
            sc = _dot(sk_ref[h * 2 + c], qt_scr[pl.ds(row0, half), :])
            tops.append(_top_rows(sc, PEER_TOPK))
        (a, ia), (b, ib) = tops
        k = PEER_TOPK
        g = SUBLANES
        assert (k // 2) % g == 0 and k % g == 0
        row = lax.broadcasted_iota(I32, (g, a.shape[1]), 0)
        cand_parts, cidx_parts = [], []

        def add(av, iav, bv, ibv, valid_rows):
            s = av + bv
            if valid_rows < g:
                s = jnp.where(row < valid_rows, s, -jnp.inf)
            cand_parts.append(s)
            cidx_parts.append(iav * N_KEYS + ibv)

        for i in range(k):
            nj = k // (i + 1)
            if nj >= g:
                for j0 in range(0, nj, g):
                    add(a[i:i + 1, :], ia[i:i + 1, :], b[j0:j0 + g, :], ib[j0:j0 + g, :], g)
            elif nj > 1:
                add(a[i:i + 1, :], ia[i:i + 1, :], b[0:g, :], ib[0:g, :], nj)
            elif i % g == 0:
                add(a[i:i + g, :], ia[i:i + g, :], b[0:1, :], ib[0:1, :], g)
        cand = jnp.concatenate(cand_parts, axis=0)
        cidx = jnp.concatenate(cidx_parts, axis=0)
        ts, te = _top_rows(cand, PEER_TOPK, payload=cidx)
        ex = jnp.exp(ts - ts[0:1, :])
        gate = ex / jnp.sum(ex, axis=0, keepdims=True)
        r0 = pl.multiple_of(h * PEER_TOPK, PEER_TOPK)
        et_scr[pl.ds(r0, PEER_TOPK), :] = te
        gt_scr[pl.ds(r0, PEER_TOPK), :] = gate
        return carry

    lax.fori_loop(0, PEER_HEADS, head, 0)
    e_ref[...] = et_scr[...].T
    g_ref[...] = gt_scr[...].T


def _peer_route(h2, w_peer_q, sub_keys, tm=256):
    T, D = h2.shape
    nsel = PEER_HEADS * PEER_TOPK
    qd = w_peer_q.shape[1]
    half = sub_keys.shape[-1]
    skb = sub_keys.reshape(PEER_HEADS * 2, N_KEYS, half).astype(BF16)
    return pl.pallas_call(
        functools.partial(_route_kernel, tm=tm),
        grid=(T // tm,),
        in_specs=[pl.BlockSpec((tm, D), lambda i: (i, 0)),
                  _single((qd, D), lambda i: (0, 0)),
                  _single((PEER_HEADS * 2, N_KEYS, half), lambda i: (0, 0, 0))],
        out_specs=[pl.BlockSpec((tm, nsel), lambda i: (i, 0)), pl.BlockSpec((tm, nsel), lambda i: (i, 0))],
        out_shape=[jax.ShapeDtypeStruct((T, nsel), I32), jax.ShapeDtypeStruct((T, nsel), F32)],
        scratch_shapes=[pltpu.VMEM((qd, tm), BF16), pltpu.VMEM((nsel, tm), I32), pltpu.VMEM((nsel, tm), F32)],
        compiler_params=_params(1),
        name="peer_route",
    )(h2, w_peer_q.T.astype(BF16), skb)


TOK = 8


def _gelu_tanh(x):
    return 0.5 * x * (1.0 + jnp.tanh(math.sqrt(2.0 / math.pi) * (x + 0.044715 * (x * x * x))))


def _sublane_sums(ps):
    sub = lax.broadcasted_iota(I32, (SUBLANES, LANES), 0)
    lvl, stride = list(ps), SUBLANES // 2
    while len(lvl) > 1:
        half = len(lvl) // 2
        low = (sub & stride) == 0
        nxt = []
        for n in range(half):
            a, b = lvl[n], lvl[n + half]
            nxt.append(jnp.where(low, a + pltpu.roll(a, SUBLANES - stride, 0), b + pltpu.roll(b, stride, 0)))
        lvl, stride = nxt, stride // 2
    return lvl[0]


def _pack_uv(u, v):
    ne, d = u.shape
    hi = lax.bitcast_convert_type(u.astype(BF16), jnp.uint16).astype(jnp.uint32)
    lo = lax.bitcast_convert_type(v.astype(BF16), jnp.uint16).astype(jnp.uint32)
    return ((hi << 16) | lo).reshape(ne, d // LANES, LANES)


def _u_of(word):
    return pltpu.bitcast(word & jnp.uint32(0xFFFF0000), F32)


def _v_of(word):
    return pltpu.bitcast(word << 16, F32)


def _expert_group(row, h2_ref, x1_ref, gate_ref, o_ref, abc_scr, base, nsel, before_dots=None, before_sum=None):
    dsub = h2_ref.shape[1]
    lane_id = lax.broadcasted_iota(I32, (nsel, LANES), 1)
    st = jnp.zeros((nsel, LANES), F32)
    for t in range(TOK):
        if before_dots is not None:
            before_dots(t)
        xt = h2_ref[base + t]
        qs = []
        for g in range(nsel // SUBLANES):
            ps = [_u_of(row(t, g * SUBLANES + k)) * xt for k in range(SUBLANES)]
            qs.append(_sublane_sums(ps))
        q = jnp.concatenate(qs, axis=0)
        st = jnp.where(lane_id == t, jnp.sum(q, axis=1, keepdims=True), st)
    g8 = gate_ref[base:base + TOK, :]
    gt = jnp.concatenate([g8, jnp.zeros((nsel - TOK, nsel), F32)], axis=0).T
    at = _gelu_tanh(st) * gt

    for t in range(TOK):
        if before_sum is not None:
            before_sum(t)
        abc_scr[t] = jnp.broadcast_to(at[:, t:t + 1], (nsel, LANES))
        accs = [jnp.zeros((dsub, LANES), F32) for _ in range(4)]
        for n in range(nsel):
            accs[n % 4] = accs[n % 4] + abc_scr[t, n:n + 1, :] * _v_of(row(t, n))
        o_ref[base + t] = x1_ref[base + t] + ((accs[0] + accs[1]) + (accs[2] + accs[3]))


def _expert_kernel(idxc_ref, idxn_ref, h2_ref, x1_ref, gate_ref, uv_ref, *rest, nsel):
    o_ref, buf, sem, abc_scr = rest[-4:]
    i = pl.program_id(0)
    nsteps = pl.num_programs(0)
    rows = TOK * nsel

    def issue_token(idx_ref, tok, s, t, n0=0, n1=nsel):
        for n in range(n0, n1):
            pltpu.make_async_copy(uv_ref.at[idx_ref[tok, n]], buf.at[s, t * nsel + n],
                                  sem.at[s]).start(priority=n % 2)

    def wait_slot(s):
        pltpu.make_async_copy(uv_ref.at[pl.ds(0, rows)], buf.at[s], sem.at[s]).wait()

    @pl.when(i == 0)
    def _():
        for t in range(TOK):
            issue_token(idxc_ref, t, 0, t)

    n_early = (3 * nsel) // 4

    for grp in range(2):
        wait_slot(grp)
        nxt_idx, nxt_tok, nxt_slot = (idxc_ref, TOK, 1) if grp == 0 else (idxn_ref, 0, 0)
        _expert_group(lambda t, n, grp=grp: buf[grp, t * nsel + n], h2_ref, x1_ref, gate_ref, o_ref, abc_scr,
                      grp * TOK, nsel,
                      before_dots=lambda t: issue_token(nxt_idx, nxt_tok + t, nxt_slot, t, 0, n_early),
                      before_sum=lambda t: issue_token(nxt_idx, nxt_tok + t, nxt_slot, t, n_early, nsel))

    @pl.when(i == nsteps - 1)
    def _():
        wait_slot(0)


def _expert_staged_kernel(rows_ref, h2_ref, x1_ref, gate_ref, prev_ref, o_ref, abc_scr, *, nsel):
    del prev_ref
    for grp in range(h2_ref.shape[0] // TOK):
        _expert_group(lambda t, n, grp=grp: rows_ref[(grp * TOK + t) * nsel + n], h2_ref, x1_ref, gate_ref, o_ref,
                      abc_scr, grp * TOK, nsel)


SC_CORES = 2
SC_SUBCORES = 16
SC_CHUNK = 16
SC_NBUF = 7
SC_SLAB = 2048


def _sc_gather(table, idx):
    n = idx.shape[0]
    nw = SC_CORES * SC_SUBCORES
    per_w = n // nw
    nb = SC_NBUF
    assert n % nw == 0 and per_w % SC_SLAB == 0 and SC_SLAB % SC_CHUNK == 0
    nslab, nchunk = per_w // SC_SLAB, SC_SLAB // SC_CHUNK
    assert nchunk >= nb
    row_shape = table.shape[1:]
    mesh = plsc.VectorSubcoreMesh(core_axis_name="c", subcore_axis_name="s")

    @functools.partial(
        pl.kernel, mesh=mesh, out_type=jax.ShapeDtypeStruct((n,) + row_shape, table.dtype),
        scratch_types=[pltpu.VMEM((SC_SLAB,), jnp.int32)]
        + [pltpu.VMEM((SC_CHUNK,) + row_shape, table.dtype) for _ in range(nb)]
        + [pltpu.SemaphoreType.DMA for _ in range(2 * nb)])
    def gather(table_hbm, idx_hbm, out_hbm, idx_v, *scr):
        wid = lax.axis_index("s") * SC_CORES + lax.axis_index("c")
        bufs, gsem, wsem = scr[:nb], scr[nb:2 * nb], scr[2 * nb:]

        def gather_copy(c, b):
            return pltpu.make_async_copy(table_hbm.at[idx_v.at[pl.ds(c * SC_CHUNK, SC_CHUNK)]], bufs[b], gsem[b])

        @pl.loop(0, nslab)
        def _(sl):
            base = wid * per_w + sl * SC_SLAB
            pltpu.sync_copy(idx_hbm.at[pl.ds(base, SC_SLAB)], idx_v)

            def write_copy(c, b):
                return pltpu.make_async_copy(bufs[b], out_hbm.at[pl.ds(base + c * SC_CHUNK, SC_CHUNK)], wsem[b])

            for b in range(nb - 1):
                gather_copy(b, b).start()

            @pl.loop(0, -(-nchunk // nb) * nb, step=nb)
            def _(c0):
                for b in range(nb):
                    c = c0 + b
                    prev = (b - 1) % nb

                    @pl.when(c < nchunk)
                    def _():
                        gather_copy(c, b).wait()
                        write_copy(c, b).start()

                    @pl.when((c >= 1) & (c < nchunk))
                    def _():
                        write_copy(c - 1, prev).wait()

                    @pl.when(c + nb - 1 < nchunk)
                    def _():
                        gather_copy(c + nb - 1, prev).start()

            write_copy(nchunk - 1, (nchunk - 1) % nb).wait()

    return gather(table, idx)


def _experts_dma(eidx, gates, h2, x1, uv, ta, after=()):
    T, dsub, _ = x1.shape
    nsel = eidx.shape[1]
    tb = 2 * TOK
    nsteps = ta // tb
    return pl.pallas_call(
        functools.partial(_expert_kernel, nsel=nsel),
        grid=(nsteps,),
        in_specs=[pl.BlockSpec((tb, nsel), lambda i: (i, 0), memory_space=pltpu.SMEM),
                  pl.BlockSpec((tb, nsel), lambda i: (jnp.minimum(i + 1, nsteps - 1), 0), memory_space=pltpu.SMEM),
                  pl.BlockSpec((tb, dsub, LANES), lambda i: (i, 0, 0)),
                  pl.BlockSpec((tb, dsub, LANES), lambda i: (i, 0, 0)),
                  pl.BlockSpec((tb, nsel), lambda i: (i, 0)),
                  pl.BlockSpec(memory_space=pl.ANY)] + [pl.BlockSpec(memory_space=pl.ANY)] * len(after),
        out_specs=pl.BlockSpec((tb, dsub, LANES), lambda i: (i, 0, 0)),
        out_shape=jax.ShapeDtypeStruct((T, dsub, LANES), F32),
        scratch_shapes=[pltpu.VMEM((2, TOK * nsel, dsub, LANES), jnp.uint32), pltpu.SemaphoreType.DMA((2,)),
                        pltpu.VMEM((TOK, nsel, LANES), F32)],
        compiler_params=_params(1),
        name="peer_experts",
    )(eidx, eidx, h2, x1, gates, uv, *after)


def _experts_staged(rows, gates, h2, x1, out, ta):
    T, dsub, _ = x1.shape
    nsel = gates.shape[1]
    tb = 2 * TOK
    first = ta // tb
    tok = lambda i: (first + i, 0, 0)
    return pl.pallas_call(
        functools.partial(_expert_staged_kernel, nsel=nsel),
        grid=((T - ta) // tb,),
        in_specs=[pl.BlockSpec((tb * nsel, dsub, LANES), lambda i: (i, 0, 0)),
                  pl.BlockSpec((tb, dsub, LANES), tok),
                  pl.BlockSpec((tb, dsub, LANES), tok),
                  pl.BlockSpec((tb, nsel), lambda i: (first + i, 0)),
                  pl.BlockSpec(memory_space=pl.ANY)],
        out_specs=pl.BlockSpec((tb, dsub, LANES), tok),
        out_shape=jax.ShapeDtypeStruct((T, dsub, LANES), F32),
        scratch_shapes=[pltpu.VMEM((TOK, nsel, LANES), F32)],
        input_output_aliases={4: 0},
        compiler_params=_params(1),
        name="peer_experts_staged",
    )(rows, h2, x1, gates, out)


def _staged_tokens(T, use_sc):
    unit = SC_CORES * SC_SUBCORES * SC_SLAB // (PEER_HEADS * PEER_TOPK)
    return int(T * use_sc) // unit * unit


def _peer_gather_start(eidx, uv, use_sc):
    T, nsel = eidx.shape
    ts = _staged_tokens(T, use_sc)
    return _sc_gather(uv, eidx[T - ts:].reshape(ts * nsel)) if ts else None


def _peer_dma_part(eidx, gates, h2, x1, uv, use_sc, after):
    T, D = x1.shape
    dsub = D // LANES
    ta = T - _staged_tokens(T, use_sc)
    h3, x3 = h2.reshape(T, dsub, LANES), x1.reshape(T, dsub, LANES)
    return _experts_dma(eidx, gates, h3, x3, uv, ta, after) if ta else jnp.zeros_like(x3)


def _peer_finish(rows, out, gates, h2, x1, use_sc):
    T, D = x1.shape
    dsub = D // LANES
    if rows is not None:
        out = _experts_staged(rows, gates, h2.reshape(T, dsub, LANES), x1.reshape(T, dsub, LANES), out,
                              T - _staged_tokens(T, use_sc))
    return out.reshape(T, D)


def _ple_kernel(x_ref, p_ref, g_ref, wg_ref, wp_ref, o_ref):
    x = x_ref[...]
    h = x * lax.rsqrt(jnp.mean(x * x, axis=-1, keepdims=True) + EPS) * g_ref[...]
    gate = jax.nn.sigmoid(_dot(h.astype(BF16), wg_ref[...]))
    o_ref[...] = x + gate * _dot(p_ref[...].astype(BF16), wp_ref[...])


def _ple(x2, p2, ple_g, w_gate, w_proj, tm=512):
    T, D = x2.shape
    pd = p2.shape[1]
    return pl.pallas_call(
        _ple_kernel,
        grid=(T // tm,),
        in_specs=[pl.BlockSpec((tm, D), lambda i: (i, 0)), pl.BlockSpec((tm, pd), lambda i: (i, 0)),
                  _single((1, D), lambda i: (0, 0)), _single((D, D), lambda i: (0, 0)),
                  _single((pd, D), lambda i: (0, 0))],
        out_specs=pl.BlockSpec((tm, D), lambda i: (i, 0)),
        out_shape=jax.ShapeDtypeStruct((T, D), F32),
        compiler_params=_params(1),
        name="ple",
    )(x2, p2, ple_g.reshape(1, D), w_gate.astype(BF16), w_proj.astype(BF16))


def kernel(x, p, rel_bias, attn_norm_g, w_in, b_gate, q_norm_g, k_norm_g, w_att_out, conv_w, conv_b, conv_ln_g,
           conv_ln_b, w_conv_out, w_out, ffn_norm_g, w_peer_q, peer_sub_keys, peer_u, peer_v, ple_norm_g,
           w_ple_gate, w_ple_proj):
    B, S, D = x.shape
    depth = w_in.shape[0]
    assert S % KQUAD == 0 and D % LANES == 0 and S % 512 == 0
    n_near = _num_near_tiles(S)
    bias = _bias_tiles(rel_bias, n_near)
    nsl = 2 if S % (2 * 2048) == 0 else 1
    sl = S // nsl
    xs = [x[b, s * sl:(s + 1) * sl] for b in range(B) for s in range(nsl)]
    units = [(b, s) for b in range(B) for s in range(nsl)]
    for i in range(depth):
        uv = _pack_uv(peer_u[i], peer_v[i])

        def finish(pending, i=i):
            rows, part, gates, h2, x1, (b, s), use_sc = pending
            x2 = _peer_finish(rows, part, gates, h2, x1, use_sc)
            return _ple(x2, p[i, b, s * sl:(s + 1) * sl], ple_norm_g[i], w_ple_gate[i], w_ple_proj[i])

        pending, outs, sc_rows = None, [], []
        keys, history = [], None
        for n, (b, s) in enumerate(units):
            use_sc = (1.0, 0.5, 0.0)[max(0, n - (len(units) - 3))]
            q, k, qi, ki, vt, wit, u, gate = _in_proj(xs[n], attn_norm_g[i], w_in[i], b_gate[i], q_norm_g[i],
                                                      k_norm_g[i])
            if s == 0:
                keys, history = [], jnp.zeros((HALO, CONV_CH), F32)
            keys.append((k, ki, vt))
            att = _attention(q, qi, wit, [kp[0] for kp in keys], [kp[1] for kp in keys], [kp[2] for kp in keys],
                             bias, S, n_near)
            x1, h2 = _mix(att, u, history, gate, xs[n], conv_w[i], conv_b[i], conv_ln_g[i], conv_ln_b[i],
                          w_att_out[i], w_conv_out[i], w_out[i], ffn_norm_g[i])
            history = u[sl - HALO:]
            eidx, gates = _peer_route(h2, w_peer_q[i], peer_sub_keys[i])
            rows = _peer_gather_start(eidx, uv, use_sc)
            if pending is not None:
                outs.append(finish(pending))
            if rows is not None:
                sc_rows.append(rows)
            part = _peer_dma_part(eidx, gates, h2, x1, uv, use_sc, tuple(sc_rows))
            pending = (rows, part, gates, h2, x1, (b, s), use_sc)
        outs.append(finish(pending))
        xs = outs
    return jnp.concatenate(xs, axis=0).reshape(B, S, D)
```

```python
import functools
import math

import numpy as np
import jax
import jax.numpy as jnp
from jax import lax
from jax.experimental import pallas as pl
from jax.experimental.pallas import tpu as pltpu
from jax.experimental.pallas import tpu_sc as plsc

CHUNK = 64
ATT_HEADS = 8
ATT_HEAD_DIM = 64
IDX_HEADS = 8
IDX_DIM = 64
TOPK_MAX = 256
REL_BUCKETS = 32
REL_MAX_DIST = 1024
CONV_CH = 512
CONV_WIDTH = 31
N_BRANCH = 2
PEER_HEADS = 8
N_KEYS = 128
PEER_TOPK = 16
EPS = 1e-6

LANES = 128
SUBLANES = 8
VMEM_LIMIT = 56 * 1024 * 1024

QB = 128
KPAIR = 2 * QB
KQUAD = 4 * QB
INT_MIN = -(2 ** 31)
NEG_BIG = -1e30
LOG2E = math.log2(math.e)

F32 = jnp.float32
BF16 = jnp.bfloat16
I32 = jnp.int32


def _nt_dot(a, b, precision=None):
    return lax.dot_general(a, b, (((1,), (1,)), ((), ())), precision=precision,
                           preferred_element_type=F32)


def _dot(a, b):
    return jnp.dot(a, b, preferred_element_type=F32)


def _single(shape, index_map):
    return pl.BlockSpec(shape, index_map)


def _params(n_grid_dims):
    return pltpu.CompilerParams(dimension_semantics=("arbitrary",) * n_grid_dims,
                                vmem_limit_bytes=VMEM_LIMIT)


def _t5_bucket_np(rel):
    half = REL_BUCKETS // 2
    max_exact = half // 2
    ret = np.where(rel > 0, half, 0)
    n = np.abs(rel)
    nf = np.maximum(n, 1).astype(np.float32)
    large = max_exact + (np.log(nf / np.float32(max_exact)) / np.float32(math.log(REL_MAX_DIST / max_exact))
                         * np.float32(half - max_exact)).astype(np.int32)
    large = np.minimum(large, half - 1)
    return ret + np.where(n < max_exact, n, large)


def _num_near_tiles(seq):
    n = np.arange(1, max(seq, 2 * REL_MAX_DIST) + 1)
    b = _t5_bucket_np(-n)
    sat = REL_BUCKETS // 2 - 1
    unsat = np.nonzero(b != sat)[0]
    n_sat = int(n[unsat[-1]]) + 1 if unsat.size else 1
    return -(-(n_sat + QB - 1) // QB)


def _bias_kernel(rb_ref, o_ref, *, n_near):
    d = pl.program_id(0)
    i = lax.broadcasted_iota(I32, (QB, QB), 0)
    j = lax.broadcasted_iota(I32, (QB, QB), 1)
    rel = i - j - d * QB
    rel = jnp.where(d >= n_near, -8 * REL_MAX_DIST, rel)
    half = REL_BUCKETS // 2
    max_exact = half // 2
    ret = jnp.where(rel > 0, half, 0)
    n = jnp.abs(rel)
    nf = jnp.maximum(n, 1).astype(F32)
    large = max_exact + (jnp.log(nf / max_exact) / math.log(REL_MAX_DIST / max_exact)
                         * (half - max_exact)).astype(I32)
    large = jnp.minimum(large, half - 1)
    bucket = ret + jnp.where(n < max_exact, n, large)
    for h in range(ATT_HEADS):
        acc = jnp.zeros((QB, QB), F32)
        for b in range(REL_BUCKETS):
            acc = jnp.where(bucket == b, rb_ref[b, h], acc)
        o_ref[0, h] = (acc - rb_ref[half - 1, h]) * LOG2E


def _bias_tiles(rel_bias, n_near):
    return pl.pallas_call(
        functools.partial(_bias_kernel, n_near=n_near),
        grid=(n_near + 1,),
        in_specs=[pl.BlockSpec(memory_space=pltpu.SMEM)],
        out_specs=pl.BlockSpec((1, ATT_HEADS, QB, QB), lambda d: (d, 0, 0, 0)),
        out_shape=jax.ShapeDtypeStruct((n_near + 1, ATT_HEADS, QB, QB), F32),
        compiler_params=_params(1),
        name="bias_tiles",
    )(rel_bias)


def _inproj_kernel(x_ref, g_ref, wa_ref, wvt_ref, wwit_ref, wglu_ref, wgate_ref, bgate_ref, gq_ref, gk_ref,
                   q_ref, k_ref, qi_ref, ki_ref, vt_ref, wit_ref, u_ref, gate_ref, *, tm):
    x = x_ref[...]
    h = x * lax.rsqrt(jnp.mean(x * x, axis=-1, keepdims=True) + EPS) * g_ref[...]
    hb = h.astype(BF16)
    hp = ATT_HEADS * LANES
    ya = _dot(hb, wa_ref[...])
    for h_i in range(ATT_HEADS):
        sl = slice(h_i * LANES, (h_i + 1) * LANES)
        qh = ya[:, h_i * LANES:(h_i + 1) * LANES]
        ms = jnp.sum(qh * qh, axis=-1, keepdims=True) * (1.0 / ATT_HEAD_DIM)
        q_ref[:, sl] = (qh * lax.rsqrt(ms + EPS) * gq_ref[...]).astype(BF16)
        kh = ya[:, hp + h_i * LANES:hp + (h_i + 1) * LANES]
        ms = jnp.sum(kh * kh, axis=-1, keepdims=True) * (1.0 / ATT_HEAD_DIM)
        k_ref[:, sl] = (kh * lax.rsqrt(ms + EPS) * gk_ref[...]).astype(BF16)
    qi_ref[...] = ya[:, 2 * hp:3 * hp].astype(BF16)
    ki_ref[...] = ya[:, 3 * hp:3 * hp + LANES].astype(BF16)
    vt = _nt_dot(wvt_ref[...], hb).astype(BF16)
    for c in range(tm // KPAIR):
        vt_ref[c] = vt[:, c * KPAIR:(c + 1) * KPAIR]
    wit_ref[...] = _nt_dot(wwit_ref[...], hb)
    glu = _dot(hb, wglu_ref[...])
    u_ref[...] = glu[:, :CONV_CH] * jax.nn.sigmoid(glu[:, CONV_CH:])
    gate_ref[...] = jax.nn.sigmoid(_dot(hb, wgate_ref[...]) + bgate_ref[...]).astype(BF16)


def _pad_heads(w, nh, hd):
    d = w.shape[0]
    w3 = w.reshape(d, nh, hd)
    w3 = jnp.pad(w3, ((0, 0), (0, 0), (0, LANES - hd)))
    return w3.reshape(d, nh * LANES)


def _in_proj(x2, attn_g, w_in, b_gate, q_g, k_g, tm=256):
    T, D = x2.shape
    aw = ATT_HEADS * ATT_HEAD_DIM
    iw = IDX_HEADS * IDX_DIM
    o = 0
    wq = w_in[:, o:o + aw]; o += aw
    wk = w_in[:, o:o + aw]; o += aw
    wv = w_in[:, o:o + aw]; o += aw
    wqi = w_in[:, o:o + iw]; o += iw
    wki = w_in[:, o:o + IDX_DIM]; o += IDX_DIM
    wwi = w_in[:, o:o + IDX_HEADS]; o += IDX_HEADS
    wglu = w_in[:, o:o + 2 * CONV_CH]; o += 2 * CONV_CH
    wgate = w_in[:, o:o + N_BRANCH * D]
    wa = jnp.concatenate([_pad_heads(wq, ATT_HEADS, ATT_HEAD_DIM), _pad_heads(wk, ATT_HEADS, ATT_HEAD_DIM),
                          _pad_heads(wqi, IDX_HEADS, IDX_DIM),
                          jnp.pad(wki, ((0, 0), (0, LANES - IDX_DIM)))], axis=1).astype(BF16)
    na = wa.shape[1]
    hp = ATT_HEADS * LANES
    pad_g = lambda g, s: jnp.pad(g * s, (0, LANES - ATT_HEAD_DIM)).reshape(1, LANES)
    gq = pad_g(q_g, ATT_HEAD_DIM ** -0.5 * LOG2E)
    gk = pad_g(k_g, 1.0)
    const = lambda *shape: _single(shape, lambda i: (0,) * len(shape))
    outs = pl.pallas_call(
        functools.partial(_inproj_kernel, tm=tm),
        grid=(T // tm,),
        in_specs=[pl.BlockSpec((tm, D), lambda i: (i, 0)), const(1, D), const(D, na), const(aw, D),
                  const(IDX_HEADS, D), const(D, 2 * CONV_CH), const(D, N_BRANCH * D), const(1, N_BRANCH * D),
                  const(1, LANES), const(1, LANES)],
        out_specs=[pl.BlockSpec((tm, hp), lambda i: (i, 0)), pl.BlockSpec((tm, hp), lambda i: (i, 0)),
                   pl.BlockSpec((tm, hp), lambda i: (i, 0)), pl.BlockSpec((tm, LANES), lambda i: (i, 0)),
                   pl.BlockSpec((tm // KPAIR, aw, KPAIR), lambda i: (i, 0, 0)),
                   pl.BlockSpec((IDX_HEADS, tm), lambda i: (0, i)),
                   pl.BlockSpec((tm, CONV_CH), lambda i: (i, 0)),
                   pl.BlockSpec((tm, N_BRANCH * D), lambda i: (i, 0))],
        out_shape=[jax.ShapeDtypeStruct((T, hp), BF16), jax.ShapeDtypeStruct((T, hp), BF16),
                   jax.ShapeDtypeStruct((T, hp), BF16), jax.ShapeDtypeStruct((T, LANES), BF16),
                   jax.ShapeDtypeStruct((T // KPAIR, aw, KPAIR), BF16),
                   jax.ShapeDtypeStruct((IDX_HEADS, T), F32),
                   jax.ShapeDtypeStruct((T, CONV_CH), F32),
                   jax.ShapeDtypeStruct((T, N_BRANCH * D), BF16)],
        compiler_params=_params(1),
        name="in_proj",
    )(x2, attn_g.reshape(1, D), wa, wv.T.astype(BF16), wwi.T.astype(BF16), wglu.astype(BF16),
      wgate.astype(BF16), b_gate.reshape(1, N_BRANCH * D), gq, gk)
    return outs


def _attn_kernel(q_ref, qi_ref, wit_ref, *rest, seq, n_near, topk, n_parts, qb0):
    k_parts, ki_parts, vt_parts = rest[:n_parts], rest[n_parts:2 * n_parts], rest[2 * n_parts:3 * n_parts]
    (bias_hbm, o_ref, k_ref, ki_ref, vt_ref, bias_ref, load_sem, key_scr, att_scr), scr = (
        rest[3 * n_parts:3 * n_parts + 9], rest[3 * n_parts + 9:])
    qb = qb0 + pl.program_id(0)

    @pl.when(pl.program_id(0) == 0)
    def _():
        loads = [pltpu.make_async_copy(bias_hbm, bias_ref, load_sem.at[3 * n_parts])]
        kp = kq = 0
        for n in range(n_parts):
            npair, nqd = k_parts[n].shape[0], ki_parts[n].shape[0]
            loads += [pltpu.make_async_copy(k_parts[n], k_ref.at[pl.ds(kp, npair)], load_sem.at[3 * n]),
                      pltpu.make_async_copy(ki_parts[n], ki_ref.at[pl.ds(kq, nqd)], load_sem.at[3 * n + 1]),
                      pltpu.make_async_copy(vt_parts[n], vt_ref.at[pl.ds(kp, npair)], load_sem.at[3 * n + 2])]
            kp, kq = kp + npair, kq + nqd
        for c in loads:
            c.start()
        for c in loads:
            c.wait()

    nquad = (qb + 4) // 4
    lane_t = lax.broadcasted_iota(I32, (1, QB), 1) + qb * QB
    qchunk = lane_t // CHUNK
    sub = lax.broadcasted_iota(I32, (KQUAD, QB), 0)
    idx_scale = (IDX_DIM ** -0.5) * (IDX_HEADS ** -0.5)
    wrow = wit_ref[...] * idx_scale

    qi_all = jnp.concatenate([qi_ref[:, h * LANES:(h + 1) * LANES] for h in range(IDX_HEADS)], axis=0)

    def score_quad(j, carry, masked=False):
        d = _nt_dot(ki_ref[j], qi_all)
        acc = jnp.zeros((KQUAD, QB), F32)
        for h in range(IDX_HEADS):
            acc = acc + jnp.maximum(d[:, h * QB:(h + 1) * QB], 0.0) * wrow[h:h + 1, :]
        bits = pltpu.bitcast(acc, I32)
        skey = bits ^ ((bits >> 31) & 0x7FFFFFFF)
        if masked:
            skey = jnp.where((sub + j * KQUAD) // CHUNK <= qchunk, skey, INT_MIN)
        key_scr[j] = skey
        return carry

    lax.fori_loop(0, nquad - 1, score_quad, 0)
    score_quad(nquad - 1, 0, masked=True)

    nvis = (qchunk + 1) * CHUNK
    kk = jnp.minimum(topk, nvis)

    def count(pred):
        def body(j, acc8):
            m = jnp.where(pred(key_scr[j], j), 1, 0)
            return acc8 + m.reshape(KQUAD // SUBLANES, SUBLANES, QB).sum(axis=0)
        return lax.fori_loop(0, nquad, body, jnp.zeros((SUBLANES, QB), I32)).sum(axis=0, keepdims=True)

    def bit_step(i, carry):
        ans, cnt = carry
        cand = ans + lax.shift_left(jnp.int32(1), 31 - i)
        c = count(lambda blk, j: blk >= cand)
        take = c >= kk
        return jnp.where(take, cand, ans), jnp.where(take, c, cnt)

    thr, cnt = lax.fori_loop(0, 32, bit_step,
                             (jnp.full((1, QB), INT_MIN, I32), jnp.full((1, QB), 0, I32) + nquad * KQUAD))

    @pl.when(jnp.max(cnt - kk) > 0)
    def _():
        n_gt = count(lambda blk, j: blk > thr)
        need = kk - n_gt

        def idx_step(i, jbound):
            cand = jbound + lax.shift_left(jnp.int32(1), (seq.bit_length() - 1) - i)
            c = count(lambda blk, j: (blk == thr) & (sub + j * KQUAD < cand))
            return jnp.where(c <= need, cand, jbound)

        jbound = lax.fori_loop(0, seq.bit_length(), idx_step, jnp.zeros((1, QB), I32))

        def drop(j, carry):
            blk = key_scr[j]
            key_scr[j] = jnp.where((blk == thr) & (sub + j * KQUAD >= jbound), INT_MIN, blk)
            return carry

        lax.fori_loop(0, nquad, drop, 0)

    acc_refs, s_even, s_odd = scr[:ATT_HEADS], scr[ATT_HEADS], scr[ATT_HEADS + 1]
    for acc_ref in acc_refs:
        acc_ref[...] = jnp.zeros(acc_ref.shape, F32)
    head_row = lax.broadcasted_iota(I32, (ATT_HEADS, QB), 0)
    last_pair = k_ref.shape[0] - 1

    def qk(jp, s_ref):
        jc = jnp.minimum(jp, last_pair)
        for h in range(ATT_HEADS):
            s_ref[h] = _nt_dot(k_ref[jc, :, h * LANES:(h + 1) * LANES], q_ref[:, h * LANES:(h + 1) * LANES])

    def softmax_pv(jq, half, s_ref, carry, far):
        m_all, l_all = carry
        jp = 2 * jq + half
        keys = key_scr[jq, half * KPAIR:(half + 1) * KPAIR, :]
        masked = jnp.where(keys >= thr, 0.0, -jnp.inf)
        tile0 = jnp.clip(qb - 2 * jp, 0, n_near)
        tile1 = jnp.clip(qb - 2 * jp - 1, 0, n_near)
        for h in range(ATT_HEADS):
            rows = slice(h * ATT_HEAD_DIM, (h + 1) * ATT_HEAD_DIM)
            if far:
                s = s_ref[h] + masked
            else:
                s = s_ref[h] + jnp.concatenate([bias_ref[tile0, h], bias_ref[tile1, h]], axis=0) + masked
            m = m_all[h:h + 1, :]
            m_new = jnp.maximum(m, jnp.max(s, axis=0, keepdims=True))
            p = jnp.exp2(s - m_new)
            alpha = jnp.exp2(m - m_new)
            l_new = alpha * l_all[h:h + 1, :] + jnp.sum(p, axis=0, keepdims=True)
            m_all = jnp.where(head_row == h, m_new, m_all)
            l_all = jnp.where(head_row == h, l_new, l_all)
            acc_refs[h][...] = alpha * acc_refs[h][...] + _dot(vt_ref[jp, rows, :], p.astype(BF16))
        return m_all, l_all

    def att_quad(jq, carry, far):
        qk(2 * jq + 1, s_odd)
        carry = softmax_pv(jq, 0, s_even, carry, far)
        qk(2 * jq + 2, s_even)
        return softmax_pv(jq, 1, s_odd, carry, far)

    nfar = jnp.maximum(qb - n_near + 1, 0) // 4
    qk(0, s_even)
    carry = lax.fori_loop(0, nfar, functools.partial(att_quad, far=True),
                          (jnp.full((ATT_HEADS, QB), NEG_BIG, F32), jnp.zeros((ATT_HEADS, QB), F32)))
    _, l_all = lax.fori_loop(nfar, nquad, functools.partial(att_quad, far=False), carry)
    for h in range(ATT_HEADS):
        rows = slice(h * ATT_HEAD_DIM, (h + 1) * ATT_HEAD_DIM)
        att_scr[rows, :] = acc_refs[h][...] / l_all[h:h + 1, :]
    o_ref[...] = att_scr[...].T.astype(o_ref.dtype)


def _attention(q, qi, wit, k_parts, ki_parts, vt_parts, bias, S, n_near):
    Tq = q.shape[0]
    hp = ATT_HEADS * LANES
    aw = ATT_HEADS * ATT_HEAD_DIM
    topk = min(TOPK_MAX, S // 4)
    n_parts = len(k_parts)
    k4 = [k.reshape(-1, KPAIR, hp) for k in k_parts]
    ki4 = [ki.reshape(-1, KQUAD, LANES) for ki in ki_parts]
    npr = sum(k.shape[0] for k in k4)
    nqd = sum(ki.shape[0] for ki in ki4)
    assert npr * KPAIR == nqd * KQUAD and vt_parts[0].shape[1:] == (aw, KPAIR)
    qb0 = (npr * KPAIR - Tq) // QB
    return pl.pallas_call(
        functools.partial(_attn_kernel, seq=S, n_near=n_near, topk=topk, n_parts=n_parts, qb0=qb0),
        grid=(Tq // QB,),
        in_specs=[pl.BlockSpec((QB, hp), lambda i: (i, 0)),
                  pl.BlockSpec((QB, hp), lambda i: (i, 0)),
                  pl.BlockSpec((IDX_HEADS, QB), lambda i: (0, i))]
        + [pl.BlockSpec(memory_space=pl.ANY)] * (3 * n_parts + 1),
        out_specs=pl.BlockSpec((QB, aw), lambda i: (i, 0)),
        out_shape=jax.ShapeDtypeStruct((Tq, aw), BF16),
        scratch_shapes=[pltpu.VMEM((npr, KPAIR, hp), BF16), pltpu.VMEM((nqd, KQUAD, LANES), BF16),
                        pltpu.VMEM((npr, aw, KPAIR), BF16), pltpu.VMEM((n_near + 1, ATT_HEADS, QB, QB), F32),
                        pltpu.SemaphoreType.DMA((3 * n_parts + 1,)),
                        pltpu.VMEM((nqd, KQUAD, QB), I32), pltpu.VMEM((aw, QB), F32)]
        + [pltpu.VMEM((ATT_HEAD_DIM, QB), F32) for _ in range(ATT_HEADS)]
        + [pltpu.VMEM((ATT_HEADS, KPAIR, QB), F32) for _ in range(2)],
        compiler_params=_params(1),
        name="dsa_attention",
    )(q, qi, wit, *k4, *ki4, *vt_parts, bias)


HALO = 32


def _mix_kernel(att_ref, u_ref, halo_ref, hist_ref, gate_ref, x_ref, cw_ref, cb_ref, lng_ref, lnb_ref, wao_ref,
                wco_ref, wout_ref, gffn_ref, x1_ref, h2_ref, ext_scr, *, tm):
    i = pl.program_id(0)
    ext_scr[0:HALO, :] = jnp.where(i == 0, hist_ref[...], halo_ref[...])
    ext_scr[HALO:HALO + tm, :] = u_ref[...]
    y = jnp.zeros((tm, CONV_CH), F32)
    for j in range(CONV_WIDTH):
        y = y + cw_ref[j:j + 1, :] * ext_scr[pl.ds(HALO - (CONV_WIDTH - 1) + j, tm), :]
    y = y + cb_ref[...]
    mu = jnp.mean(y, axis=-1, keepdims=True)
    yc = y - mu
    yn = yc * lax.rsqrt(jnp.mean(yc * yc, axis=-1, keepdims=True) + EPS) * lng_ref[...] + lnb_ref[...]
    z = yn * jax.nn.sigmoid(yn)
    y_conv = _dot(z.astype(BF16), wco_ref[...])
    y_att = _dot(att_ref[...], wao_ref[...])
    d = y_att.shape[1]
    g = gate_ref[...]
    mixed = g[:, :d].astype(F32) * y_att + g[:, d:].astype(F32) * y_conv
    x1 = x_ref[...] + _dot(mixed.astype(BF16), wout_ref[...])
    x1_ref[...] = x1
    h2_ref[...] = x1 * lax.rsqrt(jnp.mean(x1 * x1, axis=-1, keepdims=True) + EPS) * gffn_ref[...]


def _mix(att, u, history, gate, x2, conv_w, conv_b, ln_g, ln_b, w_att_out, w_conv_out, w_out, ffn_g, tm=256):
    T, D = x2.shape
    aw = att.shape[1]
    const = lambda *shape: _single(shape, lambda i: (0,) * len(shape))
    hb = tm // HALO
    return pl.pallas_call(
        functools.partial(_mix_kernel, tm=tm),
        grid=(T // tm,),
        in_specs=[pl.BlockSpec((tm, aw), lambda i: (i, 0)),
                  pl.BlockSpec((tm, CONV_CH), lambda i: (i, 0)),
                  pl.BlockSpec((HALO, CONV_CH), lambda i: (jnp.maximum(i * hb - 1, 0), 0)),
                  const(HALO, CONV_CH),
                  pl.BlockSpec((tm, N_BRANCH * D), lambda i: (i, 0)),
                  pl.BlockSpec((tm, D), lambda i: (i, 0)),
                  const(CONV_WIDTH, CONV_CH), const(1, CONV_CH), const(1, CONV_CH), const(1, CONV_CH),
                  const(aw, D), const(CONV_CH, D), const(D, D), const(1, D)],
        out_specs=[pl.BlockSpec((tm, D), lambda i: (i, 0)), pl.BlockSpec((tm, D), lambda i: (i, 0))],
        out_shape=[jax.ShapeDtypeStruct((T, D), F32), jax.ShapeDtypeStruct((T, D), F32)],
        scratch_shapes=[pltpu.VMEM((HALO + tm, CONV_CH), F32)],
        compiler_params=_params(1),
        name="mix_out_proj",
    )(att, u, u, history, gate, x2, conv_w.reshape(CONV_WIDTH, CONV_CH), conv_b.reshape(1, CONV_CH),
      ln_g.reshape(1, CONV_CH), ln_b.reshape(1, CONV_CH), w_att_out.astype(BF16), w_conv_out.astype(BF16),
      w_out.astype(BF16), ffn_g.reshape(1, D))


def _top_rows(sc, k, payload=None):
    rows = sc.shape[0]
    iota = lax.broadcasted_iota(I32, sc.shape, 0)
    out_row = lax.broadcasted_iota(I32, (k, sc.shape[1]), 0)
    vals = jnp.zeros((k, sc.shape[1]), F32)
    idxs = jnp.zeros((k, sc.shape[1]), I32)
    for r in range(k):
        m = jnp.max(sc, axis=0, keepdims=True)
        idx = jnp.min(jnp.where(sc == m, iota, rows), axis=0, keepdims=True)
        hit = iota == idx
        rec = idx if payload is None else jnp.max(jnp.where(hit, payload, -1), axis=0, keepdims=True)
        vals = jnp.where(out_row == r, m, vals)
        idxs = jnp.where(out_row == r, rec, idxs)
        sc = jnp.where(hit, -jnp.inf, sc)
    return vals, idxs


def _route_kernel(h2_ref, wq_ref, sk_ref, e_ref, g_ref, qt_scr, et_scr, gt_scr, *, tm):
    half = N_KEYS
    qt_scr[...] = _nt_dot(wq_ref[...], h2_ref[...].astype(BF16)).astype(BF16)

    def head(h, carry):
        tops = []
        for c in range(2):
            row0 = pl.multiple_of((h * 2 + c) * half, half)
            sc = _dot(sk_ref[h * 2 + c], qt_scr[pl.ds(row0, half), :])
            tops.append(_top_rows(sc, PEER_TOPK))
        (a, ia), (b, ib) = tops
        k = PEER_TOPK
        g = SUBLANES
        assert (k // 2) % g == 0 and k % g == 0
        row = lax.broadcasted_iota(I32, (g, a.shape[1]), 0)
        cand_parts, cidx_parts = [], []

        def add(av, iav, bv, ibv, valid_rows):
            s = av + bv
            if valid_rows < g:
                s = jnp.where(row < valid_rows, s, -jnp.inf)
            cand_parts.append(s)
            cidx_parts.append(iav * N_KEYS + ibv)

        for i in range(k):
            nj = k // (i + 1)
            if nj >= g:
                for j0 in range(0, nj, g):
                    add(a[i:i + 1, :], ia[i:i + 1, :], b[j0:j0 + g, :], ib[j0:j0 + g, :], g)
            elif nj > 1:
                add(a[i:i + 1, :], ia[i:i + 1, :], b[0:g, :], ib[0:g, :], nj)
            elif i % g == 0:
                add(a[i:i + g, :], ia[i:i + g, :], b[0:1, :], ib[0:1, :], g)
        cand = jnp.concatenate(cand_parts, axis=0)
        cidx = jnp.concatenate(cidx_parts, axis=0)
        ts, te = _top_rows(cand, PEER_TOPK, payload=cidx)
        ex = jnp.exp(ts - ts[0:1, :])
        gate = ex / jnp.sum(ex, axis=0, keepdims=True)
        r0 = pl.multiple_of(h * PEER_TOPK, PEER_TOPK)
        et_scr[pl.ds(r0, PEER_TOPK), :] = te
        gt_scr[pl.ds(r0, PEER_TOPK), :] = gate
        return carry

    lax.fori_loop(0, PEER_HEADS, head, 0)
    e_ref[...] = et_scr[...].T
    g_ref[...] = gt_scr[...].T


def _peer_route(h2, w_peer_q, sub_keys, tm=256):
    T, D = h2.shape
    nsel = PEER_HEADS * PEER_TOPK
    qd = w_peer_q.shape[1]
    half = sub_keys.shape[-1]
    skb = sub_keys.reshape(PEER_HEADS * 2, N_KEYS, half).astype(BF16)
    return pl.pallas_call(
        functools.partial(_route_kernel, tm=tm),
        grid=(T // tm,),
        in_specs=[pl.BlockSpec((tm, D), lambda i: (i, 0)),
                  _single((qd, D), lambda i: (0, 0)),
                  _single((PEER_HEADS * 2, N_KEYS, half), lambda i: (0, 0, 0))],
        out_specs=[pl.BlockSpec((tm, nsel), lambda i: (i, 0)), pl.BlockSpec((tm, nsel), lambda i: (i, 0))],
        out_shape=[jax.ShapeDtypeStruct((T, nsel), I32), jax.ShapeDtypeStruct((T, nsel), F32)],
        scratch_shapes=[pltpu.VMEM((qd, tm), BF16), pltpu.VMEM((nsel, tm), I32), pltpu.VMEM((nsel, tm), F32)],
        compiler_params=_params(1),
        name="peer_route",
    )(h2, w_peer_q.T.astype(BF16), skb)


TOK = 8


def _gelu_tanh(x):
    return 0.5 * x * (1.0 + jnp.tanh(math.sqrt(2.0 / math.pi) * (x + 0.044715 * (x * x * x))))


def _sublane_sums(ps):
    sub = lax.broadcasted_iota(I32, (SUBLANES, LANES), 0)
    lvl, stride = list(ps), SUBLANES // 2
    while len(lvl) > 1:
        half = len(lvl) // 2
        low = (sub & stride) == 0
        nxt = []
        for n in range(half):
            a, b = lvl[n], lvl[n + half]
            nxt.append(jnp.where(low, a + pltpu.roll(a, SUBLANES - stride, 0), b + pltpu.roll(b, stride, 0)))
        lvl, stride = nxt, stride // 2
    return lvl[0]


def _pack_uv(u, v):
    ne, d = u.shape
    hi = lax.bitcast_convert_type(u.astype(BF16), jnp.uint16).astype(jnp.uint32)
    lo = lax.bitcast_convert_type(v.astype(BF16), jnp.uint16).astype(jnp.uint32)
    return ((hi << 16) | lo).reshape(ne, d // LANES, LANES)


def _u_of(word):
    return pltpu.bitcast(word & jnp.uint32(0xFFFF0000), F32)


def _v_of(word):
    return pltpu.bitcast(word << 16, F32)


def _expert_group(row, h2_ref, x1_ref, gate_ref, o_ref, abc_scr, base, nsel, before_dots=None, before_sum=None):
    dsub = h2_ref.shape[1]
    lane_id = lax.broadcasted_iota(I32, (nsel, LANES), 1)
    st = jnp.zeros((nsel, LANES), F32)
    for t in range(TOK):
        if before_dots is not None:
            before_dots(t)
        xt = h2_ref[base + t]
        qs = []
        for g in range(nsel // SUBLANES):
            ps = [_u_of(row(t, g * SUBLANES + k)) * xt for k in range(SUBLANES)]
            qs.append(_sublane_sums(ps))
        q = jnp.concatenate(qs, axis=0)
        st = jnp.where(lane_id == t, jnp.sum(q, axis=1, keepdims=True), st)
    g8 = gate_ref[base:base + TOK, :]
    gt = jnp.concatenate([g8, jnp.zeros((nsel - TOK, nsel), F32)], axis=0).T
    at = _gelu_tanh(st) * gt

    for t in range(TOK):
        if before_sum is not None:
            before_sum(t)
        abc_scr[t] = jnp.broadcast_to(at[:, t:t + 1], (nsel, LANES))
        accs = [jnp.zeros((dsub, LANES), F32) for _ in range(4)]
        for n in range(nsel):
            accs[n % 4] = accs[n % 4] + abc_scr[t, n:n + 1, :] * _v_of(row(t, n))
        o_ref[base + t] = x1_ref[base + t] + ((accs[0] + accs[1]) + (accs[2] + accs[3]))


def _expert_kernel(idxc_ref, idxn_ref, h2_ref, x1_ref, gate_ref, uv_ref, *rest, nsel):
    o_ref, buf, sem, abc_scr = rest[-4:]
    i = pl.program_id(0)
    nsteps = pl.num_programs(0)
    rows = TOK * nsel

    def issue_token(idx_ref, tok, s, t, n0=0, n1=nsel):
        for n in range(n0, n1):
            pltpu.make_async_copy(uv_ref.at[idx_ref[tok, n]], buf.at[s, t * nsel + n],
                                  sem.at[s]).start(priority=n % 2)

    def wait_slot(s):
        pltpu.make_async_copy(uv_ref.at[pl.ds(0, rows)], buf.at[s], sem.at[s]).wait()

    @pl.when(i == 0)
    def _():
        for t in range(TOK):
            issue_token(idxc_ref, t, 0, t)

    n_early = (3 * nsel) // 4

    for grp in range(2):
        wait_slot(grp)
        nxt_idx, nxt_tok, nxt_slot = (idxc_ref, TOK, 1) if grp == 0 else (idxn_ref, 0, 0)
        _expert_group(lambda t, n, grp=grp: buf[grp, t * nsel + n], h2_ref, x1_ref, gate_ref, o_ref, abc_scr,
                      grp * TOK, nsel,
                      before_dots=lambda t: issue_token(nxt_idx, nxt_tok + t, nxt_slot, t, 0, n_early),
                      before_sum=lambda t: issue_token(nxt_idx, nxt_tok + t, nxt_slot, t, n_early, nsel))

    @pl.when(i == nsteps - 1)
    def _():
        wait_slot(0)


def _expert_staged_kernel(rows_ref, h2_ref, x1_ref, gate_ref, prev_ref, o_ref, abc_scr, *, nsel):
    del prev_ref
    for grp in range(h2_ref.shape[0] // TOK):
        _expert_group(lambda t, n, grp=grp: rows_ref[(grp * TOK + t) * nsel + n], h2_ref, x1_ref, gate_ref, o_ref,
                      abc_scr, grp * TOK, nsel)


SC_CORES = 2
SC_SUBCORES = 16
SC_CHUNK = 16
SC_NBUF = 7
SC_SLAB = 2048


def _sc_gather(table, idx):
    n = idx.shape[0]
    nw = SC_CORES * SC_SUBCORES
    per_w = n // nw
    nb = SC_NBUF
    assert n % nw == 0 and per_w % SC_SLAB == 0 and SC_SLAB % SC_CHUNK == 0
    nslab, nchunk = per_w // SC_SLAB, SC_SLAB // SC_CHUNK
    assert nchunk >= nb
    row_shape = table.shape[1:]
    mesh = plsc.VectorSubcoreMesh(core_axis_name="c", subcore_axis_name="s")

    @functools.partial(
        pl.kernel, mesh=mesh, out_type=jax.ShapeDtypeStruct((n,) + row_shape, table.dtype),
        scratch_types=[pltpu.VMEM((SC_SLAB,), jnp.int32)]
        + [pltpu.VMEM((SC_CHUNK,) + row_shape, table.dtype) for _ in range(nb)]
        + [pltpu.SemaphoreType.DMA for _ in range(2 * nb)])
    def gather(table_hbm, idx_hbm, out_hbm, idx_v, *scr):
        wid = lax.axis_index("s") * SC_CORES + lax.axis_index("c")
        bufs, gsem, wsem = scr[:nb], scr[nb:2 * nb], scr[2 * nb:]

        def gather_copy(c, b):
            return pltpu.make_async_copy(table_hbm.at[idx_v.at[pl.ds(c * SC_CHUNK, SC_CHUNK)]], bufs[b], gsem[b])

        @pl.loop(0, nslab)
        def _(sl):
            base = wid * per_w + sl * SC_SLAB
            pltpu.sync_copy(idx_hbm.at[pl.ds(base, SC_SLAB)], idx_v)

            def write_copy(c, b):
                return pltpu.make_async_copy(bufs[b], out_hbm.at[pl.ds(base + c * SC_CHUNK, SC_CHUNK)], wsem[b])

            for b in range(nb - 1):
                gather_copy(b, b).start()

            @pl.loop(0, -(-nchunk // nb) * nb, step=nb)
            def _(c0):
                for b in range(nb):
                    c = c0 + b
                    prev = (b - 1) % nb

                    @pl.when(c < nchunk)
                    def _():
                        gather_copy(c, b).wait()
                        write_copy(c, b).start()

                    @pl.when((c >= 1) & (c < nchunk))
                    def _():
                        write_copy(c - 1, prev).wait()

                    @pl.when(c + nb - 1 < nchunk)
                    def _():
                        gather_copy(c + nb - 1, prev).start()

            write_copy(nchunk - 1, (nchunk - 1) % nb).wait()

    return gather(table, idx)


def _experts_dma(eidx, gates, h2, x1, uv, ta, after=()):
    T, dsub, _ = x1.shape
    nsel = eidx.shape[1]
    tb = 2 * TOK
    nsteps = ta // tb
    return pl.pallas_call(
        functools.partial(_expert_kernel, nsel=nsel),
        grid=(nsteps,),
        in_specs=[pl.BlockSpec((tb, nsel), lambda i: (i, 0), memory_space=pltpu.SMEM),
                  pl.BlockSpec((tb, nsel), lambda i: (jnp.minimum(i + 1, nsteps - 1), 0), memory_space=pltpu.SMEM),
                  pl.BlockSpec((tb, dsub, LANES), lambda i: (i, 0, 0)),
                  pl.BlockSpec((tb, dsub, LANES), lambda i: (i, 0, 0)),
                  pl.BlockSpec((tb, nsel), lambda i: (i, 0)),
                  pl.BlockSpec(memory_space=pl.ANY)] + [pl.BlockSpec(memory_space=pl.ANY)] * len(after),
        out_specs=pl.BlockSpec((tb, dsub, LANES), lambda i: (i, 0, 0)),
        out_shape=jax.ShapeDtypeStruct((T, dsub, LANES), F32),
        scratch_shapes=[pltpu.VMEM((2, TOK * nsel, dsub, LANES), jnp.uint32), pltpu.SemaphoreType.DMA((2,)),
                        pltpu.VMEM((TOK, nsel, LANES), F32)],
        compiler_params=_params(1),
        name="peer_experts",
    )(eidx, eidx, h2, x1, gates, uv, *after)


def _experts_staged(rows, gates, h2, x1, out, ta):
    T, dsub, _ = x1.shape
    nsel = gates.shape[1]
    tb = 4 * TOK
    first = ta // tb
    tok = lambda i: (first + i, 0, 0)
    return pl.pallas_call(
        functools.partial(_expert_staged_kernel, nsel=nsel),
        grid=((T - ta) // tb,),
        in_specs=[pl.BlockSpec((tb * nsel, dsub, LANES), lambda i: (i, 0, 0)),
                  pl.BlockSpec((tb, dsub, LANES), tok),
                  pl.BlockSpec((tb, dsub, LANES), tok),
                  pl.BlockSpec((tb, nsel), lambda i: (first + i, 0)),
                  pl.BlockSpec(memory_space=pl.ANY)],
        out_specs=pl.BlockSpec((tb, dsub, LANES), tok),
        out_shape=jax.ShapeDtypeStruct((T, dsub, LANES), F32),
        scratch_shapes=[pltpu.VMEM((TOK, nsel, LANES), F32)],
        input_output_aliases={4: 0},
        compiler_params=_params(1),
        name="peer_experts_staged",
    )(rows, h2, x1, gates, out)


def _staged_tokens(T, use_sc):
    unit = SC_CORES * SC_SUBCORES * SC_SLAB // (PEER_HEADS * PEER_TOPK)
    return int(T * use_sc) // unit * unit


def _peer_gather_start(eidx, uv, use_sc):
    T, nsel = eidx.shape
    ts = _staged_tokens(T, use_sc)
    return _sc_gather(uv, eidx[T - ts:].reshape(ts * nsel)) if ts else None


def _peer_dma_part(eidx, gates, h2, x1, uv, use_sc, after):
    T, D = x1.shape
    dsub = D // LANES
    ta = T - _staged_tokens(T, use_sc)
    h3, x3 = h2.reshape(T, dsub, LANES), x1.reshape(T, dsub, LANES)
    return _experts_dma(eidx, gates, h3, x3, uv, ta, after) if ta else jnp.zeros_like(x3)


def _peer_finish(rows, out, gates, h2, x1, use_sc):
    T, D = x1.shape
    dsub = D // LANES
    if rows is not None:
        out = _experts_staged(rows, gates, h2.reshape(T, dsub, LANES), x1.reshape(T, dsub, LANES), out,
                              T - _staged_tokens(T, use_sc))
    return out.reshape(T, D)


def _ple_kernel(x_ref, p_ref, g_ref, wg_ref, wp_ref, o_ref):
    x = x_ref[...]
    h = x * lax.rsqrt(jnp.mean(x * x, axis=-1, keepdims=True) + EPS) * g_ref[...]
    gate = jax.nn.sigmoid(_dot(h.astype(BF16), wg_ref[...]))
    o_ref[...] = x + gate * _dot(p_ref[...].astype(BF16), wp_ref[...])


def _ple(x2, p2, ple_g, w_gate, w_proj, tm=512):
    T, D = x2.shape
    pd = p2.shape[1]
    return pl.pallas_call(
        _ple_kernel,
        grid=(T // tm,),
        in_specs=[pl.BlockSpec((tm, D), lambda i: (i, 0)), pl.BlockSpec((tm, pd), lambda i: (i, 0)),
                  _single((1, D), lambda i: (0, 0)), _single((D, D), lambda i: (0, 0)),
                  _single((pd, D), lambda i: (0, 0))],
        out_specs=pl.BlockSpec((tm, D), lambda i: (i, 0)),
        out_shape=jax.ShapeDtypeStruct((T, D), F32),
        compiler_params=_params(1),
        name="ple",
    )(x2, p2, ple_g.reshape(1, D), w_gate.astype(BF16), w_proj.astype(BF16))


def kernel(x, p, rel_bias, attn_norm_g, w_in, b_gate, q_norm_g, k_norm_g, w_att_out, conv_w, conv_b, conv_ln_g,
           conv_ln_b, w_conv_out, w_out, ffn_norm_g, w_peer_q, peer_sub_keys, peer_u, peer_v, ple_norm_g,
           w_ple_gate, w_ple_proj):
    B, S, D = x.shape
    depth = w_in.shape[0]
    assert S % KQUAD == 0 and D % LANES == 0 and S % 512 == 0
    n_near = _num_near_tiles(S)
    bias = _bias_tiles(rel_bias, n_near)
    nsl = 2 if S % (2 * 2048) == 0 else 1
    sl = S // nsl
    xs = [x[b, s * sl:(s + 1) * sl] for b in range(B) for s in range(nsl)]
    units = [(b, s) for b in range(B) for s in range(nsl)]
    for i in range(depth):
        uv = _pack_uv(peer_u[i], peer_v[i])

        def finish(pending, i=i):
            rows, part, gates, h2, x1, (b, s), use_sc = pending
            x2 = _peer_finish(rows, part, gates, h2, x1, use_sc)
            return _ple(x2, p[i, b, s * sl:(s + 1) * sl], ple_norm_g[i], w_ple_gate[i], w_ple_proj[i])

        pending, outs, sc_rows = None, [], []
        keys, history = [], None
        for n, (b, s) in enumerate(units):
            use_sc = (1.0, 0.5, 0.0)[max(0, n - (len(units) - 3))]
            q, k, qi, ki, vt, wit, u, gate = _in_proj(xs[n], attn_norm_g[i], w_in[i], b_gate[i], q_norm_g[i],
                                                      k_norm_g[i])
            if s == 0:
                keys, history = [], jnp.zeros((HALO, CONV_CH), F32)
            keys.append((k, ki, vt))
            att = _attention(q, qi, wit, [kp[0] for kp in keys], [kp[1] for kp in keys], [kp[2] for kp in keys],
                             bias, S, n_near)
            x1, h2 = _mix(att, u, history, gate, xs[n], conv_w[i], conv_b[i], conv_ln_g[i], conv_ln_b[i],
                          w_att_out[i], w_conv_out[i], w_out[i], ffn_norm_g[i])
            history = u[sl - HALO:]
            eidx, gates = _peer_route(h2, w_peer_q[i], peer_sub_keys[i])
            rows = _peer_gather_start(eidx, uv, use_sc)
            if pending is not None:
                outs.append(finish(pending))
            if rows is not None:
                sc_rows.append(rows)
            part = _peer_dma_part(eidx, gates, h2, x1, uv, use_sc, tuple(sc_rows))
            pending = (rows, part, gates, h2, x1, (b, s), use_sc)
        outs.append(finish(pending))
        xs = outs
    return jnp.concatenate(xs, axis=0).reshape(B, S, D)
```

```python
import functools
import math

import numpy as np
import jax
import jax.numpy as jnp
from jax import lax
from jax.experimental import pallas as pl
from jax.experimental.pallas import tpu as pltpu
from jax.experimental.pallas import tpu_sc as plsc

CHUNK = 64
ATT_HEADS = 8
ATT_HEAD_DIM = 64
IDX_HEADS = 8
IDX_DIM = 64
TOPK_MAX = 256
REL_BUCKETS = 32
REL_MAX_DIST = 1024
CONV_CH = 512
CONV_WIDTH = 31
N_BRANCH = 2
PEER_HEADS = 8
N_KEYS = 128
PEER_TOPK = 16
EPS = 1e-6

LANES = 128
SUBLANES = 8
VMEM_LIMIT = 56 * 1024 * 1024

QB = 128
KPAIR = 2 * QB
KQUAD = 4 * QB
INT_MIN = -(2 ** 31)
NEG_BIG = -1e30
LOG2E = math.log2(math.e)

F32 = jnp.float32
BF16 = jnp.bfloat16
I32 = jnp.int32


def _nt_dot(a, b, precision=None):
    return lax.dot_general(a, b, (((1,), (1,)), ((), ())), precision=precision,
                           preferred_element_type=F32)


def _dot(a, b):
    return jnp.dot(a, b, preferred_element_type=F32)


def _single(shape, index_map):
    return pl.BlockSpec(shape, index_map)


def _params(n_grid_dims):
    return pltpu.CompilerParams(dimension_semantics=("arbitrary",) * n_grid_dims,
                                vmem_limit_bytes=VMEM_LIMIT)


def _t5_bucket_np(rel):
    half = REL_BUCKETS // 2
    max_exact = half // 2
    ret = np.where(rel > 0, half, 0)
    n = np.abs(rel)
    nf = np.maximum(n, 1).astype(np.float32)
    large = max_exact + (np.log(nf / np.float32(max_exact)) / np.float32(math.log(REL_MAX_DIST / max_exact))
                         * np.float32(half - max_exact)).astype(np.int32)
    large = np.minimum(large, half - 1)
    return ret + np.where(n < max_exact, n, large)


def _num_near_tiles(seq):
    n = np.arange(1, max(seq, 2 * REL_MAX_DIST) + 1)
    b = _t5_bucket_np(-n)
    sat = REL_BUCKETS // 2 - 1
    unsat = np.nonzero(b != sat)[0]
    n_sat = int(n[unsat[-1]]) + 1 if unsat.size else 1
    return -(-(n_sat + QB - 1) // QB)


def _bias_kernel(rb_ref, o_ref, *, n_near):
    d = pl.program_id(0)
    i = lax.broadcasted_iota(I32, (QB, QB), 0)
    j = lax.broadcasted_iota(I32, (QB, QB), 1)
    rel = i - j - d * QB
    rel = jnp.where(d >= n_near, -8 * REL_MAX_DIST, rel)
    half = REL_BUCKETS // 2
    max_exact = half // 2
    ret = jnp.where(rel > 0, half, 0)
    n = jnp.abs(rel)
    nf = jnp.maximum(n, 1).astype(F32)
    large = max_exact + (jnp.log(nf / max_exact) / math.log(REL_MAX_DIST / max_exact)
                         * (half - max_exact)).astype(I32)
    large = jnp.minimum(large, half - 1)
    bucket = ret + jnp.where(n < max_exact, n, large)
    for h in range(ATT_HEADS):
        acc = jnp.zeros((QB, QB), F32)
        for b in range(REL_BUCKETS):
            acc = jnp.where(bucket == b, rb_ref[b, h], acc)
        o_ref[0, h] = (acc - rb_ref[half - 1, h]) * LOG2E


def _bias_tiles(rel_bias, n_near):
    return pl.pallas_call(
        functools.partial(_bias_kernel, n_near=n_near),
        grid=(n_near + 1,),
        in_specs=[pl.BlockSpec(memory_space=pltpu.SMEM)],
        out_specs=pl.BlockSpec((1, ATT_HEADS, QB, QB), lambda d: (d, 0, 0, 0)),
        out_shape=jax.ShapeDtypeStruct((n_near + 1, ATT_HEADS, QB, QB), F32),
        compiler_params=_params(1),
        name="bias_tiles",
    )(rel_bias)


def _inproj_kernel(x_ref, g_ref, wa_ref, wvt_ref, wwit_ref, wglu_ref, wgate_ref, bgate_ref, gq_ref, gk_ref,
                   q_ref, k_ref, qi_ref, ki_ref, vt_ref, wit_ref, u_ref, gate_ref, *, tm):
    x = x_ref[...]
    h = x * lax.rsqrt(jnp.mean(x * x, axis=-1, keepdims=True) + EPS) * g_ref[...]
    hb = h.astype(BF16)
    hp = ATT_HEADS * LANES
    ya = _dot(hb, wa_ref[...])
    for h_i in range(ATT_HEADS):
        sl = slice(h_i * LANES, (h_i + 1) * LANES)
        qh = ya[:, h_i * LANES:(h_i + 1) * LANES]
        ms = jnp.sum(qh * qh, axis=-1, keepdims=True) * (1.0 / ATT_HEAD_DIM)
        q_ref[:, sl] = (qh * lax.rsqrt(ms + EPS) * gq_ref[...]).astype(BF16)
        kh = ya[:, hp + h_i * LANES:hp + (h_i + 1) * LANES]
        ms = jnp.sum(kh * kh, axis=-1, keepdims=True) * (1.0 / ATT_HEAD_DIM)
        k_ref[:, sl] = (kh * lax.rsqrt(ms + EPS) * gk_ref[...]).astype(BF16)
    qi_ref[...] = ya[:, 2 * hp:3 * hp].astype(BF16)
    ki_ref[...] = ya[:, 3 * hp:3 * hp + LANES].astype(BF16)
    vt = _nt_dot(wvt_ref[...], hb).astype(BF16)
    for c in range(tm // KPAIR):
        vt_ref[c] = vt[:, c * KPAIR:(c + 1) * KPAIR]
    wit_ref[...] = _nt_dot(wwit_ref[...], hb)
    glu = _dot(hb, wglu_ref[...])
    u_ref[...] = glu[:, :CONV_CH] * jax.nn.sigmoid(glu[:, CONV_CH:])
    gate_ref[...] = jax.nn.sigmoid(_dot(hb, wgate_ref[...]) + bgate_ref[...]).astype(BF16)


def _pad_heads(w, nh, hd):
    d = w.shape[0]
    w3 = w.reshape(d, nh, hd)
    w3 = jnp.pad(w3, ((0, 0), (0, 0), (0, LANES - hd)))
    return w3.reshape(d, nh * LANES)


def _in_proj(x2, attn_g, w_in, b_gate, q_g, k_g, tm=256):
    T, D = x2.shape
    aw = ATT_HEADS * ATT_HEAD_DIM
    iw = IDX_HEADS * IDX_DIM
    o = 0
    wq = w_in[:, o:o + aw]; o += aw
    wk = w_in[:, o:o + aw]; o += aw
    wv = w_in[:, o:o + aw]; o += aw
    wqi = w_in[:, o:o + iw]; o += iw
    wki = w_in[:, o:o + IDX_DIM]; o += IDX_DIM
    wwi = w_in[:, o:o + IDX_HEADS]; o += IDX_HEADS
    wglu = w_in[:, o:o + 2 * CONV_CH]; o += 2 * CONV_CH
    wgate = w_in[:, o:o + N_BRANCH * D]
    wa = jnp.concatenate([_pad_heads(wq, ATT_HEADS, ATT_HEAD_DIM), _pad_heads(wk, ATT_HEADS, ATT_HEAD_DIM),
                          _pad_heads(wqi, IDX_HEADS, IDX_DIM),
                          jnp.pad(wki, ((0, 0), (0, LANES - IDX_DIM)))], axis=1).astype(BF16)
    na = wa.shape[1]
    hp = ATT_HEADS * LANES
    pad_g = lambda g, s: jnp.pad(g * s, (0, LANES - ATT_HEAD_DIM)).reshape(1, LANES)
    gq = pad_g(q_g, ATT_HEAD_DIM ** -0.5 * LOG2E)
    gk = pad_g(k_g, 1.0)
    const = lambda *shape: _single(shape, lambda i: (0,) * len(shape))
    outs = pl.pallas_call(
        functools.partial(_inproj_kernel, tm=tm),
        grid=(T // tm,),
        in_specs=[pl.BlockSpec((tm, D), lambda i: (i, 0)), const(1, D), const(D, na), const(aw, D),
                  const(IDX_HEADS, D), const(D, 2 * CONV_CH), const(D, N_BRANCH * D), const(1, N_BRANCH * D),
                  const(1, LANES), const(1, LANES)],
        out_specs=[pl.BlockSpec((tm, hp), lambda i: (i, 0)), pl.BlockSpec((tm, hp), lambda i: (i, 0)),
                   pl.BlockSpec((tm, hp), lambda i: (i, 0)), pl.BlockSpec((tm, LANES), lambda i: (i, 0)),
                   pl.BlockSpec((tm // KPAIR, aw, KPAIR), lambda i: (i, 0, 0)),
                   pl.BlockSpec((IDX_HEADS, tm), lambda i: (0, i)),
                   pl.BlockSpec((tm, CONV_CH), lambda i: (i, 0)),
                   pl.BlockSpec((tm, N_BRANCH * D), lambda i: (i, 0))],
        out_shape=[jax.ShapeDtypeStruct((T, hp), BF16), jax.ShapeDtypeStruct((T, hp), BF16),
                   jax.ShapeDtypeStruct((T, hp), BF16), jax.ShapeDtypeStruct((T, LANES), BF16),
                   jax.ShapeDtypeStruct((T // KPAIR, aw, KPAIR), BF16),
                   jax.ShapeDtypeStruct((IDX_HEADS, T), F32),
                   jax.ShapeDtypeStruct((T, CONV_CH), F32),
                   jax.ShapeDtypeStruct((T, N_BRANCH * D), BF16)],
        compiler_params=_params(1),
        name="in_proj",
    )(x2, attn_g.reshape(1, D), wa, wv.T.astype(BF16), wwi.T.astype(BF16), wglu.astype(BF16),
      wgate.astype(BF16), b_gate.reshape(1, N_BRANCH * D), gq, gk)
    return outs


def _attn_kernel(q_ref, qi_ref, wit_ref, *rest, seq, n_near, topk, n_parts, qb0):
    k_parts, ki_parts, vt_parts = rest[:n_parts], rest[n_parts:2 * n_parts], rest[2 * n_parts:3 * n_parts]
    (bias_hbm, o_ref, k_ref, ki_ref, vt_ref, bias_ref, load_sem, key_scr, att_scr), scr = (
        rest[3 * n_parts:3 * n_parts + 9], rest[3 * n_parts + 9:])
    qb = qb0 + pl.program_id(0)

    @pl.when(pl.program_id(0) == 0)
    def _():
        loads = [pltpu.make_async_copy(bias_hbm, bias_ref, load_sem.at[3 * n_parts])]
        kp = kq = 0
        for n in range(n_parts):
            npair, nqd = k_parts[n].shape[0], ki_parts[n].shape[0]
            loads += [pltpu.make_async_copy(k_parts[n], k_ref.at[pl.ds(kp, npair)], load_sem.at[3 * n]),
                      pltpu.make_async_copy(ki_parts[n], ki_ref.at[pl.ds(kq, nqd)], load_sem.at[3 * n + 1]),
                      pltpu.make_async_copy(vt_parts[n], vt_ref.at[pl.ds(kp, npair)], load_sem.at[3 * n + 2])]
            kp, kq = kp + npair, kq + nqd
        for c in loads:
            c.start()
        for c in loads:
            c.wait()

    nquad = (qb + 4) // 4
    lane_t = lax.broadcasted_iota(I32, (1, QB), 1) + qb * QB
    qchunk = lane_t // CHUNK
    sub = lax.broadcasted_iota(I32, (KQUAD, QB), 0)
    idx_scale = (IDX_DIM ** -0.5) * (IDX_HEADS ** -0.5)
    wrow = wit_ref[...] * idx_scale

    qi_all = jnp.concatenate([qi_ref[:, h * LANES:(h + 1) * LANES] for h in range(IDX_HEADS)], axis=0)

    def score_quad(j, carry, masked=False):
        d = _nt_dot(ki_ref[j], qi_all)
        acc = jnp.zeros((KQUAD, QB), F32)
        for h in range(IDX_HEADS):
            acc = acc + jnp.maximum(d[:, h * QB:(h + 1) * QB], 0.0) * wrow[h:h + 1, :]
        bits = pltpu.bitcast(acc, I32)
        skey = bits ^ ((bits >> 31) & 0x7FFFFFFF)
        if masked:
            skey = jnp.where((sub + j * KQUAD) // CHUNK <= qchunk, skey, INT_MIN)
        key_scr[j] = skey
        return carry

    lax.fori_loop(0, nquad - 1, score_quad, 0)
    score_quad(nquad - 1, 0, masked=True)

    nvis = (qchunk + 1) * CHUNK
    kk = jnp.minimum(topk, nvis)

    def count(pred):
        def body(j, acc8):
            m = jnp.where(pred(key_scr[j], j), 1, 0)
            return acc8 + m.reshape(KQUAD // SUBLANES, SUBLANES, QB).sum(axis=0)
        return lax.fori_loop(0, nquad, body, jnp.zeros((SUBLANES, QB), I32)).sum(axis=0, keepdims=True)

    def bit_step(i, carry):
        ans, cnt = carry
        cand = ans + lax.shift_left(jnp.int32(1), 31 - i)
        c = count(lambda blk, j: blk >= cand)
        take = c >= kk
        return jnp.where(take, cand, ans), jnp.where(take, c, cnt)

    bits_per_trip = 4

    def more_bits(state):
        g, _, cnt = state
        return (g < 32 // bits_per_trip) & (jnp.max(jnp.abs(cnt - kk)) > 0)

    def four_bits(state):
        g, ans, cnt = state
        ans, cnt = lax.fori_loop(g * bits_per_trip, (g + 1) * bits_per_trip, bit_step, (ans, cnt))
        return g + 1, ans, cnt

    _, thr, cnt = lax.while_loop(more_bits, four_bits,
                                 (jnp.int32(0), jnp.full((1, QB), INT_MIN, I32),
                                  jnp.full((1, QB), 0, I32) + nquad * KQUAD))

    @pl.when(jnp.max(cnt - kk) > 0)
    def _():
        n_gt = count(lambda blk, j: blk > thr)
        need = kk - n_gt

        def idx_step(i, jbound):
            cand = jbound + lax.shift_left(jnp.int32(1), (seq.bit_length() - 1) - i)
            c = count(lambda blk, j: (blk == thr) & (sub + j * KQUAD < cand))
            return jnp.where(c <= need, cand, jbound)

        jbound = lax.fori_loop(0, seq.bit_length(), idx_step, jnp.zeros((1, QB), I32))

        def drop(j, carry):
            blk = key_scr[j]
            key_scr[j] = jnp.where((blk == thr) & (sub + j * KQUAD >= jbound), INT_MIN, blk)
            return carry

        lax.fori_loop(0, nquad, drop, 0)

    acc_refs, s_even, s_odd = scr[:ATT_HEADS], scr[ATT_HEADS], scr[ATT_HEADS + 1]
    for acc_ref in acc_refs:
        acc_ref[...] = jnp.zeros(acc_ref.shape, F32)
    head_row = lax.broadcasted_iota(I32, (ATT_HEADS, QB), 0)
    last_pair = k_ref.shape[0] - 1

    def qk(jp, s_ref):
        jc = jnp.minimum(jp, last_pair)
        for h in range(ATT_HEADS):
            s_ref[h] = _nt_dot(k_ref[jc, :, h * LANES:(h + 1) * LANES], q_ref[:, h * LANES:(h + 1) * LANES])

    def softmax_pv(jq, half, s_ref, carry, far):
        m_all, l_all = carry
        jp = 2 * jq + half
        keys = key_scr[jq, half * KPAIR:(half + 1) * KPAIR, :]
        masked = jnp.where(keys >= thr, 0.0, -jnp.inf)
        tile0 = jnp.clip(qb - 2 * jp, 0, n_near)
        tile1 = jnp.clip(qb - 2 * jp - 1, 0, n_near)
        for h in range(ATT_HEADS):
            rows = slice(h * ATT_HEAD_DIM, (h + 1) * ATT_HEAD_DIM)
            if far:
                s = s_ref[h] + masked
            else:
                s = s_ref[h] + jnp.concatenate([bias_ref[tile0, h], bias_ref[tile1, h]], axis=0) + masked
            m = m_all[h:h + 1, :]
            m_new = jnp.maximum(m, jnp.max(s, axis=0, keepdims=True))
            p = jnp.exp2(s - m_new)
            alpha = jnp.exp2(m - m_new)
            l_new = alpha * l_all[h:h + 1, :] + jnp.sum(p, axis=0, keepdims=True)
            m_all = jnp.where(head_row == h, m_new, m_all)
            l_all = jnp.where(head_row == h, l_new, l_all)
            acc_refs[h][...] = alpha * acc_refs[h][...] + _dot(vt_ref[jp, rows, :], p.astype(BF16))
        return m_all, l_all

    def att_quad(jq, carry, far):
        qk(2 * jq + 1, s_odd)
        carry = softmax_pv(jq, 0, s_even, carry, far)
        qk(2 * jq + 2, s_even)
        return softmax_pv(jq, 1, s_odd, carry, far)

    nfar = jnp.maximum(qb - n_near + 1, 0) // 4
    qk(0, s_even)
    carry = lax.fori_loop(0, nfar, functools.partial(att_quad, far=True),
                          (jnp.full((ATT_HEADS, QB), NEG_BIG, F32), jnp.zeros((ATT_HEADS, QB), F32)))
    _, l_all = lax.fori_loop(nfar, nquad, functools.partial(att_quad, far=False), carry)
    for h in range(ATT_HEADS):
        rows = slice(h * ATT_HEAD_DIM, (h + 1) * ATT_HEAD_DIM)
        att_scr[rows, :] = acc_refs[h][...] / l_all[h:h + 1, :]
    o_ref[...] = att_scr[...].T.astype(o_ref.dtype)


def _attention(q, qi, wit, k_parts, ki_parts, vt_parts, bias, S, n_near):
    Tq = q.shape[0]
    hp = ATT_HEADS * LANES
    aw = ATT_HEADS * ATT_HEAD_DIM
    topk = min(TOPK_MAX, S // 4)
    n_parts = len(k_parts)
    k4 = [k.reshape(-1, KPAIR, hp) for k in k_parts]
    ki4 = [ki.reshape(-1, KQUAD, LANES) for ki in ki_parts]
    npr = sum(k.shape[0] for k in k4)
    nqd = sum(ki.shape[0] for ki in ki4)
    assert npr * KPAIR == nqd * KQUAD and vt_parts[0].shape[1:] == (aw, KPAIR)
    qb0 = (npr * KPAIR - Tq) // QB
    return pl.pallas_call(
        functools.partial(_attn_kernel, seq=S, n_near=n_near, topk=topk, n_parts=n_parts, qb0=qb0),
        grid=(Tq // QB,),
        in_specs=[pl.BlockSpec((QB, hp), lambda i: (i, 0)),
                  pl.BlockSpec((QB, hp), lambda i: (i, 0)),
                  pl.BlockSpec((IDX_HEADS, QB), lambda i: (0, i))]
        + [pl.BlockSpec(memory_space=pl.ANY)] * (3 * n_parts + 1),
        out_specs=pl.BlockSpec((QB, aw), lambda i: (i, 0)),
        out_shape=jax.ShapeDtypeStruct((Tq, aw), BF16),
        scratch_shapes=[pltpu.VMEM((npr, KPAIR, hp), BF16), pltpu.VMEM((nqd, KQUAD, LANES), BF16),
                        pltpu.VMEM((npr, aw, KPAIR), BF16), pltpu.VMEM((n_near + 1, ATT_HEADS, QB, QB), F32),
                        pltpu.SemaphoreType.DMA((3 * n_parts + 1,)),
                        pltpu.VMEM((nqd, KQUAD, QB), I32), pltpu.VMEM((aw, QB), F32)]
        + [pltpu.VMEM((ATT_HEAD_DIM, QB), F32) for _ in range(ATT_HEADS)]
        + [pltpu.VMEM((ATT_HEADS, KPAIR, QB), F32) for _ in range(2)],
        compiler_params=_params(1),
        name="dsa_attention",
    )(q, qi, wit, *k4, *ki4, *vt_parts, bias)


HALO = 32


def _mix_kernel(att_ref, u_ref, halo_ref, hist_ref, gate_ref, x_ref, cw_ref, cb_ref, lng_ref, lnb_ref, wao_ref,
                wco_ref, wout_ref, gffn_ref, x1_ref, h2_ref, ext_scr, *, tm):
    i = pl.program_id(0)
    ext_scr[0:HALO, :] = jnp.where(i == 0, hist_ref[...], halo_ref[...])
    ext_scr[HALO:HALO + tm, :] = u_ref[...]
    y = jnp.zeros((tm, CONV_CH), F32)
    for j in range(CONV_WIDTH):
        y = y + cw_ref[j:j + 1, :] * ext_scr[pl.ds(HALO - (CONV_WIDTH - 1) + j, tm), :]
    y = y + cb_ref[...]
    mu = jnp.mean(y, axis=-1, keepdims=True)
    yc = y - mu
    yn = yc * lax.rsqrt(jnp.mean(yc * yc, axis=-1, keepdims=True) + EPS) * lng_ref[...] + lnb_ref[...]
    z = yn * jax.nn.sigmoid(yn)
    y_conv = _dot(z.astype(BF16), wco_ref[...])
    y_att = _dot(att_ref[...], wao_ref[...])
    d = y_att.shape[1]
    g = gate_ref[...]
    mixed = g[:, :d].astype(F32) * y_att + g[:, d:].astype(F32) * y_conv
    x1 = x_ref[...] + _dot(mixed.astype(BF16), wout_ref[...])
    x1_ref[...] = x1
    h2_ref[...] = x1 * lax.rsqrt(jnp.mean(x1 * x1, axis=-1, keepdims=True) + EPS) * gffn_ref[...]


def _mix(att, u, history, gate, x2, conv_w, conv_b, ln_g, ln_b, w_att_out, w_conv_out, w_out, ffn_g, tm=256):
    T, D = x2.shape
    aw = att.shape[1]
    const = lambda *shape: _single(shape, lambda i: (0,) * len(shape))
    hb = tm // HALO
    return pl.pallas_call(
        functools.partial(_mix_kernel, tm=tm),
        grid=(T // tm,),
        in_specs=[pl.BlockSpec((tm, aw), lambda i: (i, 0)),
                  pl.BlockSpec((tm, CONV_CH), lambda i: (i, 0)),
                  pl.BlockSpec((HALO, CONV_CH), lambda i: (jnp.maximum(i * hb - 1, 0), 0)),
                  const(HALO, CONV_CH),
                  pl.BlockSpec((tm, N_BRANCH * D), lambda i: (i, 0)),
                  pl.BlockSpec((tm, D), lambda i: (i, 0)),
                  const(CONV_WIDTH, CONV_CH), const(1, CONV_CH), const(1, CONV_CH), const(1, CONV_CH),
                  const(aw, D), const(CONV_CH, D), const(D, D), const(1, D)],
        out_specs=[pl.BlockSpec((tm, D), lambda i: (i, 0)), pl.BlockSpec((tm, D), lambda i: (i, 0))],
        out_shape=[jax.ShapeDtypeStruct((T, D), F32), jax.ShapeDtypeStruct((T, D), F32)],
        scratch_shapes=[pltpu.VMEM((HALO + tm, CONV_CH), F32)],
        compiler_params=_params(1),
        name="mix_out_proj",
    )(att, u, u, history, gate, x2, conv_w.reshape(CONV_WIDTH, CONV_CH), conv_b.reshape(1, CONV_CH),
      ln_g.reshape(1, CONV_CH), ln_b.reshape(1, CONV_CH), w_att_out.astype(BF16), w_conv_out.astype(BF16),
      w_out.astype(BF16), ffn_g.reshape(1, D))


def _top_rows(sc, k, payload=None):
    rows = sc.shape[0]
    iota = lax.broadcasted_iota(I32, sc.shape, 0)
    out_row = lax.broadcasted_iota(I32, (k, sc.shape[1]), 0)
    vals = jnp.zeros((k, sc.shape[1]), F32)
    idxs = jnp.zeros((k, sc.shape[1]), I32)
    for r in range(k):
        m = jnp.max(sc, axis=0, keepdims=True)
        idx = jnp.min(jnp.where(sc == m, iota, rows), axis=0, keepdims=True)
        hit = iota == idx
        rec = idx if payload is None else jnp.max(jnp.where(hit, payload, -1), axis=0, keepdims=True)
        vals = jnp.where(out_row == r, m, vals)
        idxs = jnp.where(out_row == r, rec, idxs)
        sc = jnp.where(hit, -jnp.inf, sc)
    return vals, idxs


def _route_kernel(h2_ref, wq_ref, sk_ref, e_ref, g_ref, qt_scr, et_scr, gt_scr, *, tm):
    half = N_KEYS
    qt_scr[...] = _nt_dot(wq_ref[...], h2_ref[...].astype(BF16)).astype(BF16)

    def head(h, carry):
        tops = []
        for c in range(2):
            row0 = pl.multiple_of((h * 2 + c) * half, half)
            sc = _dot(sk_ref[h * 2 + c], qt_scr[pl.ds(row0, half), :])
            tops.append(_top_rows(sc, PEER_TOPK))
        (a, ia), (b, ib) = tops
        k = PEER_TOPK
        g = SUBLANES
        assert (k // 2) % g == 0 and k % g == 0
        row = lax.broadcasted_iota(I32, (g, a.shape[1]), 0)
        cand_parts, cidx_parts = [], []

        def add(av, iav, bv, ibv, valid_rows):
            s = av + bv
            if valid_rows < g:
                s = jnp.where(row < valid_rows, s, -jnp.inf)
            cand_parts.append(s)
            cidx_parts.append(iav * N_KEYS + ibv)

        for i in range(k):
            nj = k // (i + 1)
            if nj >= g:
                for j0 in range(0, nj, g):
                    add(a[i:i + 1, :], ia[i:i + 1, :], b[j0:j0 + g, :], ib[j0:j0 + g, :], g)
            elif nj > 1:
                add(a[i:i + 1, :], ia[i:i + 1, :], b[0:g, :], ib[0:g, :], nj)
            elif i % g == 0:
                add(a[i:i + g, :], ia[i:i + g, :], b[0:1, :], ib[0:1, :], g)
        cand = jnp.concatenate(cand_parts, axis=0)
        cidx = jnp.concatenate(cidx_parts, axis=0)
        ts, te = _top_rows(cand, PEER_TOPK, payload=cidx)
        ex = jnp.exp(ts - ts[0:1, :])
        gate = ex / jnp.sum(ex, axis=0, keepdims=True)
        r0 = pl.multiple_of(h * PEER_TOPK, PEER_TOPK)
        et_scr[pl.ds(r0, PEER_TOPK), :] = te
        gt_scr[pl.ds(r0, PEER_TOPK), :] = gate
        return carry

    lax.fori_loop(0, PEER_HEADS, head, 0)
    e_ref[...] = et_scr[...].T
    g_ref[...] = gt_scr[...].T


def _peer_route(h2, w_peer_q, sub_keys, tm=256):
    T, D = h2.shape
    nsel = PEER_HEADS * PEER_TOPK
    qd = w_peer_q.shape[1]
    half = sub_keys.shape[-1]
    skb = sub_keys.reshape(PEER_HEADS * 2, N_KEYS, half).astype(BF16)
    return pl.pallas_call(
        functools.partial(_route_kernel, tm=tm),
        grid=(T // tm,),
        in_specs=[pl.BlockSpec((tm, D), lambda i: (i, 0)),
                  _single((qd, D), lambda i: (0, 0)),
                  _single((PEER_HEADS * 2, N_KEYS, half), lambda i: (0, 0, 0))],
        out_specs=[pl.BlockSpec((tm, nsel), lambda i: (i, 0)), pl.BlockSpec((tm, nsel), lambda i: (i, 0))],
        out_shape=[jax.ShapeDtypeStruct((T, nsel), I32), jax.ShapeDtypeStruct((T, nsel), F32)],
        scratch_shapes=[pltpu.VMEM((qd, tm), BF16), pltpu.VMEM((nsel, tm), I32), pltpu.VMEM((nsel, tm), F32)],
        compiler_params=_params(1),
        name="peer_route",
    )(h2, w_peer_q.T.astype(BF16), skb)


TOK = 8


def _gelu_tanh(x):
    return 0.5 * x * (1.0 + jnp.tanh(math.sqrt(2.0 / math.pi) * (x + 0.044715 * (x * x * x))))


def _sublane_sums(ps):
    sub = lax.broadcasted_iota(I32, (SUBLANES, LANES), 0)
    lvl, stride = list(ps), SUBLANES // 2
    while len(lvl) > 1:
        half = len(lvl) // 2
        low = (sub & stride) == 0
        nxt = []
        for n in range(half):
            a, b = lvl[n], lvl[n + half]
            nxt.append(jnp.where(low, a + pltpu.roll(a, SUBLANES - stride, 0), b + pltpu.roll(b, stride, 0)))
        lvl, stride = nxt, stride // 2
    return lvl[0]


def _pack_uv(u, v):
    ne, d = u.shape
    hi = lax.bitcast_convert_type(u.astype(BF16), jnp.uint16).astype(jnp.uint32)
    lo = lax.bitcast_convert_type(v.astype(BF16), jnp.uint16).astype(jnp.uint32)
    return ((hi << 16) | lo).reshape(ne, d // LANES, LANES)


def _u_of(word):
    return pltpu.bitcast(word & jnp.uint32(0xFFFF0000), F32)


def _v_of(word):
    return pltpu.bitcast(word << 16, F32)


def _expert_group(row, h2_ref, x1_ref, gate_ref, o_ref, abc_scr, base, nsel, before_dots=None, before_sum=None):
    dsub = h2_ref.shape[1]
    lane_id = lax.broadcasted_iota(I32, (nsel, LANES), 1)
    st = jnp.zeros((nsel, LANES), F32)
    for t in range(TOK):
        if before_dots is not None:
            before_dots(t)
        xt = h2_ref[base + t]
        qs = []
        for g in range(nsel // SUBLANES):
            ps = [_u_of(row(t, g * SUBLANES + k)) * xt for k in range(SUBLANES)]
            qs.append(_sublane_sums(ps))
        q = jnp.concatenate(qs, axis=0)
        st = jnp.where(lane_id == t, jnp.sum(q, axis=1, keepdims=True), st)
    g8 = gate_ref[base:base + TOK, :]
    gt = jnp.concatenate([g8, jnp.zeros((nsel - TOK, nsel), F32)], axis=0).T
    at = _gelu_tanh(st) * gt

    for t in range(TOK):
        if before_sum is not None:
            before_sum(t)
        abc_scr[t] = jnp.broadcast_to(at[:, t:t + 1], (nsel, LANES))
        accs = [jnp.zeros((dsub, LANES), F32) for _ in range(4)]
        for n in range(nsel):
            accs[n % 4] = accs[n % 4] + abc_scr[t, n:n + 1, :] * _v_of(row(t, n))
        o_ref[base + t] = x1_ref[base + t] + ((accs[0] + accs[1]) + (accs[2] + accs[3]))


def _expert_kernel(idxc_ref, idxn_ref, h2_ref, x1_ref, gate_ref, uv_ref, *rest, nsel):
    o_ref, buf, sem, abc_scr = rest[-4:]
    i = pl.program_id(0)
    nsteps = pl.num_programs(0)
    rows = TOK * nsel

    def issue_token(idx_ref, tok, s, t, n0=0, n1=nsel):
        for n in range(n0, n1):
            pltpu.make_async_copy(uv_ref.at[idx_ref[tok, n]], buf.at[s, t * nsel + n],
                                  sem.at[s]).start(priority=n % 2)

    def wait_slot(s):
        pltpu.make_async_copy(uv_ref.at[pl.ds(0, rows)], buf.at[s], sem.at[s]).wait()

    @pl.when(i == 0)
    def _():
        for t in range(TOK):
            issue_token(idxc_ref, t, 0, t)

    n_early = (3 * nsel) // 4

    for grp in range(2):
        wait_slot(grp)
        nxt_idx, nxt_tok, nxt_slot = (idxc_ref, TOK, 1) if grp == 0 else (idxn_ref, 0, 0)
        _expert_group(lambda t, n, grp=grp: buf[grp, t * nsel + n], h2_ref, x1_ref, gate_ref, o_ref, abc_scr,
                      grp * TOK, nsel,
                      before_dots=lambda t: issue_token(nxt_idx, nxt_tok + t, nxt_slot, t, 0, n_early),
                      before_sum=lambda t: issue_token(nxt_idx, nxt_tok + t, nxt_slot, t, n_early, nsel))

    @pl.when(i == nsteps - 1)
    def _():
        wait_slot(0)


def _expert_staged_kernel(rows_ref, h2_ref, x1_ref, gate_ref, prev_ref, o_ref, abc_scr, *, nsel):
    del prev_ref
    for grp in range(h2_ref.shape[0] // TOK):
        _expert_group(lambda t, n, grp=grp: rows_ref[(grp * TOK + t) * nsel + n], h2_ref, x1_ref, gate_ref, o_ref,
                      abc_scr, grp * TOK, nsel)


SC_CORES = 2
SC_SUBCORES = 16
SC_CHUNK = 16
SC_NBUF = 7
SC_SLAB = 2048


def _sc_gather(table, idx):
    n = idx.shape[0]
    nw = SC_CORES * SC_SUBCORES
    per_w = n // nw
    nb = SC_NBUF
    assert n % nw == 0 and per_w % SC_SLAB == 0 and SC_SLAB % SC_CHUNK == 0
    nslab, nchunk = per_w // SC_SLAB, SC_SLAB // SC_CHUNK
    assert nchunk >= nb
    row_shape = table.shape[1:]
    mesh = plsc.VectorSubcoreMesh(core_axis_name="c", subcore_axis_name="s")

    @functools.partial(
        pl.kernel, mesh=mesh, out_type=jax.ShapeDtypeStruct((n,) + row_shape, table.dtype),
        scratch_types=[pltpu.VMEM((SC_SLAB,), jnp.int32)]
        + [pltpu.VMEM((SC_CHUNK,) + row_shape, table.dtype) for _ in range(nb)]
        + [pltpu.SemaphoreType.DMA for _ in range(2 * nb)])
    def gather(table_hbm, idx_hbm, out_hbm, idx_v, *scr):
        wid = lax.axis_index("s") * SC_CORES + lax.axis_index("c")
        bufs, gsem, wsem = scr[:nb], scr[nb:2 * nb], scr[2 * nb:]

        def gather_copy(c, b):
            return pltpu.make_async_copy(table_hbm.at[idx_v.at[pl.ds(c * SC_CHUNK, SC_CHUNK)]], bufs[b], gsem[b])

        @pl.loop(0, nslab)
        def _(sl):
            base = wid * per_w + sl * SC_SLAB
            pltpu.sync_copy(idx_hbm.at[pl.ds(base, SC_SLAB)], idx_v)

            def write_copy(c, b):
                return pltpu.make_async_copy(bufs[b], out_hbm.at[pl.ds(base + c * SC_CHUNK, SC_CHUNK)], wsem[b])

            for b in range(nb - 1):
                gather_copy(b, b).start()

            @pl.loop(0, -(-nchunk // nb) * nb, step=nb)
            def _(c0):
                for b in range(nb):
                    c = c0 + b
                    prev = (b - 1) % nb

                    @pl.when(c < nchunk)
                    def _():
                        gather_copy(c, b).wait()
                        write_copy(c, b).start()

                    @pl.when((c >= 1) & (c < nchunk))
                    def _():
                        write_copy(c - 1, prev).wait()

                    @pl.when(c + nb - 1 < nchunk)
                    def _():
                        gather_copy(c + nb - 1, prev).start()

            write_copy(nchunk - 1, (nchunk - 1) % nb).wait()

    return gather(table, idx)


def _experts_dma(eidx, gates, h2, x1, uv, ta, after=()):
    T, dsub, _ = x1.shape
    nsel = eidx.shape[1]
    tb = 2 * TOK
    nsteps = ta // tb
    return pl.pallas_call(
        functools.partial(_expert_kernel, nsel=nsel),
        grid=(nsteps,),
        in_specs=[pl.BlockSpec((tb, nsel), lambda i: (i, 0), memory_space=pltpu.SMEM),
                  pl.BlockSpec((tb, nsel), lambda i: (jnp.minimum(i + 1, nsteps - 1), 0), memory_space=pltpu.SMEM),
                  pl.BlockSpec((tb, dsub, LANES), lambda i: (i, 0, 0)),
                  pl.BlockSpec((tb, dsub, LANES), lambda i: (i, 0, 0)),
                  pl.BlockSpec((tb, nsel), lambda i: (i, 0)),
                  pl.BlockSpec(memory_space=pl.ANY)] + [pl.BlockSpec(memory_space=pl.ANY)] * len(after),
        out_specs=pl.BlockSpec((tb, dsub, LANES), lambda i: (i, 0, 0)),
        out_shape=jax.ShapeDtypeStruct((T, dsub, LANES), F32),
        scratch_shapes=[pltpu.VMEM((2, TOK * nsel, dsub, LANES), jnp.uint32), pltpu.SemaphoreType.DMA((2,)),
                        pltpu.VMEM((TOK, nsel, LANES), F32)],
        compiler_params=_params(1),
        name="peer_experts",
    )(eidx, eidx, h2, x1, gates, uv, *after)


def _experts_staged(rows, gates, h2, x1, out, ta):
    T, dsub, _ = x1.shape
    nsel = gates.shape[1]
    tb = 4 * TOK
    first = ta // tb
    tok = lambda i: (first + i, 0, 0)
    return pl.pallas_call(
        functools.partial(_expert_staged_kernel, nsel=nsel),
        grid=((T - ta) // tb,),
        in_specs=[pl.BlockSpec((tb * nsel, dsub, LANES), lambda i: (i, 0, 0)),
                  pl.BlockSpec((tb, dsub, LANES), tok),
                  pl.BlockSpec((tb, dsub, LANES), tok),
                  pl.BlockSpec((tb, nsel), lambda i: (first + i, 0)),
                  pl.BlockSpec(memory_space=pl.ANY)],
        out_specs=pl.BlockSpec((tb, dsub, LANES), tok),
        out_shape=jax.ShapeDtypeStruct((T, dsub, LANES), F32),
        scratch_shapes=[pltpu.VMEM((TOK, nsel, LANES), F32)],
        input_output_aliases={4: 0},
        compiler_params=_params(1),
        name="peer_experts_staged",
    )(rows, h2, x1, gates, out)


def _staged_tokens(T, use_sc):
    unit = SC_CORES * SC_SUBCORES * SC_SLAB // (PEER_HEADS * PEER_TOPK)
    return int(T * use_sc) // unit * unit


def _peer_gather_start(eidx, uv, use_sc):
    T, nsel = eidx.shape
    ts = _staged_tokens(T, use_sc)
    return _sc_gather(uv, eidx[T - ts:].reshape(ts * nsel)) if ts else None


def _peer_dma_part(eidx, gates, h2, x1, uv, use_sc, after):
    T, D = x1.shape
    dsub = D // LANES
    ta = T - _staged_tokens(T, use_sc)
    h3, x3 = h2.reshape(T, dsub, LANES), x1.reshape(T, dsub, LANES)
    return _experts_dma(eidx, gates, h3, x3, uv, ta, after) if ta else jnp.zeros_like(x3)


def _peer_finish(rows, out, gates, h2, x1, use_sc):
    T, D = x1.shape
    dsub = D // LANES
    if rows is not None:
        out = _experts_staged(rows, gates, h2.reshape(T, dsub, LANES), x1.reshape(T, dsub, LANES), out,
                              T - _staged_tokens(T, use_sc))
    return out.reshape(T, D)


def _ple_kernel(x_ref, p_ref, g_ref, wg_ref, wp_ref, o_ref):
    x = x_ref[...]
    h = x * lax.rsqrt(jnp.mean(x * x, axis=-1, keepdims=True) + EPS) * g_ref[...]
    gate = jax.nn.sigmoid(_dot(h.astype(BF16), wg_ref[...]))
    o_ref[...] = x + gate * _dot(p_ref[...].astype(BF16), wp_ref[...])


def _ple(x2, p2, ple_g, w_gate, w_proj, tm=512):
    T, D = x2.shape
    pd = p2.shape[1]
    return pl.pallas_call(
        _ple_kernel,
        grid=(T // tm,),
        in_specs=[pl.BlockSpec((tm, D), lambda i: (i, 0)), pl.BlockSpec((tm, pd), lambda i: (i, 0)),
                  _single((1, D), lambda i: (0, 0)), _single((D, D), lambda i: (0, 0)),
                  _single((pd, D), lambda i: (0, 0))],
        out_specs=pl.BlockSpec((tm, D), lambda i: (i, 0)),
        out_shape=jax.ShapeDtypeStruct((T, D), F32),
        compiler_params=_params(1),
        name="ple",
    )(x2, p2, ple_g.reshape(1, D), w_gate.astype(BF16), w_proj.astype(BF16))


def kernel(x, p, rel_bias, attn_norm_g, w_in, b_gate, q_norm_g, k_norm_g, w_att_out, conv_w, conv_b, conv_ln_g,
           conv_ln_b, w_conv_out, w_out, ffn_norm_g, w_peer_q, peer_sub_keys, peer_u, peer_v, ple_norm_g,
           w_ple_gate, w_ple_proj):
    B, S, D = x.shape
    depth = w_in.shape[0]
    assert S % KQUAD == 0 and D % LANES == 0 and S % 512 == 0
    n_near = _num_near_tiles(S)
    bias = _bias_tiles(rel_bias, n_near)
    nsl = 2 if S % (2 * 2048) == 0 else 1
    sl = S // nsl
    xs = [x[b, s * sl:(s + 1) * sl] for b in range(B) for s in range(nsl)]
    units = [(b, s) for b in range(B) for s in range(nsl)]
    for i in range(depth):
        uv = _pack_uv(peer_u[i], peer_v[i])

        def finish(pending, i=i):
            rows, part, gates, h2, x1, (b, s), use_sc = pending
            x2 = _peer_finish(rows, part, gates, h2, x1, use_sc)
            return _ple(x2, p[i, b, s * sl:(s + 1) * sl], ple_norm_g[i], w_ple_gate[i], w_ple_proj[i])

        pending, outs, sc_rows = None, [], []
        keys, history = [], None
        for n, (b, s) in enumerate(units):
            use_sc = (1.0, 0.5, 0.0)[max(0, n - (len(units) - 3))]
            q, k, qi, ki, vt, wit, u, gate = _in_proj(xs[n], attn_norm_g[i], w_in[i], b_gate[i], q_norm_g[i],
                                                      k_norm_g[i])
            if s == 0:
                keys, history = [], jnp.zeros((HALO, CONV_CH), F32)
            keys.append((k, ki, vt))
            att = _attention(q, qi, wit, [kp[0] for kp in keys], [kp[1] for kp in keys], [kp[2] for kp in keys],
                             bias, S, n_near)
            x1, h2 = _mix(att, u, history, gate, xs[n], conv_w[i], conv_b[i], conv_ln_g[i], conv_ln_b[i],
                          w_att_out[i], w_conv_out[i], w_out[i], ffn_norm_g[i])
            history = u[sl - HALO:]
            eidx, gates = _peer_route(h2, w_peer_q[i], peer_sub_keys[i])
            rows = _peer_gather_start(eidx, uv, use_sc)
            if pending is not None:
                outs.append(finish(pending))
            if rows is not None:
                sc_rows.append(rows)
            part = _peer_dma_part(eidx, gates, h2, x1, uv, use_sc, tuple(sc_rows))
            pending = (rows, part, gates, h2, x1, (b, s), use_sc)
        outs.append(finish(pending))
        xs = outs
    return jnp.concatenate(xs, axis=0).reshape(B, S, D)
```
